```python
import jax, jax.numpy as jnp
from jax import lax
import numpy as np

D_MODEL = 1024
BATCH = 8
SEQ = 2048
DEPTH = 2
DEC_BATCH = 32
DEC_SEQ = 8
PAST_LEN = 8192
PAGE_SIZE = 128

HEAD_DIM = 64
POOL_WIDTH = D_MODEL // 4
POOL_WINDOWS = (2, 4, 8, 16)
POOL_GROUP = POOL_WIDTH // len(POOL_WINDOWS)
POOL_HIST = max(POOL_WINDOWS) - 1
ATTN_WIDTH = 3 * D_MODEL // 8
DIL_PAIRS = ((128, 1), (512, 4), (2048, 16))
N_DIL = len(DIL_PAIRS)
ATTN_HEADS = ATTN_WIDTH // HEAD_DIM
HEADS_PER_DIL = ATTN_HEADS // N_DIL
ATTN_OUT = HEADS_PER_DIL * HEAD_DIM
CONV_WIDTH = D_MODEL - POOL_WIDTH - ATTN_WIDTH
CONV_K = 3
ROPE_DIM = HEAD_DIM // 4
ROPE_THETA = 500000.0
D_FF = 4 * D_MODEL
IN_COLS = POOL_WIDTH + 3 * ATTN_WIDTH + 3 * CONV_WIDTH
MIX_OUT = POOL_WIDTH + ATTN_OUT + CONV_WIDTH
EPS = 1e-6
NEG_INF = -1e30

kernel_name = "hybrid_pool_dilattn_shortconv_decode_step"


def rmsnorm(x, g):
    xf = x.astype(jnp.float32)
    y = xf * lax.rsqrt(jnp.mean(xf * xf, axis=-1, keepdims=True) + EPS)
    return (y * g.astype(jnp.float32)).astype(x.dtype)


def partial_rope(x, pos):
    half = ROPE_DIM // 2
    inv = jnp.power(jnp.float32(ROPE_THETA), -jnp.arange(half, dtype=jnp.float32) / half)
    ang = pos.astype(jnp.float32)[:, None] * inv[None, :]
    cos = jnp.cos(ang)[:, None, :]
    sin = jnp.sin(ang)[:, None, :]
    xf = x.astype(jnp.float32)
    x1 = xf[..., :half]
    x2 = xf[..., half:ROPE_DIM]
    out = jnp.concatenate([x1 * cos - x2 * sin, x2 * cos + x1 * sin, xf[..., ROPE_DIM:]], axis=-1)
    return out.astype(x.dtype)


def softmax_stats(s):
    m = jnp.max(s, axis=-1, keepdims=True)
    p = jnp.exp(s - m)
    den = jnp.sum(p, axis=-1, keepdims=True)
    return p / den, (m + jnp.log(den))[..., 0]


def pool_mix(u_hist, u_new, pos, pool_w, pool_scale):
    n, t, c = u_new.shape
    u_ext = jnp.concatenate([u_hist, u_new], axis=1).astype(jnp.float32)
    cs = jnp.concatenate([jnp.zeros((n, 1, c), jnp.float32), jnp.cumsum(u_ext, axis=1)], axis=1)
    end = cs[:, POOL_HIST + 1:]
    uf = u_new.astype(jnp.float32)
    outs = []
    for j, w in enumerate(POOL_WINDOWS):
        sl = slice(j * POOL_GROUP, (j + 1) * POOL_GROUP)
        win_sum = end[..., sl] - cs[:, POOL_HIST + 1 - w:POOL_HIST + 1 - w + t, sl]
        cnt = jnp.minimum(pos + 1, w).astype(jnp.float32)[None, :, None]
        outs.append(win_sum / cnt - uf[..., sl])
    d = jnp.stack(outs, axis=2).astype(u_new.dtype)
    y = jnp.einsum("ntgc,gcd->ntgd", d, pool_w).reshape(n, t, POOL_WIDTH)
    return y * pool_scale


def conv_mix(z_hist, z_new, conv_w):
    t = z_new.shape[1]
    z_ext = jnp.concatenate([z_hist, z_new], axis=1)
    return sum(conv_w[i] * z_ext[:, i:i + t] for i in range(CONV_K))


def band_attn(q, k, v, n_back):
    n, l, h, dh = q.shape
    qb_len = n_back
    nb = -(-l // qb_len)
    lp = nb * qb_len
    pad = ((0, 0), (0, lp - l), (0, 0), (0, 0))

    def blocks(a):
        return jnp.pad(a, pad).reshape(n, nb, qb_len, h, dh)

    def with_prev(a):
        prev = jnp.concatenate([jnp.zeros_like(a[:, :1]), a[:, :-1]], axis=1)
        return jnp.concatenate([prev, a], axis=2)

    qb = blocks(q)
    kk = with_prev(blocks(k))
    vv = with_prev(blocks(v))
    s = jnp.einsum("nbqhd,nbkhd->nbhqk", qb, kk, preferred_element_type=jnp.float32) * (HEAD_DIM ** -0.5)
    qi = jnp.arange(qb_len)[:, None]
    ki = jnp.arange(2 * qb_len)[None, :]
    dist = qi + qb_len - ki
    blk = jnp.arange(nb)[:, None, None]
    valid = (dist >= 0) & (dist <= n_back) & (blk * qb_len - qb_len + ki >= 0)
    s = jnp.where(valid[None, :, None], s, NEG_INF)
    p, lse = softmax_stats(s)
    o = jnp.einsum("nbhqk,nbkhd->nbqhd", p, vv.astype(jnp.float32)).reshape(n, lp, h, dh)[:, :l]
    lse = lse.transpose(0, 1, 3, 2).reshape(n, lp, h)[:, :l]
    return o, lse


def dilated_prompt(q, k, v, window, dil):
    b, s, h, dh = q.shape
    m = s // dil

    def to_res(a):
        return a.reshape(b, m, dil, h, dh).transpose(0, 2, 1, 3, 4).reshape(b * dil, m, h, dh)

    o, lse = band_attn(to_res(q), to_res(k), to_res(v), window // dil)
    o = o.reshape(b, dil, m, h, dh).transpose(0, 2, 1, 3, 4).reshape(b, s, h, dh)
    lse = lse.reshape(b, dil, m, h).transpose(0, 2, 1, 3).reshape(b, s, h)
    return o, lse


def dilated_sample(q, k_ext, v_ext, start_pos, window, dil):
    t = q.shape[1]
    n_keys = window // dil + 1
    idx = window + jnp.arange(t)[:, None] - dil * jnp.arange(n_keys)[None, :]
    kg = k_ext[:, idx]
    vg = v_ext[:, idx]
    s = jnp.einsum("nthd,ntjhd->nthj", q, kg, preferred_element_type=jnp.float32) * (HEAD_DIM ** -0.5)
    valid = (start_pos - window + idx) >= 0
    s = jnp.where(valid[None, :, None, :], s, NEG_INF)
    p, lse = softmax_stats(s)
    o = jnp.einsum("nthj,ntjhd->nthd", p, vg.astype(jnp.float32))
    return o, lse


def trunk_layer(x, pos, is_prompt, pool_hist, conv_hist, kv_bufs,
                norm1_g, w_in, q_norm_g, k_norm_g, pool_w, pool_scale, conv_w,
                w_out, norm2_g, w_up, w_down):
    n, t, _ = x.shape
    h = rmsnorm(x, norm1_g)
    proj = jnp.einsum("ntd,dc->ntc", h, w_in)
    cuts = [POOL_WIDTH, POOL_WIDTH + ATTN_WIDTH, POOL_WIDTH + 2 * ATTN_WIDTH, POOL_WIDTH + 3 * ATTN_WIDTH,
            POOL_WIDTH + 3 * ATTN_WIDTH + CONV_WIDTH, POOL_WIDTH + 3 * ATTN_WIDTH + 2 * CONV_WIDTH]
    u, q, k, v, gb, gc, gh = jnp.split(proj, cuts, axis=-1)

    y_pool = pool_mix(pool_hist, u, pos, pool_w, pool_scale)
    new_pool = jnp.concatenate([pool_hist, u], axis=1)[:, -POOL_HIST:]

    q = partial_rope(rmsnorm(q.reshape(n, t, ATTN_HEADS, HEAD_DIM), q_norm_g), pos)
    k = partial_rope(rmsnorm(k.reshape(n, t, ATTN_HEADS, HEAD_DIM), k_norm_g), pos)
    v = v.reshape(n, t, ATTN_HEADS, HEAD_DIM)
    outs, lses, new_kv = [], [], []
    for g, (window, dil) in enumerate(DIL_PAIRS):
        hs = slice(g * HEADS_PER_DIL, (g + 1) * HEADS_PER_DIL)
        qg, kg, vg = q[:, :, hs], k[:, :, hs], v[:, :, hs]
        kv_ext = jnp.concatenate([kv_bufs[g], jnp.stack([kg, vg], axis=2)], axis=1)
        if is_prompt:
            o, lse = dilated_prompt(qg, kg, vg, window, dil)
        else:
            o, lse = dilated_sample(qg, kv_ext[:, :, 0], kv_ext[:, :, 1], pos[0], window, dil)
        outs.append(o)
        lses.append(lse)
        new_kv.append(kv_ext[:, -window:])
    wts = jax.nn.softmax(jnp.stack(lses, axis=0), axis=0)
    y_attn = jnp.sum(wts[..., None] * jnp.stack(outs, axis=0), axis=0)
    y_attn = y_attn.reshape(n, t, ATTN_OUT).astype(x.dtype)

    z = gc * gh
    y_conv = gb * conv_mix(conv_hist, z, conv_w)
    new_conv = jnp.concatenate([conv_hist, z], axis=1)[:, -(CONV_K - 1):]

    mixed = jnp.concatenate([y_pool, y_attn, y_conv], axis=-1)
    x = x + jnp.einsum("ntc,cd->ntd", mixed, w_out)

    hf = jnp.einsum("ntd,df->ntf", rmsnorm(x, norm2_g), w_up)
    x = x + jnp.einsum("ntf,fd->ntd", jnp.square(jax.nn.relu(hf)), w_down)
    return x, new_pool, new_conv, new_kv


def setup_inputs(seed: int = 0) -> dict:
    key = jax.random.key(seed)
    ks = jax.random.split(key, 24)
    f32 = jnp.float32

    def nrm(k, shape, scale):
        return jax.random.normal(k, shape, f32) * scale

    kv_shape = lambda w: (DEPTH, DEC_BATCH, w, 2, HEADS_PER_DIL, HEAD_DIM)
    return {
        "x_prompt": nrm(ks[0], (BATCH, SEQ, D_MODEL), 1.0),
        "x_sample": nrm(ks[1], (DEC_BATCH, DEC_SEQ, D_MODEL), 1.0),
        "state_pool": nrm(ks[2], (DEPTH, DEC_BATCH, POOL_HIST, POOL_WIDTH), 1.0),
        "state_conv": nrm(ks[3], (DEPTH, DEC_BATCH, CONV_K - 1, CONV_WIDTH), 1.0),
        "cache_kv_w128": nrm(ks[4], kv_shape(DIL_PAIRS[0][0]), 1.0),
        "cache_kv_w512": nrm(ks[5], kv_shape(DIL_PAIRS[1][0]), 1.0),
        "cache_kv_w2048": nrm(ks[6], kv_shape(DIL_PAIRS[2][0]), 1.0),
        "norm1_g": 1.0 + nrm(ks[7], (DEPTH, D_MODEL), 0.02),
        "w_in": nrm(ks[8], (DEPTH, D_MODEL, IN_COLS), D_MODEL ** -0.5),
        "q_norm_g": 1.0 + nrm(ks[9], (DEPTH, HEAD_DIM), 0.02),
        "k_norm_g": 1.0 + nrm(ks[10], (DEPTH, HEAD_DIM), 0.02),
        "pool_w": nrm(ks[11], (DEPTH, len(POOL_WINDOWS), POOL_GROUP, POOL_GROUP), POOL_GROUP ** -0.5),
        "pool_scale": 1.0 + nrm(ks[12], (DEPTH, POOL_WIDTH), 0.02),
        "conv_w": nrm(ks[13], (DEPTH, CONV_K, CONV_WIDTH), CONV_K ** -0.5),
        "w_out": nrm(ks[14], (DEPTH, MIX_OUT, D_MODEL), MIX_OUT ** -0.5),
        "norm2_g": 1.0 + nrm(ks[15], (DEPTH, D_MODEL), 0.02),
        "w_up": nrm(ks[16], (DEPTH, D_MODEL, D_FF), D_MODEL ** -0.5),
        "w_down": nrm(ks[17], (DEPTH, D_FF, D_MODEL), D_FF ** -0.5),
    }


def reference(x_prompt, x_sample, state_pool, state_conv, cache_kv_w128, cache_kv_w512, cache_kv_w2048,
              norm1_g, w_in, q_norm_g, k_norm_g, pool_w, pool_scale, conv_w, w_out, norm2_g, w_up, w_down):
    b, s, _ = x_prompt.shape
    pos_p = jnp.arange(s, dtype=jnp.int32)
    pos_s = PAST_LEN + jnp.arange(x_sample.shape[1], dtype=jnp.int32)
    caches = (cache_kv_w128, cache_kv_w512, cache_kv_w2048)
    xp, xs = x_prompt, x_sample
    pool_p, conv_p, kv_p = [], [], [[] for _ in DIL_PAIRS]
    pool_s, conv_s, kv_s = [], [], [[] for _ in DIL_PAIRS]
    for layer in range(DEPTH):
        wts = (norm1_g[layer], w_in[layer], q_norm_g[layer], k_norm_g[layer], pool_w[layer], pool_scale[layer],
               conv_w[layer], w_out[layer], norm2_g[layer], w_up[layer], w_down[layer])
        zero_kv = [jnp.zeros((b, w, 2, HEADS_PER_DIL, HEAD_DIM), xp.dtype) for (w, _) in DIL_PAIRS]
        xp, np_pool, np_conv, np_kv = trunk_layer(
            xp, pos_p, True,
            jnp.zeros((b, POOL_HIST, POOL_WIDTH), xp.dtype),
            jnp.zeros((b, CONV_K - 1, CONV_WIDTH), xp.dtype),
            zero_kv, *wts)
        xs, ns_pool, ns_conv, ns_kv = trunk_layer(
            xs, pos_s, False, state_pool[layer], state_conv[layer],
            [c[layer] for c in caches], *wts)
        pool_p.append(np_pool)
        conv_p.append(np_conv)
        pool_s.append(ns_pool)
        conv_s.append(ns_conv)
        for g in range(N_DIL):
            kv_p[g].append(np_kv[g])
            kv_s[g].append(ns_kv[g])
    return (xp, xs,
            jnp.stack(pool_p), jnp.stack(conv_p),
            jnp.stack(kv_p[0]), jnp.stack(kv_p[1]), jnp.stack(kv_p[2]),
            jnp.stack(pool_s), jnp.stack(conv_s),
            jnp.stack(kv_s[0]), jnp.stack(kv_s[1]), jnp.stack(kv_s[2]))
```

```python
import functools

import jax
import jax.numpy as jnp
from jax import lax
from jax.experimental import pallas as pl
from jax.experimental.pallas import tpu as pltpu

D_MODEL = 1024
HEAD_DIM = 64
POOL_WIDTH = 256
POOL_WINDOWS = (2, 4, 8, 16)
POOL_GROUP = 64
POOL_HIST = 15
ATTN_WIDTH = 384
DIL_PAIRS = ((128, 1), (512, 4), (2048, 16))
N_DIL = 3
GROUP_WIDTH = 128
CONV_WIDTH = 384
CONV_K = 3
ROPE_DIM = 16
ROPE_THETA = 500000.0
D_FF = 4096
IN_COLS = 2560
MIX_OUT = 768
EPS = 1e-6
NEG_INF = -1e30
PAST_LEN = 8192
N_BACK = 128

LANES = 128
HIST_ROWS = 16
ATTN_TILE = 128
VMEM_LIMIT = 56 * 1024 * 1024

F32 = jnp.float32
BF16 = jnp.bfloat16

_Q_OFF = POOL_WIDTH
_KV_OFF = _Q_OFF + ATTN_WIDTH
_GB_OFF = _KV_OFF + 2 * ATTN_WIDTH
_GC_OFF = _GB_OFF + CONV_WIDTH
_GH_OFF = _GC_OFF + CONV_WIDTH


def _const_spec(shape):
    return pl.BlockSpec(shape, lambda *_: (0,) * len(shape), pipeline_mode=pl.Buffered(1))


def _params():
    return pltpu.CompilerParams(dimension_semantics=("arbitrary",), vmem_limit_bytes=VMEM_LIMIT)


def _proj_kernel(x_ref, g1_ref, w_ref, qg_ref, kg_ref, rc_ref, ra_ref, rb_ref, hm_ref,
                 u_ref, q0_ref, q1_ref, q2_ref, kv0_ref, kv1_ref, kv2_ref, gb_ref, z_ref):
    x = x_ref[...]
    ms = jnp.mean(x * x, axis=-1, keepdims=True)
    hb = (x * lax.rsqrt(ms + EPS) * g1_ref[...]).astype(BF16)

    def seg(lo, hi):
        return jnp.dot(hb, w_ref[:, lo:hi], preferred_element_type=F32)

    rc, ra, rb = rc_ref[...], ra_ref[...], rb_ref[...]
    hm = hm_ref[...]

    def head_norm_rope(xb, gain):
        sq = xb * xb
        hi = sq.astype(BF16)
        lo = (sq - hi.astype(F32)).astype(BF16)
        msq = (jnp.dot(hi, hm, preferred_element_type=F32) + jnp.dot(lo, hm, preferred_element_type=F32))
        xn = xb * lax.rsqrt(msq + EPS) * gain
        return xn * rc + pltpu.roll(xn, LANES - ROPE_DIM // 2, 1) * ra + pltpu.roll(xn, ROPE_DIM // 2, 1) * rb

    u_ref[...] = seg(0, POOL_WIDTH)
    for g, (q_ref, kv_ref) in enumerate(((q0_ref, kv0_ref), (q1_ref, kv1_ref), (q2_ref, kv2_ref))):
        qb = seg(_Q_OFF + g * GROUP_WIDTH, _Q_OFF + (g + 1) * GROUP_WIDTH)
        q_ref[...] = head_norm_rope(qb, qg_ref[...]).astype(q_ref.dtype)
        kvb = seg(_KV_OFF + 2 * g * GROUP_WIDTH, _KV_OFF + 2 * (g + 1) * GROUP_WIDTH)
        kv_ref[:, :GROUP_WIDTH] = head_norm_rope(kvb[:, :GROUP_WIDTH], kg_ref[...])
        kv_ref[:, GROUP_WIDTH:] = kvb[:, GROUP_WIDTH:]
    gb_ref[...] = seg(_GB_OFF, _GC_OFF)
    z_ref[...] = seg(_GC_OFF, _GH_OFF) * seg(_GH_OFF, IN_COLS)


def _proj(x, g1, w_in, qg, kg, rope, hm, *, tm, q_dtype):
    rows = x.shape[0]
    n_tab = rope[0].shape[0] // tm
    row = lambda c: pl.BlockSpec((tm, c), lambda i: (i, 0))
    tab = pl.BlockSpec((tm, LANES), lambda i: (i % n_tab, 0))
    out_shape = ([jax.ShapeDtypeStruct((rows, POOL_WIDTH), F32)]
                 + [jax.ShapeDtypeStruct((rows, GROUP_WIDTH), q_dtype)] * N_DIL
                 + [jax.ShapeDtypeStruct((rows, 2 * GROUP_WIDTH), F32)] * N_DIL
                 + [jax.ShapeDtypeStruct((rows, CONV_WIDTH), F32)] * 2)
    out_specs = ([row(POOL_WIDTH)] + [row(GROUP_WIDTH)] * N_DIL + [row(2 * GROUP_WIDTH)] * N_DIL
                 + [row(CONV_WIDTH)] * 2)
    return pl.pallas_call(
        _proj_kernel,
        grid=(rows // tm,),
        in_specs=[row(D_MODEL), _const_spec((1, D_MODEL)), _const_spec((D_MODEL, IN_COLS)),
                  _const_spec((1, LANES)), _const_spec((1, LANES)), tab, tab, tab,
                  _const_spec((LANES, LANES))],
        out_specs=out_specs,
        out_shape=out_shape,
        compiler_params=_params(),
        name="proj",
    )(x, g1, w_in, qg, kg, *rope, hm)


def _lane_lt(shape, bound):
    return lax.broadcasted_iota(jnp.int32, shape, len(shape) - 1) < bound


def _stack_heads(q):
    first = _lane_lt(q.shape, HEAD_DIM)
    zero = jnp.zeros_like(q)
    return jnp.concatenate([jnp.where(first, q, zero), jnp.where(first, zero, q)], axis=0)


def _split_kv(kv):
    k = kv[:, :GROUP_WIDTH].astype(BF16)
    v1 = jnp.concatenate([kv[:, GROUP_WIDTH:].astype(BF16), jnp.ones((kv.shape[0], GROUP_WIDTH), BF16)], axis=1)
    return k, v1


def _scores(q2, k):
    return lax.dot_general(q2, k, (((1,), (1,)), ((), ())), preferred_element_type=F32)


def _unstack(r, mrow):
    n = r.shape[0] // 2
    first = _lane_lt((n, GROUP_WIDTH), HEAD_DIM)
    o = jnp.where(first, r[:n, :GROUP_WIDTH], r[n:, :GROUP_WIDTH])
    l = jnp.where(first, r[:n, GROUP_WIDTH:], r[n:, GROUP_WIDTH:])
    m = jnp.where(first, mrow[:n], mrow[n:])
    return o, l, m


def _merge(a, b):
    m = jnp.maximum(a[2], b[2])
    wa = jnp.exp(a[2] - m)
    wb = jnp.exp(b[2] - m)
    return wa * a[0] + wb * b[0], wa * a[1] + wb * b[1], m


def _attn_unit(q, kv, off):
    nk = kv.shape[0]
    k, v1 = _split_kv(kv)
    s = _scores(_stack_heads(q), k)
    qi = lax.broadcasted_iota(jnp.int32, (2 * ATTN_TILE, nk), 0) & (ATTN_TILE - 1)
    ki = lax.broadcasted_iota(jnp.int32, (2 * ATTN_TILE, nk), 1)
    dist = qi - ki + off
    s = jnp.where((dist >= 0) & (dist <= N_BACK), s, NEG_INF)
    mrow = jnp.max(s, axis=-1, keepdims=True)
    p = jnp.exp(s - mrow).astype(BF16)
    return _unstack(jnp.dot(p, v1, preferred_element_type=F32), mrow)


def _attn_p_kernel(q0_ref, q1_ref, q2_ref, kv0_ref, kv1_ref, kv2_ref, y_ref,
                   acc_o, acc_l, acc_m, tmp_o, tmp_l, tmp_m, *, seq):
    q_refs = (q0_ref, q1_ref, q2_ref)
    kv_refs = (kv0_ref, kv1_ref, kv2_ref)
    for g, (_, dil) in enumerate(DIL_PAIRS):
        q_ref, kv_ref = q_refs[g], kv_refs[g]
        dst = (acc_o, acc_l, acc_m) if g == 0 else (tmp_o, tmp_l, tmp_m)
        n_sub = seq // dil // ATTN_TILE
        for r in range(dil):
            ql = slice(r * GROUP_WIDTH, (r + 1) * GROUP_WIDTH)
            kl = slice(2 * r * GROUP_WIDTH, 2 * (r + 1) * GROUP_WIDTH)

            def store(sub, res, r=r, dil=dil, dst=dst):
                start = sub * (ATTN_TILE * dil) + r
                if dil > 1:
                    rows = pl.ds(start, ATTN_TILE, stride=dil)
                else:
                    rows = pl.ds(start if isinstance(start, int) else pl.multiple_of(start, ATTN_TILE), ATTN_TILE)
                for ref, val in zip(dst, res):
                    ref[rows, :] = val

            store(0, _attn_unit(q_ref[0, :ATTN_TILE, ql], kv_ref[0, :ATTN_TILE, kl], 0))

            if n_sub > 1:
                def body(sub, carry, q_ref=q_ref, kv_ref=kv_ref, ql=ql, kl=kl, store=store):
                    q0 = pl.multiple_of(sub * ATTN_TILE, ATTN_TILE)
                    k0 = pl.multiple_of((sub - 1) * ATTN_TILE, ATTN_TILE)
                    res = _attn_unit(q_ref[0, pl.ds(q0, ATTN_TILE), ql],
                                     kv_ref[0, pl.ds(k0, 2 * ATTN_TILE), kl], ATTN_TILE)
                    store(sub, res)
                    return carry
                lax.fori_loop(1, n_sub, body, 0)

        if g > 0:
            def merge_body(c, carry):
                rows = pl.ds(pl.multiple_of(c * 256, 256), 256)
                o, l, m = _merge((acc_o[rows, :], acc_l[rows, :], acc_m[rows, :]),
                                 (tmp_o[rows, :], tmp_l[rows, :], tmp_m[rows, :]))
                acc_o[rows, :] = o
                acc_l[rows, :] = l
                acc_m[rows, :] = m
                return carry
            lax.fori_loop(0, seq // 256, merge_body, 0)

    def out_body(c, carry):
        rows = pl.ds(pl.multiple_of(c * 256, 256), 256)
        y_ref[0, rows, :] = (acc_o[rows, :] / acc_l[rows, :]).astype(y_ref.dtype)
        return carry
    lax.fori_loop(0, seq // 256, out_body, 0)


def _attn_p(qs, kvs, *, batch, seq):
    in_specs, args = [], []
    for width, arrs in ((GROUP_WIDTH, qs), (2 * GROUP_WIDTH, kvs)):
        for (_, dil), a in zip(DIL_PAIRS, arrs):
            args.append(a.reshape(batch, seq // dil, dil * width))
            in_specs.append(pl.BlockSpec((1, seq // dil, dil * width), lambda b: (b, 0, 0)))
    acc = pltpu.VMEM((seq, GROUP_WIDTH), F32)
    return pl.pallas_call(
        functools.partial(_attn_p_kernel, seq=seq),
        grid=(batch,),
        in_specs=in_specs,
        out_specs=pl.BlockSpec((1, seq, GROUP_WIDTH), lambda b: (b, 0, 0)),
        out_shape=jax.ShapeDtypeStruct((batch, seq, GROUP_WIDTH), BF16),
        scratch_shapes=[acc] * 6,
        compiler_params=_params(),
        name="attn_p",
    )(*args)


def _pool_rows(ext, pos, pw, ps):
    s2 = ext + pltpu.roll(ext, 1, 0)
    s4 = s2 + pltpu.roll(s2, 2, 0)
    s8 = s4 + pltpu.roll(s4, 4, 0)
    s16 = s8 + pltpu.roll(s8, 8, 0)
    lane = lax.broadcasted_iota(jnp.int32, (1, POOL_WIDTH), 1)
    grp = [lane < (j + 1) * POOL_GROUP for j in range(3)]
    win = jnp.where(grp[0], s2, jnp.where(grp[1], s4, jnp.where(grp[2], s8, s16)))[HIST_ROWS:]
    width = jnp.where(grp[0], POOL_WINDOWS[0], jnp.where(grp[1], POOL_WINDOWS[1],
                      jnp.where(grp[2], POOL_WINDOWS[2], POOL_WINDOWS[3])))
    cnt = jnp.minimum(pos + 1, width).astype(F32)
    d = win / cnt - ext[HIST_ROWS:]
    return jnp.dot(d.astype(BF16), pw, preferred_element_type=F32) * ps


def _conv_rows(ext, gb, cw, hist):
    y = cw[0:1] * pltpu.roll(ext, 2, 0) + cw[1:2] * pltpu.roll(ext, 1, 0) + cw[2:3] * ext
    return gb * y[hist:]


def _local_p_kernel(u_ref, gb_ref, z_ref, pw_ref, ps_ref, cw_ref, yp_ref, yc_ref, upad, zpad, *, seq, chunk):
    upad[:HIST_ROWS, :] = jnp.zeros((HIST_ROWS, POOL_WIDTH), F32)
    zpad[:HIST_ROWS, :] = jnp.zeros((HIST_ROWS, CONV_WIDTH), F32)
    upad[HIST_ROWS:, :] = u_ref[0]
    zpad[HIST_ROWS:, :] = z_ref[0]
    pw, ps, cw = pw_ref[...], ps_ref[...], cw_ref[...]

    def body(c, carry):
        r0 = pl.multiple_of(c * chunk, chunk)
        ext_rows = pl.ds(r0, chunk + HIST_ROWS)
        rows = pl.ds(r0, chunk)
        pos = r0 + lax.broadcasted_iota(jnp.int32, (chunk, 1), 0)
        yp_ref[0, rows, :] = _pool_rows(upad[ext_rows, :], pos, pw, ps).astype(yp_ref.dtype)
        yc_ref[0, rows, :] = _conv_rows(zpad[ext_rows, :], gb_ref[0, rows, :], cw, HIST_ROWS).astype(yc_ref.dtype)
        return carry
    lax.fori_loop(0, seq // chunk, body, 0)


def _local_p(u, gb, z, pw, ps, cw, *, batch, seq):
    blk = lambda c: pl.BlockSpec((1, seq, c), lambda b: (b, 0, 0))
    return pl.pallas_call(
        functools.partial(_local_p_kernel, seq=seq, chunk=256),
        grid=(batch,),
        in_specs=[blk(POOL_WIDTH), blk(CONV_WIDTH), blk(CONV_WIDTH),
                  _const_spec((POOL_WIDTH, POOL_WIDTH)), _const_spec((1, POOL_WIDTH)),
                  _const_spec((CONV_K, CONV_WIDTH))],
        out_specs=[blk(POOL_WIDTH), blk(CONV_WIDTH)],
        out_shape=[jax.ShapeDtypeStruct((batch, seq, POOL_WIDTH), BF16),
                   jax.ShapeDtypeStruct((batch, seq, CONV_WIDTH), BF16)],
        scratch_shapes=[pltpu.VMEM((seq + HIST_ROWS, POOL_WIDTH), F32),
                        pltpu.VMEM((seq + HIST_ROWS, CONV_WIDTH), F32)],
        compiler_params=_params(),
        name="local_p",
    )(u.reshape(batch, seq, POOL_WIDTH), gb.reshape(batch, seq, CONV_WIDTH), z.reshape(batch, seq, CONV_WIDTH),
      pw, ps, cw)


def _attn_s_group(q, kv_new, cache, window, dil):
    t = q.shape[0]
    q2 = _stack_heads(q).astype(BF16)
    kc, v1c = _split_kv(cache)
    kn, v1n = _split_kv(jnp.concatenate([kv_new, jnp.zeros_like(kv_new)], axis=0))
    s_c = _scores(q2, kc)
    s_n = _scores(q2, kn)
    tq = lax.broadcasted_iota(jnp.int32, (2 * t, 1), 0) & (t - 1)
    rc = lax.broadcasted_iota(jnp.int32, (1, window), 1)
    dist_c = window + tq - rc
    ok_c = (dist_c <= window) & ((dist_c & (dil - 1)) == 0) & (PAST_LEN - window + rc >= 0)
    rn = lax.broadcasted_iota(jnp.int32, (1, 2 * t), 1)
    dist_n = tq - rn
    ok_n = (dist_n >= 0) & ((dist_n & (dil - 1)) == 0) & (rn < t)
    s_c = jnp.where(ok_c, s_c, NEG_INF)
    s_n = jnp.where(ok_n, s_n, NEG_INF)
    mrow = jnp.maximum(jnp.max(s_c, axis=-1, keepdims=True), jnp.max(s_n, axis=-1, keepdims=True))
    r = (jnp.dot(jnp.exp(s_c - mrow).astype(BF16), v1c, preferred_element_type=F32)
         + jnp.dot(jnp.exp(s_n - mrow).astype(BF16), v1n, preferred_element_type=F32))
    return _unstack(r, mrow)


def _mix_s_kernel(u_ref, q0_ref, q1_ref, q2_ref, kn0_ref, kn1_ref, kn2_ref, gb_ref, z_ref, pst_ref, cst_ref,
                  c0_ref, c1_ref, c2_ref, pw_ref, ps_ref, cw_ref,
                  yp_ref, ya_ref, yc_ref, nc0_ref, nc1_ref, nc2_ref, *, t_new):
    state = None
    for (window, dil), q_ref, kn_ref, c_ref, nc_ref in zip(
            DIL_PAIRS, (q0_ref, q1_ref, q2_ref), (kn0_ref, kn1_ref, kn2_ref),
            (c0_ref, c1_ref, c2_ref), (nc0_ref, nc1_ref, nc2_ref)):
        kv_new = kn_ref[0]
        cache = c_ref[0]
        part = _attn_s_group(q_ref[0], kv_new, cache, window, dil)
        state = part if state is None else _merge(state, part)
        nc_ref[0, :window - t_new, :] = cache[t_new:]
        nc_ref[0, window - t_new:, :] = kv_new
    ya_ref[0] = state[0] / state[1]

    pos = PAST_LEN + lax.broadcasted_iota(jnp.int32, (t_new, 1), 0)
    u_ext = jnp.concatenate([pst_ref[0], u_ref[0]], axis=0)
    yp_ref[0] = _pool_rows(u_ext, pos, pw_ref[...], ps_ref[...])
    z_ext = jnp.concatenate([cst_ref[0], z_ref[0]], axis=0)
    yc_ref[0] = _conv_rows(z_ext, gb_ref[0], cw_ref[...], cst_ref.shape[1])


def _mix_s(u, qs, kns, gb, z, pst, cst, caches, pw, ps, cw, *, n_seq, t_new):
    blk = lambda r, c: pl.BlockSpec((1, r, c), lambda b: (b, 0, 0))
    three = lambda a: a.reshape(n_seq, -1, a.shape[-1])
    windows = [w for (w, _) in DIL_PAIRS]
    in_specs = ([blk(t_new, POOL_WIDTH)] + [blk(t_new, GROUP_WIDTH)] * N_DIL + [blk(t_new, 2 * GROUP_WIDTH)] * N_DIL
                + [blk(t_new, CONV_WIDTH)] * 2 + [blk(pst.shape[1], POOL_WIDTH), blk(cst.shape[1], CONV_WIDTH)]
                + [blk(w, 2 * GROUP_WIDTH) for w in windows]
                + [_const_spec((POOL_WIDTH, POOL_WIDTH)), _const_spec((1, POOL_WIDTH)),
                   _const_spec((CONV_K, CONV_WIDTH))])
    out_specs = ([blk(t_new, POOL_WIDTH), blk(t_new, GROUP_WIDTH), blk(t_new, CONV_WIDTH)]
                 + [blk(w, 2 * GROUP_WIDTH) for w in windows])
    out_shape = ([jax.ShapeDtypeStruct((n_seq, t_new, c), F32) for c in (POOL_WIDTH, GROUP_WIDTH, CONV_WIDTH)]
                 + [jax.ShapeDtypeStruct((n_seq, w, 2 * GROUP_WIDTH), F32) for w in windows])
    return pl.pallas_call(
        functools.partial(_mix_s_kernel, t_new=t_new),
        grid=(n_seq,),
        in_specs=in_specs,
        out_specs=out_specs,
        out_shape=out_shape,
        compiler_params=_params(),
        name="mix_s",
    )(three(u), *[three(q) for q in qs], *[three(k) for k in kns], three(gb), three(z), pst, cst,
      *caches, pw, ps, cw)


def _mlp_kernel(x_ref, yp_ref, ya_ref, yc_ref, wo_ref, g2_ref, wu_ref, wd_ref, o_ref, *, tf):
    mixed = jnp.concatenate([yp_ref[...].astype(BF16), ya_ref[...].astype(BF16), yc_ref[...].astype(BF16)], axis=1)
    x1 = x_ref[...] + jnp.dot(mixed, wo_ref[...], preferred_element_type=F32)
    ms = jnp.mean(x1 * x1, axis=-1, keepdims=True)
    hb = (x1 * lax.rsqrt(ms + EPS) * g2_ref[...]).astype(BF16)
    acc = x1
    for c in range(D_FF // tf):
        hf = jnp.dot(hb, wu_ref[:, c * tf:(c + 1) * tf], preferred_element_type=F32)
        act = jnp.square(jnp.maximum(hf, 0.0)).astype(BF16)
        acc = acc + jnp.dot(act, wd_ref[c * tf:(c + 1) * tf, :], preferred_element_type=F32)
    o_ref[...] = acc


def _mlp(x, yp, ya, yc, w_out, g2, w_up, w_down, *, tm):
    rows = x.shape[0]
    row = lambda c: pl.BlockSpec((tm, c), lambda i: (i, 0))
    return pl.pallas_call(
        functools.partial(_mlp_kernel, tf=512),
        grid=(rows // tm,),
        in_specs=[row(D_MODEL), row(POOL_WIDTH), row(GROUP_WIDTH), row(CONV_WIDTH),
                  _const_spec((MIX_OUT, D_MODEL)), _const_spec((1, D_MODEL)),
                  _const_spec((D_MODEL, D_FF)), _const_spec((D_FF, D_MODEL))],
        out_specs=row(D_MODEL),
        out_shape=jax.ShapeDtypeStruct((rows, D_MODEL), F32),
        compiler_params=_params(),
        name="mlp",
    )(x, yp, ya, yc, w_out, g2, w_up, w_down)


def _rope_tables(pos):
    half = ROPE_DIM // 2
    inv = jnp.power(jnp.float32(ROPE_THETA), -jnp.arange(half, dtype=F32) / half)
    ang = pos.astype(F32)[:, None] * inv[None, :]
    cos, sin = jnp.cos(ang), jnp.sin(ang)
    n = pos.shape[0]
    rest = HEAD_DIM - ROPE_DIM
    zh = jnp.zeros((n, half), F32)
    c = jnp.concatenate([cos, cos, jnp.ones((n, rest), F32)], axis=1)
    a = jnp.concatenate([-sin, zh, jnp.zeros((n, rest), F32)], axis=1)
    b = jnp.concatenate([zh, sin, jnp.zeros((n, rest), F32)], axis=1)
    return tuple(jnp.tile(t, (1, GROUP_WIDTH // HEAD_DIM)) for t in (c, a, b))


def _in_col_order():
    cols = list(range(POOL_WIDTH + ATTN_WIDTH))
    k_off = POOL_WIDTH + ATTN_WIDTH
    v_off = k_off + ATTN_WIDTH
    for g in range(N_DIL):
        cols += list(range(k_off + g * GROUP_WIDTH, k_off + (g + 1) * GROUP_WIDTH))
        cols += list(range(v_off + g * GROUP_WIDTH, v_off + (g + 1) * GROUP_WIDTH))
    cols += list(range(v_off + ATTN_WIDTH, IN_COLS))
    return jnp.asarray(cols, jnp.int32)


def kernel(x_prompt, x_sample, state_pool, state_conv, cache_kv_w128, cache_kv_w512, cache_kv_w2048,
           norm1_g, w_in, q_norm_g, k_norm_g, pool_w, pool_scale, conv_w, w_out, norm2_g, w_up, w_down):
    batch, seq, _ = x_prompt.shape
    n_seq, t_new, _ = x_sample.shape
    depth = w_in.shape[0]
    caches = (cache_kv_w128, cache_kv_w512, cache_kv_w2048)

    rope_p = _rope_tables(jnp.arange(seq, dtype=jnp.int32))
    rope_s = _rope_tables(jnp.tile(PAST_LEN + jnp.arange(t_new, dtype=jnp.int32), n_seq))
    head_id = jnp.arange(LANES) // HEAD_DIM
    hm = jnp.where(head_id[:, None] == head_id[None, :], 1.0 / HEAD_DIM, 0.0).astype(BF16)
    cols = _in_col_order()
    two_heads = lambda gain: jnp.tile(gain, GROUP_WIDTH // HEAD_DIM)[None, :]

    xp = x_prompt.reshape(batch * seq, D_MODEL)
    xs = x_sample.reshape(n_seq * t_new, D_MODEL)
    outs = {k: [] for k in ("pool_p", "conv_p", "kv_p0", "kv_p1", "kv_p2", "pool_s", "conv_s", "kv_s0", "kv_s1", "kv_s2")}
    for layer in range(depth):
        w_in_l = w_in[layer][:, cols].astype(BF16)
        w_out_l = w_out[layer].astype(BF16)
        w_up_l = w_up[layer].astype(BF16)
        w_down_l = w_down[layer].astype(BF16)
        pw = jax.scipy.linalg.block_diag(*[pool_w[layer, j] for j in range(len(POOL_WINDOWS))]).astype(BF16)
        ps = pool_scale[layer][None, :]
        cw = conv_w[layer]
        g1 = norm1_g[layer][None, :]
        g2 = norm2_g[layer][None, :]
        qg = two_heads(q_norm_g[layer]) * (HEAD_DIM ** -0.5)
        kg = two_heads(k_norm_g[layer])

        u, q0, q1, q2, kv0, kv1, kv2, gb, z = _proj(xp, g1, w_in_l, qg, kg, rope_p, hm, tm=512, q_dtype=BF16)
        ya = _attn_p((q0, q1, q2), (kv0, kv1, kv2), batch=batch, seq=seq)
        yp, yc = _local_p(u, gb, z, pw, ps, cw, batch=batch, seq=seq)
        xp = _mlp(xp, yp.reshape(-1, POOL_WIDTH), ya.reshape(-1, GROUP_WIDTH), yc.reshape(-1, CONV_WIDTH),
                  w_out_l, g2, w_up_l, w_down_l, tm=512)
        outs["pool_p"].append(u.reshape(batch, seq, POOL_WIDTH)[:, seq - POOL_HIST:])
        outs["conv_p"].append(z.reshape(batch, seq, CONV_WIDTH)[:, seq - (CONV_K - 1):])
        for g, ((window, _), kv) in enumerate(zip(DIL_PAIRS, (kv0, kv1, kv2))):
            kv = kv.reshape(batch, seq, 2, 2, HEAD_DIM)
            outs[f"kv_p{g}"].append(kv[:, seq - window:])

        us, q0, q1, q2, kn0, kn1, kn2, gbs, zs = _proj(xs, g1, w_in_l, qg, kg, rope_s, hm,
                                                       tm=n_seq * t_new, q_dtype=F32)
        pst = jnp.pad(state_pool[layer], ((0, 0), (HIST_ROWS - POOL_HIST, 0), (0, 0)))
        cst = jnp.pad(state_conv[layer], ((0, 0), (8 - (CONV_K - 1), 0), (0, 0)))
        cache_l = [c[layer].reshape(n_seq, w, 2 * GROUP_WIDTH) for c, (w, _) in zip(caches, DIL_PAIRS)]
        yps, yas, ycs, nc0, nc1, nc2 = _mix_s(us, (q0, q1, q2), (kn0, kn1, kn2), gbs, zs, pst, cst, cache_l,
                                              pw, ps, cw, n_seq=n_seq, t_new=t_new)
        xs = _mlp(xs, yps.reshape(-1, POOL_WIDTH), yas.reshape(-1, GROUP_WIDTH), ycs.reshape(-1, CONV_WIDTH),
                  w_out_l, g2, w_up_l, w_down_l, tm=n_seq * t_new)
        us3 = us.reshape(n_seq, t_new, POOL_WIDTH)
        zs3 = zs.reshape(n_seq, t_new, CONV_WIDTH)
        outs["pool_s"].append(jnp.concatenate([state_pool[layer], us3], axis=1)[:, -POOL_HIST:])
        outs["conv_s"].append(jnp.concatenate([state_conv[layer], zs3], axis=1)[:, -(CONV_K - 1):])
        for g, ((window, _), nc) in enumerate(zip(DIL_PAIRS, (nc0, nc1, nc2))):
            outs[f"kv_s{g}"].append(nc.reshape(n_seq, window, 2, 2, HEAD_DIM))

    st = lambda k: jnp.stack(outs[k])
    return (xp.reshape(batch, seq, D_MODEL), xs.reshape(n_seq, t_new, D_MODEL),
            st("pool_p"), st("conv_p"), st("kv_p0"), st("kv_p1"), st("kv_p2"),
            st("pool_s"), st("conv_s"), st("kv_s0"), st("kv_s1"), st("kv_s2"))
```

```python
import functools

import jax
import jax.numpy as jnp
from jax import lax
from jax.experimental import pallas as pl
from jax.experimental.pallas import tpu as pltpu

D_MODEL = 1024
HEAD_DIM = 64
POOL_WIDTH = 256
POOL_WINDOWS = (2, 4, 8, 16)
POOL_GROUP = 64
POOL_HIST = 15
ATTN_WIDTH = 384
DIL_PAIRS = ((128, 1), (512, 4), (2048, 16))
N_DIL = 3
GROUP_WIDTH = 128
CONV_WIDTH = 384
CONV_K = 3
ROPE_DIM = 16
ROPE_THETA = 500000.0
D_FF = 4096
IN_COLS = 2560
MIX_OUT = 768
EPS = 1e-6
NEG_INF = -1e30
PAST_LEN = 8192
N_BACK = 128

LANES = 128
HIST_ROWS = 16
ATTN_TILE = 128
VMEM_LIMIT = 56 * 1024 * 1024

F32 = jnp.float32
BF16 = jnp.bfloat16

_Q_OFF = POOL_WIDTH
_K_OFF = _Q_OFF + ATTN_WIDTH
_V_OFF = _K_OFF + ATTN_WIDTH
_GB_OFF = _V_OFF + ATTN_WIDTH
_GC_OFF = _GB_OFF + CONV_WIDTH
_GH_OFF = _GC_OFF + CONV_WIDTH

_NT = (((1,), (1,)), ((), ()))


def _const_spec(shape):
    return pl.BlockSpec(shape, lambda *_: (0,) * len(shape), pipeline_mode=pl.Buffered(1))


def _params():
    return pltpu.CompilerParams(dimension_semantics=("arbitrary",), vmem_limit_bytes=VMEM_LIMIT)


def _proj_kernel(x_ref, g1_ref, w_ref, qg_ref, kg_ref, rc_ref, ra_ref, rb_ref, hm_ref, *rest, class_major, tiles):
    if class_major:
        (u_ref, gb_ref, z_ref, qc0, qc1, qc2, kvc0, kvc1, kvc2, kvt0, kvt1, kvt2, nat) = rest
    else:
        (u_ref, gb_ref, z_ref, q0, q1, q2, kv0, kv1, kv2) = rest
    x = x_ref[...]
    tm = x.shape[0]
    ms = jnp.mean(x * x, axis=-1, keepdims=True)
    hb = (x * lax.rsqrt(ms + EPS) * g1_ref[...]).astype(BF16)
    p = jnp.dot(hb, w_ref[...], preferred_element_type=F32)

    rc, ra, rb = rc_ref[...], ra_ref[...], rb_ref[...]
    hm = hm_ref[...]

    def head_norm_rope(xb, gain):
        sq = xb * xb
        hi = sq.astype(BF16)
        lo = (sq - hi.astype(F32)).astype(BF16)
        msq = (jnp.dot(hi, hm, preferred_element_type=F32) + jnp.dot(lo, hm, preferred_element_type=F32))
        xn = xb * lax.rsqrt(msq + EPS) * gain
        return xn * rc + pltpu.roll(xn, LANES - ROPE_DIM // 2, 1) * ra + pltpu.roll(xn, ROPE_DIM // 2, 1) * rb

    u_ref[...] = p[:, :POOL_WIDTH]
    gb_ref[...] = p[:, _GB_OFF:_GC_OFF]
    z_ref[...] = p[:, _GC_OFF:_GH_OFF] * p[:, _GH_OFF:]

    last_tile = pl.program_id(0) % tiles == tiles - 1
    for g, (window, dil) in enumerate(DIL_PAIRS):
        lanes = slice(g * GROUP_WIDTH, (g + 1) * GROUP_WIDTH)
        qn = head_norm_rope(p[:, _Q_OFF:_K_OFF][:, lanes], qg_ref[...])
        kn = head_norm_rope(p[:, _K_OFF:_V_OFF][:, lanes], kg_ref[...])
        vv = p[:, _V_OFF:_GB_OFF][:, lanes]
        if not class_major:
            q_ref, kv_ref = ((q0, kv0), (q1, kv1), (q2, kv2))[g]
            q_ref[...] = qn
            kv_ref[:, :GROUP_WIDTH] = kn
            kv_ref[:, GROUP_WIDTH:] = vv
            continue
        qc_ref, kvc_ref, kvt_ref = ((qc0, kvc0, kvt0), (qc1, kvc1, kvt1), (qc2, kvc2, kvt2))[g]
        nat[0] = qn
        nat[1] = kn
        nat[2] = vv
        per = tm // dil
        for r in range(dil):
            rows = pl.ds(r, per, stride=dil) if dil > 1 else slice(None)
            qc_ref[0, r] = nat[0, rows, :].astype(BF16)
            kvc_ref[0, r, :, :GROUP_WIDTH] = nat[1, rows, :].astype(BF16)
            kvc_ref[0, r, :, GROUP_WIDTH:] = nat[2, rows, :].astype(BF16)
        keep = min(window, tm)
        if window >= tiles * tm:
            kvt_ref[0] = jnp.concatenate([kn, vv], axis=1).T
        else:
            @pl.when(last_tile)
            def _(kn=kn, vv=vv, kvt_ref=kvt_ref, keep=keep):
                kvt_ref[0] = jnp.concatenate([kn[tm - keep:], vv[tm - keep:]], axis=1).T


def _proj(x, g1, w_in, qg, kg, rope, hm, *, tm, batch=None, seq=None):
    rows = x.shape[0]
    n_tab = rope[0].shape[0] // tm
    class_major = batch is not None
    tiles = seq // tm if class_major else 1
    row = lambda c: pl.BlockSpec((tm, c), lambda i: (i, 0))
    tab = pl.BlockSpec((tm, LANES), lambda i: (i % n_tab, 0))
    out_shape = [jax.ShapeDtypeStruct((rows, POOL_WIDTH), F32)] + [jax.ShapeDtypeStruct((rows, CONV_WIDTH), F32)] * 2
    out_specs = [row(POOL_WIDTH), row(CONV_WIDTH), row(CONV_WIDTH)]
    scratch = []
    if class_major:
        for width in (GROUP_WIDTH, 2 * GROUP_WIDTH):
            for (_, dil) in DIL_PAIRS:
                out_shape.append(jax.ShapeDtypeStruct((batch, dil, seq // dil, width), BF16))
                out_specs.append(pl.BlockSpec((1, dil, tm // dil, width), lambda i: (i // tiles, 0, i % tiles, 0)))
        for (window, _) in DIL_PAIRS:
            keep = min(window, tm)
            out_shape.append(jax.ShapeDtypeStruct((batch, 2 * GROUP_WIDTH, window), F32))
            if window >= seq:
                out_specs.append(pl.BlockSpec((1, 2 * GROUP_WIDTH, tm), lambda i: (i // tiles, 0, i % tiles)))
            else:
                assert window == keep, "kv buffer rows must come from the last row tile"
                out_specs.append(pl.BlockSpec((1, 2 * GROUP_WIDTH, window), lambda i: (i // tiles, 0, 0)))
        scratch = [pltpu.VMEM((3, tm, GROUP_WIDTH), F32)]
    else:
        out_shape += [jax.ShapeDtypeStruct((rows, GROUP_WIDTH), F32)] * N_DIL
        out_shape += [jax.ShapeDtypeStruct((rows, 2 * GROUP_WIDTH), F32)] * N_DIL
        out_specs += [row(GROUP_WIDTH)] * N_DIL + [row(2 * GROUP_WIDTH)] * N_DIL
    return pl.pallas_call(
        functools.partial(_proj_kernel, class_major=class_major, tiles=tiles),
        grid=(rows // tm,),
        in_specs=[row(D_MODEL), _const_spec((1, D_MODEL)), _const_spec((D_MODEL, IN_COLS)),
                  _const_spec((1, LANES)), _const_spec((1, LANES)), tab, tab, tab,
                  _const_spec((LANES, LANES))],
        out_specs=out_specs,
        out_shape=out_shape,
        scratch_shapes=scratch,
        compiler_params=_params(),
        name="proj",
    )(x, g1, w_in, qg, kg, *rope, hm)


def _lane_lt(shape, bound):
    return lax.broadcasted_iota(jnp.int32, shape, len(shape) - 1) < bound


def _stack_heads(q):
    first = _lane_lt(q.shape, HEAD_DIM)
    zero = jnp.zeros_like(q)
    return jnp.concatenate([jnp.where(first, q, zero), jnp.where(first, zero, q)], axis=0)


def _unstack(o2, l2, m2):
    n = o2.shape[0] // 2
    first = _lane_lt((n, GROUP_WIDTH), HEAD_DIM)
    return tuple(jnp.where(first, a[:n], a[n:]) for a in (o2, l2, m2))


def _merge(a, b):
    m = jnp.maximum(a[2], b[2])
    wa = jnp.exp(a[2] - m)
    wb = jnp.exp(b[2] - m)
    return wa * a[0] + wb * b[0], wa * a[1] + wb * b[1], m


def _attn_unit(q, kv, off):
    nk = kv.shape[0]
    k = kv[:, :GROUP_WIDTH]
    v1 = jnp.concatenate([kv[:, GROUP_WIDTH:], jnp.ones((nk, GROUP_WIDTH), BF16)], axis=1)
    s = lax.dot_general(_stack_heads(q), k, _NT, preferred_element_type=F32)
    qi = lax.broadcasted_iota(jnp.int32, (2 * ATTN_TILE, nk), 0) & (ATTN_TILE - 1)
    ki = lax.broadcasted_iota(jnp.int32, (2 * ATTN_TILE, nk), 1)
    dist = qi - ki + off
    s = jnp.where((dist >= 0) & (dist <= N_BACK), s, NEG_INF)
    mrow = jnp.max(s, axis=-1, keepdims=True)
    p = jnp.exp(s - mrow).astype(BF16)
    r = jnp.dot(p, v1, preferred_element_type=F32)
    return _unstack(r[:, :GROUP_WIDTH], r[:, GROUP_WIDTH:], mrow)


def _attn_p_kernel(qc0, qc1, qc2, kvc0, kvc1, kvc2, y_ref, acc_o, acc_l, acc_m, tmp_o, tmp_l, tmp_m, *, seq):
    for g, ((_, dil), q_ref, kv_ref) in enumerate(zip(DIL_PAIRS, (qc0, qc1, qc2), (kvc0, kvc1, kvc2))):
        dst = (acc_o, acc_l, acc_m) if g == 0 else (tmp_o, tmp_l, tmp_m)
        n_sub = seq // dil // ATTN_TILE

        def store(r, sub, res, dil=dil, dst=dst):
            start = sub * (ATTN_TILE * dil) + r
            rows = pl.ds(start, ATTN_TILE, stride=dil) if dil > 1 else pl.ds(pl.multiple_of(start, ATTN_TILE), ATTN_TILE)
            for ref, val in zip(dst, res):
                ref[rows, :] = val

        def class_body(r, carry, q_ref=q_ref, kv_ref=kv_ref, n_sub=n_sub, store=store):
            store(r, 0, _attn_unit(q_ref[0, r, :ATTN_TILE, :], kv_ref[0, r, :ATTN_TILE, :], 0))

            def sub_body(sub, c):
                q0 = pl.multiple_of(sub * ATTN_TILE, ATTN_TILE)
                k0 = pl.multiple_of((sub - 1) * ATTN_TILE, ATTN_TILE)
                store(r, sub, _attn_unit(q_ref[0, r, pl.ds(q0, ATTN_TILE), :],
                                         kv_ref[0, r, pl.ds(k0, 2 * ATTN_TILE), :], ATTN_TILE))
                return c
            if n_sub > 1:
                lax.fori_loop(1, n_sub, sub_body, 0)
            return carry
        lax.fori_loop(0, dil, class_body, 0)

        if g > 0:
            def merge_body(c, carry):
                rows = pl.ds(pl.multiple_of(c * 256, 256), 256)
                o, l, m = _merge((acc_o[rows, :], acc_l[rows, :], acc_m[rows, :]),
                                 (tmp_o[rows, :], tmp_l[rows, :], tmp_m[rows, :]))
                acc_o[rows, :] = o
                acc_l[rows, :] = l
                acc_m[rows, :] = m
                return carry
            lax.fori_loop(0, seq // 256, merge_body, 0)

    def out_body(c, carry):
        rows = pl.ds(pl.multiple_of(c * 256, 256), 256)
        y_ref[0, rows, :] = (acc_o[rows, :] / acc_l[rows, :]).astype(y_ref.dtype)
        return carry
    lax.fori_loop(0, seq // 256, out_body, 0)


def _attn_p(qcs, kvcs, *, batch, seq):
    in_specs = [pl.BlockSpec((1,) + a.shape[1:], lambda b: (b, 0, 0, 0)) for a in (*qcs, *kvcs)]
    acc = pltpu.VMEM((seq, GROUP_WIDTH), F32)
    return pl.pallas_call(
        functools.partial(_attn_p_kernel, seq=seq),
        grid=(batch,),
        in_specs=in_specs,
        out_specs=pl.BlockSpec((1, seq, GROUP_WIDTH), lambda b: (b, 0, 0)),
        out_shape=jax.ShapeDtypeStruct((batch, seq, GROUP_WIDTH), BF16),
        scratch_shapes=[acc] * 6,
        compiler_params=_params(),
        name="attn_p",
    )(*qcs, *kvcs)


def _pool_rows(ext, pos, pw, ps):
    s2 = ext + pltpu.roll(ext, 1, 0)
    s4 = s2 + pltpu.roll(s2, 2, 0)
    s8 = s4 + pltpu.roll(s4, 4, 0)
    s16 = s8 + pltpu.roll(s8, 8, 0)
    lane = lax.broadcasted_iota(jnp.int32, (1, POOL_WIDTH), 1)
    grp = [lane < (j + 1) * POOL_GROUP for j in range(3)]
    win = jnp.where(grp[0], s2, jnp.where(grp[1], s4, jnp.where(grp[2], s8, s16)))[HIST_ROWS:]
    width = jnp.where(grp[0], POOL_WINDOWS[0], jnp.where(grp[1], POOL_WINDOWS[1],
                      jnp.where(grp[2], POOL_WINDOWS[2], POOL_WINDOWS[3])))
    cnt = jnp.minimum(pos + 1, width).astype(F32)
    d = win / cnt - ext[HIST_ROWS:]
    return jnp.dot(d.astype(BF16), pw, preferred_element_type=F32) * ps


def _conv_rows(ext, gb, cw, hist):
    y = cw[0:1] * pltpu.roll(ext, 2, 0) + cw[1:2] * pltpu.roll(ext, 1, 0) + cw[2:3] * ext
    return gb * y[hist:]


def _local_p_kernel(u_ref, gb_ref, z_ref, pw_ref, ps_ref, cw_ref, yp_ref, yc_ref, upad, zpad, *, seq, chunk):
    upad[:HIST_ROWS, :] = jnp.zeros((HIST_ROWS, POOL_WIDTH), F32)
    zpad[:HIST_ROWS, :] = jnp.zeros((HIST_ROWS, CONV_WIDTH), F32)
    upad[HIST_ROWS:, :] = u_ref[0]
    zpad[HIST_ROWS:, :] = z_ref[0]
    pw, ps, cw = pw_ref[...], ps_ref[...], cw_ref[...]

    def body(c, carry):
        r0 = pl.multiple_of(c * chunk, chunk)
        ext_rows = pl.ds(r0, chunk + HIST_ROWS)
        rows = pl.ds(r0, chunk)
        pos = r0 + lax.broadcasted_iota(jnp.int32, (chunk, 1), 0)
        yp_ref[0, rows, :] = _pool_rows(upad[ext_rows, :], pos, pw, ps).astype(yp_ref.dtype)
        yc_ref[0, rows, :] = _conv_rows(zpad[ext_rows, :], gb_ref[0, rows, :], cw, HIST_ROWS).astype(yc_ref.dtype)
        return carry
    lax.fori_loop(0, seq // chunk, body, 0)


def _local_p(u, gb, z, pw, ps, cw, *, batch, seq):
    blk = lambda c: pl.BlockSpec((1, seq, c), lambda b: (b, 0, 0))
    return pl.pallas_call(
        functools.partial(_local_p_kernel, seq=seq, chunk=256),
        grid=(batch,),
        in_specs=[blk(POOL_WIDTH), blk(CONV_WIDTH), blk(CONV_WIDTH),
                  _const_spec((POOL_WIDTH, POOL_WIDTH)), _const_spec((1, POOL_WIDTH)),
                  _const_spec((CONV_K, CONV_WIDTH))],
        out_specs=[blk(POOL_WIDTH), blk(CONV_WIDTH)],
        out_shape=[jax.ShapeDtypeStruct((batch, seq, POOL_WIDTH), BF16),
                   jax.ShapeDtypeStruct((batch, seq, CONV_WIDTH), BF16)],
        scratch_shapes=[pltpu.VMEM((seq + HIST_ROWS, POOL_WIDTH), F32),
                        pltpu.VMEM((seq + HIST_ROWS, CONV_WIDTH), F32)],
        compiler_params=_params(),
        name="local_p",
    )(u.reshape(batch, seq, POOL_WIDTH), gb.reshape(batch, seq, CONV_WIDTH), z.reshape(batch, seq, CONV_WIDTH),
      pw, ps, cw)


def _attn_s_group(q, new_t, cache_t, window, dil):
    t = q.shape[0]
    q2 = _stack_heads(q).astype(BF16)
    s_c = jnp.dot(q2, cache_t[:GROUP_WIDTH].astype(BF16), preferred_element_type=F32)
    s_n = jnp.dot(q2, new_t[:GROUP_WIDTH].astype(BF16), preferred_element_type=F32)
    tq = lax.broadcasted_iota(jnp.int32, (2 * t, 1), 0) & (t - 1)
    rc = lax.broadcasted_iota(jnp.int32, (1, window), 1)
    dist_c = window + tq - rc
    ok_c = (dist_c <= window) & ((dist_c & (dil - 1)) == 0) & (PAST_LEN - window + rc >= 0)
    tn = lax.broadcasted_iota(jnp.int32, (1, LANES), 1) - (LANES - t)
    dist_n = tq - tn
    ok_n = (tn >= 0) & (dist_n >= 0) & ((dist_n & (dil - 1)) == 0)
    s_c = jnp.where(ok_c, s_c, NEG_INF)
    s_n = jnp.where(ok_n, s_n, NEG_INF)
    mrow = jnp.maximum(jnp.max(s_c, axis=-1, keepdims=True), jnp.max(s_n, axis=-1, keepdims=True))
    p_c = jnp.exp(s_c - mrow)
    p_n = jnp.exp(s_n - mrow)
    o2 = (lax.dot_general(p_c.astype(BF16), cache_t[GROUP_WIDTH:].astype(BF16), _NT, preferred_element_type=F32)
          + lax.dot_general(p_n.astype(BF16), new_t[GROUP_WIDTH:].astype(BF16), _NT, preferred_element_type=F32))
    l2 = jnp.sum(p_c, axis=-1, keepdims=True) + jnp.sum(p_n, axis=-1, keepdims=True)
    return _unstack(o2, l2, mrow)


def _mix_s_kernel(u_ref, q0_ref, q1_ref, q2_ref, kn0_ref, kn1_ref, kn2_ref, gb_ref, z_ref, pst_ref, cst_ref,
                  c0_ref, c1_ref, c2_ref, pw_ref, ps_ref, cw_ref, *rest, t_new):
    yp_ref, ya_ref, yc_ref, nc0_ref, nc1_ref, nc2_ref = rest[-6:]
    state = None
    for (window, dil), q_ref, kn_ref, c_ref, nc_ref in zip(
            DIL_PAIRS, (q0_ref, q1_ref, q2_ref), (kn0_ref, kn1_ref, kn2_ref),
            (c0_ref, c1_ref, c2_ref), (nc0_ref, nc1_ref, nc2_ref)):
        kv_new = kn_ref[0]
        new_t = jnp.concatenate([jnp.zeros((LANES - t_new, 2 * GROUP_WIDTH), F32), kv_new], axis=0).T
        cache_t = c_ref[0, 0]
        part = _attn_s_group(q_ref[0], new_t, cache_t, window, dil)
        state = part if state is None else _merge(state, part)
        rolled = pltpu.roll(cache_t, window - t_new, 1)
        if window > LANES:
            nc_ref[0, 0, :, :window - LANES] = rolled[:, :window - LANES]
        keep_old = _lane_lt((2 * GROUP_WIDTH, LANES), LANES - t_new)
        nc_ref[0, 0, :, window - LANES:] = jnp.where(keep_old, rolled[:, window - LANES:], new_t)
    ya_ref[0] = state[0] / state[1]

    pos = PAST_LEN + lax.broadcasted_iota(jnp.int32, (t_new, 1), 0)
    u_ext = jnp.concatenate([pst_ref[0], u_ref[0]], axis=0)
    yp_ref[0] = _pool_rows(u_ext, pos, pw_ref[...], ps_ref[...])
    z_ext = jnp.concatenate([cst_ref[0], z_ref[0]], axis=0)
    yc_ref[0] = _conv_rows(z_ext, gb_ref[0], cw_ref[...], cst_ref.shape[1])


def _mix_s(u, qs, kns, gb, z, pst, cst, caches_t, pw, ps, cw, prev, *, layer, n_seq, t_new):
    blk = lambda r, c: pl.BlockSpec((1, r, c), lambda b: (b, 0, 0))
    cache_blk = lambda w: pl.BlockSpec((1, 1, 2 * GROUP_WIDTH, w), lambda b: (layer, b, 0, 0))
    three = lambda a: a.reshape(n_seq, -1, a.shape[-1])
    windows = [w for (w, _) in DIL_PAIRS]
    in_specs = ([blk(t_new, POOL_WIDTH)] + [blk(t_new, GROUP_WIDTH)] * N_DIL + [blk(t_new, 2 * GROUP_WIDTH)] * N_DIL
                + [blk(t_new, CONV_WIDTH)] * 2 + [blk(pst.shape[1], POOL_WIDTH), blk(cst.shape[1], CONV_WIDTH)]
                + [cache_blk(w) for w in windows]
                + [_const_spec((POOL_WIDTH, POOL_WIDTH)), _const_spec((1, POOL_WIDTH)),
                   _const_spec((CONV_K, CONV_WIDTH))])
    args = [three(u), *[three(q) for q in qs], *[three(k) for k in kns], three(gb), three(z), pst, cst,
            *caches_t, pw, ps, cw]
    aliases = {}
    if prev is not None:
        for k, p in enumerate(prev):
            aliases[len(args)] = 3 + k
            in_specs.append(pl.BlockSpec(memory_space=pl.ANY))
            args.append(p)
    out_specs = ([blk(t_new, POOL_WIDTH), blk(t_new, GROUP_WIDTH), blk(t_new, CONV_WIDTH)]
                 + [cache_blk(w) for w in windows])
    out_shape = ([jax.ShapeDtypeStruct((n_seq, t_new, c), F32) for c in (POOL_WIDTH, GROUP_WIDTH, CONV_WIDTH)]
                 + [jax.ShapeDtypeStruct(c.shape, F32) for c in caches_t])
    return pl.pallas_call(
        functools.partial(_mix_s_kernel, t_new=t_new),
        grid=(n_seq,),
        in_specs=in_specs,
        out_specs=out_specs,
        out_shape=out_shape,
        input_output_aliases=aliases,
        compiler_params=_params(),
        name="mix_s",
    )(*args)


def _mlp_kernel(x_ref, yp_ref, ya_ref, yc_ref, wo_ref, g2_ref, wu_ref, wd_ref, o_ref, *, tf):
    mixed = jnp.concatenate([yp_ref[...].astype(BF16), ya_ref[...].astype(BF16), yc_ref[...].astype(BF16)], axis=1)
    x1 = x_ref[...] + jnp.dot(mixed, wo_ref[...], preferred_element_type=F32)
    ms = jnp.mean(x1 * x1, axis=-1, keepdims=True)
    hb = (x1 * lax.rsqrt(ms + EPS) * g2_ref[...]).astype(BF16)
    acc = x1
    for c in range(D_FF // tf):
        hf = jnp.dot(hb, wu_ref[:, c * tf:(c + 1) * tf], preferred_element_type=F32)
        act = jnp.square(jnp.maximum(hf, 0.0)).astype(BF16)
        acc = acc + jnp.dot(act, wd_ref[c * tf:(c + 1) * tf, :], preferred_element_type=F32)
    o_ref[...] = acc


def _mlp(x, yp, ya, yc, w_out, g2, w_up, w_down, *, tm):
    rows = x.shape[0]
    row = lambda c: pl.BlockSpec((tm, c), lambda i: (i, 0))
    return pl.pallas_call(
        functools.partial(_mlp_kernel, tf=512),
        grid=(rows // tm,),
        in_specs=[row(D_MODEL), row(POOL_WIDTH), row(GROUP_WIDTH), row(CONV_WIDTH),
                  _const_spec((MIX_OUT, D_MODEL)), _const_spec((1, D_MODEL)),
                  _const_spec((D_MODEL, D_FF)), _const_spec((D_FF, D_MODEL))],
        out_specs=row(D_MODEL),
        out_shape=jax.ShapeDtypeStruct((rows, D_MODEL), F32),
        compiler_params=_params(),
        name="mlp",
    )(x, yp, ya, yc, w_out, g2, w_up, w_down)


def _rope_tables(pos):
    half = ROPE_DIM // 2
    inv = jnp.power(jnp.float32(ROPE_THETA), -jnp.arange(half, dtype=F32) / half)
    ang = pos.astype(F32)[:, None] * inv[None, :]
    cos, sin = jnp.cos(ang), jnp.sin(ang)
    n = pos.shape[0]
    rest = HEAD_DIM - ROPE_DIM
    zh = jnp.zeros((n, half), F32)
    c = jnp.concatenate([cos, cos, jnp.ones((n, rest), F32)], axis=1)
    a = jnp.concatenate([-sin, zh, jnp.zeros((n, rest), F32)], axis=1)
    b = jnp.concatenate([zh, sin, jnp.zeros((n, rest), F32)], axis=1)
    return tuple(jnp.tile(t, (1, GROUP_WIDTH // HEAD_DIM)) for t in (c, a, b))


def _to_buffer_layout(c):
    lead = c.shape[:-4]
    n = len(lead)
    t = jnp.transpose(c, (*range(n), n + 1, n + 2, n + 3, n))
    return t.reshape(*lead, 2 * GROUP_WIDTH, c.shape[-4])


def _from_buffer_layout(t):
    lead = t.shape[:-2]
    n = len(lead)
    c = t.reshape(*lead, 2, 2, HEAD_DIM, t.shape[-1])
    return jnp.transpose(c, (*range(n), n + 3, n, n + 1, n + 2))


def kernel(x_prompt, x_sample, state_pool, state_conv, cache_kv_w128, cache_kv_w512, cache_kv_w2048,
           norm1_g, w_in, q_norm_g, k_norm_g, pool_w, pool_scale, conv_w, w_out, norm2_g, w_up, w_down):
    batch, seq, _ = x_prompt.shape
    n_seq, t_new, _ = x_sample.shape
    depth = w_in.shape[0]
    caches_t = [_to_buffer_layout(c) for c in (cache_kv_w128, cache_kv_w512, cache_kv_w2048)]

    rope_p = _rope_tables(jnp.arange(seq, dtype=jnp.int32))
    rope_s = _rope_tables(jnp.tile(PAST_LEN + jnp.arange(t_new, dtype=jnp.int32), n_seq))
    head_id = jnp.arange(LANES) // HEAD_DIM
    hm = jnp.where(head_id[:, None] == head_id[None, :], 1.0 / HEAD_DIM, 0.0).astype(BF16)
    two_heads = lambda gain: jnp.tile(gain, GROUP_WIDTH // HEAD_DIM)[None, :]

    xp = x_prompt.reshape(batch * seq, D_MODEL)
    xs = x_sample.reshape(n_seq * t_new, D_MODEL)
    outs = {k: [] for k in ("pool_p", "conv_p", "kv_p0", "kv_p1", "kv_p2", "pool_s", "conv_s")}
    new_caches = None
    for layer in range(depth):
        w_in_l = w_in[layer].astype(BF16)
        w_out_l = w_out[layer].astype(BF16)
        w_up_l = w_up[layer].astype(BF16)
        w_down_l = w_down[layer].astype(BF16)
        pw = jax.scipy.linalg.block_diag(*[pool_w[layer, j] for j in range(len(POOL_WINDOWS))]).astype(BF16)
        ps = pool_scale[layer][None, :]
        cw = conv_w[layer]
        g1 = norm1_g[layer][None, :]
        g2 = norm2_g[layer][None, :]
        qg = two_heads(q_norm_g[layer]) * (HEAD_DIM ** -0.5)
        kg = two_heads(k_norm_g[layer])

        (u, gb, z, qc0, qc1, qc2, kvc0, kvc1, kvc2, kvt0, kvt1, kvt2) = _proj(
            xp, g1, w_in_l, qg, kg, rope_p, hm, tm=512, batch=batch, seq=seq)
        ya = _attn_p((qc0, qc1, qc2), (kvc0, kvc1, kvc2), batch=batch, seq=seq)
        yp, yc = _local_p(u, gb, z, pw, ps, cw, batch=batch, seq=seq)
        xp = _mlp(xp, yp.reshape(-1, POOL_WIDTH), ya.reshape(-1, GROUP_WIDTH), yc.reshape(-1, CONV_WIDTH),
                  w_out_l, g2, w_up_l, w_down_l, tm=512)
        outs["pool_p"].append(u.reshape(batch, seq, POOL_WIDTH)[:, seq - POOL_HIST:])
        outs["conv_p"].append(z.reshape(batch, seq, CONV_WIDTH)[:, seq - (CONV_K - 1):])
        for g, kvt in enumerate((kvt0, kvt1, kvt2)):
            outs[f"kv_p{g}"].append(kvt)

        us, gbs, zs, q0, q1, q2, kn0, kn1, kn2 = _proj(xs, g1, w_in_l, qg, kg, rope_s, hm, tm=n_seq * t_new)
        pst = jnp.pad(state_pool[layer], ((0, 0), (HIST_ROWS - POOL_HIST, 0), (0, 0)))
        cst = jnp.pad(state_conv[layer], ((0, 0), (8 - (CONV_K - 1), 0), (0, 0)))
        yps, yas, ycs, *new_caches = _mix_s(us, (q0, q1, q2), (kn0, kn1, kn2), gbs, zs, pst, cst, caches_t,
                                           pw, ps, cw, new_caches, layer=layer, n_seq=n_seq, t_new=t_new)
        xs = _mlp(xs, yps.reshape(-1, POOL_WIDTH), yas.reshape(-1, GROUP_WIDTH), ycs.reshape(-1, CONV_WIDTH),
                  w_out_l, g2, w_up_l, w_down_l, tm=n_seq * t_new)
        us3 = us.reshape(n_seq, t_new, POOL_WIDTH)
        zs3 = zs.reshape(n_seq, t_new, CONV_WIDTH)
        outs["pool_s"].append(jnp.concatenate([state_pool[layer], us3], axis=1)[:, -POOL_HIST:])
        outs["conv_s"].append(jnp.concatenate([state_conv[layer], zs3], axis=1)[:, -(CONV_K - 1):])

    st = lambda k: jnp.stack(outs[k])
    return (xp.reshape(batch, seq, D_MODEL), xs.reshape(n_seq, t_new, D_MODEL),
            st("pool_p"), st("conv_p"),
            *[_from_buffer_layout(st(f"kv_p{g}")) for g in range(N_DIL)],
            st("pool_s"), st("conv_s"),
            *[_from_buffer_layout(c) for c in new_caches])
```

```python
import functools

import jax
import jax.numpy as jnp
from jax import lax
from jax.experimental import pallas as pl
from jax.experimental.pallas import tpu as pltpu

D_MODEL = 1024
HEAD_DIM = 64
POOL_WIDTH = 256
POOL_WINDOWS = (2, 4, 8, 16)
POOL_GROUP = 64
POOL_HIST = 15
ATTN_WIDTH = 384
DIL_PAIRS = ((128, 1), (512, 4), (2048, 16))
N_DIL = 3
GROUP_WIDTH = 128
CONV_WIDTH = 384
CONV_K = 3
ROPE_DIM = 16
ROPE_THETA = 500000.0
D_FF = 4096
IN_COLS = 2560
MIX_OUT = 768
EPS = 1e-6
NEG_INF = -1e30
PAST_LEN = 8192
N_BACK = 128

LANES = 128
HIST_ROWS = 16
ATTN_TILE = 128
ATTN_UNROLL = 16
VMEM_LIMIT = 56 * 1024 * 1024

F32 = jnp.float32
BF16 = jnp.bfloat16

_Q_OFF = POOL_WIDTH
_K_OFF = _Q_OFF + ATTN_WIDTH
_V_OFF = _K_OFF + ATTN_WIDTH
_GB_OFF = _V_OFF + ATTN_WIDTH
_GC_OFF = _GB_OFF + CONV_WIDTH
_GH_OFF = _GC_OFF + CONV_WIDTH

_NT = (((1,), (1,)), ((), ()))


def _const_spec(shape):
    return pl.BlockSpec(shape, lambda *_: (0,) * len(shape), pipeline_mode=pl.Buffered(1))


def _params():
    return pltpu.CompilerParams(dimension_semantics=("arbitrary",), vmem_limit_bytes=VMEM_LIMIT)


def _proj_kernel(x_ref, g1_ref, w_ref, qg_ref, kg_ref, rc_ref, ra_ref, rb_ref, hm_ref, *rest, class_major, tiles):
    if class_major:
        (u_ref, gb_ref, z_ref, qc0, qc1, qc2, kvc0, kvc1, kvc2, kvt0, kvt1, kvt2, nat) = rest
    else:
        (u_ref, gb_ref, z_ref, q0, q1, q2, kv0, kv1, kv2) = rest
    x = x_ref[...]
    tm = x.shape[0]
    ms = jnp.mean(x * x, axis=-1, keepdims=True)
    hb = (x * lax.rsqrt(ms + EPS) * g1_ref[...]).astype(BF16)
    p = jnp.dot(hb, w_ref[...], preferred_element_type=F32)

    rc, ra, rb = rc_ref[...], ra_ref[...], rb_ref[...]
    hm = hm_ref[...]

    def head_norm_rope(xb, gain):
        sq = xb * xb
        hi = sq.astype(BF16)
        lo = (sq - hi.astype(F32)).astype(BF16)
        msq = (jnp.dot(hi, hm, preferred_element_type=F32) + jnp.dot(lo, hm, preferred_element_type=F32))
        xn = xb * lax.rsqrt(msq + EPS) * gain
        return xn * rc + pltpu.roll(xn, LANES - ROPE_DIM // 2, 1) * ra + pltpu.roll(xn, ROPE_DIM // 2, 1) * rb

    u_ref[...] = p[:, :POOL_WIDTH]
    gb_ref[...] = p[:, _GB_OFF:_GC_OFF]
    z_ref[...] = p[:, _GC_OFF:_GH_OFF] * p[:, _GH_OFF:]

    last_tile = pl.program_id(0) % tiles == tiles - 1
    for g, (window, dil) in enumerate(DIL_PAIRS):
        lanes = slice(g * GROUP_WIDTH, (g + 1) * GROUP_WIDTH)
        qn = head_norm_rope(p[:, _Q_OFF:_K_OFF][:, lanes], qg_ref[...])
        kn = head_norm_rope(p[:, _K_OFF:_V_OFF][:, lanes], kg_ref[...])
        vv = p[:, _V_OFF:_GB_OFF][:, lanes]
        if not class_major:
            q_ref, kv_ref = ((q0, kv0), (q1, kv1), (q2, kv2))[g]
            q_ref[...] = qn
            kv_ref[:, :GROUP_WIDTH] = kn
            kv_ref[:, GROUP_WIDTH:] = vv
            continue
        qc_ref, kvc_ref, kvt_ref = ((qc0, kvc0, kvt0), (qc1, kvc1, kvt1), (qc2, kvc2, kvt2))[g]
        nat[0] = qn
        nat[1] = kn
        nat[2] = vv
        per = tm // dil
        for r in range(dil):
            rows = pl.ds(r, per, stride=dil) if dil > 1 else slice(None)
            qc_ref[0, r] = nat[0, rows, :].astype(BF16)
            kvc_ref[0, r, :, :GROUP_WIDTH] = nat[1, rows, :].astype(BF16)
            kvc_ref[0, r, :, GROUP_WIDTH:] = nat[2, rows, :].astype(BF16)
        keep = min(window, tm)
        if window >= tiles * tm:
            kvt_ref[0] = jnp.concatenate([kn, vv], axis=1).T
        else:
            @pl.when(last_tile)
            def _(kn=kn, vv=vv, kvt_ref=kvt_ref, keep=keep):
                kvt_ref[0] = jnp.concatenate([kn[tm - keep:], vv[tm - keep:]], axis=1).T


def _proj(x, g1, w_in, qg, kg, rope, hm, *, tm, batch=None, seq=None):
    rows = x.shape[0]
    n_tab = rope[0].shape[0] // tm
    class_major = batch is not None
    tiles = seq // tm if class_major else 1
    row = lambda c: pl.BlockSpec((tm, c), lambda i: (i, 0))
    tab = pl.BlockSpec((tm, LANES), lambda i: (i % n_tab, 0))
    out_shape = [jax.ShapeDtypeStruct((rows, POOL_WIDTH), F32)] + [jax.ShapeDtypeStruct((rows, CONV_WIDTH), F32)] * 2
    out_specs = [row(POOL_WIDTH), row(CONV_WIDTH), row(CONV_WIDTH)]
    scratch = []
    if class_major:
        for width in (GROUP_WIDTH, 2 * GROUP_WIDTH):
            for (_, dil) in DIL_PAIRS:
                out_shape.append(jax.ShapeDtypeStruct((batch, dil, seq // dil, width), BF16))
                out_specs.append(pl.BlockSpec((1, dil, tm // dil, width), lambda i: (i // tiles, 0, i % tiles, 0)))
        for (window, _) in DIL_PAIRS:
            keep = min(window, tm)
            out_shape.append(jax.ShapeDtypeStruct((batch, 2 * GROUP_WIDTH, window), F32))
            if window >= seq:
                out_specs.append(pl.BlockSpec((1, 2 * GROUP_WIDTH, tm), lambda i: (i // tiles, 0, i % tiles)))
            else:
                assert window == keep, "kv buffer rows must come from the last row tile"
                out_specs.append(pl.BlockSpec((1, 2 * GROUP_WIDTH, window), lambda i: (i // tiles, 0, 0)))
        scratch = [pltpu.VMEM((3, tm, GROUP_WIDTH), F32)]
    else:
        out_shape += [jax.ShapeDtypeStruct((rows, GROUP_WIDTH), F32)] * N_DIL
        out_shape += [jax.ShapeDtypeStruct((rows, 2 * GROUP_WIDTH), F32)] * N_DIL
        out_specs += [row(GROUP_WIDTH)] * N_DIL + [row(2 * GROUP_WIDTH)] * N_DIL
    return pl.pallas_call(
        functools.partial(_proj_kernel, class_major=class_major, tiles=tiles),
        grid=(rows // tm,),
        in_specs=[row(D_MODEL), _const_spec((1, D_MODEL)), _const_spec((D_MODEL, IN_COLS)),
                  _const_spec((1, LANES)), _const_spec((1, LANES)), tab, tab, tab,
                  _const_spec((LANES, LANES))],
        out_specs=out_specs,
        out_shape=out_shape,
        scratch_shapes=scratch,
        compiler_params=_params(),
        name="proj",
    )(x, g1, w_in, qg, kg, *rope, hm)


def _lane_lt(shape, bound):
    return lax.broadcasted_iota(jnp.int32, shape, len(shape) - 1) < bound


def _stack_heads(q):
    first = _lane_lt(q.shape, HEAD_DIM)
    zero = jnp.zeros_like(q)
    return jnp.concatenate([jnp.where(first, q, zero), jnp.where(first, zero, q)], axis=0)


def _unstack(o2, l2, m2):
    n = o2.shape[0] // 2
    first = _lane_lt((n, GROUP_WIDTH), HEAD_DIM)
    return tuple(jnp.where(first, a[:n], a[n:]) for a in (o2, l2, m2))


def _merge(a, b):
    m = jnp.maximum(a[2], b[2])
    wa = jnp.exp(a[2] - m)
    wb = jnp.exp(b[2] - m)
    return wa * a[0] + wb * b[0], wa * a[1] + wb * b[1], m


def _attn_unit(q, kv, bias):
    nk = kv.shape[0]
    k = kv[:, :GROUP_WIDTH]
    v1 = jnp.concatenate([kv[:, GROUP_WIDTH:], jnp.ones((nk, GROUP_WIDTH), BF16)], axis=1)
    s = lax.dot_general(_stack_heads(q), k, _NT, preferred_element_type=F32) + bias
    mrow = jnp.max(s, axis=-1, keepdims=True)
    p = jnp.exp(s - mrow).astype(BF16)
    r = jnp.dot(p, v1, preferred_element_type=F32)
    return _unstack(r[:, :GROUP_WIDTH], r[:, GROUP_WIDTH:], mrow)


def _attn_p_kernel(qc0, qc1, qc2, kvc0, kvc1, kvc2, y_ref, acc_o, acc_l, acc_m, tmp_o, tmp_l, tmp_m, bias_ref,
                   *, seq):
    @pl.when(pl.program_id(0) == 0)
    def _():
        qi = lax.broadcasted_iota(jnp.int32, (2 * ATTN_TILE, 2 * ATTN_TILE), 0) & (ATTN_TILE - 1)
        ki = lax.broadcasted_iota(jnp.int32, (2 * ATTN_TILE, 2 * ATTN_TILE), 1)
        for first_key_back in (0, 1):
            dist = qi - ki + first_key_back * ATTN_TILE
            bias_ref[first_key_back] = jnp.where((dist >= 0) & (dist <= N_BACK), 0.0, NEG_INF)

    for g, ((_, dil), q_ref, kv_ref) in enumerate(zip(DIL_PAIRS, (qc0, qc1, qc2), (kvc0, kvc1, kvc2))):
        dst = (acc_o, acc_l, acc_m) if g == 0 else (tmp_o, tmp_l, tmp_m)
        n_sub = seq // dil // ATTN_TILE

        def unit(idx, q_ref=q_ref, kv_ref=kv_ref, n_sub=n_sub, dil=dil):
            if n_sub == 1:
                r, sub = idx, 0
                res = _attn_unit(q_ref[0, r], kv_ref[0, r], bias_ref[0, :, :ATTN_TILE])
            else:
                r, sub = (idx // n_sub, idx % n_sub) if dil > 1 else (0, idx)
                back = jnp.minimum(sub, 1)
                q0 = pl.multiple_of(sub * ATTN_TILE, ATTN_TILE)
                k0 = pl.multiple_of((sub - back) * ATTN_TILE, ATTN_TILE)
                res = _attn_unit(q_ref[0, r, pl.ds(q0, ATTN_TILE), :], kv_ref[0, r, pl.ds(k0, 2 * ATTN_TILE), :],
                                 bias_ref[back])
            start = sub * (ATTN_TILE * dil) + r
            rows = pl.ds(start, ATTN_TILE, stride=dil) if dil > 1 else pl.ds(pl.multiple_of(start, ATTN_TILE), ATTN_TILE)
            return rows, res

        def units_body(it, carry, unit=unit, dst=dst):
            for j in range(ATTN_UNROLL):
                rows, res = unit(it * ATTN_UNROLL + j)
                for ref, val in zip(dst, res):
                    ref[rows, :] = val
            return carry
        lax.fori_loop(0, dil * n_sub // ATTN_UNROLL, units_body, 0)

        if g > 0:
            def merge_body(c, carry):
                rows = pl.ds(pl.multiple_of(c * 256, 256), 256)
                o, l, m = _merge((acc_o[rows, :], acc_l[rows, :], acc_m[rows, :]),
                                 (tmp_o[rows, :], tmp_l[rows, :], tmp_m[rows, :]))
                acc_o[rows, :] = o
                acc_l[rows, :] = l
                acc_m[rows, :] = m
                return carry
            lax.fori_loop(0, seq // 256, merge_body, 0)

    def out_body(c, carry):
        rows = pl.ds(pl.multiple_of(c * 256, 256), 256)
        y_ref[0, rows, :] = (acc_o[rows, :] / acc_l[rows, :]).astype(y_ref.dtype)
        return carry
    lax.fori_loop(0, seq // 256, out_body, 0)


def _attn_p(qcs, kvcs, *, batch, seq):
    in_specs = [pl.BlockSpec((1,) + a.shape[1:], lambda b: (b, 0, 0, 0)) for a in (*qcs, *kvcs)]
    acc = pltpu.VMEM((seq, GROUP_WIDTH), F32)
    return pl.pallas_call(
        functools.partial(_attn_p_kernel, seq=seq),
        grid=(batch,),
        in_specs=in_specs,
        out_specs=pl.BlockSpec((1, seq, GROUP_WIDTH), lambda b: (b, 0, 0)),
        out_shape=jax.ShapeDtypeStruct((batch, seq, GROUP_WIDTH), BF16),
        scratch_shapes=[acc] * 6 + [pltpu.VMEM((2, 2 * ATTN_TILE, 2 * ATTN_TILE), F32)],
        compiler_params=_params(),
        name="attn_p",
    )(*qcs, *kvcs)


def _pool_rows(ext, pos, pw, ps):
    s2 = ext + pltpu.roll(ext, 1, 0)
    s4 = s2 + pltpu.roll(s2, 2, 0)
    s8 = s4 + pltpu.roll(s4, 4, 0)
    s16 = s8 + pltpu.roll(s8, 8, 0)
    lane = lax.broadcasted_iota(jnp.int32, (1, POOL_WIDTH), 1)
    grp = [lane < (j + 1) * POOL_GROUP for j in range(3)]
    win = jnp.where(grp[0], s2, jnp.where(grp[1], s4, jnp.where(grp[2], s8, s16)))[HIST_ROWS:]
    width = jnp.where(grp[0], POOL_WINDOWS[0], jnp.where(grp[1], POOL_WINDOWS[1],
                      jnp.where(grp[2], POOL_WINDOWS[2], POOL_WINDOWS[3])))
    cnt = jnp.minimum(pos + 1, width).astype(F32)
    d = win / cnt - ext[HIST_ROWS:]
    return jnp.dot(d.astype(BF16), pw, preferred_element_type=F32) * ps


def _conv_rows(ext, gb, cw, hist):
    y = cw[0:1] * pltpu.roll(ext, 2, 0) + cw[1:2] * pltpu.roll(ext, 1, 0) + cw[2:3] * ext
    return gb * y[hist:]


def _local_p_kernel(u_ref, gb_ref, z_ref, pw_ref, ps_ref, cw_ref, yp_ref, yc_ref, upad, zpad, *, seq, chunk):
    upad[:HIST_ROWS, :] = jnp.zeros((HIST_ROWS, POOL_WIDTH), F32)
    zpad[:HIST_ROWS, :] = jnp.zeros((HIST_ROWS, CONV_WIDTH), F32)
    upad[HIST_ROWS:, :] = u_ref[0]
    zpad[HIST_ROWS:, :] = z_ref[0]
    pw, ps, cw = pw_ref[...], ps_ref[...], cw_ref[...]

    def body(c, carry):
        r0 = pl.multiple_of(c * chunk, chunk)
        ext_rows = pl.ds(r0, chunk + HIST_ROWS)
        rows = pl.ds(r0, chunk)
        pos = r0 + lax.broadcasted_iota(jnp.int32, (chunk, 1), 0)
        yp_ref[0, rows, :] = _pool_rows(upad[ext_rows, :], pos, pw, ps).astype(yp_ref.dtype)
        yc_ref[0, rows, :] = _conv_rows(zpad[ext_rows, :], gb_ref[0, rows, :], cw, HIST_ROWS).astype(yc_ref.dtype)
        return carry
    lax.fori_loop(0, seq // chunk, body, 0)


def _local_p(u, gb, z, pw, ps, cw, *, batch, seq):
    blk = lambda c: pl.BlockSpec((1, seq, c), lambda b: (b, 0, 0))
    return pl.pallas_call(
        functools.partial(_local_p_kernel, seq=seq, chunk=256),
        grid=(batch,),
        in_specs=[blk(POOL_WIDTH), blk(CONV_WIDTH), blk(CONV_WIDTH),
                  _const_spec((POOL_WIDTH, POOL_WIDTH)), _const_spec((1, POOL_WIDTH)),
                  _const_spec((CONV_K, CONV_WIDTH))],
        out_specs=[blk(POOL_WIDTH), blk(CONV_WIDTH)],
        out_shape=[jax.ShapeDtypeStruct((batch, seq, POOL_WIDTH), BF16),
                   jax.ShapeDtypeStruct((batch, seq, CONV_WIDTH), BF16)],
        scratch_shapes=[pltpu.VMEM((seq + HIST_ROWS, POOL_WIDTH), F32),
                        pltpu.VMEM((seq + HIST_ROWS, CONV_WIDTH), F32)],
        compiler_params=_params(),
        name="local_p",
    )(u.reshape(batch, seq, POOL_WIDTH), gb.reshape(batch, seq, CONV_WIDTH), z.reshape(batch, seq, CONV_WIDTH),
      pw, ps, cw)


def _attn_s_group(q, new_t, cache_t, window, dil):
    t = q.shape[0]
    q2 = _stack_heads(q).astype(BF16)
    s_c = jnp.dot(q2, cache_t[:GROUP_WIDTH].astype(BF16), preferred_element_type=F32)
    s_n = jnp.dot(q2, new_t[:GROUP_WIDTH].astype(BF16), preferred_element_type=F32)
    tq = lax.broadcasted_iota(jnp.int32, (2 * t, 1), 0) & (t - 1)
    rc = lax.broadcasted_iota(jnp.int32, (1, window), 1)
    dist_c = window + tq - rc
    ok_c = (dist_c <= window) & ((dist_c & (dil - 1)) == 0) & (PAST_LEN - window + rc >= 0)
    tn = lax.broadcasted_iota(jnp.int32, (1, LANES), 1) - (LANES - t)
    dist_n = tq - tn
    ok_n = (tn >= 0) & (dist_n >= 0) & ((dist_n & (dil - 1)) == 0)
    s_c = jnp.where(ok_c, s_c, NEG_INF)
    s_n = jnp.where(ok_n, s_n, NEG_INF)
    mrow = jnp.maximum(jnp.max(s_c, axis=-1, keepdims=True), jnp.max(s_n, axis=-1, keepdims=True))
    p_c = jnp.exp(s_c - mrow)
    p_n = jnp.exp(s_n - mrow)
    o2 = (lax.dot_general(p_c.astype(BF16), cache_t[GROUP_WIDTH:].astype(BF16), _NT, preferred_element_type=F32)
          + lax.dot_general(p_n.astype(BF16), new_t[GROUP_WIDTH:].astype(BF16), _NT, preferred_element_type=F32))
    l2 = jnp.sum(p_c, axis=-1, keepdims=True) + jnp.sum(p_n, axis=-1, keepdims=True)
    return _unstack(o2, l2, mrow)


def _mix_s_kernel(u_ref, q0_ref, q1_ref, q2_ref, kn0_ref, kn1_ref, kn2_ref, gb_ref, z_ref, pst_ref, cst_ref,
                  c0_ref, c1_ref, c2_ref, pw_ref, ps_ref, cw_ref, *rest, t_new):
    yp_ref, ya_ref, yc_ref, nc0_ref, nc1_ref, nc2_ref = rest[-6:]
    state = None
    for (window, dil), q_ref, kn_ref, c_ref, nc_ref in zip(
            DIL_PAIRS, (q0_ref, q1_ref, q2_ref), (kn0_ref, kn1_ref, kn2_ref),
            (c0_ref, c1_ref, c2_ref), (nc0_ref, nc1_ref, nc2_ref)):
        kv_new = kn_ref[0]
        new_t = jnp.concatenate([jnp.zeros((LANES - t_new, 2 * GROUP_WIDTH), F32), kv_new], axis=0).T
        cache_t = c_ref[0, 0]
        part = _attn_s_group(q_ref[0], new_t, cache_t, window, dil)
        state = part if state is None else _merge(state, part)
        rolled = pltpu.roll(cache_t, window - t_new, 1)
        if window > LANES:
            nc_ref[0, 0, :, :window - LANES] = rolled[:, :window - LANES]
        keep_old = _lane_lt((2 * GROUP_WIDTH, LANES), LANES - t_new)
        nc_ref[0, 0, :, window - LANES:] = jnp.where(keep_old, rolled[:, window - LANES:], new_t)
    ya_ref[0] = state[0] / state[1]

    pos = PAST_LEN + lax.broadcasted_iota(jnp.int32, (t_new, 1), 0)
    u_ext = jnp.concatenate([pst_ref[0], u_ref[0]], axis=0)
    yp_ref[0] = _pool_rows(u_ext, pos, pw_ref[...], ps_ref[...])
    z_ext = jnp.concatenate([cst_ref[0], z_ref[0]], axis=0)
    yc_ref[0] = _conv_rows(z_ext, gb_ref[0], cw_ref[...], cst_ref.shape[1])


def _mix_s(u, qs, kns, gb, z, pst, cst, caches_t, pw, ps, cw, prev, *, layer, n_seq, t_new):
    blk = lambda r, c: pl.BlockSpec((1, r, c), lambda b: (b, 0, 0))
    cache_blk = lambda w: pl.BlockSpec((1, 1, 2 * GROUP_WIDTH, w), lambda b: (layer, b, 0, 0))
    three = lambda a: a.reshape(n_seq, -1, a.shape[-1])
    windows = [w for (w, _) in DIL_PAIRS]
    in_specs = ([blk(t_new, POOL_WIDTH)] + [blk(t_new, GROUP_WIDTH)] * N_DIL + [blk(t_new, 2 * GROUP_WIDTH)] * N_DIL
                + [blk(t_new, CONV_WIDTH)] * 2 + [blk(pst.shape[1], POOL_WIDTH), blk(cst.shape[1], CONV_WIDTH)]
                + [cache_blk(w) for w in windows]
                + [_const_spec((POOL_WIDTH, POOL_WIDTH)), _const_spec((1, POOL_WIDTH)),
                   _const_spec((CONV_K, CONV_WIDTH))])
    args = [three(u), *[three(q) for q in qs], *[three(k) for k in kns], three(gb), three(z), pst, cst,
            *caches_t, pw, ps, cw]
    aliases = {}
    if prev is not None:
        for k, p in enumerate(prev):
            aliases[len(args)] = 3 + k
            in_specs.append(pl.BlockSpec(memory_space=pl.ANY))
            args.append(p)
    out_specs = ([blk(t_new, POOL_WIDTH), blk(t_new, GROUP_WIDTH), blk(t_new, CONV_WIDTH)]
                 + [cache_blk(w) for w in windows])
    out_shape = ([jax.ShapeDtypeStruct((n_seq, t_new, c), F32) for c in (POOL_WIDTH, GROUP_WIDTH, CONV_WIDTH)]
                 + [jax.ShapeDtypeStruct(c.shape, F32) for c in caches_t])
    return pl.pallas_call(
        functools.partial(_mix_s_kernel, t_new=t_new),
        grid=(n_seq,),
        in_specs=in_specs,
        out_specs=out_specs,
        out_shape=out_shape,
        input_output_aliases=aliases,
        compiler_params=_params(),
        name="mix_s",
    )(*args)


def _mlp_kernel(x_ref, yp_ref, ya_ref, yc_ref, wo_ref, g2_ref, wu_ref, wd_ref, o_ref, *, tf):
    mixed = jnp.concatenate([yp_ref[...].astype(BF16), ya_ref[...].astype(BF16), yc_ref[...].astype(BF16)], axis=1)
    x1 = x_ref[...] + jnp.dot(mixed, wo_ref[...], preferred_element_type=F32)
    ms = jnp.mean(x1 * x1, axis=-1, keepdims=True)
    hb = (x1 * lax.rsqrt(ms + EPS) * g2_ref[...]).astype(BF16)
    acc = x1
    for c in range(D_FF // tf):
        hf = jnp.dot(hb, wu_ref[:, c * tf:(c + 1) * tf], preferred_element_type=F32)
        act = jnp.square(jnp.maximum(hf, 0.0)).astype(BF16)
        acc = acc + jnp.dot(act, wd_ref[c * tf:(c + 1) * tf, :], preferred_element_type=F32)
    o_ref[...] = acc


def _mlp(x, yp, ya, yc, w_out, g2, w_up, w_down, *, tm):
    rows = x.shape[0]
    row = lambda c: pl.BlockSpec((tm, c), lambda i: (i, 0))
    return pl.pallas_call(
        functools.partial(_mlp_kernel, tf=512),
        grid=(rows // tm,),
        in_specs=[row(D_MODEL), row(POOL_WIDTH), row(GROUP_WIDTH), row(CONV_WIDTH),
                  _const_spec((MIX_OUT, D_MODEL)), _const_spec((1, D_MODEL)),
                  _const_spec((D_MODEL, D_FF)), _const_spec((D_FF, D_MODEL))],
        out_specs=row(D_MODEL),
        out_shape=jax.ShapeDtypeStruct((rows, D_MODEL), F32),
        compiler_params=_params(),
        name="mlp",
    )(x, yp, ya, yc, w_out, g2, w_up, w_down)


def _rope_tables(pos):
    half = ROPE_DIM // 2
    inv = jnp.power(jnp.float32(ROPE_THETA), -jnp.arange(half, dtype=F32) / half)
    ang = pos.astype(F32)[:, None] * inv[None, :]
    cos, sin = jnp.cos(ang), jnp.sin(ang)
    n = pos.shape[0]
    rest = HEAD_DIM - ROPE_DIM
    zh = jnp.zeros((n, half), F32)
    c = jnp.concatenate([cos, cos, jnp.ones((n, rest), F32)], axis=1)
    a = jnp.concatenate([-sin, zh, jnp.zeros((n, rest), F32)], axis=1)
    b = jnp.concatenate([zh, sin, jnp.zeros((n, rest), F32)], axis=1)
    return tuple(jnp.tile(t, (1, GROUP_WIDTH // HEAD_DIM)) for t in (c, a, b))


def _to_buffer_layout(c):
    lead = c.shape[:-4]
    n = len(lead)
    t = jnp.transpose(c, (*range(n), n + 1, n + 2, n + 3, n))
    return t.reshape(*lead, 2 * GROUP_WIDTH, c.shape[-4])


def _from_buffer_layout(t):
    lead = t.shape[:-2]
    n = len(lead)
    c = t.reshape(*lead, 2, 2, HEAD_DIM, t.shape[-1])
    return jnp.transpose(c, (*range(n), n + 3, n, n + 1, n + 2))


def kernel(x_prompt, x_sample, state_pool, state_conv, cache_kv_w128, cache_kv_w512, cache_kv_w2048,
           norm1_g, w_in, q_norm_g, k_norm_g, pool_w, pool_scale, conv_w, w_out, norm2_g, w_up, w_down):
    batch, seq, _ = x_prompt.shape
    n_seq, t_new, _ = x_sample.shape
    depth = w_in.shape[0]
    caches_t = [_to_buffer_layout(c) for c in (cache_kv_w128, cache_kv_w512, cache_kv_w2048)]

    rope_p = _rope_tables(jnp.arange(seq, dtype=jnp.int32))
    rope_s = _rope_tables(jnp.tile(PAST_LEN + jnp.arange(t_new, dtype=jnp.int32), n_seq))
    head_id = jnp.arange(LANES) // HEAD_DIM
    hm = jnp.where(head_id[:, None] == head_id[None, :], 1.0 / HEAD_DIM, 0.0).astype(BF16)
    two_heads = lambda gain: jnp.tile(gain, GROUP_WIDTH // HEAD_DIM)[None, :]

    xp = x_prompt.reshape(batch * seq, D_MODEL)
    xs = x_sample.reshape(n_seq * t_new, D_MODEL)
    outs = {k: [] for k in ("pool_p", "conv_p", "kv_p0", "kv_p1", "kv_p2", "pool_s", "conv_s")}
    new_caches = None
    for layer in range(depth):
        w_in_l = w_in[layer].astype(BF16)
        w_out_l = w_out[layer].astype(BF16)
        w_up_l = w_up[layer].astype(BF16)
        w_down_l = w_down[layer].astype(BF16)
        pw = jax.scipy.linalg.block_diag(*[pool_w[layer, j] for j in range(len(POOL_WINDOWS))]).astype(BF16)
        ps = pool_scale[layer][None, :]
        cw = conv_w[layer]
        g1 = norm1_g[layer][None, :]
        g2 = norm2_g[layer][None, :]
        qg = two_heads(q_norm_g[layer]) * (HEAD_DIM ** -0.5)
        kg = two_heads(k_norm_g[layer])

        (u, gb, z, qc0, qc1, qc2, kvc0, kvc1, kvc2, kvt0, kvt1, kvt2) = _proj(
            xp, g1, w_in_l, qg, kg, rope_p, hm, tm=512, batch=batch, seq=seq)
        ya = _attn_p((qc0, qc1, qc2), (kvc0, kvc1, kvc2), batch=batch, seq=seq)
        yp, yc = _local_p(u, gb, z, pw, ps, cw, batch=batch, seq=seq)
        xp = _mlp(xp, yp.reshape(-1, POOL_WIDTH), ya.reshape(-1, GROUP_WIDTH), yc.reshape(-1, CONV_WIDTH),
                  w_out_l, g2, w_up_l, w_down_l, tm=512)
        outs["pool_p"].append(u.reshape(batch, seq, POOL_WIDTH)[:, seq - POOL_HIST:])
        outs["conv_p"].append(z.reshape(batch, seq, CONV_WIDTH)[:, seq - (CONV_K - 1):])
        for g, kvt in enumerate((kvt0, kvt1, kvt2)):
            outs[f"kv_p{g}"].append(kvt)

        us, gbs, zs, q0, q1, q2, kn0, kn1, kn2 = _proj(xs, g1, w_in_l, qg, kg, rope_s, hm, tm=n_seq * t_new)
        pst = jnp.pad(state_pool[layer], ((0, 0), (HIST_ROWS - POOL_HIST, 0), (0, 0)))
        cst = jnp.pad(state_conv[layer], ((0, 0), (8 - (CONV_K - 1), 0), (0, 0)))
        yps, yas, ycs, *new_caches = _mix_s(us, (q0, q1, q2), (kn0, kn1, kn2), gbs, zs, pst, cst, caches_t,
                                           pw, ps, cw, new_caches, layer=layer, n_seq=n_seq, t_new=t_new)
        xs = _mlp(xs, yps.reshape(-1, POOL_WIDTH), yas.reshape(-1, GROUP_WIDTH), ycs.reshape(-1, CONV_WIDTH),
                  w_out_l, g2, w_up_l, w_down_l, tm=n_seq * t_new)
        us3 = us.reshape(n_seq, t_new, POOL_WIDTH)
        zs3 = zs.reshape(n_seq, t_new, CONV_WIDTH)
        outs["pool_s"].append(jnp.concatenate([state_pool[layer], us3], axis=1)[:, -POOL_HIST:])
        outs["conv_s"].append(jnp.concatenate([state_conv[layer], zs3], axis=1)[:, -(CONV_K - 1):])

    st = lambda k: jnp.stack(outs[k])
    return (xp.reshape(batch, seq, D_MODEL), xs.reshape(n_seq, t_new, D_MODEL),
            st("pool_p"), st("conv_p"),
            *[_from_buffer_layout(st(f"kv_p{g}")) for g in range(N_DIL)],
            st("pool_s"), st("conv_s"),
            *[_from_buffer_layout(c) for c in new_caches])
```

```python
import functools

import jax
import jax.numpy as jnp
from jax import lax
from jax.experimental import pallas as pl
from jax.experimental.pallas import tpu as pltpu

D_MODEL = 1024
HEAD_DIM = 64
POOL_WIDTH = 256
POOL_WINDOWS = (2, 4, 8, 16)
POOL_GROUP = 64
POOL_HIST = 15
ATTN_WIDTH = 384
DIL_PAIRS = ((128, 1), (512, 4), (2048, 16))
N_DIL = 3
GROUP_WIDTH = 128
CONV_WIDTH = 384
CONV_K = 3
ROPE_DIM = 16
ROPE_THETA = 500000.0
D_FF = 4096
IN_COLS = 2560
MIX_OUT = 768
EPS = 1e-6
NEG_INF = -1e30
PAST_LEN = 8192
N_BACK = 128

LANES = 128
HIST_ROWS = 16
ATTN_TILE = 128
ATTN_UNROLL = 16
PROJ_TILE = 1024
PROJ_SUB = 256
VMEM_LIMIT = 56 * 1024 * 1024

F32 = jnp.float32
BF16 = jnp.bfloat16

_Q_OFF = POOL_WIDTH
_K_OFF = _Q_OFF + ATTN_WIDTH
_V_OFF = _K_OFF + ATTN_WIDTH
_GB_OFF = _V_OFF + ATTN_WIDTH
_GC_OFF = _GB_OFF + CONV_WIDTH
_GH_OFF = _GC_OFF + CONV_WIDTH

_NT = (((1,), (1,)), ((), ()))


def _const_spec(shape):
    return pl.BlockSpec(shape, lambda *_: (0,) * len(shape), pipeline_mode=pl.Buffered(1))


def _params():
    return pltpu.CompilerParams(dimension_semantics=("arbitrary",), vmem_limit_bytes=VMEM_LIMIT)


def _proj_kernel(x_ref, g1_ref, w_ref, qg_ref, kg_ref, rc_ref, ra_ref, rb_ref, hm_ref, *rest,
                 class_major, tiles, sub):
    if class_major:
        (u_ref, gb_ref, z_ref, qc0, qc1, qc2, kvc0, kvc1, kvc2, kvt0, kvt1, kvt2, nat) = rest
    else:
        (u_ref, gb_ref, z_ref, q0, q1, q2, kv0, kv1, kv2) = rest
    tm = x_ref.shape[0]
    hm = hm_ref[...]
    last_tile = pl.program_id(0) % tiles == tiles - 1

    for s in range(tm // sub):
        r0 = s * sub
        rows = slice(r0, r0 + sub)
        x = x_ref[rows, :]
        ms = jnp.mean(x * x, axis=-1, keepdims=True)
        hb = (x * lax.rsqrt(ms + EPS) * g1_ref[...]).astype(BF16)
        p = jnp.dot(hb, w_ref[...], preferred_element_type=F32)
        rc, ra, rb = rc_ref[rows, :], ra_ref[rows, :], rb_ref[rows, :]

        def head_norm_rope(xb, gain):
            sq = xb * xb
            hi = sq.astype(BF16)
            lo = (sq - hi.astype(F32)).astype(BF16)
            msq = (jnp.dot(hi, hm, preferred_element_type=F32) + jnp.dot(lo, hm, preferred_element_type=F32))
            xn = xb * lax.rsqrt(msq + EPS) * gain
            return xn * rc + pltpu.roll(xn, LANES - ROPE_DIM // 2, 1) * ra + pltpu.roll(xn, ROPE_DIM // 2, 1) * rb

        u_ref[rows, :] = p[:, :POOL_WIDTH]
        gb_ref[rows, :] = p[:, _GB_OFF:_GC_OFF]
        z_ref[rows, :] = p[:, _GC_OFF:_GH_OFF] * p[:, _GH_OFF:]

        for g, (window, dil) in enumerate(DIL_PAIRS):
            lanes = slice(g * GROUP_WIDTH, (g + 1) * GROUP_WIDTH)
            qn = head_norm_rope(p[:, _Q_OFF:_K_OFF][:, lanes], qg_ref[...])
            kn = head_norm_rope(p[:, _K_OFF:_V_OFF][:, lanes], kg_ref[...])
            vv = p[:, _V_OFF:_GB_OFF][:, lanes]
            if not class_major:
                q_ref, kv_ref = ((q0, kv0), (q1, kv1), (q2, kv2))[g]
                q_ref[rows, :] = qn
                kv_ref[rows, :GROUP_WIDTH] = kn
                kv_ref[rows, GROUP_WIDTH:] = vv
                continue
            qc_ref, kvc_ref, kvt_ref = ((qc0, kvc0, kvt0), (qc1, kvc1, kvt1), (qc2, kvc2, kvt2))[g]
            slab = 3 * (s * N_DIL + g)
            nat[slab] = qn
            nat[slab + 1] = kn
            nat[slab + 2] = vv
            per = sub // dil
            dst = slice(s * per, (s + 1) * per)
            for r in range(dil):
                src = pl.ds(r, per, stride=dil) if dil > 1 else slice(None)
                qc_ref[0, r, dst, :] = nat[slab, src, :].astype(BF16)
                kvc_ref[0, r, dst, :GROUP_WIDTH] = nat[slab + 1, src, :].astype(BF16)
                kvc_ref[0, r, dst, GROUP_WIDTH:] = nat[slab + 2, src, :].astype(BF16)
            if window >= tiles * tm:
                kvt_ref[0, :, rows] = jnp.concatenate([kn, vv], axis=1).T
            else:
                first_kept = tm - min(window, tm)
                lo = max(r0, first_kept)
                if lo < r0 + sub:
                    @pl.when(last_tile)
                    def _(kn=kn, vv=vv, kvt_ref=kvt_ref, lo=lo, r0=r0, first_kept=first_kept):
                        part = jnp.concatenate([kn[lo - r0:], vv[lo - r0:]], axis=1).T
                        kvt_ref[0, :, lo - first_kept:r0 + sub - first_kept] = part


def _proj(x, g1, w_in, qg, kg, rope, hm, *, tm, batch=None, seq=None):
    rows = x.shape[0]
    n_tab = rope[0].shape[0] // tm
    class_major = batch is not None
    tiles = seq // tm if class_major else 1
    row = lambda c: pl.BlockSpec((tm, c), lambda i: (i, 0))
    tab = pl.BlockSpec((tm, LANES), lambda i: (i % n_tab, 0))
    out_shape = [jax.ShapeDtypeStruct((rows, POOL_WIDTH), F32)] + [jax.ShapeDtypeStruct((rows, CONV_WIDTH), F32)] * 2
    out_specs = [row(POOL_WIDTH), row(CONV_WIDTH), row(CONV_WIDTH)]
    scratch = []
    if class_major:
        for width in (GROUP_WIDTH, 2 * GROUP_WIDTH):
            for (_, dil) in DIL_PAIRS:
                out_shape.append(jax.ShapeDtypeStruct((batch, dil, seq // dil, width), BF16))
                out_specs.append(pl.BlockSpec((1, dil, tm // dil, width), lambda i: (i // tiles, 0, i % tiles, 0)))
        for (window, _) in DIL_PAIRS:
            keep = min(window, tm)
            out_shape.append(jax.ShapeDtypeStruct((batch, 2 * GROUP_WIDTH, window), F32))
            if window >= seq:
                out_specs.append(pl.BlockSpec((1, 2 * GROUP_WIDTH, tm), lambda i: (i // tiles, 0, i % tiles)))
            else:
                assert window == keep, "kv buffer rows must come from the last row tile"
                out_specs.append(pl.BlockSpec((1, 2 * GROUP_WIDTH, window), lambda i: (i // tiles, 0, 0)))
        scratch = [pltpu.VMEM((3 * N_DIL * (tm // PROJ_SUB), PROJ_SUB, GROUP_WIDTH), F32)]
    else:
        out_shape += [jax.ShapeDtypeStruct((rows, GROUP_WIDTH), F32)] * N_DIL
        out_shape += [jax.ShapeDtypeStruct((rows, 2 * GROUP_WIDTH), F32)] * N_DIL
        out_specs += [row(GROUP_WIDTH)] * N_DIL + [row(2 * GROUP_WIDTH)] * N_DIL
    return pl.pallas_call(
        functools.partial(_proj_kernel, class_major=class_major, tiles=tiles, sub=PROJ_SUB),
        grid=(rows // tm,),
        in_specs=[row(D_MODEL), _const_spec((1, D_MODEL)), _const_spec((D_MODEL, IN_COLS)),
                  _const_spec((1, LANES)), _const_spec((1, LANES)), tab, tab, tab,
                  _const_spec((LANES, LANES))],
        out_specs=out_specs,
        out_shape=out_shape,
        scratch_shapes=scratch,
        compiler_params=_params(),
        name="proj",
    )(x, g1, w_in, qg, kg, *rope, hm)


def _lane_lt(shape, bound):
    return lax.broadcasted_iota(jnp.int32, shape, len(shape) - 1) < bound


def _stack_heads(q):
    first = _lane_lt(q.shape, HEAD_DIM)
    zero = jnp.zeros_like(q)
    return jnp.concatenate([jnp.where(first, q, zero), jnp.where(first, zero, q)], axis=0)


def _unstack(o2, l2, m2):
    n = o2.shape[0] // 2
    first = _lane_lt((n, GROUP_WIDTH), HEAD_DIM)
    return tuple(jnp.where(first, a[:n], a[n:]) for a in (o2, l2, m2))


def _merge(a, b):
    m = jnp.maximum(a[2], b[2])
    wa = jnp.exp(a[2] - m)
    wb = jnp.exp(b[2] - m)
    return wa * a[0] + wb * b[0], wa * a[1] + wb * b[1], m


def _attn_unit(q, kv, bias):
    nk = kv.shape[0]
    k = kv[:, :GROUP_WIDTH]
    v1 = jnp.concatenate([kv[:, GROUP_WIDTH:], jnp.ones((nk, GROUP_WIDTH), BF16)], axis=1)
    s = lax.dot_general(_stack_heads(q), k, _NT, preferred_element_type=F32) + bias
    mrow = jnp.max(s, axis=-1, keepdims=True)
    p = jnp.exp(s - mrow).astype(BF16)
    r = jnp.dot(p, v1, preferred_element_type=F32)
    return _unstack(r[:, :GROUP_WIDTH], r[:, GROUP_WIDTH:], mrow)


def _attn_p_kernel(qc0, qc1, qc2, kvc0, kvc1, kvc2, y_ref, acc_o, acc_l, acc_m, tmp_o, tmp_l, tmp_m, bias_ref,
                   *, seq):
    @pl.when(pl.program_id(0) == 0)
    def _():
        qi = lax.broadcasted_iota(jnp.int32, (2 * ATTN_TILE, 2 * ATTN_TILE), 0) & (ATTN_TILE - 1)
        ki = lax.broadcasted_iota(jnp.int32, (2 * ATTN_TILE, 2 * ATTN_TILE), 1)
        for first_key_back in (0, 1):
            dist = qi - ki + first_key_back * ATTN_TILE
            bias_ref[first_key_back] = jnp.where((dist >= 0) & (dist <= N_BACK), 0.0, NEG_INF)

    for g, ((_, dil), q_ref, kv_ref) in enumerate(zip(DIL_PAIRS, (qc0, qc1, qc2), (kvc0, kvc1, kvc2))):
        dst = (acc_o, acc_l, acc_m) if g == 0 else (tmp_o, tmp_l, tmp_m)
        n_sub = seq // dil // ATTN_TILE

        def unit(idx, q_ref=q_ref, kv_ref=kv_ref, n_sub=n_sub, dil=dil):
            if n_sub == 1:
                r, sub = idx, 0
                res = _attn_unit(q_ref[0, r], kv_ref[0, r], bias_ref[0, :, :ATTN_TILE])
            else:
                r, sub = (idx // n_sub, idx % n_sub) if dil > 1 else (0, idx)
                back = jnp.minimum(sub, 1)
                q0 = pl.multiple_of(sub * ATTN_TILE, ATTN_TILE)
                k0 = pl.multiple_of((sub - back) * ATTN_TILE, ATTN_TILE)
                res = _attn_unit(q_ref[0, r, pl.ds(q0, ATTN_TILE), :], kv_ref[0, r, pl.ds(k0, 2 * ATTN_TILE), :],
                                 bias_ref[back])
            start = sub * (ATTN_TILE * dil) + r
            rows = pl.ds(start, ATTN_TILE, stride=dil) if dil > 1 else pl.ds(pl.multiple_of(start, ATTN_TILE), ATTN_TILE)
            return rows, res

        def units_body(it, carry, unit=unit, dst=dst):
            for j in range(ATTN_UNROLL):
                rows, res = unit(it * ATTN_UNROLL + j)
                for ref, val in zip(dst, res):
                    ref[rows, :] = val
            return carry
        lax.fori_loop(0, dil * n_sub // ATTN_UNROLL, units_body, 0)

        if g > 0:
            def merge_body(c, carry):
                rows = pl.ds(pl.multiple_of(c * 256, 256), 256)
                o, l, m = _merge((acc_o[rows, :], acc_l[rows, :], acc_m[rows, :]),
                                 (tmp_o[rows, :], tmp_l[rows, :], tmp_m[rows, :]))
                acc_o[rows, :] = o
                acc_l[rows, :] = l
                acc_m[rows, :] = m
                return carry
            lax.fori_loop(0, seq // 256, merge_body, 0)

    def out_body(c, carry):
        rows = pl.ds(pl.multiple_of(c * 256, 256), 256)
        y_ref[0, rows, :] = (acc_o[rows, :] / acc_l[rows, :]).astype(y_ref.dtype)
        return carry
    lax.fori_loop(0, seq // 256, out_body, 0)


def _attn_p(qcs, kvcs, *, batch, seq):
    in_specs = [pl.BlockSpec((1,) + a.shape[1:], lambda b: (b, 0, 0, 0)) for a in (*qcs, *kvcs)]
    acc = pltpu.VMEM((seq, GROUP_WIDTH), F32)
    return pl.pallas_call(
        functools.partial(_attn_p_kernel, seq=seq),
        grid=(batch,),
        in_specs=in_specs,
        out_specs=pl.BlockSpec((1, seq, GROUP_WIDTH), lambda b: (b, 0, 0)),
        out_shape=jax.ShapeDtypeStruct((batch, seq, GROUP_WIDTH), BF16),
        scratch_shapes=[acc] * 6 + [pltpu.VMEM((2, 2 * ATTN_TILE, 2 * ATTN_TILE), F32)],
        compiler_params=_params(),
        name="attn_p",
    )(*qcs, *kvcs)


def _pool_rows(ext, pos, pw, ps):
    s2 = ext + pltpu.roll(ext, 1, 0)
    s4 = s2 + pltpu.roll(s2, 2, 0)
    s8 = s4 + pltpu.roll(s4, 4, 0)
    s16 = s8 + pltpu.roll(s8, 8, 0)
    lane = lax.broadcasted_iota(jnp.int32, (1, POOL_WIDTH), 1)
    grp = [lane < (j + 1) * POOL_GROUP for j in range(3)]
    win = jnp.where(grp[0], s2, jnp.where(grp[1], s4, jnp.where(grp[2], s8, s16)))[HIST_ROWS:]
    width = jnp.where(grp[0], POOL_WINDOWS[0], jnp.where(grp[1], POOL_WINDOWS[1],
                      jnp.where(grp[2], POOL_WINDOWS[2], POOL_WINDOWS[3])))
    cnt = jnp.minimum(pos + 1, width).astype(F32)
    d = win / cnt - ext[HIST_ROWS:]
    return jnp.dot(d.astype(BF16), pw, preferred_element_type=F32) * ps


def _conv_rows(ext, gb, cw, hist):
    y = cw[0:1] * pltpu.roll(ext, 2, 0) + cw[1:2] * pltpu.roll(ext, 1, 0) + cw[2:3] * ext
    return gb * y[hist:]


def _attn_s_group(q, new_t, cache_t, window, dil, shifted):
    t = q.shape[0]
    q2 = _stack_heads(q).astype(BF16)
    s_c = jnp.dot(q2, cache_t[:GROUP_WIDTH].astype(BF16), preferred_element_type=F32)
    s_n = jnp.dot(q2, new_t[:GROUP_WIDTH].astype(BF16), preferred_element_type=F32)
    tq = lax.broadcasted_iota(jnp.int32, (2 * t, 1), 0) & (t - 1)
    rc = lax.broadcasted_iota(jnp.int32, (1, window), 1)
    if shifted:
        rc = (rc + t) & (window - 1)
    dist_c = window + tq - rc
    ok_c = (dist_c <= window) & ((dist_c & (dil - 1)) == 0) & (PAST_LEN - window + rc >= 0)
    tn = lax.broadcasted_iota(jnp.int32, (1, LANES), 1) - (LANES - t)
    dist_n = tq - tn
    ok_n = (tn >= 0) & (dist_n >= 0) & ((dist_n & (dil - 1)) == 0)
    s_c = jnp.where(ok_c, s_c, NEG_INF)
    s_n = jnp.where(ok_n, s_n, NEG_INF)
    mrow = jnp.maximum(jnp.max(s_c, axis=-1, keepdims=True), jnp.max(s_n, axis=-1, keepdims=True))
    p_c = jnp.exp(s_c - mrow)
    p_n = jnp.exp(s_n - mrow)
    o2 = (lax.dot_general(p_c.astype(BF16), cache_t[GROUP_WIDTH:].astype(BF16), _NT, preferred_element_type=F32)
          + lax.dot_general(p_n.astype(BF16), new_t[GROUP_WIDTH:].astype(BF16), _NT, preferred_element_type=F32))
    l2 = jnp.sum(p_c, axis=-1, keepdims=True) + jnp.sum(p_n, axis=-1, keepdims=True)
    return _unstack(o2, l2, mrow)


def _mix_s_kernel(u_ref, q0_ref, q1_ref, q2_ref, kn0_ref, kn1_ref, kn2_ref, gb_ref, z_ref, pst_ref, cst_ref,
                  pw_ref, ps_ref, cw_ref, *rest, t_new, first):
    yp_ref, ya_ref, yc_ref, nc0_ref, nc1_ref, nc2_ref = rest[-6:]
    cache_refs = rest[:-6]
    keep_old = _lane_lt((2 * GROUP_WIDTH, LANES), LANES - t_new)
    state = None
    for g, ((window, dil), q_ref, kn_ref, nc_ref) in enumerate(zip(
            DIL_PAIRS, (q0_ref, q1_ref, q2_ref), (kn0_ref, kn1_ref, kn2_ref), (nc0_ref, nc1_ref, nc2_ref))):
        kv_new = kn_ref[0]
        new_t = jnp.concatenate([jnp.zeros((LANES - t_new, 2 * GROUP_WIDTH), F32), kv_new], axis=0).T
        cache_t = cache_refs[g][0, 0]
        part = _attn_s_group(q_ref[0], new_t, cache_t, window, dil, shifted=not first)
        state = part if state is None else _merge(state, part)
        rolled = pltpu.roll(cache_t, window - t_new, 1) if first else cache_t
        last = jnp.where(keep_old, rolled[:, window - LANES:], new_t)
        if first:
            if window > LANES:
                nc_ref[0, 0, :, :window - LANES] = rolled[:, :window - LANES]
            nc_ref[0, 0, :, window - LANES:] = last
            nc_ref[1, 0] = pltpu.roll(cache_refs[N_DIL + g][0, 0], window - t_new, 1)
        else:
            nc_ref[0, 0] = last
    ya_ref[0] = state[0] / state[1]

    pos = PAST_LEN + lax.broadcasted_iota(jnp.int32, (t_new, 1), 0)
    u_ext = jnp.concatenate([pst_ref[0], u_ref[0]], axis=0)
    yp_ref[0] = _pool_rows(u_ext, pos, pw_ref[...], ps_ref[...])
    z_ext = jnp.concatenate([cst_ref[0], z_ref[0]], axis=0)
    yc_ref[0] = _conv_rows(z_ext, gb_ref[0], cw_ref[...], cst_ref.shape[1])


def _mix_s(u, qs, kns, gb, z, pst, cst, caches_t, pw, ps, cw, prev, *, n_seq, t_new):
    first = prev is None
    blk = lambda r, c: pl.BlockSpec((1, r, c), lambda b: (b, 0, 0))
    cache_blk = lambda layer, w: pl.BlockSpec((1, 1, 2 * GROUP_WIDTH, w), lambda b: (layer, b, 0, 0))
    three = lambda a: a.reshape(n_seq, -1, a.shape[-1])
    windows = [w for (w, _) in DIL_PAIRS]
    in_specs = ([blk(t_new, POOL_WIDTH)] + [blk(t_new, GROUP_WIDTH)] * N_DIL + [blk(t_new, 2 * GROUP_WIDTH)] * N_DIL
                + [blk(t_new, CONV_WIDTH)] * 2 + [blk(pst.shape[1], POOL_WIDTH), blk(cst.shape[1], CONV_WIDTH)]
                + [_const_spec((POOL_WIDTH, POOL_WIDTH)), _const_spec((1, POOL_WIDTH)),
                   _const_spec((CONV_K, CONV_WIDTH))])
    args = [three(u), *[three(q) for q in qs], *[three(k) for k in kns], three(gb), three(z), pst, cst, pw, ps, cw]
    aliases = {}
    if first:
        assert all(c.shape[0] == 2 for c in caches_t), "stacked kv buffer update is written for two layers"
        in_specs += [cache_blk(0, w) for w in windows] + [cache_blk(1, w) for w in windows]
        args += [*caches_t, *caches_t]
        cache_out = [pl.BlockSpec((2, 1, 2 * GROUP_WIDTH, w), lambda b: (0, b, 0, 0)) for w in windows]
    else:
        for k, (p, w) in enumerate(zip(prev, windows)):
            aliases[len(args)] = 3 + k
            in_specs.append(cache_blk(1, w))
            args.append(p)
        cache_out = [pl.BlockSpec((1, 1, 2 * GROUP_WIDTH, LANES), lambda b, w=w: (1, b, 0, w // LANES - 1))
                     for w in windows]
    out_specs = [blk(t_new, POOL_WIDTH), blk(t_new, GROUP_WIDTH), blk(t_new, CONV_WIDTH)] + cache_out
    out_shape = ([jax.ShapeDtypeStruct((n_seq, t_new, c), F32) for c in (POOL_WIDTH, GROUP_WIDTH, CONV_WIDTH)]
                 + [jax.ShapeDtypeStruct(c.shape, F32) for c in caches_t])
    return pl.pallas_call(
        functools.partial(_mix_s_kernel, t_new=t_new, first=first),
        grid=(n_seq,),
        in_specs=in_specs,
        out_specs=out_specs,
        out_shape=out_shape,
        input_output_aliases=aliases,
        compiler_params=_params(),
        name="mix_s",
    )(*args)


def _mlp_kernel(x_ref, *refs, tf, tiles):
    if tiles:
        (u_ref, uh_ref, gb_ref, z_ref, zh_ref, ya_ref, pw_ref, ps_ref, cw_ref,
         wo_ref, g2_ref, wu_ref, wd_ref, o_ref) = refs
        tm = x_ref.shape[0]
        tile = pl.program_id(0) % tiles
        has_hist = tile > 0
        pos = tile * tm + lax.broadcasted_iota(jnp.int32, (tm, 1), 0)
        u_ext = jnp.concatenate([jnp.where(has_hist, uh_ref[...], 0.0), u_ref[...]], axis=0)
        z_ext = jnp.concatenate([jnp.where(has_hist, zh_ref[...], 0.0), z_ref[...]], axis=0)
        yp = _pool_rows(u_ext, pos, pw_ref[...], ps_ref[...])
        yc = _conv_rows(z_ext, gb_ref[...], cw_ref[...], HIST_ROWS)
    else:
        yp_ref, ya_ref, yc_ref, wo_ref, g2_ref, wu_ref, wd_ref, o_ref = refs
        yp, yc = yp_ref[...], yc_ref[...]
    mixed = jnp.concatenate([yp.astype(BF16), ya_ref[...].astype(BF16), yc.astype(BF16)], axis=1)
    x1 = x_ref[...] + jnp.dot(mixed, wo_ref[...], preferred_element_type=F32)
    ms = jnp.mean(x1 * x1, axis=-1, keepdims=True)
    hb = (x1 * lax.rsqrt(ms + EPS) * g2_ref[...]).astype(BF16)
    acc = x1
    for c in range(D_FF // tf):
        hf = jnp.dot(hb, wu_ref[:, c * tf:(c + 1) * tf], preferred_element_type=F32)
        act = jnp.square(jnp.maximum(hf, 0.0)).astype(BF16)
        acc = acc + jnp.dot(act, wd_ref[c * tf:(c + 1) * tf, :], preferred_element_type=F32)
    o_ref[...] = acc


def _mlp(x, mix_in, w_out, g2, w_up, w_down, *, tm, local=None):
    rows = x.shape[0]
    row = lambda c: pl.BlockSpec((tm, c), lambda i: (i, 0))
    weights = [_const_spec((MIX_OUT, D_MODEL)), _const_spec((1, D_MODEL)),
               _const_spec((D_MODEL, D_FF)), _const_spec((D_FF, D_MODEL))]
    if local is None:
        tiles = 0
        in_specs = [row(D_MODEL), row(POOL_WIDTH), row(GROUP_WIDTH), row(CONV_WIDTH)] + weights
        args = (x, *mix_in, w_out, g2, w_up, w_down)
    else:
        u, gb, z, ya = mix_in
        pw, ps, cw, seq = local
        tiles = seq // tm
        hist = lambda c: pl.BlockSpec((HIST_ROWS, c), lambda i: (jnp.maximum(i * (tm // HIST_ROWS) - 1, 0), 0))
        in_specs = [row(D_MODEL), row(POOL_WIDTH), hist(POOL_WIDTH), row(CONV_WIDTH), row(CONV_WIDTH),
                    hist(CONV_WIDTH), row(GROUP_WIDTH), _const_spec((POOL_WIDTH, POOL_WIDTH)),
                    _const_spec((1, POOL_WIDTH)), _const_spec((CONV_K, CONV_WIDTH))] + weights
        args = (x, u, u, gb, z, z, ya, pw, ps, cw, w_out, g2, w_up, w_down)
    return pl.pallas_call(
        functools.partial(_mlp_kernel, tf=512, tiles=tiles),
        grid=(rows // tm,),
        in_specs=in_specs,
        out_specs=row(D_MODEL),
        out_shape=jax.ShapeDtypeStruct((rows, D_MODEL), F32),
        compiler_params=_params(),
        name="mlp",
    )(*args)


def _rope_tables(pos):
    half = ROPE_DIM // 2
    inv = jnp.power(jnp.float32(ROPE_THETA), -jnp.arange(half, dtype=F32) / half)
    ang = pos.astype(F32)[:, None] * inv[None, :]
    cos, sin = jnp.cos(ang), jnp.sin(ang)
    n = pos.shape[0]
    rest = HEAD_DIM - ROPE_DIM
    zh = jnp.zeros((n, half), F32)
    c = jnp.concatenate([cos, cos, jnp.ones((n, rest), F32)], axis=1)
    a = jnp.concatenate([-sin, zh, jnp.zeros((n, rest), F32)], axis=1)
    b = jnp.concatenate([zh, sin, jnp.zeros((n, rest), F32)], axis=1)
    return tuple(jnp.tile(t, (1, GROUP_WIDTH // HEAD_DIM)) for t in (c, a, b))


def _to_buffer_layout(c):
    lead = c.shape[:-4]
    n = len(lead)
    t = jnp.transpose(c, (*range(n), n + 1, n + 2, n + 3, n))
    return t.reshape(*lead, 2 * GROUP_WIDTH, c.shape[-4])


def _from_buffer_layout(t):
    lead = t.shape[:-2]
    n = len(lead)
    c = t.reshape(*lead, 2, 2, HEAD_DIM, t.shape[-1])
    return jnp.transpose(c, (*range(n), n + 3, n, n + 1, n + 2))


def kernel(x_prompt, x_sample, state_pool, state_conv, cache_kv_w128, cache_kv_w512, cache_kv_w2048,
           norm1_g, w_in, q_norm_g, k_norm_g, pool_w, pool_scale, conv_w, w_out, norm2_g, w_up, w_down):
    batch, seq, _ = x_prompt.shape
    n_seq, t_new, _ = x_sample.shape
    depth = w_in.shape[0]
    caches_t = [_to_buffer_layout(c) for c in (cache_kv_w128, cache_kv_w512, cache_kv_w2048)]

    rope_p = _rope_tables(jnp.arange(seq, dtype=jnp.int32))
    rope_s = _rope_tables(jnp.tile(PAST_LEN + jnp.arange(t_new, dtype=jnp.int32), n_seq))
    head_id = jnp.arange(LANES) // HEAD_DIM
    hm = jnp.where(head_id[:, None] == head_id[None, :], 1.0 / HEAD_DIM, 0.0).astype(BF16)
    two_heads = lambda gain: jnp.tile(gain, GROUP_WIDTH // HEAD_DIM)[None, :]

    xp = x_prompt.reshape(batch * seq, D_MODEL)
    xs = x_sample.reshape(n_seq * t_new, D_MODEL)
    outs = {k: [] for k in ("pool_p", "conv_p", "kv_p0", "kv_p1", "kv_p2", "pool_s", "conv_s")}
    new_caches = None
    for layer in range(depth):
        w_in_l = w_in[layer].astype(BF16)
        w_out_l = w_out[layer].astype(BF16)
        w_up_l = w_up[layer].astype(BF16)
        w_down_l = w_down[layer].astype(BF16)
        pw = jax.scipy.linalg.block_diag(*[pool_w[layer, j] for j in range(len(POOL_WINDOWS))]).astype(BF16)
        ps = pool_scale[layer][None, :]
        cw = conv_w[layer]
        g1 = norm1_g[layer][None, :]
        g2 = norm2_g[layer][None, :]
        qg = two_heads(q_norm_g[layer]) * (HEAD_DIM ** -0.5)
        kg = two_heads(k_norm_g[layer])

        (u, gb, z, qc0, qc1, qc2, kvc0, kvc1, kvc2, kvt0, kvt1, kvt2) = _proj(
            xp, g1, w_in_l, qg, kg, rope_p, hm, tm=PROJ_TILE, batch=batch, seq=seq)
        ya = _attn_p((qc0, qc1, qc2), (kvc0, kvc1, kvc2), batch=batch, seq=seq)
        xp = _mlp(xp, (u, gb, z, ya.reshape(-1, GROUP_WIDTH)), w_out_l, g2, w_up_l, w_down_l, tm=512,
                  local=(pw, ps, cw, seq))
        outs["pool_p"].append(u.reshape(batch, seq, POOL_WIDTH)[:, seq - POOL_HIST:])
        outs["conv_p"].append(z.reshape(batch, seq, CONV_WIDTH)[:, seq - (CONV_K - 1):])
        for g, kvt in enumerate((kvt0, kvt1, kvt2)):
            outs[f"kv_p{g}"].append(kvt)

        us, gbs, zs, q0, q1, q2, kn0, kn1, kn2 = _proj(xs, g1, w_in_l, qg, kg, rope_s, hm, tm=n_seq * t_new)
        pst = jnp.pad(state_pool[layer], ((0, 0), (HIST_ROWS - POOL_HIST, 0), (0, 0)))
        cst = jnp.pad(state_conv[layer], ((0, 0), (8 - (CONV_K - 1), 0), (0, 0)))
        yps, yas, ycs, *new_caches = _mix_s(us, (q0, q1, q2), (kn0, kn1, kn2), gbs, zs, pst, cst, caches_t,
                                           pw, ps, cw, new_caches, n_seq=n_seq, t_new=t_new)
        xs = _mlp(xs, (yps.reshape(-1, POOL_WIDTH), yas.reshape(-1, GROUP_WIDTH), ycs.reshape(-1, CONV_WIDTH)),
                  w_out_l, g2, w_up_l, w_down_l, tm=n_seq * t_new)
        us3 = us.reshape(n_seq, t_new, POOL_WIDTH)
        zs3 = zs.reshape(n_seq, t_new, CONV_WIDTH)
        outs["pool_s"].append(jnp.concatenate([state_pool[layer], us3], axis=1)[:, -POOL_HIST:])
        outs["conv_s"].append(jnp.concatenate([state_conv[layer], zs3], axis=1)[:, -(CONV_K - 1):])

    st = lambda k: jnp.stack(outs[k])
    return (xp.reshape(batch, seq, D_MODEL), xs.reshape(n_seq, t_new, D_MODEL),
            st("pool_p"), st("conv_p"),
            *[_from_buffer_layout(st(f"kv_p{g}")) for g in range(N_DIL)],
            st("pool_s"), st("conv_s"),
            *[_from_buffer_layout(c) for c in new_caches])
```

```python
import functools

import jax
import jax.numpy as jnp
from jax import lax
from jax.experimental import pallas as pl
from jax.experimental.pallas import tpu as pltpu

D_MODEL = 1024
HEAD_DIM = 64
POOL_WIDTH = 256
POOL_WINDOWS = (2, 4, 8, 16)
POOL_GROUP = 64
POOL_HIST = 15
ATTN_WIDTH = 384
DIL_PAIRS = ((128, 1), (512, 4), (2048, 16))
N_DIL = 3
GROUP_WIDTH = 128
CONV_WIDTH = 384
CONV_K = 3
ROPE_DIM = 16
ROPE_THETA = 500000.0
D_FF = 4096
IN_COLS = 2560
MIX_OUT = 768
EPS = 1e-6
NEG_INF = -1e30
PAST_LEN = 8192
N_BACK = 128

LANES = 128
HIST_ROWS = 16
ATTN_TILE = 128
ATTN_UNROLL = 16
PROJ_TILE = 1024
PROJ_SUB = 256
MIX_S_SEQS = 4
VMEM_LIMIT = 56 * 1024 * 1024

F32 = jnp.float32
BF16 = jnp.bfloat16

_Q_OFF = POOL_WIDTH
_K_OFF = _Q_OFF + ATTN_WIDTH
_V_OFF = _K_OFF + ATTN_WIDTH
_GB_OFF = _V_OFF + ATTN_WIDTH
_GC_OFF = _GB_OFF + CONV_WIDTH
_GH_OFF = _GC_OFF + CONV_WIDTH

_NT = (((1,), (1,)), ((), ()))


def _const_spec(shape):
    return pl.BlockSpec(shape, lambda *_: (0,) * len(shape), pipeline_mode=pl.Buffered(1))


def _layer_spec(shape, layer):
    return pl.BlockSpec((None, *shape), lambda *_: (layer,) + (0,) * len(shape), pipeline_mode=pl.Buffered(1))


def _params():
    return pltpu.CompilerParams(dimension_semantics=("arbitrary",), vmem_limit_bytes=VMEM_LIMIT)


def _proj_kernel(x_ref, g1_ref, w_ref, qg_ref, kg_ref, rc_ref, ra_ref, rb_ref, hm_ref, *rest,
                 class_major, tiles, sub, stacked):
    prev_kvt = None
    if class_major:
        if stacked:
            prev_kvt, rest = rest[:N_DIL], rest[N_DIL:]
        (u_ref, gb_ref, z_ref, qc0, qc1, qc2, kvc0, kvc1, kvc2, kvt0, kvt1, kvt2, nat) = rest
    else:
        (u_ref, gb_ref, z_ref, q0, q1, q2, kv0, kv1, kv2) = rest
    tm = x_ref.shape[0]
    hm = hm_ref[...]
    last_tile = pl.program_id(0) % tiles == tiles - 1

    for s in range(tm // sub):
        r0 = s * sub
        rows = slice(r0, r0 + sub)
        x = x_ref[rows, :]
        ms = jnp.mean(x * x, axis=-1, keepdims=True)
        hb = (x * lax.rsqrt(ms + EPS) * g1_ref[...]).astype(BF16)
        p = jnp.dot(hb, w_ref[...], preferred_element_type=F32)
        rc, ra, rb = rc_ref[rows, :], ra_ref[rows, :], rb_ref[rows, :]

        def head_norm_rope(xb, gain):
            sq = xb * xb
            hi = sq.astype(BF16)
            lo = (sq - hi.astype(F32)).astype(BF16)
            msq = (jnp.dot(hi, hm, preferred_element_type=F32) + jnp.dot(lo, hm, preferred_element_type=F32))
            xn = xb * lax.rsqrt(msq + EPS) * gain
            return xn * rc + pltpu.roll(xn, LANES - ROPE_DIM // 2, 1) * ra + pltpu.roll(xn, ROPE_DIM // 2, 1) * rb

        u_ref[rows, :] = p[:, :POOL_WIDTH]
        gb_ref[rows, :] = p[:, _GB_OFF:_GC_OFF]
        z_ref[rows, :] = p[:, _GC_OFF:_GH_OFF] * p[:, _GH_OFF:]

        for g, (window, dil) in enumerate(DIL_PAIRS):
            lanes = slice(g * GROUP_WIDTH, (g + 1) * GROUP_WIDTH)
            qn = head_norm_rope(p[:, _Q_OFF:_K_OFF][:, lanes], qg_ref[...])
            kn = head_norm_rope(p[:, _K_OFF:_V_OFF][:, lanes], kg_ref[...])
            vv = p[:, _V_OFF:_GB_OFF][:, lanes]
            if not class_major:
                q_ref, kv_ref = ((q0, kv0), (q1, kv1), (q2, kv2))[g]
                q_ref[rows, :] = qn
                kv_ref[rows, :GROUP_WIDTH] = kn
                kv_ref[rows, GROUP_WIDTH:] = vv
                continue
            qc_ref, kvc_ref, kvt_ref = ((qc0, kvc0, kvt0), (qc1, kvc1, kvt1), (qc2, kvc2, kvt2))[g]
            slab = 3 * (s * N_DIL + g)
            nat[slab] = qn
            nat[slab + 1] = kn
            nat[slab + 2] = vv
            per = sub // dil
            dst = slice(s * per, (s + 1) * per)
            for r in range(dil):
                src = pl.ds(r, per, stride=dil) if dil > 1 else slice(None)
                qc_ref[0, r, dst, :] = nat[slab, src, :].astype(BF16)
                kvc_ref[0, r, dst, :GROUP_WIDTH] = nat[slab + 1, src, :].astype(BF16)
                kvc_ref[0, r, dst, GROUP_WIDTH:] = nat[slab + 2, src, :].astype(BF16)
            def put_kvt(cols, part, g=g, kvt_ref=kvt_ref):
                if stacked:
                    kvt_ref[0, 0, :, cols] = prev_kvt[g][0, :, cols]
                    kvt_ref[1, 0, :, cols] = part
                else:
                    kvt_ref[0, :, cols] = part

            if window >= tiles * tm:
                put_kvt(rows, jnp.concatenate([kn, vv], axis=1).T)
            else:
                first_kept = tm - min(window, tm)
                lo = max(r0, first_kept)
                if lo < r0 + sub:
                    @pl.when(last_tile)
                    def _(kn=kn, vv=vv, put_kvt=put_kvt, lo=lo, r0=r0, first_kept=first_kept):
                        part = jnp.concatenate([kn[lo - r0:], vv[lo - r0:]], axis=1).T
                        put_kvt(slice(lo - first_kept, r0 + sub - first_kept), part)


def _proj(x, g1, w_in, qg, kg, rope, hm, *, layer, tm, batch=None, seq=None, prev_kvt=None):
    rows = x.shape[0]
    stacked = prev_kvt is not None
    n_tab = rope[0].shape[0] // tm
    class_major = batch is not None
    tiles = seq // tm if class_major else 1
    row = lambda c: pl.BlockSpec((tm, c), lambda i: (i, 0))
    tab = pl.BlockSpec((tm, LANES), lambda i: (i % n_tab, 0))
    out_shape = [jax.ShapeDtypeStruct((rows, POOL_WIDTH), F32)] + [jax.ShapeDtypeStruct((rows, CONV_WIDTH), F32)] * 2
    out_specs = [row(POOL_WIDTH), row(CONV_WIDTH), row(CONV_WIDTH)]
    scratch = []
    prev_specs = []
    if class_major:
        for width in (GROUP_WIDTH, 2 * GROUP_WIDTH):
            for (_, dil) in DIL_PAIRS:
                out_shape.append(jax.ShapeDtypeStruct((batch, dil, seq // dil, width), BF16))
                out_specs.append(pl.BlockSpec((1, dil, tm // dil, width), lambda i: (i // tiles, 0, i % tiles, 0)))
        for (window, _) in DIL_PAIRS:
            assert window >= seq or window <= tm, "kv buffer rows must come from the last row tile"
            cols = tm if window >= seq else window
            col_blk = (lambda i: i % tiles) if window >= seq else (lambda i: 0)
            kvt_spec = pl.BlockSpec((1, 2 * GROUP_WIDTH, cols), lambda i, c=col_blk: (i // tiles, 0, c(i)))
            if stacked:
                prev_specs.append(kvt_spec)
                out_shape.append(jax.ShapeDtypeStruct((2, batch, 2 * GROUP_WIDTH, window), F32))
                out_specs.append(pl.BlockSpec((2, 1, 2 * GROUP_WIDTH, cols),
                                              lambda i, c=col_blk: (0, i // tiles, 0, c(i))))
            else:
                out_shape.append(jax.ShapeDtypeStruct((batch, 2 * GROUP_WIDTH, window), F32))
                out_specs.append(kvt_spec)
        scratch = [pltpu.VMEM((3 * N_DIL * (tm // PROJ_SUB), PROJ_SUB, GROUP_WIDTH), F32)]
    else:
        out_shape += [jax.ShapeDtypeStruct((rows, GROUP_WIDTH), F32)] * N_DIL
        out_shape += [jax.ShapeDtypeStruct((rows, 2 * GROUP_WIDTH), F32)] * N_DIL
        out_specs += [row(GROUP_WIDTH)] * N_DIL + [row(2 * GROUP_WIDTH)] * N_DIL
    return pl.pallas_call(
        functools.partial(_proj_kernel, class_major=class_major, tiles=tiles, sub=PROJ_SUB, stacked=stacked),
        grid=(rows // tm,),
        in_specs=[row(D_MODEL), _const_spec((1, D_MODEL)), _layer_spec((D_MODEL, IN_COLS), layer),
                  _const_spec((1, LANES)), _const_spec((1, LANES)), tab, tab, tab,
                  _const_spec((LANES, LANES))] + prev_specs,
        out_specs=out_specs,
        out_shape=out_shape,
        scratch_shapes=scratch,
        compiler_params=_params(),
        name="proj",
    )(x, g1, w_in, qg, kg, *rope, hm, *(prev_kvt or ()))


def _lane_lt(shape, bound):
    return lax.broadcasted_iota(jnp.int32, shape, len(shape) - 1) < bound


def _stack_heads(q):
    first = _lane_lt(q.shape, HEAD_DIM)
    zero = jnp.zeros_like(q)
    return jnp.concatenate([jnp.where(first, q, zero), jnp.where(first, zero, q)], axis=0)


def _unstack(o2, l2, m2):
    n = o2.shape[0] // 2
    first = _lane_lt((n, GROUP_WIDTH), HEAD_DIM)
    return tuple(jnp.where(first, a[:n], a[n:]) for a in (o2, l2, m2))


def _merge(a, b):
    m = jnp.maximum(a[2], b[2])
    wa = jnp.exp(a[2] - m)
    wb = jnp.exp(b[2] - m)
    return wa * a[0] + wb * b[0], wa * a[1] + wb * b[1], m


def _attn_unit(q, kv, bias):
    nk = kv.shape[0]
    k = kv[:, :GROUP_WIDTH]
    v1 = jnp.concatenate([kv[:, GROUP_WIDTH:], jnp.ones((nk, GROUP_WIDTH), BF16)], axis=1)
    s = lax.dot_general(_stack_heads(q), k, _NT, preferred_element_type=F32) + bias
    mrow = jnp.max(s, axis=-1, keepdims=True)
    p = jnp.exp(s - mrow).astype(BF16)
    r = jnp.dot(p, v1, preferred_element_type=F32)
    return _unstack(r[:, :GROUP_WIDTH], r[:, GROUP_WIDTH:], mrow)


def _attn_p_kernel(qc0, qc1, qc2, kvc0, kvc1, kvc2, y_ref, acc_o, acc_l, acc_m, tmp_o, tmp_l, tmp_m, bias_ref,
                   *, seq):
    @pl.when(pl.program_id(0) == 0)
    def _():
        qi = lax.broadcasted_iota(jnp.int32, (2 * ATTN_TILE, 2 * ATTN_TILE), 0) & (ATTN_TILE - 1)
        ki = lax.broadcasted_iota(jnp.int32, (2 * ATTN_TILE, 2 * ATTN_TILE), 1)
        for first_key_back in (0, 1):
            dist = qi - ki + first_key_back * ATTN_TILE
            bias_ref[first_key_back] = jnp.where((dist >= 0) & (dist <= N_BACK), 0.0, NEG_INF)

    for g, ((_, dil), q_ref, kv_ref) in enumerate(zip(DIL_PAIRS, (qc0, qc1, qc2), (kvc0, kvc1, kvc2))):
        dst = (acc_o, acc_l, acc_m) if g == 0 else (tmp_o, tmp_l, tmp_m)
        n_sub = seq // dil // ATTN_TILE

        def unit(idx, q_ref=q_ref, kv_ref=kv_ref, n_sub=n_sub, dil=dil):
            if n_sub == 1:
                r, sub = idx, 0
                res = _attn_unit(q_ref[0, r], kv_ref[0, r], bias_ref[0, :, :ATTN_TILE])
            else:
                r, sub = (idx // n_sub, idx % n_sub) if dil > 1 else (0, idx)
                back = jnp.minimum(sub, 1)
                q0 = pl.multiple_of(sub * ATTN_TILE, ATTN_TILE)
                k0 = pl.multiple_of((sub - back) * ATTN_TILE, ATTN_TILE)
                res = _attn_unit(q_ref[0, r, pl.ds(q0, ATTN_TILE), :], kv_ref[0, r, pl.ds(k0, 2 * ATTN_TILE), :],
                                 bias_ref[back])
            start = sub * (ATTN_TILE * dil) + r
            rows = pl.ds(start, ATTN_TILE, stride=dil) if dil > 1 else pl.ds(pl.multiple_of(start, ATTN_TILE), ATTN_TILE)
            return rows, res

        def units_body(it, carry, unit=unit, dst=dst):
            for j in range(ATTN_UNROLL):
                rows, res = unit(it * ATTN_UNROLL + j)
                for ref, val in zip(dst, res):
                    ref[rows, :] = val
            return carry
        lax.fori_loop(0, dil * n_sub // ATTN_UNROLL, units_body, 0)

        if g > 0:
            def merge_body(c, carry):
                rows = pl.ds(pl.multiple_of(c * 256, 256), 256)
                o, l, m = _merge((acc_o[rows, :], acc_l[rows, :], acc_m[rows, :]),
                                 (tmp_o[rows, :], tmp_l[rows, :], tmp_m[rows, :]))
                acc_o[rows, :] = o
                acc_l[rows, :] = l
                acc_m[rows, :] = m
                return carry
            lax.fori_loop(0, seq // 256, merge_body, 0)

    def out_body(c, carry):
        rows = pl.ds(pl.multiple_of(c * 256, 256), 256)
        y_ref[0, rows, :] = (acc_o[rows, :] / acc_l[rows, :]).astype(y_ref.dtype)
        return carry
    lax.fori_loop(0, seq // 256, out_body, 0)


def _attn_p(qcs, kvcs, *, batch, seq):
    in_specs = [pl.BlockSpec((1,) + a.shape[1:], lambda b: (b, 0, 0, 0)) for a in (*qcs, *kvcs)]
    acc = pltpu.VMEM((seq, GROUP_WIDTH), F32)
    return pl.pallas_call(
        functools.partial(_attn_p_kernel, seq=seq),
        grid=(batch,),
        in_specs=in_specs,
        out_specs=pl.BlockSpec((1, seq, GROUP_WIDTH), lambda b: (b, 0, 0)),
        out_shape=jax.ShapeDtypeStruct((batch, seq, GROUP_WIDTH), BF16),
        scratch_shapes=[acc] * 6 + [pltpu.VMEM((2, 2 * ATTN_TILE, 2 * ATTN_TILE), F32)],
        compiler_params=_params(),
        name="attn_p",
    )(*qcs, *kvcs)


def _pool_rows(ext, pos, pw, ps):
    s2 = ext + pltpu.roll(ext, 1, 0)
    s4 = s2 + pltpu.roll(s2, 2, 0)
    s8 = s4 + pltpu.roll(s4, 4, 0)
    s16 = s8 + pltpu.roll(s8, 8, 0)
    lane = lax.broadcasted_iota(jnp.int32, (1, POOL_WIDTH), 1)
    grp = [lane < (j + 1) * POOL_GROUP for j in range(3)]
    win = jnp.where(grp[0], s2, jnp.where(grp[1], s4, jnp.where(grp[2], s8, s16)))[HIST_ROWS:]
    width = jnp.where(grp[0], POOL_WINDOWS[0], jnp.where(grp[1], POOL_WINDOWS[1],
                      jnp.where(grp[2], POOL_WINDOWS[2], POOL_WINDOWS[3])))
    cnt = jnp.minimum(pos + 1, width).astype(F32)
    d = win / cnt - ext[HIST_ROWS:]
    return jnp.dot(d.astype(BF16), pw, preferred_element_type=F32) * ps


def _conv_rows(ext, gb, cw, hist):
    y = cw[0:1] * pltpu.roll(ext, 2, 0) + cw[1:2] * pltpu.roll(ext, 1, 0) + cw[2:3] * ext
    return gb * y[hist:]


def _attn_s_group(q, new_t, cache_t, window, dil, shifted):
    t = q.shape[0]
    q2 = _stack_heads(q).astype(BF16)
    s_c = jnp.dot(q2, cache_t[:GROUP_WIDTH].astype(BF16), preferred_element_type=F32)
    s_n = jnp.dot(q2, new_t[:GROUP_WIDTH].astype(BF16), preferred_element_type=F32)
    tq = lax.broadcasted_iota(jnp.int32, (2 * t, 1), 0) & (t - 1)
    rc = lax.broadcasted_iota(jnp.int32, (1, window), 1)
    if shifted:
        rc = (rc + t) & (window - 1)
    dist_c = window + tq - rc
    ok_c = (dist_c <= window) & ((dist_c & (dil - 1)) == 0) & (PAST_LEN - window + rc >= 0)
    tn = lax.broadcasted_iota(jnp.int32, (1, LANES), 1) - (LANES - t)
    dist_n = tq - tn
    ok_n = (tn >= 0) & (dist_n >= 0) & ((dist_n & (dil - 1)) == 0)
    s_c = jnp.where(ok_c, s_c, NEG_INF)
    s_n = jnp.where(ok_n, s_n, NEG_INF)
    mrow = jnp.maximum(jnp.max(s_c, axis=-1, keepdims=True), jnp.max(s_n, axis=-1, keepdims=True))
    p_c = jnp.exp(s_c - mrow)
    p_n = jnp.exp(s_n - mrow)
    o2 = (lax.dot_general(p_c.astype(BF16), cache_t[GROUP_WIDTH:].astype(BF16), _NT, preferred_element_type=F32)
          + lax.dot_general(p_n.astype(BF16), new_t[GROUP_WIDTH:].astype(BF16), _NT, preferred_element_type=F32))
    l2 = jnp.sum(p_c, axis=-1, keepdims=True) + jnp.sum(p_n, axis=-1, keepdims=True)
    return _unstack(o2, l2, mrow)


def _mix_s_kernel(u_ref, q0_ref, q1_ref, q2_ref, kn0_ref, kn1_ref, kn2_ref, gb_ref, z_ref, pst_ref, cst_ref,
                  pw_ref, ps_ref, cw_ref, *rest, t_new, first, n_per):
    yp_ref, ya_ref, yc_ref, nc0_ref, nc1_ref, nc2_ref = rest[-6:]
    cache_refs = rest[:-6]
    keep_old = _lane_lt((2 * GROUP_WIDTH, LANES), LANES - t_new)
    pos = PAST_LEN + lax.broadcasted_iota(jnp.int32, (t_new, 1), 0)
    for i in range(n_per):
        state = None
        for g, ((window, dil), q_ref, kn_ref, nc_ref) in enumerate(zip(
                DIL_PAIRS, (q0_ref, q1_ref, q2_ref), (kn0_ref, kn1_ref, kn2_ref), (nc0_ref, nc1_ref, nc2_ref))):
            kv_new = kn_ref[i]
            new_t = jnp.concatenate([jnp.zeros((LANES - t_new, 2 * GROUP_WIDTH), F32), kv_new], axis=0).T
            cache_t = cache_refs[g][0, i]
            part = _attn_s_group(q_ref[i], new_t, cache_t, window, dil, shifted=not first)
            state = part if state is None else _merge(state, part)
            rolled = pltpu.roll(cache_t, window - t_new, 1) if first else cache_t
            last = jnp.where(keep_old, rolled[:, window - LANES:], new_t)
            if first:
                if window > LANES:
                    nc_ref[0, i, :, :window - LANES] = rolled[:, :window - LANES]
                nc_ref[0, i, :, window - LANES:] = last
                nc_ref[1, i] = pltpu.roll(cache_refs[N_DIL + g][0, i], window - t_new, 1)
            else:
                nc_ref[0, i] = last
        ya_ref[i] = state[0] / state[1]

        u_ext = jnp.concatenate([pst_ref[i], u_ref[i]], axis=0)
        yp_ref[i] = _pool_rows(u_ext, pos, pw_ref[...], ps_ref[...])
        z_ext = jnp.concatenate([cst_ref[i], z_ref[i]], axis=0)
        yc_ref[i] = _conv_rows(z_ext, gb_ref[i], cw_ref[...], cst_ref.shape[1])


def _mix_s(u, qs, kns, gb, z, pst, cst, caches_t, pw, ps, cw, prev, *, n_seq, t_new):
    first = prev is None
    n_per = 1 if first else MIX_S_SEQS
    blk = lambda r, c: pl.BlockSpec((n_per, r, c), lambda b: (b, 0, 0))
    cache_blk = lambda layer, w: pl.BlockSpec((1, n_per, 2 * GROUP_WIDTH, w), lambda b: (layer, b, 0, 0))
    three = lambda a: a.reshape(n_seq, -1, a.shape[-1])
    windows = [w for (w, _) in DIL_PAIRS]
    in_specs = ([blk(t_new, POOL_WIDTH)] + [blk(t_new, GROUP_WIDTH)] * N_DIL + [blk(t_new, 2 * GROUP_WIDTH)] * N_DIL
                + [blk(t_new, CONV_WIDTH)] * 2 + [blk(pst.shape[1], POOL_WIDTH), blk(cst.shape[1], CONV_WIDTH)]
                + [_const_spec((POOL_WIDTH, POOL_WIDTH)), _const_spec((1, POOL_WIDTH)),
                   _const_spec((CONV_K, CONV_WIDTH))])
    args = [three(u), *[three(q) for q in qs], *[three(k) for k in kns], three(gb), three(z), pst, cst, pw, ps, cw]
    aliases = {}
    if first:
        assert all(c.shape[0] == 2 for c in caches_t), "stacked kv buffer update is written for two layers"
        in_specs += [cache_blk(0, w) for w in windows] + [cache_blk(1, w) for w in windows]
        args += [*caches_t, *caches_t]
        cache_out = [pl.BlockSpec((2, 1, 2 * GROUP_WIDTH, w), lambda b: (0, b, 0, 0)) for w in windows]
    else:
        for k, (p, w) in enumerate(zip(prev, windows)):
            aliases[len(args)] = 3 + k
            in_specs.append(cache_blk(1, w))
            args.append(p)
        cache_out = [pl.BlockSpec((1, n_per, 2 * GROUP_WIDTH, LANES), lambda b, w=w: (1, b, 0, w // LANES - 1))
                     for w in windows]
    out_specs = [blk(t_new, POOL_WIDTH), blk(t_new, GROUP_WIDTH), blk(t_new, CONV_WIDTH)] + cache_out
    out_shape = ([jax.ShapeDtypeStruct((n_seq, t_new, c), F32) for c in (POOL_WIDTH, GROUP_WIDTH, CONV_WIDTH)]
                 + [jax.ShapeDtypeStruct(c.shape, F32) for c in caches_t])
    return pl.pallas_call(
        functools.partial(_mix_s_kernel, t_new=t_new, first=first, n_per=n_per),
        grid=(n_seq // n_per,),
        in_specs=in_specs,
        out_specs=out_specs,
        out_shape=out_shape,
        input_output_aliases=aliases,
        compiler_params=_params(),
        name="mix_s",
    )(*args)


def _mlp_kernel(x_ref, *refs, tf, tiles):
    if tiles:
        (u_ref, uh_ref, gb_ref, z_ref, zh_ref, ya_ref, pw_ref, ps_ref, cw_ref,
         wo_ref, g2_ref, wu_ref, wd_ref, o_ref) = refs
        tm = x_ref.shape[0]
        tile = pl.program_id(0) % tiles
        has_hist = tile > 0
        pos = tile * tm + lax.broadcasted_iota(jnp.int32, (tm, 1), 0)
        u_ext = jnp.concatenate([jnp.where(has_hist, uh_ref[...], 0.0), u_ref[...]], axis=0)
        z_ext = jnp.concatenate([jnp.where(has_hist, zh_ref[...], 0.0), z_ref[...]], axis=0)
        yp = _pool_rows(u_ext, pos, pw_ref[...], ps_ref[...])
        yc = _conv_rows(z_ext, gb_ref[...], cw_ref[...], HIST_ROWS)
    else:
        yp_ref, ya_ref, yc_ref, wo_ref, g2_ref, wu_ref, wd_ref, o_ref = refs
        yp, yc = yp_ref[...], yc_ref[...]
    mixed = jnp.concatenate([yp.astype(BF16), ya_ref[...].astype(BF16), yc.astype(BF16)], axis=1)
    x1 = x_ref[...] + jnp.dot(mixed, wo_ref[...], preferred_element_type=F32)
    ms = jnp.mean(x1 * x1, axis=-1, keepdims=True)
    hb = (x1 * lax.rsqrt(ms + EPS) * g2_ref[...]).astype(BF16)
    acc = x1
    for c in range(D_FF // tf):
        hf = jnp.dot(hb, wu_ref[:, c * tf:(c + 1) * tf], preferred_element_type=F32)
        act = jnp.square(jnp.maximum(hf, 0.0)).astype(BF16)
        acc = acc + jnp.dot(act, wd_ref[c * tf:(c + 1) * tf, :], preferred_element_type=F32)
    o_ref[...] = acc


def _mlp(x, mix_in, w_out, g2, w_up, w_down, *, layer, tm, local=None):
    rows = x.shape[0]
    row = lambda c: pl.BlockSpec((tm, c), lambda i: (i, 0))
    weights = [_layer_spec((MIX_OUT, D_MODEL), layer), _const_spec((1, D_MODEL)),
               _layer_spec((D_MODEL, D_FF), layer), _layer_spec((D_FF, D_MODEL), layer)]
    if local is None:
        tiles = 0
        in_specs = [row(D_MODEL), row(POOL_WIDTH), row(GROUP_WIDTH), row(CONV_WIDTH)] + weights
        args = (x, *mix_in, w_out, g2, w_up, w_down)
    else:
        u, gb, z, ya = mix_in
        pw, ps, cw, seq = local
        tiles = seq // tm
        hist = lambda c: pl.BlockSpec((HIST_ROWS, c), lambda i: (jnp.maximum(i * (tm // HIST_ROWS) - 1, 0), 0))
        in_specs = [row(D_MODEL), row(POOL_WIDTH), hist(POOL_WIDTH), row(CONV_WIDTH), row(CONV_WIDTH),
                    hist(CONV_WIDTH), row(GROUP_WIDTH), _const_spec((POOL_WIDTH, POOL_WIDTH)),
                    _const_spec((1, POOL_WIDTH)), _const_spec((CONV_K, CONV_WIDTH))] + weights
        args = (x, u, u, gb, z, z, ya, pw, ps, cw, w_out, g2, w_up, w_down)
    return pl.pallas_call(
        functools.partial(_mlp_kernel, tf=512, tiles=tiles),
        grid=(rows // tm,),
        in_specs=in_specs,
        out_specs=row(D_MODEL),
        out_shape=jax.ShapeDtypeStruct((rows, D_MODEL), F32),
        compiler_params=_params(),
        name="mlp",
    )(*args)


def _rope_tables(pos):
    half = ROPE_DIM // 2
    inv = jnp.power(jnp.float32(ROPE_THETA), -jnp.arange(half, dtype=F32) / half)
    ang = pos.astype(F32)[:, None] * inv[None, :]
    cos, sin = jnp.cos(ang), jnp.sin(ang)
    n = pos.shape[0]
    rest = HEAD_DIM - ROPE_DIM
    zh = jnp.zeros((n, half), F32)
    c = jnp.concatenate([cos, cos, jnp.ones((n, rest), F32)], axis=1)
    a = jnp.concatenate([-sin, zh, jnp.zeros((n, rest), F32)], axis=1)
    b = jnp.concatenate([zh, sin, jnp.zeros((n, rest), F32)], axis=1)
    return tuple(jnp.tile(t, (1, GROUP_WIDTH // HEAD_DIM)) for t in (c, a, b))


def _to_buffer_layout(c):
    lead = c.shape[:-4]
    n = len(lead)
    t = jnp.transpose(c, (*range(n), n + 1, n + 2, n + 3, n))
    return t.reshape(*lead, 2 * GROUP_WIDTH, c.shape[-4])


def _from_buffer_layout(t):
    lead = t.shape[:-2]
    n = len(lead)
    c = t.reshape(*lead, 2, 2, HEAD_DIM, t.shape[-1])
    return jnp.transpose(c, (*range(n), n + 3, n, n + 1, n + 2))


def kernel(x_prompt, x_sample, state_pool, state_conv, cache_kv_w128, cache_kv_w512, cache_kv_w2048,
           norm1_g, w_in, q_norm_g, k_norm_g, pool_w, pool_scale, conv_w, w_out, norm2_g, w_up, w_down):
    batch, seq, _ = x_prompt.shape
    n_seq, t_new, _ = x_sample.shape
    depth = w_in.shape[0]
    assert depth == 2, "the in-place stacking of the kv state outputs is written for two layers"
    caches_t = [_to_buffer_layout(c) for c in (cache_kv_w128, cache_kv_w512, cache_kv_w2048)]

    rope_p = _rope_tables(jnp.arange(seq, dtype=jnp.int32))
    rope_s = _rope_tables(jnp.tile(PAST_LEN + jnp.arange(t_new, dtype=jnp.int32), n_seq))
    head_id = jnp.arange(LANES) // HEAD_DIM
    hm = jnp.where(head_id[:, None] == head_id[None, :], 1.0 / HEAD_DIM, 0.0).astype(BF16)
    two_heads = lambda gain: jnp.tile(gain, GROUP_WIDTH // HEAD_DIM)[None, :]

    xp = x_prompt.reshape(batch * seq, D_MODEL)
    xs = x_sample.reshape(n_seq * t_new, D_MODEL)
    outs = {k: [] for k in ("pool_p", "conv_p", "pool_s", "conv_s")}
    w_in_b, w_out_b, w_up_b, w_down_b = (w.astype(BF16) for w in (w_in, w_out, w_up, w_down))
    new_caches = None
    kvts = None
    for layer in range(depth):
        pw = jax.scipy.linalg.block_diag(*[pool_w[layer, j] for j in range(len(POOL_WINDOWS))]).astype(BF16)
        ps = pool_scale[layer][None, :]
        cw = conv_w[layer]
        g1 = norm1_g[layer][None, :]
        g2 = norm2_g[layer][None, :]
        qg = two_heads(q_norm_g[layer]) * (HEAD_DIM ** -0.5)
        kg = two_heads(k_norm_g[layer])

        (u, gb, z, qc0, qc1, qc2, kvc0, kvc1, kvc2, *kvts) = _proj(
            xp, g1, w_in_b, qg, kg, rope_p, hm, layer=layer, tm=PROJ_TILE, batch=batch, seq=seq, prev_kvt=kvts)
        ya = _attn_p((qc0, qc1, qc2), (kvc0, kvc1, kvc2), batch=batch, seq=seq)
        xp = _mlp(xp, (u, gb, z, ya.reshape(-1, GROUP_WIDTH)), w_out_b, g2, w_up_b, w_down_b, layer=layer, tm=512,
                  local=(pw, ps, cw, seq))
        outs["pool_p"].append(u.reshape(batch, seq, POOL_WIDTH)[:, seq - POOL_HIST:])
        outs["conv_p"].append(z.reshape(batch, seq, CONV_WIDTH)[:, seq - (CONV_K - 1):])

        us, gbs, zs, q0, q1, q2, kn0, kn1, kn2 = _proj(xs, g1, w_in_b, qg, kg, rope_s, hm, layer=layer,
                                                       tm=n_seq * t_new)
        pst = jnp.pad(state_pool[layer], ((0, 0), (HIST_ROWS - POOL_HIST, 0), (0, 0)))
        cst = jnp.pad(state_conv[layer], ((0, 0), (8 - (CONV_K - 1), 0), (0, 0)))
        yps, yas, ycs, *new_caches = _mix_s(us, (q0, q1, q2), (kn0, kn1, kn2), gbs, zs, pst, cst, caches_t,
                                           pw, ps, cw, new_caches, n_seq=n_seq, t_new=t_new)
        xs = _mlp(xs, (yps.reshape(-1, POOL_WIDTH), yas.reshape(-1, GROUP_WIDTH), ycs.reshape(-1, CONV_WIDTH)),
                  w_out_b, g2, w_up_b, w_down_b, layer=layer, tm=n_seq * t_new)
        us3 = us.reshape(n_seq, t_new, POOL_WIDTH)
        zs3 = zs.reshape(n_seq, t_new, CONV_WIDTH)
        outs["pool_s"].append(jnp.concatenate([state_pool[layer], us3], axis=1)[:, -POOL_HIST:])
        outs["conv_s"].append(jnp.concatenate([state_conv[layer], zs3], axis=1)[:, -(CONV_K - 1):])

    st = lambda k: jnp.stack(outs[k])
    return (xp.reshape(batch, seq, D_MODEL), xs.reshape(n_seq, t_new, D_MODEL),
            st("pool_p"), st("conv_p"),
            *[_from_buffer_layout(kvt) for kvt in kvts],
            st("pool_s"), st("conv_s"),
            *[_from_buffer_layout(c) for c in new_caches])
```

```python
import functools

import jax
import jax.numpy as jnp
from jax import lax
from jax.experimental import pallas as pl
from jax.experimental.pallas import tpu as pltpu

D_MODEL = 1024
HEAD_DIM = 64
POOL_WIDTH = 256
POOL_WINDOWS = (2, 4, 8, 16)
POOL_GROUP = 64
POOL_HIST = 15
ATTN_WIDTH = 384
DIL_PAIRS = ((128, 1), (512, 4), (2048, 16))
N_DIL = 3
GROUP_WIDTH = 128
CONV_WIDTH = 384
CONV_K = 3
ROPE_DIM = 16
ROPE_THETA = 500000.0
D_FF = 4096
IN_COLS = 2560
MIX_OUT = 768
EPS = 1e-6
NEG_INF = -1e30
PAST_LEN = 8192
N_BACK = 128

LANES = 128
HIST_ROWS = 16
ATTN_TILE = 128
ATTN_UNROLL = 16
PROJ_TILE = 1024
PROJ_SUB = 512
MIX_S_SEQS = 4
VMEM_LIMIT = 56 * 1024 * 1024

F32 = jnp.float32
BF16 = jnp.bfloat16

_Q_OFF = POOL_WIDTH
_K_OFF = _Q_OFF + ATTN_WIDTH
_V_OFF = _K_OFF + ATTN_WIDTH
_GB_OFF = _V_OFF + ATTN_WIDTH
_GC_OFF = _GB_OFF + CONV_WIDTH
_GH_OFF = _GC_OFF + CONV_WIDTH

_NT = (((1,), (1,)), ((), ()))


def _const_spec(shape):
    return pl.BlockSpec(shape, lambda *_: (0,) * len(shape), pipeline_mode=pl.Buffered(1))


def _layer_spec(shape, layer):
    return pl.BlockSpec((None, *shape), lambda *_: (layer,) + (0,) * len(shape), pipeline_mode=pl.Buffered(1))


def _params():
    return pltpu.CompilerParams(dimension_semantics=("arbitrary",), vmem_limit_bytes=VMEM_LIMIT)


def _proj_kernel(x_ref, g1_ref, w_ref, qg_ref, kg_ref, rc_ref, ra_ref, rb_ref, hm_ref, *rest,
                 class_major, tiles, sub, stacked):
    prev_kvt = None
    if class_major:
        if stacked:
            prev_kvt, rest = rest[:N_DIL], rest[N_DIL:]
        (u_ref, gb_ref, z_ref, qc0, qc1, qc2, kvc0, kvc1, kvc2, kvt0, kvt1, kvt2, nat) = rest
    else:
        (u_ref, gb_ref, z_ref, q0, q1, q2, kv0, kv1, kv2) = rest
    tm = x_ref.shape[0]
    hm = hm_ref[...]
    last_tile = pl.program_id(0) % tiles == tiles - 1

    for s in range(tm // sub):
        r0 = s * sub
        rows = slice(r0, r0 + sub)
        x = x_ref[rows, :]
        ms = jnp.mean(x * x, axis=-1, keepdims=True)
        hb = (x * lax.rsqrt(ms + EPS) * g1_ref[...]).astype(BF16)
        p_att = jnp.dot(hb, w_ref[:, _Q_OFF:_GB_OFF], preferred_element_type=F32)
        rc, ra, rb = rc_ref[rows, :], ra_ref[rows, :], rb_ref[rows, :]

        def head_norm_rope(xb, gain):
            msq = jnp.dot((xb * xb).astype(BF16), hm, preferred_element_type=F32)
            xn = xb * lax.rsqrt(msq + EPS) * gain
            return xn * rc + pltpu.roll(xn, LANES - ROPE_DIM // 2, 1) * ra + pltpu.roll(xn, ROPE_DIM // 2, 1) * rb

        for g, (window, dil) in enumerate(DIL_PAIRS):
            lanes = slice(g * GROUP_WIDTH, (g + 1) * GROUP_WIDTH)
            qn = head_norm_rope(p_att[:, :ATTN_WIDTH][:, lanes], qg_ref[...])
            kn = head_norm_rope(p_att[:, ATTN_WIDTH:2 * ATTN_WIDTH][:, lanes], kg_ref[...])
            vv = p_att[:, 2 * ATTN_WIDTH:][:, lanes]
            if not class_major:
                q_ref, kv_ref = ((q0, kv0), (q1, kv1), (q2, kv2))[g]
                q_ref[rows, :] = qn
                kv_ref[rows, :GROUP_WIDTH] = kn
                kv_ref[rows, GROUP_WIDTH:] = vv
                continue
            qc_ref, kvc_ref, kvt_ref = ((qc0, kvc0, kvt0), (qc1, kvc1, kvt1), (qc2, kvc2, kvt2))[g]
            slab = 3 * (s * N_DIL + g)
            nat[slab] = qn
            nat[slab + 1] = kn
            nat[slab + 2] = vv
            per = sub // dil
            dst = slice(s * per, (s + 1) * per)
            for r in range(dil):
                src = pl.ds(r, per, stride=dil) if dil > 1 else slice(None)
                qc_ref[0, r, dst, :] = nat[slab, src, :].astype(BF16)
                kvc_ref[0, r, dst, :GROUP_WIDTH] = nat[slab + 1, src, :].astype(BF16)
                kvc_ref[0, r, dst, GROUP_WIDTH:] = nat[slab + 2, src, :].astype(BF16)
            def put_kvt(cols, part, g=g, kvt_ref=kvt_ref):
                if stacked:
                    kvt_ref[0, 0, :, cols] = prev_kvt[g][0, :, cols]
                    kvt_ref[1, 0, :, cols] = part
                else:
                    kvt_ref[0, :, cols] = part

            if window >= tiles * tm:
                put_kvt(rows, jnp.concatenate([kn, vv], axis=1).T)
            else:
                first_kept = tm - min(window, tm)
                lo = max(r0, first_kept)
                if lo < r0 + sub:
                    @pl.when(last_tile)
                    def _(kn=kn, vv=vv, put_kvt=put_kvt, lo=lo, r0=r0, first_kept=first_kept):
                        part = jnp.concatenate([kn[lo - r0:], vv[lo - r0:]], axis=1).T
                        put_kvt(slice(lo - first_kept, r0 + sub - first_kept), part)

        u_ref[rows, :] = jnp.dot(hb, w_ref[:, :POOL_WIDTH], preferred_element_type=F32)
        p_gate = jnp.dot(hb, w_ref[:, _GB_OFF:], preferred_element_type=F32)
        gb_ref[rows, :] = p_gate[:, :CONV_WIDTH]
        z_ref[rows, :] = p_gate[:, CONV_WIDTH:2 * CONV_WIDTH] * p_gate[:, 2 * CONV_WIDTH:]


def _proj(x, g1, w_in, qg, kg, rope, hm, *, layer, tm, batch=None, seq=None, prev_kvt=None):
    rows = x.shape[0]
    stacked = prev_kvt is not None
    sub = min(PROJ_SUB, tm)
    n_tab = rope[0].shape[0] // tm
    class_major = batch is not None
    tiles = seq // tm if class_major else 1
    row = lambda c: pl.BlockSpec((tm, c), lambda i: (i, 0))
    tab = pl.BlockSpec((tm, LANES), lambda i: (i % n_tab, 0))
    out_shape = [jax.ShapeDtypeStruct((rows, POOL_WIDTH), F32)] + [jax.ShapeDtypeStruct((rows, CONV_WIDTH), F32)] * 2
    out_specs = [row(POOL_WIDTH), row(CONV_WIDTH), row(CONV_WIDTH)]
    scratch = []
    prev_specs = []
    if class_major:
        for width in (GROUP_WIDTH, 2 * GROUP_WIDTH):
            for (_, dil) in DIL_PAIRS:
                out_shape.append(jax.ShapeDtypeStruct((batch, dil, seq // dil, width), BF16))
                out_specs.append(pl.BlockSpec((1, dil, tm // dil, width), lambda i: (i // tiles, 0, i % tiles, 0)))
        for (window, _) in DIL_PAIRS:
            assert window >= seq or window <= tm, "kv buffer rows must come from the last row tile"
            cols = tm if window >= seq else window
            col_blk = (lambda i: i % tiles) if window >= seq else (lambda i: 0)
            kvt_spec = pl.BlockSpec((1, 2 * GROUP_WIDTH, cols), lambda i, c=col_blk: (i // tiles, 0, c(i)))
            if stacked:
                prev_specs.append(kvt_spec)
                out_shape.append(jax.ShapeDtypeStruct((2, batch, 2 * GROUP_WIDTH, window), F32))
                out_specs.append(pl.BlockSpec((2, 1, 2 * GROUP_WIDTH, cols),
                                              lambda i, c=col_blk: (0, i // tiles, 0, c(i))))
            else:
                out_shape.append(jax.ShapeDtypeStruct((batch, 2 * GROUP_WIDTH, window), F32))
                out_specs.append(kvt_spec)
        scratch = [pltpu.VMEM((3 * N_DIL * (tm // sub), sub, GROUP_WIDTH), F32)]
    else:
        out_shape += [jax.ShapeDtypeStruct((rows, GROUP_WIDTH), F32)] * N_DIL
        out_shape += [jax.ShapeDtypeStruct((rows, 2 * GROUP_WIDTH), F32)] * N_DIL
        out_specs += [row(GROUP_WIDTH)] * N_DIL + [row(2 * GROUP_WIDTH)] * N_DIL
    return pl.pallas_call(
        functools.partial(_proj_kernel, class_major=class_major, tiles=tiles, sub=sub, stacked=stacked),
        grid=(rows // tm,),
        in_specs=[row(D_MODEL), _const_spec((1, D_MODEL)), _layer_spec((D_MODEL, IN_COLS), layer),
                  _const_spec((1, LANES)), _const_spec((1, LANES)), tab, tab, tab,
                  _const_spec((LANES, LANES))] + prev_specs,
        out_specs=out_specs,
        out_shape=out_shape,
        scratch_shapes=scratch,
        compiler_params=_params(),
        name="proj",
    )(x, g1, w_in, qg, kg, *rope, hm, *(prev_kvt or ()))


def _lane_lt(shape, bound):
    return lax.broadcasted_iota(jnp.int32, shape, len(shape) - 1) < bound


def _stack_heads(q):
    first = _lane_lt(q.shape, HEAD_DIM)
    zero = jnp.zeros_like(q)
    return jnp.concatenate([jnp.where(first, q, zero), jnp.where(first, zero, q)], axis=0)


def _unstack(o2, l2, m2):
    n = o2.shape[0] // 2
    first = _lane_lt((n, GROUP_WIDTH), HEAD_DIM)
    return tuple(jnp.where(first, a[:n], a[n:]) for a in (o2, l2, m2))


def _merge(a, b):
    m = jnp.maximum(a[2], b[2])
    wa = jnp.exp(a[2] - m)
    wb = jnp.exp(b[2] - m)
    return wa * a[0] + wb * b[0], wa * a[1] + wb * b[1], m


def _attn_unit(q, kv, bias):
    nk = kv.shape[0]
    k = kv[:, :GROUP_WIDTH]
    v1 = jnp.concatenate([kv[:, GROUP_WIDTH:], jnp.ones((nk, GROUP_WIDTH), BF16)], axis=1)
    s = lax.dot_general(_stack_heads(q), k, _NT, preferred_element_type=F32) + bias
    mrow = jnp.max(s, axis=-1, keepdims=True)
    p = jnp.exp(s - mrow).astype(BF16)
    r = jnp.dot(p, v1, preferred_element_type=F32)
    return _unstack(r[:, :GROUP_WIDTH], r[:, GROUP_WIDTH:], mrow)


def _attn_p_kernel(qc0, qc1, qc2, kvc0, kvc1, kvc2, y_ref, acc_o, acc_l, acc_m, tmp_o, tmp_l, tmp_m, bias_ref,
                   *, seq):
    @pl.when(pl.program_id(0) == 0)
    def _():
        qi = lax.broadcasted_iota(jnp.int32, (2 * ATTN_TILE, 2 * ATTN_TILE), 0) & (ATTN_TILE - 1)
        ki = lax.broadcasted_iota(jnp.int32, (2 * ATTN_TILE, 2 * ATTN_TILE), 1)
        for first_key_back in (0, 1):
            dist = qi - ki + first_key_back * ATTN_TILE
            bias_ref[first_key_back] = jnp.where((dist >= 0) & (dist <= N_BACK), 0.0, NEG_INF)

    for g, ((_, dil), q_ref, kv_ref) in enumerate(zip(DIL_PAIRS, (qc0, qc1, qc2), (kvc0, kvc1, kvc2))):
        dst = (acc_o, acc_l, acc_m) if g == 0 else (tmp_o, tmp_l, tmp_m)
        n_sub = seq // dil // ATTN_TILE

        def unit(idx, q_ref=q_ref, kv_ref=kv_ref, n_sub=n_sub, dil=dil):
            if n_sub == 1:
                r, sub = idx, 0
                res = _attn_unit(q_ref[0, r], kv_ref[0, r], bias_ref[0, :, :ATTN_TILE])
            else:
                r, sub = (idx // n_sub, idx % n_sub) if dil > 1 else (0, idx)
                back = jnp.minimum(sub, 1)
                q0 = pl.multiple_of(sub * ATTN_TILE, ATTN_TILE)
                k0 = pl.multiple_of((sub - back) * ATTN_TILE, ATTN_TILE)
                res = _attn_unit(q_ref[0, r, pl.ds(q0, ATTN_TILE), :], kv_ref[0, r, pl.ds(k0, 2 * ATTN_TILE), :],
                                 bias_ref[back])
            start = sub * (ATTN_TILE * dil) + r
            rows = pl.ds(start, ATTN_TILE, stride=dil) if dil > 1 else pl.ds(pl.multiple_of(start, ATTN_TILE), ATTN_TILE)
            return rows, res

        def units_body(it, carry, unit=unit, dst=dst):
            for j in range(ATTN_UNROLL):
                rows, res = unit(it * ATTN_UNROLL + j)
                for ref, val in zip(dst, res):
                    ref[rows, :] = val
            return carry
        lax.fori_loop(0, dil * n_sub // ATTN_UNROLL, units_body, 0)

        if g > 0:
            def merge_body(c, carry):
                rows = pl.ds(pl.multiple_of(c * 256, 256), 256)
                o, l, m = _merge((acc_o[rows, :], acc_l[rows, :], acc_m[rows, :]),
                                 (tmp_o[rows, :], tmp_l[rows, :], tmp_m[rows, :]))
                acc_o[rows, :] = o
                acc_l[rows, :] = l
                acc_m[rows, :] = m
                return carry
            lax.fori_loop(0, seq // 256, merge_body, 0)

    def out_body(c, carry):
        rows = pl.ds(pl.multiple_of(c * 256, 256), 256)
        y_ref[0, rows, :] = (acc_o[rows, :] / acc_l[rows, :]).astype(y_ref.dtype)
        return carry
    lax.fori_loop(0, seq // 256, out_body, 0)


def _attn_p(qcs, kvcs, *, batch, seq):
    in_specs = [pl.BlockSpec((1,) + a.shape[1:], lambda b: (b, 0, 0, 0)) for a in (*qcs, *kvcs)]
    acc = pltpu.VMEM((seq, GROUP_WIDTH), F32)
    return pl.pallas_call(
        functools.partial(_attn_p_kernel, seq=seq),
        grid=(batch,),
        in_specs=in_specs,
        out_specs=pl.BlockSpec((1, seq, GROUP_WIDTH), lambda b: (b, 0, 0)),
        out_shape=jax.ShapeDtypeStruct((batch, seq, GROUP_WIDTH), BF16),
        scratch_shapes=[acc] * 6 + [pltpu.VMEM((2, 2 * ATTN_TILE, 2 * ATTN_TILE), F32)],
        compiler_params=_params(),
        name="attn_p",
    )(*qcs, *kvcs)


def _pool_rows(ext, pos, pw, ps):
    s2 = ext + pltpu.roll(ext, 1, 0)
    s4 = s2 + pltpu.roll(s2, 2, 0)
    s8 = s4 + pltpu.roll(s4, 4, 0)
    s16 = s8 + pltpu.roll(s8, 8, 0)
    lane = lax.broadcasted_iota(jnp.int32, (1, POOL_WIDTH), 1)
    grp = [lane < (j + 1) * POOL_GROUP for j in range(3)]
    win = jnp.where(grp[0], s2, jnp.where(grp[1], s4, jnp.where(grp[2], s8, s16)))[HIST_ROWS:]
    width = jnp.where(grp[0], POOL_WINDOWS[0], jnp.where(grp[1], POOL_WINDOWS[1],
                      jnp.where(grp[2], POOL_WINDOWS[2], POOL_WINDOWS[3])))
    cnt = jnp.minimum(pos + 1, width).astype(F32)
    d = win / cnt - ext[HIST_ROWS:]
    return jnp.dot(d.astype(BF16), pw, preferred_element_type=F32) * ps


def _conv_rows(ext, gb, cw, hist):
    y = cw[0:1] * pltpu.roll(ext, 2, 0) + cw[1:2] * pltpu.roll(ext, 1, 0) + cw[2:3] * ext
    return gb * y[hist:]


def _attn_s_group(q, new_t, cache_t, window, dil, shifted):
    t = q.shape[0]
    q2 = _stack_heads(q).astype(BF16)
    s_c = jnp.dot(q2, cache_t[:GROUP_WIDTH].astype(BF16), preferred_element_type=F32)
    s_n = jnp.dot(q2, new_t[:GROUP_WIDTH].astype(BF16), preferred_element_type=F32)
    tq = lax.broadcasted_iota(jnp.int32, (2 * t, 1), 0) & (t - 1)
    rc = lax.broadcasted_iota(jnp.int32, (1, window), 1)
    if shifted:
        rc = (rc + t) & (window - 1)
    dist_c = window + tq - rc
    ok_c = (dist_c <= window) & ((dist_c & (dil - 1)) == 0) & (PAST_LEN - window + rc >= 0)
    tn = lax.broadcasted_iota(jnp.int32, (1, LANES), 1) - (LANES - t)
    dist_n = tq - tn
    ok_n = (tn >= 0) & (dist_n >= 0) & ((dist_n & (dil - 1)) == 0)
    s_c = jnp.where(ok_c, s_c, NEG_INF)
    s_n = jnp.where(ok_n, s_n, NEG_INF)
    mrow = jnp.maximum(jnp.max(s_c, axis=-1, keepdims=True), jnp.max(s_n, axis=-1, keepdims=True))
    p_c = jnp.exp(s_c - mrow)
    p_n = jnp.exp(s_n - mrow)
    o2 = (lax.dot_general(p_c.astype(BF16), cache_t[GROUP_WIDTH:].astype(BF16), _NT, preferred_element_type=F32)
          + lax.dot_general(p_n.astype(BF16), new_t[GROUP_WIDTH:].astype(BF16), _NT, preferred_element_type=F32))
    l2 = jnp.sum(p_c, axis=-1, keepdims=True) + jnp.sum(p_n, axis=-1, keepdims=True)
    return _unstack(o2, l2, mrow)


def _mix_s_kernel(u_ref, q0_ref, q1_ref, q2_ref, kn0_ref, kn1_ref, kn2_ref, gb_ref, z_ref, pst_ref, cst_ref,
                  pw_ref, ps_ref, cw_ref, *rest, t_new, first, n_per):
    yp_ref, ya_ref, yc_ref, nc0_ref, nc1_ref, nc2_ref = rest[-6:]
    cache_refs = rest[:-6]
    keep_old = _lane_lt((2 * GROUP_WIDTH, LANES), LANES - t_new)
    pos = PAST_LEN + lax.broadcasted_iota(jnp.int32, (t_new, 1), 0)
    for i in range(n_per):
        state = None
        for g, ((window, dil), q_ref, kn_ref, nc_ref) in enumerate(zip(
                DIL_PAIRS, (q0_ref, q1_ref, q2_ref), (kn0_ref, kn1_ref, kn2_ref), (nc0_ref, nc1_ref, nc2_ref))):
            kv_new = kn_ref[i]
            new_t = jnp.concatenate([jnp.zeros((LANES - t_new, 2 * GROUP_WIDTH), F32), kv_new], axis=0).T
            cache_t = cache_refs[g][0, i]
            part = _attn_s_group(q_ref[i], new_t, cache_t, window, dil, shifted=not first)
            state = part if state is None else _merge(state, part)
            rolled = pltpu.roll(cache_t, window - t_new, 1) if first else cache_t
            last = jnp.where(keep_old, rolled[:, window - LANES:], new_t)
            if first:
                if window > LANES:
                    nc_ref[0, i, :, :window - LANES] = rolled[:, :window - LANES]
                nc_ref[0, i, :, window - LANES:] = last
                nc_ref[1, i] = pltpu.roll(cache_refs[N_DIL + g][0, i], window - t_new, 1)
            else:
                nc_ref[0, i] = last
        ya_ref[i] = state[0] / state[1]

        u_ext = jnp.concatenate([pst_ref[i], u_ref[i]], axis=0)
        yp_ref[i] = _pool_rows(u_ext, pos, pw_ref[...], ps_ref[...])
        z_ext = jnp.concatenate([cst_ref[i], z_ref[i]], axis=0)
        yc_ref[i] = _conv_rows(z_ext, gb_ref[i], cw_ref[...], cst_ref.shape[1])


def _mix_s(u, qs, kns, gb, z, pst, cst, caches_t, pw, ps, cw, prev, *, n_seq, t_new):
    first = prev is None
    n_per = 1 if first else MIX_S_SEQS
    blk = lambda r, c: pl.BlockSpec((n_per, r, c), lambda b: (b, 0, 0))
    cache_blk = lambda layer, w: pl.BlockSpec((1, n_per, 2 * GROUP_WIDTH, w), lambda b: (layer, b, 0, 0))
    three = lambda a: a.reshape(n_seq, -1, a.shape[-1])
    windows = [w for (w, _) in DIL_PAIRS]
    in_specs = ([blk(t_new, POOL_WIDTH)] + [blk(t_new, GROUP_WIDTH)] * N_DIL + [blk(t_new, 2 * GROUP_WIDTH)] * N_DIL
                + [blk(t_new, CONV_WIDTH)] * 2 + [blk(pst.shape[1], POOL_WIDTH), blk(cst.shape[1], CONV_WIDTH)]
                + [_const_spec((POOL_WIDTH, POOL_WIDTH)), _const_spec((1, POOL_WIDTH)),
                   _const_spec((CONV_K, CONV_WIDTH))])
    args = [three(u), *[three(q) for q in qs], *[three(k) for k in kns], three(gb), three(z), pst, cst, pw, ps, cw]
    aliases = {}
    if first:
        assert all(c.shape[0] == 2 for c in caches_t), "stacked kv buffer update is written for two layers"
        in_specs += [cache_blk(0, w) for w in windows] + [cache_blk(1, w) for w in windows]
        args += [*caches_t, *caches_t]
        cache_out = [pl.BlockSpec((2, 1, 2 * GROUP_WIDTH, w), lambda b: (0, b, 0, 0)) for w in windows]
    else:
        for k, (p, w) in enumerate(zip(prev, windows)):
            aliases[len(args)] = 3 + k
            in_specs.append(cache_blk(1, w))
            args.append(p)
        cache_out = [pl.BlockSpec((1, n_per, 2 * GROUP_WIDTH, LANES), lambda b, w=w: (1, b, 0, w // LANES - 1))
                     for w in windows]
    out_specs = [blk(t_new, POOL_WIDTH), blk(t_new, GROUP_WIDTH), blk(t_new, CONV_WIDTH)] + cache_out
    out_shape = ([jax.ShapeDtypeStruct((n_seq, t_new, c), F32) for c in (POOL_WIDTH, GROUP_WIDTH, CONV_WIDTH)]
                 + [jax.ShapeDtypeStruct(c.shape, F32) for c in caches_t])
    return pl.pallas_call(
        functools.partial(_mix_s_kernel, t_new=t_new, first=first, n_per=n_per),
        grid=(n_seq // n_per,),
        in_specs=in_specs,
        out_specs=out_specs,
        out_shape=out_shape,
        input_output_aliases=aliases,
        compiler_params=_params(),
        name="mix_s",
    )(*args)


def _mlp_kernel(x_ref, *refs, tf, tiles):
    if tiles:
        (u_ref, uh_ref, gb_ref, z_ref, zh_ref, ya_ref, pw_ref, ps_ref, cw_ref,
         wo_ref, g2_ref, wu_ref, wd_ref, o_ref) = refs
        tm = x_ref.shape[0]
        tile = pl.program_id(0) % tiles
        has_hist = tile > 0
        pos = tile * tm + lax.broadcasted_iota(jnp.int32, (tm, 1), 0)
        u_ext = jnp.concatenate([jnp.where(has_hist, uh_ref[...], 0.0), u_ref[...]], axis=0)
        z_ext = jnp.concatenate([jnp.where(has_hist, zh_ref[...], 0.0), z_ref[...]], axis=0)
        yp = _pool_rows(u_ext, pos, pw_ref[...], ps_ref[...])
        yc = _conv_rows(z_ext, gb_ref[...], cw_ref[...], HIST_ROWS)
    else:
        yp_ref, ya_ref, yc_ref, wo_ref, g2_ref, wu_ref, wd_ref, o_ref = refs
        yp, yc = yp_ref[...], yc_ref[...]
    mixed = jnp.concatenate([yp.astype(BF16), ya_ref[...].astype(BF16), yc.astype(BF16)], axis=1)
    x1 = x_ref[...] + jnp.dot(mixed, wo_ref[...], preferred_element_type=F32)
    ms = jnp.mean(x1 * x1, axis=-1, keepdims=True)
    hb = (x1 * lax.rsqrt(ms + EPS) * g2_ref[...]).astype(BF16)
    acc = x1
    for c in range(D_FF // tf):
        hf = jnp.dot(hb, wu_ref[:, c * tf:(c + 1) * tf], preferred_element_type=F32)
        act = jnp.square(jnp.maximum(hf, 0.0)).astype(BF16)
        acc = acc + jnp.dot(act, wd_ref[c * tf:(c + 1) * tf, :], preferred_element_type=F32)
    o_ref[...] = acc


def _mlp(x, mix_in, w_out, g2, w_up, w_down, *, layer, tm, local=None):
    rows = x.shape[0]
    row = lambda c: pl.BlockSpec((tm, c), lambda i: (i, 0))
    weights = [_layer_spec((MIX_OUT, D_MODEL), layer), _const_spec((1, D_MODEL)),
               _layer_spec((D_MODEL, D_FF), layer), _layer_spec((D_FF, D_MODEL), layer)]
    if local is None:
        tiles = 0
        in_specs = [row(D_MODEL), row(POOL_WIDTH), row(GROUP_WIDTH), row(CONV_WIDTH)] + weights
        args = (x, *mix_in, w_out, g2, w_up, w_down)
    else:
        u, gb, z, ya = mix_in
        pw, ps, cw, seq = local
        tiles = seq // tm
        hist = lambda c: pl.BlockSpec((HIST_ROWS, c), lambda i: (jnp.maximum(i * (tm // HIST_ROWS) - 1, 0), 0))
        in_specs = [row(D_MODEL), row(POOL_WIDTH), hist(POOL_WIDTH), row(CONV_WIDTH), row(CONV_WIDTH),
                    hist(CONV_WIDTH), row(GROUP_WIDTH), _const_spec((POOL_WIDTH, POOL_WIDTH)),
                    _const_spec((1, POOL_WIDTH)), _const_spec((CONV_K, CONV_WIDTH))] + weights
        args = (x, u, u, gb, z, z, ya, pw, ps, cw, w_out, g2, w_up, w_down)
    return pl.pallas_call(
        functools.partial(_mlp_kernel, tf=512, tiles=tiles),
        grid=(rows // tm,),
        in_specs=in_specs,
        out_specs=row(D_MODEL),
        out_shape=jax.ShapeDtypeStruct((rows, D_MODEL), F32),
        compiler_params=_params(),
        name="mlp",
    )(*args)


def _rope_tables(pos):
    half = ROPE_DIM // 2
    inv = jnp.power(jnp.float32(ROPE_THETA), -jnp.arange(half, dtype=F32) / half)
    ang = pos.astype(F32)[:, None] * inv[None, :]
    cos, sin = jnp.cos(ang), jnp.sin(ang)
    n = pos.shape[0]
    rest = HEAD_DIM - ROPE_DIM
    zh = jnp.zeros((n, half), F32)
    c = jnp.concatenate([cos, cos, jnp.ones((n, rest), F32)], axis=1)
    a = jnp.concatenate([-sin, zh, jnp.zeros((n, rest), F32)], axis=1)
    b = jnp.concatenate([zh, sin, jnp.zeros((n, rest), F32)], axis=1)
    return tuple(jnp.tile(t, (1, GROUP_WIDTH // HEAD_DIM)) for t in (c, a, b))


def _to_buffer_layout(c):
    lead = c.shape[:-4]
    n = len(lead)
    t = jnp.transpose(c, (*range(n), n + 1, n + 2, n + 3, n))
    return t.reshape(*lead, 2 * GROUP_WIDTH, c.shape[-4])


def _from_buffer_layout(t):
    lead = t.shape[:-2]
    n = len(lead)
    c = t.reshape(*lead, 2, 2, HEAD_DIM, t.shape[-1])
    return jnp.transpose(c, (*range(n), n + 3, n, n + 1, n + 2))


def kernel(x_prompt, x_sample, state_pool, state_conv, cache_kv_w128, cache_kv_w512, cache_kv_w2048,
           norm1_g, w_in, q_norm_g, k_norm_g, pool_w, pool_scale, conv_w, w_out, norm2_g, w_up, w_down):
    batch, seq, _ = x_prompt.shape
    n_seq, t_new, _ = x_sample.shape
    depth = w_in.shape[0]
    assert depth == 2, "the in-place stacking of the kv state outputs is written for two layers"
    caches_t = [_to_buffer_layout(c) for c in (cache_kv_w128, cache_kv_w512, cache_kv_w2048)]

    rope_p = _rope_tables(jnp.arange(seq, dtype=jnp.int32))
    rope_s = _rope_tables(jnp.tile(PAST_LEN + jnp.arange(t_new, dtype=jnp.int32), n_seq))
    head_id = jnp.arange(LANES) // HEAD_DIM
    hm = jnp.where(head_id[:, None] == head_id[None, :], 1.0 / HEAD_DIM, 0.0).astype(BF16)
    two_heads = lambda gain: jnp.tile(gain, GROUP_WIDTH // HEAD_DIM)[None, :]

    xp = x_prompt.reshape(batch * seq, D_MODEL)
    xs = x_sample.reshape(n_seq * t_new, D_MODEL)
    outs = {k: [] for k in ("pool_p", "conv_p", "pool_s", "conv_s")}
    w_in_b, w_out_b, w_up_b, w_down_b = (w.astype(BF16) for w in (w_in, w_out, w_up, w_down))
    new_caches = None
    kvts = None
    for layer in range(depth):
        pw = jax.scipy.linalg.block_diag(*[pool_w[layer, j] for j in range(len(POOL_WINDOWS))]).astype(BF16)
        ps = pool_scale[layer][None, :]
        cw = conv_w[layer]
        g1 = norm1_g[layer][None, :]
        g2 = norm2_g[layer][None, :]
        qg = two_heads(q_norm_g[layer]) * (HEAD_DIM ** -0.5)
        kg = two_heads(k_norm_g[layer])

        (u, gb, z, qc0, qc1, qc2, kvc0, kvc1, kvc2, *kvts) = _proj(
            xp, g1, w_in_b, qg, kg, rope_p, hm, layer=layer, tm=PROJ_TILE, batch=batch, seq=seq, prev_kvt=kvts)
        ya = _attn_p((qc0, qc1, qc2), (kvc0, kvc1, kvc2), batch=batch, seq=seq)
        xp = _mlp(xp, (u, gb, z, ya.reshape(-1, GROUP_WIDTH)), w_out_b, g2, w_up_b, w_down_b, layer=layer, tm=512,
                  local=(pw, ps, cw, seq))
        outs["pool_p"].append(u.reshape(batch, seq, POOL_WIDTH)[:, seq - POOL_HIST:])
        outs["conv_p"].append(z.reshape(batch, seq, CONV_WIDTH)[:, seq - (CONV_K - 1):])

        us, gbs, zs, q0, q1, q2, kn0, kn1, kn2 = _proj(xs, g1, w_in_b, qg, kg, rope_s, hm, layer=layer,
                                                       tm=n_seq * t_new)
        pst = jnp.pad(state_pool[layer], ((0, 0), (HIST_ROWS - POOL_HIST, 0), (0, 0)))
        cst = jnp.pad(state_conv[layer], ((0, 0), (8 - (CONV_K - 1), 0), (0, 0)))
        yps, yas, ycs, *new_caches = _mix_s(us, (q0, q1, q2), (kn0, kn1, kn2), gbs, zs, pst, cst, caches_t,
                                           pw, ps, cw, new_caches, n_seq=n_seq, t_new=t_new)
        xs = _mlp(xs, (yps.reshape(-1, POOL_WIDTH), yas.reshape(-1, GROUP_WIDTH), ycs.reshape(-1, CONV_WIDTH)),
                  w_out_b, g2, w_up_b, w_down_b, layer=layer, tm=n_seq * t_new)
        us3 = us.reshape(n_seq, t_new, POOL_WIDTH)
        zs3 = zs.reshape(n_seq, t_new, CONV_WIDTH)
        outs["pool_s"].append(jnp.concatenate([state_pool[layer], us3], axis=1)[:, -POOL_HIST:])
        outs["conv_s"].append(jnp.concatenate([state_conv[layer], zs3], axis=1)[:, -(CONV_K - 1):])

    st = lambda k: jnp.stack(outs[k])
    return (xp.reshape(batch, seq, D_MODEL), xs.reshape(n_seq, t_new, D_MODEL),
            st("pool_p"), st("conv_p"),
            *[_from_buffer_layout(kvt) for kvt in kvts],
            st("pool_s"), st("conv_s"),
            *[_from_buffer_layout(c) for c in new_caches])
```

```python
import functools

import jax
import jax.numpy as jnp
from jax import lax
from jax.experimental import pallas as pl
from jax.experimental.pallas import tpu as pltpu

D_MODEL = 1024
HEAD_DIM = 64
POOL_WIDTH = 256
POOL_WINDOWS = (2, 4, 8, 16)
POOL_GROUP = 64
POOL_HIST = 15
ATTN_WIDTH = 384
DIL_PAIRS = ((128, 1), (512, 4), (2048, 16))
N_DIL = 3
GROUP_WIDTH = 128
CONV_WIDTH = 384
CONV_K = 3
ROPE_DIM = 16
ROPE_THETA = 500000.0
D_FF = 4096
IN_COLS = 2560
MIX_OUT = 768
EPS = 1e-6
NEG_INF = -1e30
PAST_LEN = 8192
N_BACK = 128

LANES = 128
HIST_ROWS = 16
ATTN_TILE = 128
ATTN_UNROLL = 16
PROJ_TILE = 1024
PROJ_SUB = 512
MIX_S_SEQS_FIRST = 2
MIX_S_SEQS = 4
VMEM_LIMIT = 56 * 1024 * 1024

F32 = jnp.float32
BF16 = jnp.bfloat16

_Q_OFF = POOL_WIDTH
_K_OFF = _Q_OFF + ATTN_WIDTH
_V_OFF = _K_OFF + ATTN_WIDTH
_GB_OFF = _V_OFF + ATTN_WIDTH
_GC_OFF = _GB_OFF + CONV_WIDTH
_GH_OFF = _GC_OFF + CONV_WIDTH

_NT = (((1,), (1,)), ((), ()))


def _const_spec(shape):
    return pl.BlockSpec(shape, lambda *_: (0,) * len(shape), pipeline_mode=pl.Buffered(1))


def _layer_spec(shape, layer):
    return pl.BlockSpec((None, *shape), lambda *_: (layer,) + (0,) * len(shape), pipeline_mode=pl.Buffered(1))


def _params():
    return pltpu.CompilerParams(dimension_semantics=("arbitrary",), vmem_limit_bytes=VMEM_LIMIT)


def _proj_kernel(x_ref, g1_ref, w_ref, qg_ref, kg_ref, rc_ref, ra_ref, rb_ref, hm_ref, *rest,
                 class_major, tiles, sub, stacked):
    prev_kvt = None
    if class_major:
        if stacked:
            prev_kvt, rest = rest[:N_DIL], rest[N_DIL:]
        (u_ref, gb_ref, z_ref, qc0, qc1, qc2, kvc0, kvc1, kvc2, kvt0, kvt1, kvt2, nat) = rest
    else:
        (u_ref, gb_ref, z_ref, q0, q1, q2, kv0, kv1, kv2) = rest
    tm = x_ref.shape[0]
    hm = hm_ref[...]
    last_tile = pl.program_id(0) % tiles == tiles - 1

    for s in range(tm // sub):
        r0 = s * sub
        rows = slice(r0, r0 + sub)
        x = x_ref[rows, :]
        ms = jnp.mean(x * x, axis=-1, keepdims=True)
        hb = (x * lax.rsqrt(ms + EPS) * g1_ref[...]).astype(BF16)
        p_att = jnp.dot(hb, w_ref[:, _Q_OFF:_GB_OFF], preferred_element_type=F32)
        rc, ra, rb = rc_ref[rows, :], ra_ref[rows, :], rb_ref[rows, :]

        def head_norm_rope(xb, gain):
            msq = jnp.dot((xb * xb).astype(BF16), hm, preferred_element_type=F32)
            xn = xb * lax.rsqrt(msq + EPS) * gain
            return xn * rc + pltpu.roll(xn, LANES - ROPE_DIM // 2, 1) * ra + pltpu.roll(xn, ROPE_DIM // 2, 1) * rb

        for g, (window, dil) in enumerate(DIL_PAIRS):
            lanes = slice(g * GROUP_WIDTH, (g + 1) * GROUP_WIDTH)
            qn = head_norm_rope(p_att[:, :ATTN_WIDTH][:, lanes], qg_ref[...])
            kn = head_norm_rope(p_att[:, ATTN_WIDTH:2 * ATTN_WIDTH][:, lanes], kg_ref[...])
            vv = p_att[:, 2 * ATTN_WIDTH:][:, lanes]
            if not class_major:
                q_ref, kv_ref = ((q0, kv0), (q1, kv1), (q2, kv2))[g]
                q_ref[rows, :] = qn
                kv_ref[rows, :GROUP_WIDTH] = kn
                kv_ref[rows, GROUP_WIDTH:] = vv
                continue
            qc_ref, kvc_ref, kvt_ref = ((qc0, kvc0, kvt0), (qc1, kvc1, kvt1), (qc2, kvc2, kvt2))[g]
            slab = 3 * (s * N_DIL + g)
            nat[slab] = qn
            nat[slab + 1] = kn
            nat[slab + 2] = vv
            per = sub // dil
            dst = slice(s * per, (s + 1) * per)
            for r in range(dil):
                src = pl.ds(r, per, stride=dil) if dil > 1 else slice(None)
                qc_ref[0, r, dst, :] = nat[slab, src, :].astype(BF16)
                kvc_ref[0, r, dst, :GROUP_WIDTH] = nat[slab + 1, src, :].astype(BF16)
                kvc_ref[0, r, dst, GROUP_WIDTH:] = nat[slab + 2, src, :].astype(BF16)
            def put_kvt(cols, part, g=g, kvt_ref=kvt_ref):
                if stacked:
                    kvt_ref[0, 0, :, cols] = prev_kvt[g][0, :, cols]
                    kvt_ref[1, 0, :, cols] = part
                else:
                    kvt_ref[0, :, cols] = part

            if window >= tiles * tm:
                put_kvt(rows, jnp.concatenate([kn, vv], axis=1).T)
            else:
                first_kept = tm - min(window, tm)
                lo = max(r0, first_kept)
                if lo < r0 + sub:
                    @pl.when(last_tile)
                    def _(kn=kn, vv=vv, put_kvt=put_kvt, lo=lo, r0=r0, first_kept=first_kept):
                        part = jnp.concatenate([kn[lo - r0:], vv[lo - r0:]], axis=1).T
                        put_kvt(slice(lo - first_kept, r0 + sub - first_kept), part)

        u_ref[rows, :] = jnp.dot(hb, w_ref[:, :POOL_WIDTH], preferred_element_type=F32)
        p_gate = jnp.dot(hb, w_ref[:, _GB_OFF:], preferred_element_type=F32)
        gb_ref[rows, :] = p_gate[:, :CONV_WIDTH]
        z_ref[rows, :] = p_gate[:, CONV_WIDTH:2 * CONV_WIDTH] * p_gate[:, 2 * CONV_WIDTH:]


def _proj(x, g1, w_in, qg, kg, rope, hm, *, layer, tm, batch=None, seq=None, prev_kvt=None):
    rows = x.shape[0]
    stacked = prev_kvt is not None
    sub = min(PROJ_SUB, tm)
    n_tab = rope[0].shape[0] // tm
    class_major = batch is not None
    tiles = seq // tm if class_major else 1
    row = lambda c: pl.BlockSpec((tm, c), lambda i: (i, 0))
    tab = pl.BlockSpec((tm, LANES), lambda i: (i % n_tab, 0))
    out_shape = [jax.ShapeDtypeStruct((rows, POOL_WIDTH), F32)] + [jax.ShapeDtypeStruct((rows, CONV_WIDTH), F32)] * 2
    out_specs = [row(POOL_WIDTH), row(CONV_WIDTH), row(CONV_WIDTH)]
    scratch = []
    prev_specs = []
    if class_major:
        for width in (GROUP_WIDTH, 2 * GROUP_WIDTH):
            for (_, dil) in DIL_PAIRS:
                out_shape.append(jax.ShapeDtypeStruct((batch, dil, seq // dil, width), BF16))
                out_specs.append(pl.BlockSpec((1, dil, tm // dil, width), lambda i: (i // tiles, 0, i % tiles, 0)))
        for (window, _) in DIL_PAIRS:
            assert window >= seq or window <= tm, "kv buffer rows must come from the last row tile"
            cols = tm if window >= seq else window
            col_blk = (lambda i: i % tiles) if window >= seq else (lambda i: 0)
            kvt_spec = pl.BlockSpec((1, 2 * GROUP_WIDTH, cols), lambda i, c=col_blk: (i // tiles, 0, c(i)))
            if stacked:
                prev_specs.append(kvt_spec)
                out_shape.append(jax.ShapeDtypeStruct((2, batch, 2 * GROUP_WIDTH, window), F32))
                out_specs.append(pl.BlockSpec((2, 1, 2 * GROUP_WIDTH, cols),
                                              lambda i, c=col_blk: (0, i // tiles, 0, c(i))))
            else:
                out_shape.append(jax.ShapeDtypeStruct((batch, 2 * GROUP_WIDTH, window), F32))
                out_specs.append(kvt_spec)
        scratch = [pltpu.VMEM((3 * N_DIL * (tm // sub), sub, GROUP_WIDTH), F32)]
    else:
        out_shape += [jax.ShapeDtypeStruct((rows, GROUP_WIDTH), F32)] * N_DIL
        out_shape += [jax.ShapeDtypeStruct((rows, 2 * GROUP_WIDTH), F32)] * N_DIL
        out_specs += [row(GROUP_WIDTH)] * N_DIL + [row(2 * GROUP_WIDTH)] * N_DIL
    return pl.pallas_call(
        functools.partial(_proj_kernel, class_major=class_major, tiles=tiles, sub=sub, stacked=stacked),
        grid=(rows // tm,),
        in_specs=[row(D_MODEL), _const_spec((1, D_MODEL)), _layer_spec((D_MODEL, IN_COLS), layer),
                  _const_spec((1, LANES)), _const_spec((1, LANES)), tab, tab, tab,
                  _const_spec((LANES, LANES))] + prev_specs,
        out_specs=out_specs,
        out_shape=out_shape,
        scratch_shapes=scratch,
        compiler_params=_params(),
        name="proj",
    )(x, g1, w_in, qg, kg, *rope, hm, *(prev_kvt or ()))


def _lane_lt(shape, bound):
    return lax.broadcasted_iota(jnp.int32, shape, len(shape) - 1) < bound


def _stack_heads(q):
    first = _lane_lt(q.shape, HEAD_DIM)
    zero = jnp.zeros_like(q)
    return jnp.concatenate([jnp.where(first, q, zero), jnp.where(first, zero, q)], axis=0)


def _unstack(o2, l2, m2):
    n = o2.shape[0] // 2
    first = _lane_lt((n, GROUP_WIDTH), HEAD_DIM)
    return tuple(jnp.where(first, a[:n], a[n:]) for a in (o2, l2, m2))


def _merge(a, b):
    m = jnp.maximum(a[2], b[2])
    wa = jnp.exp(a[2] - m)
    wb = jnp.exp(b[2] - m)
    return wa * a[0] + wb * b[0], wa * a[1] + wb * b[1], m


def _attn_unit(q, kv, bias):
    nk = kv.shape[0]
    k = kv[:, :GROUP_WIDTH]
    v1 = jnp.concatenate([kv[:, GROUP_WIDTH:], jnp.ones((nk, GROUP_WIDTH), BF16)], axis=1)
    s = lax.dot_general(_stack_heads(q), k, _NT, preferred_element_type=F32) + bias
    mrow = jnp.max(s, axis=-1, keepdims=True)
    p = jnp.exp(s - mrow).astype(BF16)
    r = jnp.dot(p, v1, preferred_element_type=F32)
    return _unstack(r[:, :GROUP_WIDTH], r[:, GROUP_WIDTH:], mrow)


def _attn_p_kernel(qc0, qc1, qc2, kvc0, kvc1, kvc2, y_ref, acc_o, acc_l, acc_m, tmp_o, tmp_l, tmp_m, bias_ref,
                   *, seq):
    @pl.when(pl.program_id(0) == 0)
    def _():
        qi = lax.broadcasted_iota(jnp.int32, (2 * ATTN_TILE, 2 * ATTN_TILE), 0) & (ATTN_TILE - 1)
        ki = lax.broadcasted_iota(jnp.int32, (2 * ATTN_TILE, 2 * ATTN_TILE), 1)
        for first_key_back in (0, 1):
            dist = qi - ki + first_key_back * ATTN_TILE
            bias_ref[first_key_back] = jnp.where((dist >= 0) & (dist <= N_BACK), 0.0, NEG_INF)

    for g, ((_, dil), q_ref, kv_ref) in enumerate(zip(DIL_PAIRS, (qc0, qc1, qc2), (kvc0, kvc1, kvc2))):
        dst = (acc_o, acc_l, acc_m) if g == 0 else (tmp_o, tmp_l, tmp_m)
        n_sub = seq // dil // ATTN_TILE

        def unit(idx, q_ref=q_ref, kv_ref=kv_ref, n_sub=n_sub, dil=dil):
            if n_sub == 1:
                r, sub = idx, 0
                res = _attn_unit(q_ref[0, r], kv_ref[0, r], bias_ref[0, :, :ATTN_TILE])
            else:
                r, sub = (idx // n_sub, idx % n_sub) if dil > 1 else (0, idx)
                back = jnp.minimum(sub, 1)
                q0 = pl.multiple_of(sub * ATTN_TILE, ATTN_TILE)
                k0 = pl.multiple_of((sub - back) * ATTN_TILE, ATTN_TILE)
                res = _attn_unit(q_ref[0, r, pl.ds(q0, ATTN_TILE), :], kv_ref[0, r, pl.ds(k0, 2 * ATTN_TILE), :],
                                 bias_ref[back])
            start = sub * (ATTN_TILE * dil) + r
            rows = pl.ds(start, ATTN_TILE, stride=dil) if dil > 1 else pl.ds(pl.multiple_of(start, ATTN_TILE), ATTN_TILE)
            return rows, res

        def units_body(it, carry, unit=unit, dst=dst):
            for j in range(ATTN_UNROLL):
                rows, res = unit(it * ATTN_UNROLL + j)
                for ref, val in zip(dst, res):
                    ref[rows, :] = val
            return carry
        lax.fori_loop(0, dil * n_sub // ATTN_UNROLL, units_body, 0)

        if g > 0:
            def merge_body(c, carry, final=(g == N_DIL - 1)):
                rows = pl.ds(pl.multiple_of(c * 256, 256), 256)
                o, l, m = _merge((acc_o[rows, :], acc_l[rows, :], acc_m[rows, :]),
                                 (tmp_o[rows, :], tmp_l[rows, :], tmp_m[rows, :]))
                if final:
                    y_ref[0, rows, :] = (o / l).astype(y_ref.dtype)
                else:
                    acc_o[rows, :] = o
                    acc_l[rows, :] = l
                    acc_m[rows, :] = m
                return carry
            lax.fori_loop(0, seq // 256, merge_body, 0)


def _attn_p(qcs, kvcs, *, batch, seq):
    in_specs = [pl.BlockSpec((1,) + a.shape[1:], lambda b: (b, 0, 0, 0)) for a in (*qcs, *kvcs)]
    acc = pltpu.VMEM((seq, GROUP_WIDTH), F32)
    return pl.pallas_call(
        functools.partial(_attn_p_kernel, seq=seq),
        grid=(batch,),
        in_specs=in_specs,
        out_specs=pl.BlockSpec((1, seq, GROUP_WIDTH), lambda b: (b, 0, 0)),
        out_shape=jax.ShapeDtypeStruct((batch, seq, GROUP_WIDTH), BF16),
        scratch_shapes=[acc] * 6 + [pltpu.VMEM((2, 2 * ATTN_TILE, 2 * ATTN_TILE), F32)],
        compiler_params=_params(),
        name="attn_p",
    )(*qcs, *kvcs)


def _pool_rows(ext, pos, pw, ps):
    return jnp.dot(_pool_diff(ext, pos), pw, preferred_element_type=F32) * ps


def _pool_diff(ext, pos):
    s2 = ext + pltpu.roll(ext, 1, 0)
    s4 = s2 + pltpu.roll(s2, 2, 0)
    s8 = s4 + pltpu.roll(s4, 4, 0)
    s16 = s8 + pltpu.roll(s8, 8, 0)
    lane = lax.broadcasted_iota(jnp.int32, (1, POOL_WIDTH), 1)
    grp = [lane < (j + 1) * POOL_GROUP for j in range(3)]
    win = jnp.where(grp[0], s2, jnp.where(grp[1], s4, jnp.where(grp[2], s8, s16)))[HIST_ROWS:]
    width = jnp.where(grp[0], POOL_WINDOWS[0], jnp.where(grp[1], POOL_WINDOWS[1],
                      jnp.where(grp[2], POOL_WINDOWS[2], POOL_WINDOWS[3])))
    cnt = jnp.minimum(pos + 1, width).astype(F32)
    return (win / cnt - ext[HIST_ROWS:]).astype(BF16)


def _conv_rows(ext, gb, cw, hist):
    y = cw[0:1] * pltpu.roll(ext, 2, 0) + cw[1:2] * pltpu.roll(ext, 1, 0) + cw[2:3] * ext
    return gb * y[hist:]


def _attn_s_group(q, new_t, cache_t, window, dil, shifted):
    t = q.shape[0]
    q2 = _stack_heads(q).astype(BF16)
    s_c = jnp.dot(q2, cache_t[:GROUP_WIDTH].astype(BF16), preferred_element_type=F32)
    s_n = jnp.dot(q2, new_t[:GROUP_WIDTH].astype(BF16), preferred_element_type=F32)
    tq = lax.broadcasted_iota(jnp.int32, (2 * t, 1), 0) & (t - 1)
    rc = lax.broadcasted_iota(jnp.int32, (1, window), 1)
    if shifted:
        rc = (rc + t) & (window - 1)
    dist_c = window + tq - rc
    ok_c = (dist_c <= window) & ((dist_c & (dil - 1)) == 0) & (PAST_LEN - window + rc >= 0)
    tn = lax.broadcasted_iota(jnp.int32, (1, LANES), 1) - (LANES - t)
    dist_n = tq - tn
    ok_n = (tn >= 0) & (dist_n >= 0) & ((dist_n & (dil - 1)) == 0)
    s_c = jnp.where(ok_c, s_c, NEG_INF)
    s_n = jnp.where(ok_n, s_n, NEG_INF)
    mrow = jnp.maximum(jnp.max(s_c, axis=-1, keepdims=True), jnp.max(s_n, axis=-1, keepdims=True))
    p_c = jnp.exp(s_c - mrow)
    p_n = jnp.exp(s_n - mrow)
    o2 = (lax.dot_general(p_c.astype(BF16), cache_t[GROUP_WIDTH:].astype(BF16), _NT, preferred_element_type=F32)
          + lax.dot_general(p_n.astype(BF16), new_t[GROUP_WIDTH:].astype(BF16), _NT, preferred_element_type=F32))
    l2 = jnp.sum(p_c, axis=-1, keepdims=True) + jnp.sum(p_n, axis=-1, keepdims=True)
    return _unstack(o2, l2, mrow)


def _mix_s_kernel(u_ref, q0_ref, q1_ref, q2_ref, kn0_ref, kn1_ref, kn2_ref, gb_ref, z_ref, pst_ref, cst_ref,
                  pw_ref, ps_ref, cw_ref, *rest, t_new, first, n_per):
    yp_ref, ya_ref, yc_ref, nc0_ref, nc1_ref, nc2_ref = rest[-6:]
    cache_refs = rest[:-6]
    keep_old = _lane_lt((2 * GROUP_WIDTH, LANES), LANES - t_new)
    pos = PAST_LEN + lax.broadcasted_iota(jnp.int32, (t_new, 1), 0)
    for i in range(n_per):
        state = None
        for g, ((window, dil), q_ref, kn_ref, nc_ref) in enumerate(zip(
                DIL_PAIRS, (q0_ref, q1_ref, q2_ref), (kn0_ref, kn1_ref, kn2_ref), (nc0_ref, nc1_ref, nc2_ref))):
            kv_new = kn_ref[i]
            new_t = jnp.concatenate([jnp.zeros((LANES - t_new, 2 * GROUP_WIDTH), F32), kv_new], axis=0).T
            cache_t = cache_refs[g][0, i]
            part = _attn_s_group(q_ref[i], new_t, cache_t, window, dil, shifted=not first)
            state = part if state is None else _merge(state, part)
            rolled = pltpu.roll(cache_t, window - t_new, 1) if first else cache_t
            last = jnp.where(keep_old, rolled[:, window - LANES:], new_t)
            if first:
                if window > LANES:
                    nc_ref[0, i, :, :window - LANES] = rolled[:, :window - LANES]
                nc_ref[0, i, :, window - LANES:] = last
                nc_ref[1, i] = pltpu.roll(cache_refs[N_DIL + g][0, i], window - t_new, 1)
            else:
                nc_ref[0, i] = last
        ya_ref[i] = state[0] / state[1]

        u_ext = jnp.concatenate([pst_ref[i], u_ref[i]], axis=0)
        yp_ref[i] = _pool_rows(u_ext, pos, pw_ref[...], ps_ref[...])
        z_ext = jnp.concatenate([cst_ref[i], z_ref[i]], axis=0)
        yc_ref[i] = _conv_rows(z_ext, gb_ref[i], cw_ref[...], cst_ref.shape[1])


def _mix_s(u, qs, kns, gb, z, pst, cst, caches_t, pw, ps, cw, prev, *, n_seq, t_new):
    first = prev is None
    n_per = MIX_S_SEQS_FIRST if first else MIX_S_SEQS
    blk = lambda r, c: pl.BlockSpec((n_per, r, c), lambda b: (b, 0, 0))
    cache_blk = lambda layer, w: pl.BlockSpec((1, n_per, 2 * GROUP_WIDTH, w), lambda b: (layer, b, 0, 0))
    three = lambda a: a.reshape(n_seq, -1, a.shape[-1])
    windows = [w for (w, _) in DIL_PAIRS]
    in_specs = ([blk(t_new, POOL_WIDTH)] + [blk(t_new, GROUP_WIDTH)] * N_DIL + [blk(t_new, 2 * GROUP_WIDTH)] * N_DIL
                + [blk(t_new, CONV_WIDTH)] * 2 + [blk(pst.shape[1], POOL_WIDTH), blk(cst.shape[1], CONV_WIDTH)]
                + [_const_spec((POOL_WIDTH, POOL_WIDTH)), _const_spec((1, POOL_WIDTH)),
                   _const_spec((CONV_K, CONV_WIDTH))])
    args = [three(u), *[three(q) for q in qs], *[three(k) for k in kns], three(gb), three(z), pst, cst, pw, ps, cw]
    aliases = {}
    if first:
        assert all(c.shape[0] == 2 for c in caches_t), "stacked kv buffer update is written for two layers"
        in_specs += [cache_blk(0, w) for w in windows] + [cache_blk(1, w) for w in windows]
        args += [*caches_t, *caches_t]
        cache_out = [pl.BlockSpec((2, n_per, 2 * GROUP_WIDTH, w), lambda b: (0, b, 0, 0)) for w in windows]
    else:
        for k, (p, w) in enumerate(zip(prev, windows)):
            aliases[len(args)] = 3 + k
            in_specs.append(cache_blk(1, w))
            args.append(p)
        cache_out = [pl.BlockSpec((1, n_per, 2 * GROUP_WIDTH, LANES), lambda b, w=w: (1, b, 0, w // LANES - 1))
                     for w in windows]
    out_specs = [blk(t_new, POOL_WIDTH), blk(t_new, GROUP_WIDTH), blk(t_new, CONV_WIDTH)] + cache_out
    out_shape = ([jax.ShapeDtypeStruct((n_seq, t_new, c), F32) for c in (POOL_WIDTH, GROUP_WIDTH, CONV_WIDTH)]
                 + [jax.ShapeDtypeStruct(c.shape, F32) for c in caches_t])
    return pl.pallas_call(
        functools.partial(_mix_s_kernel, t_new=t_new, first=first, n_per=n_per),
        grid=(n_seq // n_per,),
        in_specs=in_specs,
        out_specs=out_specs,
        out_shape=out_shape,
        input_output_aliases=aliases,
        compiler_params=_params(),
        name="mix_s",
    )(*args)


def _mlp_head(x, yp, ya, yc, wo_ref, g2_ref):
    mixed = jnp.concatenate([yp.astype(BF16), ya.astype(BF16), yc.astype(BF16)], axis=1)
    x1 = x + jnp.dot(mixed, wo_ref[...], preferred_element_type=F32)
    ms = jnp.mean(x1 * x1, axis=-1, keepdims=True)
    return x1, (x1 * lax.rsqrt(ms + EPS) * g2_ref[...]).astype(BF16)


def _mlp_tail(x1, hb, wu_ref, wd_ref, tf):
    acc = x1
    for c in range(D_FF // tf):
        hf = jnp.dot(hb, wu_ref[:, c * tf:(c + 1) * tf], preferred_element_type=F32)
        act = jnp.square(jnp.maximum(hf, 0.0)).astype(BF16)
        acc = acc + jnp.dot(act, wd_ref[c * tf:(c + 1) * tf, :], preferred_element_type=F32)
    return acc


def _mlp_s_kernel(x_ref, yp_ref, ya_ref, yc_ref, wo_ref, g2_ref, wu_ref, wd_ref, o_ref, hb_buf):
    @pl.when(pl.program_id(0) == 0)
    def _():
        x1, hb = _mlp_head(x_ref[...], yp_ref[...], ya_ref[...], yc_ref[...], wo_ref, g2_ref)
        o_ref[...] = x1
        hb_buf[...] = hb

    hf = jnp.dot(hb_buf[...], wu_ref[...], preferred_element_type=F32)
    act = jnp.square(jnp.maximum(hf, 0.0)).astype(BF16)
    o_ref[...] += jnp.dot(act, wd_ref[...], preferred_element_type=F32)


def _mlp_p_kernel(x_ref, u_ref, uh_ref, gb_ref, z_ref, zh_ref, ya_ref, pw_ref, ps_ref, cw_ref,
                  wo_ref, g2_ref, wu_ref, wd_ref, o_ref, *, tf, tiles):
    tm = x_ref.shape[0]
    tile = pl.program_id(0) % tiles
    has_hist = tile > 0
    pos = tile * tm + lax.broadcasted_iota(jnp.int32, (tm, 1), 0)
    yp = _pool_rows(jnp.concatenate([jnp.where(has_hist, uh_ref[...], 0.0), u_ref[...]], axis=0), pos,
                    pw_ref[...], ps_ref[...])
    yc = _conv_rows(jnp.concatenate([jnp.where(has_hist, zh_ref[...], 0.0), z_ref[...]], axis=0),
                    gb_ref[...], cw_ref[...], HIST_ROWS)
    x1, hb = _mlp_head(x_ref[...], yp, ya_ref[...], yc, wo_ref, g2_ref)
    o_ref[...] = _mlp_tail(x1, hb, wu_ref, wd_ref, tf)


def _mlp_s(x, yp, ya, yc, w_out, g2, w_up, w_down, *, layer):
    rows = x.shape[0]
    tf = 512
    full = lambda c: pl.BlockSpec((rows, c), lambda j: (0, 0))
    return pl.pallas_call(
        _mlp_s_kernel,
        grid=(D_FF // tf,),
        in_specs=[full(D_MODEL), full(POOL_WIDTH), full(GROUP_WIDTH), full(CONV_WIDTH),
                  _layer_spec((MIX_OUT, D_MODEL), layer), _const_spec((1, D_MODEL)),
                  pl.BlockSpec((None, D_MODEL, tf), lambda j: (layer, 0, j)),
                  pl.BlockSpec((None, tf, D_MODEL), lambda j: (layer, j, 0))],
        out_specs=full(D_MODEL),
        out_shape=jax.ShapeDtypeStruct((rows, D_MODEL), F32),
        scratch_shapes=[pltpu.VMEM((rows, D_MODEL), BF16)],
        compiler_params=_params(),
        name="mlp_s",
    )(x, yp, ya, yc, w_out, g2, w_up, w_down)


def _mlp_p(x, u, gb, z, ya, pw, ps, cw, w_out, g2, w_up, w_down, *, layer, tm, seq):
    rows = x.shape[0]
    row = lambda c: pl.BlockSpec((tm, c), lambda i: (i, 0))
    hist = lambda c: pl.BlockSpec((HIST_ROWS, c), lambda i: (jnp.maximum(i * (tm // HIST_ROWS) - 1, 0), 0))
    return pl.pallas_call(
        functools.partial(_mlp_p_kernel, tf=512, tiles=seq // tm),
        grid=(rows // tm,),
        in_specs=[row(D_MODEL), row(POOL_WIDTH), hist(POOL_WIDTH), row(CONV_WIDTH), row(CONV_WIDTH),
                  hist(CONV_WIDTH), row(GROUP_WIDTH), _const_spec((POOL_WIDTH, POOL_WIDTH)),
                  _const_spec((1, POOL_WIDTH)), _const_spec((CONV_K, CONV_WIDTH)),
                  _layer_spec((MIX_OUT, D_MODEL), layer), _const_spec((1, D_MODEL)),
                  _layer_spec((D_MODEL, D_FF), layer), _layer_spec((D_FF, D_MODEL), layer)],
        out_specs=row(D_MODEL),
        out_shape=jax.ShapeDtypeStruct((rows, D_MODEL), F32),
        compiler_params=_params(),
        name="mlp_p",
    )(x, u, u, gb, z, z, ya, pw, ps, cw, w_out, g2, w_up, w_down)


def _rope_tables(pos):
    half = ROPE_DIM // 2
    inv = jnp.power(jnp.float32(ROPE_THETA), -jnp.arange(half, dtype=F32) / half)
    ang = pos.astype(F32)[:, None] * inv[None, :]
    cos, sin = jnp.cos(ang), jnp.sin(ang)
    n = pos.shape[0]
    rest = HEAD_DIM - ROPE_DIM
    zh = jnp.zeros((n, half), F32)
    c = jnp.concatenate([cos, cos, jnp.ones((n, rest), F32)], axis=1)
    a = jnp.concatenate([-sin, zh, jnp.zeros((n, rest), F32)], axis=1)
    b = jnp.concatenate([zh, sin, jnp.zeros((n, rest), F32)], axis=1)
    return tuple(jnp.tile(t, (1, GROUP_WIDTH // HEAD_DIM)) for t in (c, a, b))


def _to_buffer_layout(c):
    lead = c.shape[:-4]
    n = len(lead)
    t = jnp.transpose(c, (*range(n), n + 1, n + 2, n + 3, n))
    return t.reshape(*lead, 2 * GROUP_WIDTH, c.shape[-4])


def _from_buffer_layout(t):
    lead = t.shape[:-2]
    n = len(lead)
    c = t.reshape(*lead, 2, 2, HEAD_DIM, t.shape[-1])
    return jnp.transpose(c, (*range(n), n + 3, n, n + 1, n + 2))


def kernel(x_prompt, x_sample, state_pool, state_conv, cache_kv_w128, cache_kv_w512, cache_kv_w2048,
           norm1_g, w_in, q_norm_g, k_norm_g, pool_w, pool_scale, conv_w, w_out, norm2_g, w_up, w_down):
    batch, seq, _ = x_prompt.shape
    n_seq, t_new, _ = x_sample.shape
    depth = w_in.shape[0]
    assert depth == 2, "the in-place stacking of the kv state outputs is written for two layers"
    caches_t = [_to_buffer_layout(c) for c in (cache_kv_w128, cache_kv_w512, cache_kv_w2048)]

    rope_p = _rope_tables(jnp.arange(seq, dtype=jnp.int32))
    rope_s = _rope_tables(jnp.tile(PAST_LEN + jnp.arange(t_new, dtype=jnp.int32), n_seq))
    head_id = jnp.arange(LANES) // HEAD_DIM
    hm = jnp.where(head_id[:, None] == head_id[None, :], 1.0 / HEAD_DIM, 0.0).astype(BF16)
    two_heads = lambda gain: jnp.tile(gain, GROUP_WIDTH // HEAD_DIM)[None, :]

    xp = x_prompt.reshape(batch * seq, D_MODEL)
    xs = x_sample.reshape(n_seq * t_new, D_MODEL)
    outs = {k: [] for k in ("pool_p", "conv_p", "pool_s", "conv_s")}
    w_in_b, w_out_b, w_up_b, w_down_b = (w.astype(BF16) for w in (w_in, w_out, w_up, w_down))
    new_caches = None
    kvts = None
    for layer in range(depth):
        pw = jax.scipy.linalg.block_diag(*[pool_w[layer, j] for j in range(len(POOL_WINDOWS))]).astype(BF16)
        ps = pool_scale[layer][None, :]
        cw = conv_w[layer]
        g1 = norm1_g[layer][None, :]
        g2 = norm2_g[layer][None, :]
        qg = two_heads(q_norm_g[layer]) * (HEAD_DIM ** -0.5)
        kg = two_heads(k_norm_g[layer])

        (u, gb, z, qc0, qc1, qc2, kvc0, kvc1, kvc2, *kvts) = _proj(
            xp, g1, w_in_b, qg, kg, rope_p, hm, layer=layer, tm=PROJ_TILE, batch=batch, seq=seq, prev_kvt=kvts)
        ya = _attn_p((qc0, qc1, qc2), (kvc0, kvc1, kvc2), batch=batch, seq=seq)
        xp = _mlp_p(xp, u, gb, z, ya.reshape(-1, GROUP_WIDTH), pw, ps, cw, w_out_b, g2, w_up_b, w_down_b,
                    layer=layer, tm=512, seq=seq)
        outs["pool_p"].append(u.reshape(batch, seq, POOL_WIDTH)[:, seq - POOL_HIST:])
        outs["conv_p"].append(z.reshape(batch, seq, CONV_WIDTH)[:, seq - (CONV_K - 1):])

        us, gbs, zs, q0, q1, q2, kn0, kn1, kn2 = _proj(xs, g1, w_in_b, qg, kg, rope_s, hm, layer=layer,
                                                       tm=n_seq * t_new)
        pst = jnp.pad(state_pool[layer], ((0, 0), (HIST_ROWS - POOL_HIST, 0), (0, 0)))
        cst = jnp.pad(state_conv[layer], ((0, 0), (8 - (CONV_K - 1), 0), (0, 0)))
        yps, yas, ycs, *new_caches = _mix_s(us, (q0, q1, q2), (kn0, kn1, kn2), gbs, zs, pst, cst, caches_t,
                                           pw, ps, cw, new_caches, n_seq=n_seq, t_new=t_new)
        xs = _mlp_s(xs, yps.reshape(-1, POOL_WIDTH), yas.reshape(-1, GROUP_WIDTH), ycs.reshape(-1, CONV_WIDTH),
                    w_out_b, g2, w_up_b, w_down_b, layer=layer)
        us3 = us.reshape(n_seq, t_new, POOL_WIDTH)
        zs3 = zs.reshape(n_seq, t_new, CONV_WIDTH)
        outs["pool_s"].append(jnp.concatenate([state_pool[layer], us3], axis=1)[:, -POOL_HIST:])
        outs["conv_s"].append(jnp.concatenate([state_conv[layer], zs3], axis=1)[:, -(CONV_K - 1):])

    st = lambda k: jnp.stack(outs[k])
    return (xp.reshape(batch, seq, D_MODEL), xs.reshape(n_seq, t_new, D_MODEL),
            st("pool_p"), st("conv_p"),
            *[_from_buffer_layout(kvt) for kvt in kvts],
            st("pool_s"), st("conv_s"),
            *[_from_buffer_layout(c) for c in new_caches])
```

```python
import functools

import jax
import jax.numpy as jnp
from jax import lax
from jax.experimental import pallas as pl
from jax.experimental.pallas import tpu as pltpu

D_MODEL = 1024
HEAD_DIM = 64
POOL_WIDTH = 256
POOL_WINDOWS = (2, 4, 8, 16)
POOL_GROUP = 64
POOL_HIST = 15
ATTN_WIDTH = 384
DIL_PAIRS = ((128, 1), (512, 4), (2048, 16))
N_DIL = 3
GROUP_WIDTH = 128
CONV_WIDTH = 384
CONV_K = 3
ROPE_DIM = 16
ROPE_THETA = 500000.0
D_FF = 4096
IN_COLS = 2560
MIX_OUT = 768
EPS = 1e-6
NEG_INF = -1e30
PAST_LEN = 8192
N_BACK = 128

LANES = 128
HIST_ROWS = 16
ATTN_TILE = 128
ATTN_UNROLL = 16
PROJ_TILE = 1024
PROJ_SUB = 512
MIX_S_SEQS_FIRST = 2
MIX_S_SEQS = 4
VMEM_LIMIT = 56 * 1024 * 1024

F32 = jnp.float32
BF16 = jnp.bfloat16

_Q_OFF = POOL_WIDTH
_K_OFF = _Q_OFF + ATTN_WIDTH
_V_OFF = _K_OFF + ATTN_WIDTH
_GB_OFF = _V_OFF + ATTN_WIDTH
_GC_OFF = _GB_OFF + CONV_WIDTH
_GH_OFF = _GC_OFF + CONV_WIDTH

_NT = (((1,), (1,)), ((), ()))


def _const_spec(shape):
    return pl.BlockSpec(shape, lambda *_: (0,) * len(shape), pipeline_mode=pl.Buffered(1))


def _layer_spec(shape, layer):
    return pl.BlockSpec((None, *shape), lambda *_: (layer,) + (0,) * len(shape), pipeline_mode=pl.Buffered(1))


def _params():
    return pltpu.CompilerParams(dimension_semantics=("arbitrary",), vmem_limit_bytes=VMEM_LIMIT)


def _proj_kernel(x_ref, g1_ref, w_ref, qg_ref, kg_ref, rc_ref, ra_ref, rb_ref, hm_ref, *rest,
                 class_major, tiles, sub, stacked):
    prev_kvt = None
    if class_major:
        if stacked:
            prev_kvt, rest = rest[:N_DIL], rest[N_DIL:]
        (u_ref, gb_ref, z_ref, qc0, qc1, qc2, kvc0, kvc1, kvc2, kvt0, kvt1, kvt2, nat) = rest
    else:
        (u_ref, gb_ref, z_ref, q0, q1, q2, kv0, kv1, kv2) = rest
    tm = x_ref.shape[0]
    hm = hm_ref[...]

    for s in range(tm // sub):
        r0 = s * sub
        rows = slice(r0, r0 + sub)
        x = x_ref[rows, :]
        ms = jnp.mean(x * x, axis=-1, keepdims=True)
        hb = (x * lax.rsqrt(ms + EPS) * g1_ref[...]).astype(BF16)
        p_att = jnp.dot(hb, w_ref[:, _Q_OFF:_GB_OFF], preferred_element_type=F32)
        rc, ra, rb = rc_ref[rows, :], ra_ref[rows, :], rb_ref[rows, :]

        def head_norm_rope(xb, gain):
            msq = jnp.dot((xb * xb).astype(BF16), hm, preferred_element_type=F32)
            xn = xb * lax.rsqrt(msq + EPS) * gain
            return xn * rc + pltpu.roll(xn, LANES - ROPE_DIM // 2, 1) * ra + pltpu.roll(xn, ROPE_DIM // 2, 1) * rb

        for g, (window, dil) in enumerate(DIL_PAIRS):
            lanes = slice(g * GROUP_WIDTH, (g + 1) * GROUP_WIDTH)
            qn = head_norm_rope(p_att[:, :ATTN_WIDTH][:, lanes], qg_ref[...])
            kn = head_norm_rope(p_att[:, ATTN_WIDTH:2 * ATTN_WIDTH][:, lanes], kg_ref[...])
            vv = p_att[:, 2 * ATTN_WIDTH:][:, lanes]
            if not class_major:
                q_ref, kv_ref = ((q0, kv0), (q1, kv1), (q2, kv2))[g]
                q_ref[rows, :] = qn
                kv_ref[rows, :GROUP_WIDTH] = kn
                kv_ref[rows, GROUP_WIDTH:] = vv
                continue
            qc_ref, kvc_ref, kvt_ref = ((qc0, kvc0, kvt0), (qc1, kvc1, kvt1), (qc2, kvc2, kvt2))[g]
            slab = 3 * (s * N_DIL + g)
            nat[slab] = qn
            nat[slab + 1] = kn
            nat[slab + 2] = vv
            per = sub // dil
            dst = slice(s * per, (s + 1) * per)
            for r in range(dil):
                src = pl.ds(r, per, stride=dil) if dil > 1 else slice(None)
                qc_ref[0, r, dst, :] = nat[slab, src, :].astype(BF16)
                kvc_ref[0, r, dst, :GROUP_WIDTH] = nat[slab + 1, src, :].astype(BF16)
                kvc_ref[0, r, dst, GROUP_WIDTH:] = nat[slab + 2, src, :].astype(BF16)
            def put_kvt(cols, part, g=g, kvt_ref=kvt_ref):
                if stacked:
                    kvt_ref[0, 0, :, cols] = prev_kvt[g][0, :, cols]
                    kvt_ref[1, 0, :, cols] = part
                else:
                    kvt_ref[0, :, cols] = part

            if window >= tiles * tm:
                put_kvt(rows, jnp.concatenate([kn, vv], axis=1).T)
            else:
                first_kept = tm - min(window, tm)
                lo = max(r0, first_kept)
                if lo < r0 + sub:
                    part = jnp.concatenate([kn[lo - r0:], vv[lo - r0:]], axis=1).T
                    put_kvt(slice(lo - first_kept, r0 + sub - first_kept), part)

        u_ref[rows, :] = jnp.dot(hb, w_ref[:, :POOL_WIDTH], preferred_element_type=F32)
        p_gate = jnp.dot(hb, w_ref[:, _GB_OFF:], preferred_element_type=F32)
        gb_ref[rows, :] = p_gate[:, :CONV_WIDTH]
        z_ref[rows, :] = p_gate[:, CONV_WIDTH:2 * CONV_WIDTH] * p_gate[:, 2 * CONV_WIDTH:]


def _proj(x, g1, w_in, qg, kg, rope, hm, *, layer, tm, batch=None, seq=None, prev_kvt=None):
    rows = x.shape[0]
    stacked = prev_kvt is not None
    sub = min(PROJ_SUB, tm)
    n_tab = rope[0].shape[0] // tm
    class_major = batch is not None
    tiles = seq // tm if class_major else 1
    row = lambda c: pl.BlockSpec((tm, c), lambda i: (i, 0))
    tab = pl.BlockSpec((tm, LANES), lambda i: (i % n_tab, 0))
    out_shape = [jax.ShapeDtypeStruct((rows, POOL_WIDTH), F32)] + [jax.ShapeDtypeStruct((rows, CONV_WIDTH), F32)] * 2
    out_specs = [row(POOL_WIDTH), row(CONV_WIDTH), row(CONV_WIDTH)]
    scratch = []
    prev_specs = []
    if class_major:
        for width in (GROUP_WIDTH, 2 * GROUP_WIDTH):
            for (_, dil) in DIL_PAIRS:
                out_shape.append(jax.ShapeDtypeStruct((batch, dil, seq // dil, width), BF16))
                out_specs.append(pl.BlockSpec((1, dil, tm // dil, width), lambda i: (i // tiles, 0, i % tiles, 0)))
        for (window, _) in DIL_PAIRS:
            assert window >= seq or window <= tm, "kv buffer rows must come from the last row tile"
            cols = tm if window >= seq else window
            col_blk = (lambda i: i % tiles) if window >= seq else (lambda i: 0)
            kvt_spec = pl.BlockSpec((1, 2 * GROUP_WIDTH, cols), lambda i, c=col_blk: (i // tiles, 0, c(i)))
            if stacked:
                prev_specs.append(kvt_spec)
                out_shape.append(jax.ShapeDtypeStruct((2, batch, 2 * GROUP_WIDTH, window), F32))
                out_specs.append(pl.BlockSpec((2, 1, 2 * GROUP_WIDTH, cols),
                                              lambda i, c=col_blk: (0, i // tiles, 0, c(i))))
            else:
                out_shape.append(jax.ShapeDtypeStruct((batch, 2 * GROUP_WIDTH, window), F32))
                out_specs.append(kvt_spec)
        scratch = [pltpu.VMEM((3 * N_DIL * (tm // sub), sub, GROUP_WIDTH), F32)]
    else:
        out_shape += [jax.ShapeDtypeStruct((rows, GROUP_WIDTH), F32)] * N_DIL
        out_shape += [jax.ShapeDtypeStruct((rows, 2 * GROUP_WIDTH), F32)] * N_DIL
        out_specs += [row(GROUP_WIDTH)] * N_DIL + [row(2 * GROUP_WIDTH)] * N_DIL
    return pl.pallas_call(
        functools.partial(_proj_kernel, class_major=class_major, tiles=tiles, sub=sub, stacked=stacked),
        grid=(rows // tm,),
        in_specs=[row(D_MODEL), _const_spec((1, D_MODEL)), _layer_spec((D_MODEL, IN_COLS), layer),
                  _const_spec((1, LANES)), _const_spec((1, LANES)), tab, tab, tab,
                  _const_spec((LANES, LANES))] + prev_specs,
        out_specs=out_specs,
        out_shape=out_shape,
        scratch_shapes=scratch,
        compiler_params=_params(),
        name="proj",
    )(x, g1, w_in, qg, kg, *rope, hm, *(prev_kvt or ()))


def _lane_lt(shape, bound):
    return lax.broadcasted_iota(jnp.int32, shape, len(shape) - 1) < bound


def _stack_heads(q):
    first = _lane_lt(q.shape, HEAD_DIM)
    zero = jnp.zeros_like(q)
    return jnp.concatenate([jnp.where(first, q, zero), jnp.where(first, zero, q)], axis=0)


def _unstack(o2, l2, m2):
    n = o2.shape[0] // 2
    first = _lane_lt((n, GROUP_WIDTH), HEAD_DIM)
    return tuple(jnp.where(first, a[:n], a[n:]) for a in (o2, l2, m2))


def _merge(a, b):
    m = jnp.maximum(a[2], b[2])
    wa = jnp.exp(a[2] - m)
    wb = jnp.exp(b[2] - m)
    return wa * a[0] + wb * b[0], wa * a[1] + wb * b[1], m


def _attn_unit(q, kv, bias):
    nk = kv.shape[0]
    k = kv[:, :GROUP_WIDTH]
    v1 = jnp.concatenate([kv[:, GROUP_WIDTH:], jnp.ones((nk, GROUP_WIDTH), BF16)], axis=1)
    s = lax.dot_general(_stack_heads(q), k, _NT, preferred_element_type=F32) + bias
    mrow = jnp.max(s, axis=-1, keepdims=True)
    p = jnp.exp(s - mrow).astype(BF16)
    r = jnp.dot(p, v1, preferred_element_type=F32)
    return _unstack(r[:, :GROUP_WIDTH], r[:, GROUP_WIDTH:], mrow)


def _attn_p_kernel(qc0, qc1, qc2, kvc0, kvc1, kvc2, y_ref, *scratch, seq):
    parts = [scratch[3 * g:3 * g + 3] for g in range(N_DIL)]
    bias_ref = scratch[3 * N_DIL]
    acc_o, acc_l, acc_m = parts[0]
    @pl.when(pl.program_id(0) == 0)
    def _():
        qi = lax.broadcasted_iota(jnp.int32, (2 * ATTN_TILE, 2 * ATTN_TILE), 0) & (ATTN_TILE - 1)
        ki = lax.broadcasted_iota(jnp.int32, (2 * ATTN_TILE, 2 * ATTN_TILE), 1)
        for first_key_back in (0, 1):
            dist = qi - ki + first_key_back * ATTN_TILE
            bias_ref[first_key_back] = jnp.where((dist >= 0) & (dist <= N_BACK), 0.0, NEG_INF)

    for g, ((_, dil), q_ref, kv_ref) in enumerate(zip(DIL_PAIRS, (qc0, qc1, qc2), (kvc0, kvc1, kvc2))):
        dst = parts[g]
        n_sub = seq // dil // ATTN_TILE

        def unit(idx, q_ref=q_ref, kv_ref=kv_ref, n_sub=n_sub, dil=dil):
            if n_sub == 1:
                r, sub = idx, 0
                res = _attn_unit(q_ref[0, r], kv_ref[0, r], bias_ref[0, :, :ATTN_TILE])
            else:
                r, sub = (idx // n_sub, idx % n_sub) if dil > 1 else (0, idx)
                back = jnp.minimum(sub, 1)
                q0 = pl.multiple_of(sub * ATTN_TILE, ATTN_TILE)
                k0 = pl.multiple_of((sub - back) * ATTN_TILE, ATTN_TILE)
                res = _attn_unit(q_ref[0, r, pl.ds(q0, ATTN_TILE), :], kv_ref[0, r, pl.ds(k0, 2 * ATTN_TILE), :],
                                 bias_ref[back])
            start = sub * (ATTN_TILE * dil) + r
            rows = pl.ds(start, ATTN_TILE, stride=dil) if dil > 1 else pl.ds(pl.multiple_of(start, ATTN_TILE), ATTN_TILE)
            return rows, res

        def units_body(it, carry, unit=unit, dst=dst):
            for j in range(ATTN_UNROLL):
                rows, res = unit(it * ATTN_UNROLL + j)
                for ref, val in zip(dst, res):
                    ref[rows, :] = val
            return carry
        n_iter = dil * n_sub // ATTN_UNROLL
        if n_iter == 1:
            units_body(0, 0)
        else:
            lax.fori_loop(0, n_iter, units_body, 0)

        if g > 0:
            for c in range(seq // 256):
                rows = slice(c * 256, (c + 1) * 256)
                o, l, m = _merge((acc_o[rows, :], acc_l[rows, :], acc_m[rows, :]),
                                 tuple(ref[rows, :] for ref in parts[g]))
                if g == N_DIL - 1:
                    y_ref[0, rows, :] = (o / l).astype(y_ref.dtype)
                else:
                    acc_o[rows, :] = o
                    acc_l[rows, :] = l
                    acc_m[rows, :] = m


def _attn_p(qcs, kvcs, *, batch, seq):
    in_specs = [pl.BlockSpec((1,) + a.shape[1:], lambda b: (b, 0, 0, 0)) for a in (*qcs, *kvcs)]
    acc = pltpu.VMEM((seq, GROUP_WIDTH), F32)
    return pl.pallas_call(
        functools.partial(_attn_p_kernel, seq=seq),
        grid=(batch,),
        in_specs=in_specs,
        out_specs=pl.BlockSpec((1, seq, GROUP_WIDTH), lambda b: (b, 0, 0)),
        out_shape=jax.ShapeDtypeStruct((batch, seq, GROUP_WIDTH), BF16),
        scratch_shapes=[acc] * (3 * N_DIL) + [pltpu.VMEM((2, 2 * ATTN_TILE, 2 * ATTN_TILE), F32)],
        compiler_params=_params(),
        name="attn_p",
    )(*qcs, *kvcs)


def _pool_rows(ext, pos, pw, ps):
    return jnp.dot(_pool_diff(ext, pos), pw, preferred_element_type=F32) * ps


def _pool_diff(ext, pos):
    s2 = ext + pltpu.roll(ext, 1, 0)
    s4 = s2 + pltpu.roll(s2, 2, 0)
    s8 = s4 + pltpu.roll(s4, 4, 0)
    s16 = s8 + pltpu.roll(s8, 8, 0)
    lane = lax.broadcasted_iota(jnp.int32, (1, POOL_WIDTH), 1)
    grp = [lane < (j + 1) * POOL_GROUP for j in range(3)]
    win = jnp.where(grp[0], s2, jnp.where(grp[1], s4, jnp.where(grp[2], s8, s16)))[HIST_ROWS:]
    width = jnp.where(grp[0], POOL_WINDOWS[0], jnp.where(grp[1], POOL_WINDOWS[1],
                      jnp.where(grp[2], POOL_WINDOWS[2], POOL_WINDOWS[3])))
    cnt = jnp.minimum(pos + 1, width).astype(F32)
    return (win / cnt - ext[HIST_ROWS:]).astype(BF16)


def _conv_rows(ext, gb, cw, hist):
    y = cw[0:1] * pltpu.roll(ext, 2, 0) + cw[1:2] * pltpu.roll(ext, 1, 0) + cw[2:3] * ext
    return gb * y[hist:]


def _attn_s_group(q, new_t, cache_t, window, dil, shifted):
    t = q.shape[0]
    q2 = _stack_heads(q).astype(BF16)
    s_c = jnp.dot(q2, cache_t[:GROUP_WIDTH].astype(BF16), preferred_element_type=F32)
    s_n = jnp.dot(q2, new_t[:GROUP_WIDTH].astype(BF16), preferred_element_type=F32)
    tq = lax.broadcasted_iota(jnp.int32, (2 * t, 1), 0) & (t - 1)
    rc = lax.broadcasted_iota(jnp.int32, (1, window), 1)
    if shifted:
        rc = (rc + t) & (window - 1)
    dist_c = window + tq - rc
    ok_c = (dist_c <= window) & ((dist_c & (dil - 1)) == 0) & (PAST_LEN - window + rc >= 0)
    tn = lax.broadcasted_iota(jnp.int32, (1, LANES), 1) - (LANES - t)
    dist_n = tq - tn
    ok_n = (tn >= 0) & (dist_n >= 0) & ((dist_n & (dil - 1)) == 0)
    s_c = jnp.where(ok_c, s_c, NEG_INF)
    s_n = jnp.where(ok_n, s_n, NEG_INF)
    mrow = jnp.maximum(jnp.max(s_c, axis=-1, keepdims=True), jnp.max(s_n, axis=-1, keepdims=True))
    p_c = jnp.exp(s_c - mrow)
    p_n = jnp.exp(s_n - mrow)
    o2 = (lax.dot_general(p_c.astype(BF16), cache_t[GROUP_WIDTH:].astype(BF16), _NT, preferred_element_type=F32)
          + lax.dot_general(p_n.astype(BF16), new_t[GROUP_WIDTH:].astype(BF16), _NT, preferred_element_type=F32))
    l2 = jnp.sum(p_c, axis=-1, keepdims=True) + jnp.sum(p_n, axis=-1, keepdims=True)
    return _unstack(o2, l2, mrow)


def _mix_s_kernel(u_ref, q0_ref, q1_ref, q2_ref, kn0_ref, kn1_ref, kn2_ref, gb_ref, z_ref, pst_ref, cst_ref,
                  pw_ref, ps_ref, cw_ref, *rest, t_new, first, n_per):
    yp_ref, ya_ref, yc_ref, nc0_ref, nc1_ref, nc2_ref = rest[-6:]
    cache_refs = rest[:-6]
    keep_old = _lane_lt((2 * GROUP_WIDTH, LANES), LANES - t_new)
    pos = PAST_LEN + lax.broadcasted_iota(jnp.int32, (t_new, 1), 0)
    for i in range(n_per):
        state = None
        for g, ((window, dil), q_ref, kn_ref, nc_ref) in enumerate(zip(
                DIL_PAIRS, (q0_ref, q1_ref, q2_ref), (kn0_ref, kn1_ref, kn2_ref), (nc0_ref, nc1_ref, nc2_ref))):
            kv_new = kn_ref[i]
            new_t = jnp.concatenate([jnp.zeros((LANES - t_new, 2 * GROUP_WIDTH), F32), kv_new], axis=0).T
            cache_t = cache_refs[g][0, i]
            part = _attn_s_group(q_ref[i], new_t, cache_t, window, dil, shifted=not first)
            state = part if state is None else _merge(state, part)
            rolled = pltpu.roll(cache_t, window - t_new, 1) if first else cache_t
            last = jnp.where(keep_old, rolled[:, window - LANES:], new_t)
            if first:
                if window > LANES:
                    nc_ref[0, i, :, :window - LANES] = rolled[:, :window - LANES]
                nc_ref[0, i, :, window - LANES:] = last
                nc_ref[1, i] = pltpu.roll(cache_refs[N_DIL + g][0, i], window - t_new, 1)
            else:
                nc_ref[0, i] = last
        ya_ref[i] = state[0] / state[1]

        u_ext = jnp.concatenate([pst_ref[i], u_ref[i]], axis=0)
        yp_ref[i] = _pool_rows(u_ext, pos, pw_ref[...], ps_ref[...])
        z_ext = jnp.concatenate([cst_ref[i], z_ref[i]], axis=0)
        yc_ref[i] = _conv_rows(z_ext, gb_ref[i], cw_ref[...], cst_ref.shape[1])


def _mix_s(u, qs, kns, gb, z, pst, cst, caches_t, pw, ps, cw, prev, *, n_seq, t_new):
    first = prev is None
    n_per = MIX_S_SEQS_FIRST if first else MIX_S_SEQS
    blk = lambda r, c: pl.BlockSpec((n_per, r, c), lambda b: (b, 0, 0))
    cache_blk = lambda layer, w: pl.BlockSpec((1, n_per, 2 * GROUP_WIDTH, w), lambda b: (layer, b, 0, 0))
    three = lambda a: a.reshape(n_seq, -1, a.shape[-1])
    windows = [w for (w, _) in DIL_PAIRS]
    in_specs = ([blk(t_new, POOL_WIDTH)] + [blk(t_new, GROUP_WIDTH)] * N_DIL + [blk(t_new, 2 * GROUP_WIDTH)] * N_DIL
                + [blk(t_new, CONV_WIDTH)] * 2 + [blk(pst.shape[1], POOL_WIDTH), blk(cst.shape[1], CONV_WIDTH)]
                + [_const_spec((POOL_WIDTH, POOL_WIDTH)), _const_spec((1, POOL_WIDTH)),
                   _const_spec((CONV_K, CONV_WIDTH))])
    args = [three(u), *[three(q) for q in qs], *[three(k) for k in kns], three(gb), three(z), pst, cst, pw, ps, cw]
    aliases = {}
    if first:
        assert all(c.shape[0] == 2 for c in caches_t), "stacked kv buffer update is written for two layers"
        in_specs += [cache_blk(0, w) for w in windows] + [cache_blk(1, w) for w in windows]
        args += [*caches_t, *caches_t]
        cache_out = [pl.BlockSpec((2, n_per, 2 * GROUP_WIDTH, w), lambda b: (0, b, 0, 0)) for w in windows]
    else:
        for k, (p, w) in enumerate(zip(prev, windows)):
            aliases[len(args)] = 3 + k
            in_specs.append(cache_blk(1, w))
            args.append(p)
        cache_out = [pl.BlockSpec((1, n_per, 2 * GROUP_WIDTH, LANES), lambda b, w=w: (1, b, 0, w // LANES - 1))
                     for w in windows]
    out_specs = [blk(t_new, POOL_WIDTH), blk(t_new, GROUP_WIDTH), blk(t_new, CONV_WIDTH)] + cache_out
    out_shape = ([jax.ShapeDtypeStruct((n_seq, t_new, c), F32) for c in (POOL_WIDTH, GROUP_WIDTH, CONV_WIDTH)]
                 + [jax.ShapeDtypeStruct(c.shape, F32) for c in caches_t])
    return pl.pallas_call(
        functools.partial(_mix_s_kernel, t_new=t_new, first=first, n_per=n_per),
        grid=(n_seq // n_per,),
        in_specs=in_specs,
        out_specs=out_specs,
        out_shape=out_shape,
        input_output_aliases=aliases,
        compiler_params=_params(),
        name="mix_s",
    )(*args)


def _mlp_head(x, yp, ya, yc, wo_ref, g2_ref):
    mixed = jnp.concatenate([yp.astype(BF16), ya.astype(BF16), yc.astype(BF16)], axis=1)
    x1 = x + jnp.dot(mixed, wo_ref[...], preferred_element_type=F32)
    ms = jnp.mean(x1 * x1, axis=-1, keepdims=True)
    return x1, (x1 * lax.rsqrt(ms + EPS) * g2_ref[...]).astype(BF16)


def _mlp_tail(x1, hb, wu_ref, wd_ref, tf):
    acc = x1
    for c in range(D_FF // tf):
        hf = jnp.dot(hb, wu_ref[:, c * tf:(c + 1) * tf], preferred_element_type=F32)
        act = jnp.square(jnp.maximum(hf, 0.0)).astype(BF16)
        acc = acc + jnp.dot(act, wd_ref[c * tf:(c + 1) * tf, :], preferred_element_type=F32)
    return acc


def _mlp_s_kernel(x_ref, yp_ref, ya_ref, yc_ref, wo_ref, g2_ref, wu_ref, wd_ref, o_ref, hb_buf):
    @pl.when(pl.program_id(0) == 0)
    def _():
        x1, hb = _mlp_head(x_ref[...], yp_ref[...], ya_ref[...], yc_ref[...], wo_ref, g2_ref)
        o_ref[...] = x1
        hb_buf[...] = hb

    hf = jnp.dot(hb_buf[...], wu_ref[...], preferred_element_type=F32)
    act = jnp.square(jnp.maximum(hf, 0.0)).astype(BF16)
    o_ref[...] += jnp.dot(act, wd_ref[...], preferred_element_type=F32)


def _mlp_p_kernel(x_ref, u_ref, uh_ref, gb_ref, z_ref, zh_ref, ya_ref, pw_ref, ps_ref, cw_ref,
                  wo_ref, g2_ref, wu_ref, wd_ref, o_ref, *, tf, tiles):
    tm = x_ref.shape[0]
    tile = pl.program_id(0) % tiles
    has_hist = tile > 0
    pos = tile * tm + lax.broadcasted_iota(jnp.int32, (tm, 1), 0)
    yp = _pool_rows(jnp.concatenate([jnp.where(has_hist, uh_ref[...], 0.0), u_ref[...]], axis=0), pos,
                    pw_ref[...], ps_ref[...])
    yc = _conv_rows(jnp.concatenate([jnp.where(has_hist, zh_ref[...], 0.0), z_ref[...]], axis=0),
                    gb_ref[...], cw_ref[...], HIST_ROWS)
    x1, hb = _mlp_head(x_ref[...], yp, ya_ref[...], yc, wo_ref, g2_ref)
    o_ref[...] = _mlp_tail(x1, hb, wu_ref, wd_ref, tf)


def _mlp_s(x, yp, ya, yc, w_out, g2, w_up, w_down, *, layer):
    rows = x.shape[0]
    tf = 512
    full = lambda c: pl.BlockSpec((rows, c), lambda j: (0, 0))
    return pl.pallas_call(
        _mlp_s_kernel,
        grid=(D_FF // tf,),
        in_specs=[full(D_MODEL), full(POOL_WIDTH), full(GROUP_WIDTH), full(CONV_WIDTH),
                  _layer_spec((MIX_OUT, D_MODEL), layer), _const_spec((1, D_MODEL)),
                  pl.BlockSpec((None, D_MODEL, tf), lambda j: (layer, 0, j)),
                  pl.BlockSpec((None, tf, D_MODEL), lambda j: (layer, j, 0))],
        out_specs=full(D_MODEL),
        out_shape=jax.ShapeDtypeStruct((rows, D_MODEL), F32),
        scratch_shapes=[pltpu.VMEM((rows, D_MODEL), BF16)],
        compiler_params=_params(),
        name="mlp_s",
    )(x, yp, ya, yc, w_out, g2, w_up, w_down)


def _mlp_p(x, u, gb, z, ya, pw, ps, cw, w_out, g2, w_up, w_down, *, layer, tm, seq):
    rows = x.shape[0]
    row = lambda c: pl.BlockSpec((tm, c), lambda i: (i, 0))
    hist = lambda c: pl.BlockSpec((HIST_ROWS, c), lambda i: (jnp.maximum(i * (tm // HIST_ROWS) - 1, 0), 0))
    return pl.pallas_call(
        functools.partial(_mlp_p_kernel, tf=512, tiles=seq // tm),
        grid=(rows // tm,),
        in_specs=[row(D_MODEL), row(POOL_WIDTH), hist(POOL_WIDTH), row(CONV_WIDTH), row(CONV_WIDTH),
                  hist(CONV_WIDTH), row(GROUP_WIDTH), _const_spec((POOL_WIDTH, POOL_WIDTH)),
                  _const_spec((1, POOL_WIDTH)), _const_spec((CONV_K, CONV_WIDTH)),
                  _layer_spec((MIX_OUT, D_MODEL), layer), _const_spec((1, D_MODEL)),
                  _layer_spec((D_MODEL, D_FF), layer), _layer_spec((D_FF, D_MODEL), layer)],
        out_specs=row(D_MODEL),
        out_shape=jax.ShapeDtypeStruct((rows, D_MODEL), F32),
        compiler_params=_params(),
        name="mlp_p",
    )(x, u, u, gb, z, z, ya, pw, ps, cw, w_out, g2, w_up, w_down)


def _rope_tables(pos):
    half = ROPE_DIM // 2
    inv = jnp.power(jnp.float32(ROPE_THETA), -jnp.arange(half, dtype=F32) / half)
    ang = pos.astype(F32)[:, None] * inv[None, :]
    cos, sin = jnp.cos(ang), jnp.sin(ang)
    n = pos.shape[0]
    rest = HEAD_DIM - ROPE_DIM
    zh = jnp.zeros((n, half), F32)
    c = jnp.concatenate([cos, cos, jnp.ones((n, rest), F32)], axis=1)
    a = jnp.concatenate([-sin, zh, jnp.zeros((n, rest), F32)], axis=1)
    b = jnp.concatenate([zh, sin, jnp.zeros((n, rest), F32)], axis=1)
    return tuple(jnp.tile(t, (1, GROUP_WIDTH // HEAD_DIM)) for t in (c, a, b))


def _to_buffer_layout(c):
    lead = c.shape[:-4]
    n = len(lead)
    t = jnp.transpose(c, (*range(n), n + 1, n + 2, n + 3, n))
    return t.reshape(*lead, 2 * GROUP_WIDTH, c.shape[-4])


def _from_buffer_layout(t):
    lead = t.shape[:-2]
    n = len(lead)
    c = t.reshape(*lead, 2, 2, HEAD_DIM, t.shape[-1])
    return jnp.transpose(c, (*range(n), n + 3, n, n + 1, n + 2))


def kernel(x_prompt, x_sample, state_pool, state_conv, cache_kv_w128, cache_kv_w512, cache_kv_w2048,
           norm1_g, w_in, q_norm_g, k_norm_g, pool_w, pool_scale, conv_w, w_out, norm2_g, w_up, w_down):
    batch, seq, _ = x_prompt.shape
    n_seq, t_new, _ = x_sample.shape
    depth = w_in.shape[0]
    assert depth == 2, "the in-place stacking of the kv state outputs is written for two layers"
    caches_t = [_to_buffer_layout(c) for c in (cache_kv_w128, cache_kv_w512, cache_kv_w2048)]

    rope_p = _rope_tables(jnp.arange(seq, dtype=jnp.int32))
    rope_s = _rope_tables(jnp.tile(PAST_LEN + jnp.arange(t_new, dtype=jnp.int32), n_seq))
    head_id = jnp.arange(LANES) // HEAD_DIM
    hm = jnp.where(head_id[:, None] == head_id[None, :], 1.0 / HEAD_DIM, 0.0).astype(BF16)
    two_heads = lambda gain: jnp.tile(gain, GROUP_WIDTH // HEAD_DIM)[None, :]

    xp = x_prompt.reshape(batch * seq, D_MODEL)
    xs = x_sample.reshape(n_seq * t_new, D_MODEL)
    outs = {k: [] for k in ("pool_p", "conv_p", "pool_s", "conv_s")}
    w_in_b, w_out_b, w_up_b, w_down_b = (w.astype(BF16) for w in (w_in, w_out, w_up, w_down))
    new_caches = None
    kvts = None
    for layer in range(depth):
        pw = jax.scipy.linalg.block_diag(*[pool_w[layer, j] for j in range(len(POOL_WINDOWS))]).astype(BF16)
        ps = pool_scale[layer][None, :]
        cw = conv_w[layer]
        g1 = norm1_g[layer][None, :]
        g2 = norm2_g[layer][None, :]
        qg = two_heads(q_norm_g[layer]) * (HEAD_DIM ** -0.5)
        kg = two_heads(k_norm_g[layer])

        (u, gb, z, qc0, qc1, qc2, kvc0, kvc1, kvc2, *kvts) = _proj(
            xp, g1, w_in_b, qg, kg, rope_p, hm, layer=layer, tm=PROJ_TILE, batch=batch, seq=seq, prev_kvt=kvts)
        ya = _attn_p((qc0, qc1, qc2), (kvc0, kvc1, kvc2), batch=batch, seq=seq)
        xp = _mlp_p(xp, u, gb, z, ya.reshape(-1, GROUP_WIDTH), pw, ps, cw, w_out_b, g2, w_up_b, w_down_b,
                    layer=layer, tm=512, seq=seq)
        outs["pool_p"].append(u.reshape(batch, seq, POOL_WIDTH)[:, seq - POOL_HIST:])
        outs["conv_p"].append(z.reshape(batch, seq, CONV_WIDTH)[:, seq - (CONV_K - 1):])

        us, gbs, zs, q0, q1, q2, kn0, kn1, kn2 = _proj(xs, g1, w_in_b, qg, kg, rope_s, hm, layer=layer,
                                                       tm=n_seq * t_new)
        pst = jnp.pad(state_pool[layer], ((0, 0), (HIST_ROWS - POOL_HIST, 0), (0, 0)))
        cst = jnp.pad(state_conv[layer], ((0, 0), (8 - (CONV_K - 1), 0), (0, 0)))
        yps, yas, ycs, *new_caches = _mix_s(us, (q0, q1, q2), (kn0, kn1, kn2), gbs, zs, pst, cst, caches_t,
                                           pw, ps, cw, new_caches, n_seq=n_seq, t_new=t_new)
        xs = _mlp_s(xs, yps.reshape(-1, POOL_WIDTH), yas.reshape(-1, GROUP_WIDTH), ycs.reshape(-1, CONV_WIDTH),
                    w_out_b, g2, w_up_b, w_down_b, layer=layer)
        us3 = us.reshape(n_seq, t_new, POOL_WIDTH)
        zs3 = zs.reshape(n_seq, t_new, CONV_WIDTH)
        outs["pool_s"].append(jnp.concatenate([state_pool[layer], us3], axis=1)[:, -POOL_HIST:])
        outs["conv_s"].append(jnp.concatenate([state_conv[layer], zs3], axis=1)[:, -(CONV_K - 1):])

    st = lambda k: jnp.stack(outs[k])
    return (xp.reshape(batch, seq, D_MODEL), xs.reshape(n_seq, t_new, D_MODEL),
            st("pool_p"), st("conv_p"),
            *[_from_buffer_layout(kvt) for kvt in kvts],
            st("pool_s"), st("conv_s"),
            *[_from_buffer_layout(c) for c in new_caches])
```

```python
import functools

import jax
import jax.numpy as jnp
from jax import lax
from jax.experimental import pallas as pl
from jax.experimental.pallas import tpu as pltpu

D_MODEL = 1024
HEAD_DIM = 64
POOL_WIDTH = 256
POOL_WINDOWS = (2, 4, 8, 16)
POOL_GROUP = 64
POOL_HIST = 15
ATTN_WIDTH = 384
DIL_PAIRS = ((128, 1), (512, 4), (2048, 16))
N_DIL = 3
GROUP_WIDTH = 128
CONV_WIDTH = 384
CONV_K = 3
ROPE_DIM = 16
ROPE_THETA = 500000.0
D_FF = 4096
IN_COLS = 2560
MIX_OUT = 768
EPS = 1e-6
NEG_INF = -1e30
PAST_LEN = 8192
N_BACK = 128

LANES = 128
HIST_ROWS = 16
ATTN_TILE = 128
ATTN_UNROLL = 16
PROJ_TILE = 1024
PROJ_SUB = 512
MIX_S_SEQS_FIRST = 2
MIX_S_SEQS = 4
VMEM_LIMIT = 56 * 1024 * 1024

F32 = jnp.float32
BF16 = jnp.bfloat16

_Q_OFF = POOL_WIDTH
_K_OFF = _Q_OFF + ATTN_WIDTH
_V_OFF = _K_OFF + ATTN_WIDTH
_GB_OFF = _V_OFF + ATTN_WIDTH
_GC_OFF = _GB_OFF + CONV_WIDTH
_GH_OFF = _GC_OFF + CONV_WIDTH

_NT = (((1,), (1,)), ((), ()))


def _const_spec(shape):
    return pl.BlockSpec(shape, lambda *_: (0,) * len(shape), pipeline_mode=pl.Buffered(1))


def _params():
    return pltpu.CompilerParams(dimension_semantics=("arbitrary",), vmem_limit_bytes=VMEM_LIMIT)


def _proj_kernel(x_ref, g1_ref, w_ref, qg_ref, kg_ref, rc_ref, ra_ref, rb_ref, hm_ref, *rest,
                 class_major, tiles, sub, stacked):
    prev_kvt = None
    if class_major:
        if stacked:
            prev_kvt, rest = rest[:N_DIL], rest[N_DIL:]
        (u_ref, gb_ref, z_ref, qc0, qc1, qc2, kvc0, kvc1, kvc2, kvt0, kvt1, kvt2, nat) = rest
    else:
        (u_ref, gb_ref, z_ref, q0, q1, q2, kv0, kv1, kv2) = rest
    tm = x_ref.shape[0]
    hm = hm_ref[...]

    for s in range(tm // sub):
        r0 = s * sub
        rows = slice(r0, r0 + sub)
        x = x_ref[rows, :]
        ms = jnp.mean(x * x, axis=-1, keepdims=True)
        hb = (x * lax.rsqrt(ms + EPS) * g1_ref[...]).astype(BF16)
        p_att = jnp.dot(hb, w_ref[:, _Q_OFF:_GB_OFF], preferred_element_type=F32)
        rc, ra, rb = rc_ref[rows, :], ra_ref[rows, :], rb_ref[rows, :]

        def head_norm_rope(xb, gain):
            msq = jnp.dot((xb * xb).astype(BF16), hm, preferred_element_type=F32)
            xn = xb * lax.rsqrt(msq + EPS) * gain
            return xn * rc + pltpu.roll(xn, LANES - ROPE_DIM // 2, 1) * ra + pltpu.roll(xn, ROPE_DIM // 2, 1) * rb

        for g, (window, dil) in enumerate(DIL_PAIRS):
            lanes = slice(g * GROUP_WIDTH, (g + 1) * GROUP_WIDTH)
            qn = head_norm_rope(p_att[:, :ATTN_WIDTH][:, lanes], qg_ref[...])
            kn = head_norm_rope(p_att[:, ATTN_WIDTH:2 * ATTN_WIDTH][:, lanes], kg_ref[...])
            vv = p_att[:, 2 * ATTN_WIDTH:][:, lanes]
            if not class_major:
                q_ref, kv_ref = ((q0, kv0), (q1, kv1), (q2, kv2))[g]
                q_ref[rows, :] = qn
                kv_ref[rows, :GROUP_WIDTH] = kn
                kv_ref[rows, GROUP_WIDTH:] = vv
                continue
            qc_ref, kvc_ref, kvt_ref = ((qc0, kvc0, kvt0), (qc1, kvc1, kvt1), (qc2, kvc2, kvt2))[g]
            slab = 3 * (s * N_DIL + g)
            nat[slab] = qn
            nat[slab + 1] = kn
            nat[slab + 2] = vv
            per = sub // dil
            dst = slice(s * per, (s + 1) * per)
            for r in range(dil):
                src = pl.ds(r, per, stride=dil) if dil > 1 else slice(None)
                qc_ref[0, r, dst, :] = nat[slab, src, :].astype(BF16)
                kvc_ref[0, r, dst, :GROUP_WIDTH] = nat[slab + 1, src, :].astype(BF16)
                kvc_ref[0, r, dst, GROUP_WIDTH:] = nat[slab + 2, src, :].astype(BF16)
            def put_kvt(cols, part, g=g, kvt_ref=kvt_ref):
                if stacked:
                    kvt_ref[0, 0, :, cols] = prev_kvt[g][0, :, cols]
                    kvt_ref[1, 0, :, cols] = part
                else:
                    kvt_ref[0, :, cols] = part

            if window >= tiles * tm:
                put_kvt(rows, jnp.concatenate([kn, vv], axis=1).T)
            else:
                first_kept = tm - min(window, tm)
                lo = max(r0, first_kept)
                if lo < r0 + sub:
                    part = jnp.concatenate([kn[lo - r0:], vv[lo - r0:]], axis=1).T
                    put_kvt(slice(lo - first_kept, r0 + sub - first_kept), part)

        u_ref[rows, :] = jnp.dot(hb, w_ref[:, :POOL_WIDTH], preferred_element_type=F32)
        p_gate = jnp.dot(hb, w_ref[:, _GB_OFF:], preferred_element_type=F32)
        gb_ref[rows, :] = p_gate[:, :CONV_WIDTH]
        z_ref[rows, :] = p_gate[:, CONV_WIDTH:2 * CONV_WIDTH] * p_gate[:, 2 * CONV_WIDTH:]


def _proj(x, g1, w_in, qg, kg, rope, hm, *, tm, batch=None, seq=None, prev_kvt=None):
    rows = x.shape[0]
    stacked = prev_kvt is not None
    sub = min(PROJ_SUB, tm)
    n_tab = rope[0].shape[0] // tm
    class_major = batch is not None
    tiles = seq // tm if class_major else 1
    row = lambda c: pl.BlockSpec((tm, c), lambda i: (i, 0))
    tab = pl.BlockSpec((tm, LANES), lambda i: (i % n_tab, 0))
    out_shape = [jax.ShapeDtypeStruct((rows, POOL_WIDTH), F32)] + [jax.ShapeDtypeStruct((rows, CONV_WIDTH), F32)] * 2
    out_specs = [row(POOL_WIDTH), row(CONV_WIDTH), row(CONV_WIDTH)]
    scratch = []
    prev_specs = []
    if class_major:
        for width in (GROUP_WIDTH, 2 * GROUP_WIDTH):
            for (_, dil) in DIL_PAIRS:
                out_shape.append(jax.ShapeDtypeStruct((batch, dil, seq // dil, width), BF16))
                out_specs.append(pl.BlockSpec((1, dil, tm // dil, width), lambda i: (i // tiles, 0, i % tiles, 0)))
        for (window, _) in DIL_PAIRS:
            assert window >= seq or window <= tm, "kv buffer rows must come from the last row tile"
            cols = tm if window >= seq else window
            col_blk = (lambda i: i % tiles) if window >= seq else (lambda i: 0)
            kvt_spec = pl.BlockSpec((1, 2 * GROUP_WIDTH, cols), lambda i, c=col_blk: (i // tiles, 0, c(i)))
            if stacked:
                prev_specs.append(kvt_spec)
                out_shape.append(jax.ShapeDtypeStruct((2, batch, 2 * GROUP_WIDTH, window), F32))
                out_specs.append(pl.BlockSpec((2, 1, 2 * GROUP_WIDTH, cols),
                                              lambda i, c=col_blk: (0, i // tiles, 0, c(i))))
            else:
                out_shape.append(jax.ShapeDtypeStruct((batch, 2 * GROUP_WIDTH, window), F32))
                out_specs.append(kvt_spec)
        scratch = [pltpu.VMEM((3 * N_DIL * (tm // sub), sub, GROUP_WIDTH), F32)]
    else:
        out_shape += [jax.ShapeDtypeStruct((rows, GROUP_WIDTH), F32)] * N_DIL
        out_shape += [jax.ShapeDtypeStruct((rows, 2 * GROUP_WIDTH), F32)] * N_DIL
        out_specs += [row(GROUP_WIDTH)] * N_DIL + [row(2 * GROUP_WIDTH)] * N_DIL
    return pl.pallas_call(
        functools.partial(_proj_kernel, class_major=class_major, tiles=tiles, sub=sub, stacked=stacked),
        grid=(rows // tm,),
        in_specs=[row(D_MODEL), _const_spec((1, D_MODEL)), _const_spec((D_MODEL, IN_COLS)),
                  _const_spec((1, LANES)), _const_spec((1, LANES)), tab, tab, tab,
                  _const_spec((LANES, LANES))] + prev_specs,
        out_specs=out_specs,
        out_shape=out_shape,
        scratch_shapes=scratch,
        compiler_params=_params(),
        name="proj",
    )(x, g1, w_in, qg, kg, *rope, hm, *(prev_kvt or ()))


def _lane_lt(shape, bound):
    return lax.broadcasted_iota(jnp.int32, shape, len(shape) - 1) < bound


def _stack_heads(q):
    first = _lane_lt(q.shape, HEAD_DIM)
    zero = jnp.zeros_like(q)
    return jnp.concatenate([jnp.where(first, q, zero), jnp.where(first, zero, q)], axis=0)


def _unstack(o2, l2, m2):
    n = o2.shape[0] // 2
    first = _lane_lt((n, GROUP_WIDTH), HEAD_DIM)
    return tuple(jnp.where(first, a[:n], a[n:]) for a in (o2, l2, m2))


def _merge(a, b):
    m = jnp.maximum(a[2], b[2])
    wa = jnp.exp(a[2] - m)
    wb = jnp.exp(b[2] - m)
    return wa * a[0] + wb * b[0], wa * a[1] + wb * b[1], m


def _attn_unit(q, kv, bias):
    nk = kv.shape[0]
    k = kv[:, :GROUP_WIDTH]
    v1 = jnp.concatenate([kv[:, GROUP_WIDTH:], jnp.ones((nk, GROUP_WIDTH), BF16)], axis=1)
    s = lax.dot_general(_stack_heads(q), k, _NT, preferred_element_type=F32) + bias
    mrow = jnp.max(s, axis=-1, keepdims=True)
    p = jnp.exp(s - mrow).astype(BF16)
    r = jnp.dot(p, v1, preferred_element_type=F32)
    return _unstack(r[:, :GROUP_WIDTH], r[:, GROUP_WIDTH:], mrow)


def _attn_p_kernel(qc0, qc1, qc2, kvc0, kvc1, kvc2, y_ref, *scratch, seq):
    parts = [scratch[3 * g:3 * g + 3] for g in range(N_DIL)]
    bias_ref = scratch[3 * N_DIL]
    acc_o, acc_l, acc_m = parts[0]
    @pl.when(pl.program_id(0) == 0)
    def _():
        qi = lax.broadcasted_iota(jnp.int32, (2 * ATTN_TILE, 2 * ATTN_TILE), 0) & (ATTN_TILE - 1)
        ki = lax.broadcasted_iota(jnp.int32, (2 * ATTN_TILE, 2 * ATTN_TILE), 1)
        for first_key_back in (0, 1):
            dist = qi - ki + first_key_back * ATTN_TILE
            bias_ref[first_key_back] = jnp.where((dist >= 0) & (dist <= N_BACK), 0.0, NEG_INF)

    for g, ((_, dil), q_ref, kv_ref) in enumerate(zip(DIL_PAIRS, (qc0, qc1, qc2), (kvc0, kvc1, kvc2))):
        dst = parts[g]
        n_sub = seq // dil // ATTN_TILE

        def unit(idx, q_ref=q_ref, kv_ref=kv_ref, n_sub=n_sub, dil=dil):
            if n_sub == 1:
                r, sub = idx, 0
                res = _attn_unit(q_ref[0, r], kv_ref[0, r], bias_ref[0, :, :ATTN_TILE])
            else:
                r, sub = (idx // n_sub, idx % n_sub) if dil > 1 else (0, idx)
                back = jnp.minimum(sub, 1)
                q0 = pl.multiple_of(sub * ATTN_TILE, ATTN_TILE)
                k0 = pl.multiple_of((sub - back) * ATTN_TILE, ATTN_TILE)
                res = _attn_unit(q_ref[0, r, pl.ds(q0, ATTN_TILE), :], kv_ref[0, r, pl.ds(k0, 2 * ATTN_TILE), :],
                                 bias_ref[back])
            start = sub * (ATTN_TILE * dil) + r
            rows = pl.ds(start, ATTN_TILE, stride=dil) if dil > 1 else pl.ds(pl.multiple_of(start, ATTN_TILE), ATTN_TILE)
            return rows, res

        def units_body(it, carry, unit=unit, dst=dst):
            for j in range(ATTN_UNROLL):
                rows, res = unit(it * ATTN_UNROLL + j)
                for ref, val in zip(dst, res):
                    ref[rows, :] = val
            return carry
        n_iter = dil * n_sub // ATTN_UNROLL
        if n_iter == 1:
            units_body(0, 0)
        else:
            lax.fori_loop(0, n_iter, units_body, 0)

        if g > 0:
            for c in range(seq // 256):
                rows = slice(c * 256, (c + 1) * 256)
                o, l, m = _merge((acc_o[rows, :], acc_l[rows, :], acc_m[rows, :]),
                                 tuple(ref[rows, :] for ref in parts[g]))
                if g == N_DIL - 1:
                    y_ref[0, rows, :] = (o / l).astype(y_ref.dtype)
                else:
                    acc_o[rows, :] = o
                    acc_l[rows, :] = l
                    acc_m[rows, :] = m


def _attn_p(qcs, kvcs, *, batch, seq):
    in_specs = [pl.BlockSpec((1,) + a.shape[1:], lambda b: (b, 0, 0, 0)) for a in (*qcs, *kvcs)]
    acc = pltpu.VMEM((seq, GROUP_WIDTH), F32)
    return pl.pallas_call(
        functools.partial(_attn_p_kernel, seq=seq),
        grid=(batch,),
        in_specs=in_specs,
        out_specs=pl.BlockSpec((1, seq, GROUP_WIDTH), lambda b: (b, 0, 0)),
        out_shape=jax.ShapeDtypeStruct((batch, seq, GROUP_WIDTH), BF16),
        scratch_shapes=[acc] * (3 * N_DIL) + [pltpu.VMEM((2, 2 * ATTN_TILE, 2 * ATTN_TILE), F32)],
        compiler_params=_params(),
        name="attn_p",
    )(*qcs, *kvcs)


def _pool_rows(ext, pos, pw, ps):
    return jnp.dot(_pool_diff(ext, pos), pw, preferred_element_type=F32) * ps


def _pool_diff(ext, pos):
    s2 = ext + pltpu.roll(ext, 1, 0)
    s4 = s2 + pltpu.roll(s2, 2, 0)
    s8 = s4 + pltpu.roll(s4, 4, 0)
    s16 = s8 + pltpu.roll(s8, 8, 0)
    lane = lax.broadcasted_iota(jnp.int32, (1, POOL_WIDTH), 1)
    grp = [lane < (j + 1) * POOL_GROUP for j in range(3)]
    win = jnp.where(grp[0], s2, jnp.where(grp[1], s4, jnp.where(grp[2], s8, s16)))[HIST_ROWS:]
    width = jnp.where(grp[0], POOL_WINDOWS[0], jnp.where(grp[1], POOL_WINDOWS[1],
                      jnp.where(grp[2], POOL_WINDOWS[2], POOL_WINDOWS[3])))
    cnt = jnp.minimum(pos + 1, width).astype(F32)
    return (win / cnt - ext[HIST_ROWS:]).astype(BF16)


def _conv_rows(ext, gb, cw, hist):
    y = cw[0:1] * pltpu.roll(ext, 2, 0) + cw[1:2] * pltpu.roll(ext, 1, 0) + cw[2:3] * ext
    return gb * y[hist:]


def _attn_s_group(q, new_t, cache_t, window, dil, shifted):
    t = q.shape[0]
    q2 = _stack_heads(q).astype(BF16)
    s_c = jnp.dot(q2, cache_t[:GROUP_WIDTH].astype(BF16), preferred_element_type=F32)
    s_n = jnp.dot(q2, new_t[:GROUP_WIDTH].astype(BF16), preferred_element_type=F32)
    tq = lax.broadcasted_iota(jnp.int32, (2 * t, 1), 0) & (t - 1)
    rc = lax.broadcasted_iota(jnp.int32, (1, window), 1)
    if shifted:
        rc = (rc + t) & (window - 1)
    dist_c = window + tq - rc
    ok_c = (dist_c <= window) & ((dist_c & (dil - 1)) == 0) & (PAST_LEN - window + rc >= 0)
    tn = lax.broadcasted_iota(jnp.int32, (1, LANES), 1) - (LANES - t)
    dist_n = tq - tn
    ok_n = (tn >= 0) & (dist_n >= 0) & ((dist_n & (dil - 1)) == 0)
    s_c = jnp.where(ok_c, s_c, NEG_INF)
    s_n = jnp.where(ok_n, s_n, NEG_INF)
    mrow = jnp.maximum(jnp.max(s_c, axis=-1, keepdims=True), jnp.max(s_n, axis=-1, keepdims=True))
    p_c = jnp.exp(s_c - mrow)
    p_n = jnp.exp(s_n - mrow)
    o2 = (lax.dot_general(p_c.astype(BF16), cache_t[GROUP_WIDTH:].astype(BF16), _NT, preferred_element_type=F32)
          + lax.dot_general(p_n.astype(BF16), new_t[GROUP_WIDTH:].astype(BF16), _NT, preferred_element_type=F32))
    l2 = jnp.sum(p_c, axis=-1, keepdims=True) + jnp.sum(p_n, axis=-1, keepdims=True)
    return _unstack(o2, l2, mrow)


def _mix_s_kernel(u_ref, q0_ref, q1_ref, q2_ref, kn0_ref, kn1_ref, kn2_ref, gb_ref, z_ref, pst_ref, cst_ref,
                  pw_ref, ps_ref, cw_ref, *rest, t_new, first, n_per):
    yp_ref, ya_ref, yc_ref, nc0_ref, nc1_ref, nc2_ref = rest[-6:]
    cache_refs = rest[:-6]
    keep_old = _lane_lt((2 * GROUP_WIDTH, LANES), LANES - t_new)
    pos = PAST_LEN + lax.broadcasted_iota(jnp.int32, (t_new, 1), 0)
    for i in range(n_per):
        state = None
        for g, ((window, dil), q_ref, kn_ref, nc_ref) in enumerate(zip(
                DIL_PAIRS, (q0_ref, q1_ref, q2_ref), (kn0_ref, kn1_ref, kn2_ref), (nc0_ref, nc1_ref, nc2_ref))):
            kv_new = kn_ref[i]
            new_t = jnp.concatenate([jnp.zeros((LANES - t_new, 2 * GROUP_WIDTH), F32), kv_new], axis=0).T
            cache_t = cache_refs[g][0, i]
            part = _attn_s_group(q_ref[i], new_t, cache_t, window, dil, shifted=not first)
            state = part if state is None else _merge(state, part)
            rolled = pltpu.roll(cache_t, window - t_new, 1) if first else cache_t
            last = jnp.where(keep_old, rolled[:, window - LANES:], new_t)
            if first:
                if window > LANES:
                    nc_ref[0, i, :, :window - LANES] = rolled[:, :window - LANES]
                nc_ref[0, i, :, window - LANES:] = last
                nc_ref[1, i] = pltpu.roll(cache_refs[N_DIL + g][0, i], window - t_new, 1)
            else:
                nc_ref[0, i] = last
        ya_ref[i] = state[0] / state[1]

        u_ext = jnp.concatenate([pst_ref[i], u_ref[i]], axis=0)
        yp_ref[i] = _pool_rows(u_ext, pos, pw_ref[...], ps_ref[...])
        z_ext = jnp.concatenate([cst_ref[i], z_ref[i]], axis=0)
        yc_ref[i] = _conv_rows(z_ext, gb_ref[i], cw_ref[...], cst_ref.shape[1])


def _mix_s(u, qs, kns, gb, z, pst, cst, caches_t, pw, ps, cw, prev, *, n_seq, t_new):
    first = prev is None
    n_per = MIX_S_SEQS_FIRST if first else MIX_S_SEQS
    blk = lambda r, c: pl.BlockSpec((n_per, r, c), lambda b: (b, 0, 0))
    cache_blk = lambda layer, w: pl.BlockSpec((1, n_per, 2 * GROUP_WIDTH, w), lambda b: (layer, b, 0, 0))
    three = lambda a: a.reshape(n_seq, -1, a.shape[-1])
    windows = [w for (w, _) in DIL_PAIRS]
    in_specs = ([blk(t_new, POOL_WIDTH)] + [blk(t_new, GROUP_WIDTH)] * N_DIL + [blk(t_new, 2 * GROUP_WIDTH)] * N_DIL
                + [blk(t_new, CONV_WIDTH)] * 2 + [blk(pst.shape[1], POOL_WIDTH), blk(cst.shape[1], CONV_WIDTH)]
                + [_const_spec((POOL_WIDTH, POOL_WIDTH)), _const_spec((1, POOL_WIDTH)),
                   _const_spec((CONV_K, CONV_WIDTH))])
    args = [three(u), *[three(q) for q in qs], *[three(k) for k in kns], three(gb), three(z), pst, cst, pw, ps, cw]
    aliases = {}
    if first:
        assert all(c.shape[0] == 2 for c in caches_t), "stacked kv buffer update is written for two layers"
        in_specs += [cache_blk(0, w) for w in windows] + [cache_blk(1, w) for w in windows]
        args += [*caches_t, *caches_t]
        cache_out = [pl.BlockSpec((2, n_per, 2 * GROUP_WIDTH, w), lambda b: (0, b, 0, 0)) for w in windows]
    else:
        for k, (p, w) in enumerate(zip(prev, windows)):
            aliases[len(args)] = 3 + k
            in_specs.append(cache_blk(1, w))
            args.append(p)
        cache_out = [pl.BlockSpec((1, n_per, 2 * GROUP_WIDTH, LANES), lambda b, w=w: (1, b, 0, w // LANES - 1))
                     for w in windows]
    out_specs = [blk(t_new, POOL_WIDTH), blk(t_new, GROUP_WIDTH), blk(t_new, CONV_WIDTH)] + cache_out
    out_shape = ([jax.ShapeDtypeStruct((n_seq, t_new, c), F32) for c in (POOL_WIDTH, GROUP_WIDTH, CONV_WIDTH)]
                 + [jax.ShapeDtypeStruct(c.shape, F32) for c in caches_t])
    return pl.pallas_call(
        functools.partial(_mix_s_kernel, t_new=t_new, first=first, n_per=n_per),
        grid=(n_seq // n_per,),
        in_specs=in_specs,
        out_specs=out_specs,
        out_shape=out_shape,
        input_output_aliases=aliases,
        compiler_params=_params(),
        name="mix_s",
    )(*args)


def _mlp_head(x, yp, ya, yc, wo_ref, g2_ref):
    mixed = jnp.concatenate([yp.astype(BF16), ya.astype(BF16), yc.astype(BF16)], axis=1)
    x1 = x + jnp.dot(mixed, wo_ref[...], preferred_element_type=F32)
    ms = jnp.mean(x1 * x1, axis=-1, keepdims=True)
    return x1, (x1 * lax.rsqrt(ms + EPS) * g2_ref[...]).astype(BF16)


def _mlp_tail(x1, hb, wu_ref, wd_ref, tf):
    acc = x1
    for c in range(D_FF // tf):
        hf = jnp.dot(hb, wu_ref[:, c * tf:(c + 1) * tf], preferred_element_type=F32)
        act = jnp.square(jnp.maximum(hf, 0.0)).astype(BF16)
        acc = acc + jnp.dot(act, wd_ref[c * tf:(c + 1) * tf, :], preferred_element_type=F32)
    return acc


def _mlp_s_kernel(x_ref, yp_ref, ya_ref, yc_ref, wo_ref, g2_ref, wu_ref, wd_ref, o_ref, hb_buf):
    @pl.when(pl.program_id(0) == 0)
    def _():
        x1, hb = _mlp_head(x_ref[...], yp_ref[...], ya_ref[...], yc_ref[...], wo_ref, g2_ref)
        o_ref[...] = x1
        hb_buf[...] = hb

    hf = jnp.dot(hb_buf[...], wu_ref[...], preferred_element_type=F32)
    act = jnp.square(jnp.maximum(hf, 0.0)).astype(BF16)
    o_ref[...] += jnp.dot(act, wd_ref[...], preferred_element_type=F32)


def _mlp_p_kernel(x_ref, u_ref, uh_ref, gb_ref, z_ref, zh_ref, ya_ref, pw_ref, ps_ref, cw_ref,
                  wo_ref, g2_ref, wu_ref, wd_ref, *rest, tf, tiles):
    if len(rest) > 1:
        n = len(rest) // 2
        for src, dst in zip(rest[:n], rest[n + 1:]):
            dst[...] = src[...].astype(BF16)
        o_ref = rest[n]
    else:
        (o_ref,) = rest
    tm = x_ref.shape[0]
    tile = pl.program_id(0) % tiles
    has_hist = tile > 0
    pos = tile * tm + lax.broadcasted_iota(jnp.int32, (tm, 1), 0)
    yp = _pool_rows(jnp.concatenate([jnp.where(has_hist, uh_ref[...], 0.0), u_ref[...]], axis=0), pos,
                    pw_ref[...], ps_ref[...])
    yc = _conv_rows(jnp.concatenate([jnp.where(has_hist, zh_ref[...], 0.0), z_ref[...]], axis=0),
                    gb_ref[...], cw_ref[...], HIST_ROWS)
    x1, hb = _mlp_head(x_ref[...], yp, ya_ref[...], yc, wo_ref, g2_ref)
    o_ref[...] = _mlp_tail(x1, hb, wu_ref, wd_ref, tf)


def _mlp_s(x, yp, ya, yc, w_out, g2, w_up, w_down):
    rows = x.shape[0]
    tf = 512
    full = lambda c: pl.BlockSpec((rows, c), lambda j: (0, 0))
    return pl.pallas_call(
        _mlp_s_kernel,
        grid=(D_FF // tf,),
        in_specs=[full(D_MODEL), full(POOL_WIDTH), full(GROUP_WIDTH), full(CONV_WIDTH),
                  _const_spec((MIX_OUT, D_MODEL)), _const_spec((1, D_MODEL)),
                  pl.BlockSpec((D_MODEL, tf), lambda j: (0, j)),
                  pl.BlockSpec((tf, D_MODEL), lambda j: (j, 0))],
        out_specs=full(D_MODEL),
        out_shape=jax.ShapeDtypeStruct((rows, D_MODEL), F32),
        scratch_shapes=[pltpu.VMEM((rows, D_MODEL), BF16)],
        compiler_params=_params(),
        name="mlp_s",
    )(x, yp, ya, yc, w_out, g2, w_up, w_down)


def _mlp_p(x, u, gb, z, ya, pw, ps, cw, w_out, g2, w_up, w_down, *, tm, seq, convert=None):
    rows = x.shape[0]
    n_tiles = rows // tm
    row = lambda c: pl.BlockSpec((tm, c), lambda i: (i, 0))
    conv_in, conv_out_specs, conv_out_shape, conv_args = [], [], [], []
    if convert is not None:
        layer, weights = convert
        for w in weights:
            _, k, n = w.shape
            conv_in.append(pl.BlockSpec((None, k // n_tiles, n), lambda i: (layer, i, 0)))
            conv_out_specs.append(pl.BlockSpec((k // n_tiles, n), lambda i: (i, 0)))
            conv_out_shape.append(jax.ShapeDtypeStruct((k, n), BF16))
            conv_args.append(w)
    hist = lambda c: pl.BlockSpec((HIST_ROWS, c), lambda i: (jnp.maximum(i * (tm // HIST_ROWS) - 1, 0), 0))
    return pl.pallas_call(
        functools.partial(_mlp_p_kernel, tf=512, tiles=seq // tm),
        grid=(rows // tm,),
        in_specs=[row(D_MODEL), row(POOL_WIDTH), hist(POOL_WIDTH), row(CONV_WIDTH), row(CONV_WIDTH),
                  hist(CONV_WIDTH), row(GROUP_WIDTH), _const_spec((POOL_WIDTH, POOL_WIDTH)),
                  _const_spec((1, POOL_WIDTH)), _const_spec((CONV_K, CONV_WIDTH)),
                  _const_spec((MIX_OUT, D_MODEL)), _const_spec((1, D_MODEL)),
                  _const_spec((D_MODEL, D_FF)), _const_spec((D_FF, D_MODEL))] + conv_in,
        out_specs=[row(D_MODEL)] + conv_out_specs,
        out_shape=[jax.ShapeDtypeStruct((rows, D_MODEL), F32)] + conv_out_shape,
        compiler_params=_params(),
        name="mlp_p",
    )(x, u, u, gb, z, z, ya, pw, ps, cw, w_out, g2, w_up, w_down, *conv_args)


def _rope_tables(pos):
    half = ROPE_DIM // 2
    inv = jnp.power(jnp.float32(ROPE_THETA), -jnp.arange(half, dtype=F32) / half)
    ang = pos.astype(F32)[:, None] * inv[None, :]
    cos, sin = jnp.cos(ang), jnp.sin(ang)
    n = pos.shape[0]
    rest = HEAD_DIM - ROPE_DIM
    zh = jnp.zeros((n, half), F32)
    c = jnp.concatenate([cos, cos, jnp.ones((n, rest), F32)], axis=1)
    a = jnp.concatenate([-sin, zh, jnp.zeros((n, rest), F32)], axis=1)
    b = jnp.concatenate([zh, sin, jnp.zeros((n, rest), F32)], axis=1)
    return tuple(jnp.tile(t, (1, GROUP_WIDTH // HEAD_DIM)) for t in (c, a, b))


def _to_buffer_layout(c):
    lead = c.shape[:-4]
    n = len(lead)
    t = jnp.transpose(c, (*range(n), n + 1, n + 2, n + 3, n))
    return t.reshape(*lead, 2 * GROUP_WIDTH, c.shape[-4])


def _from_buffer_layout(t):
    lead = t.shape[:-2]
    n = len(lead)
    c = t.reshape(*lead, 2, 2, HEAD_DIM, t.shape[-1])
    return jnp.transpose(c, (*range(n), n + 3, n, n + 1, n + 2))


def kernel(x_prompt, x_sample, state_pool, state_conv, cache_kv_w128, cache_kv_w512, cache_kv_w2048,
           norm1_g, w_in, q_norm_g, k_norm_g, pool_w, pool_scale, conv_w, w_out, norm2_g, w_up, w_down):
    batch, seq, _ = x_prompt.shape
    n_seq, t_new, _ = x_sample.shape
    depth = w_in.shape[0]
    assert depth == 2, "the in-place stacking of the kv state outputs is written for two layers"
    caches_t = [_to_buffer_layout(c) for c in (cache_kv_w128, cache_kv_w512, cache_kv_w2048)]

    rope_p = _rope_tables(jnp.arange(seq, dtype=jnp.int32))
    rope_s = _rope_tables(jnp.tile(PAST_LEN + jnp.arange(t_new, dtype=jnp.int32), n_seq))
    head_id = jnp.arange(LANES) // HEAD_DIM
    hm = jnp.where(head_id[:, None] == head_id[None, :], 1.0 / HEAD_DIM, 0.0).astype(BF16)
    two_heads = lambda gain: jnp.tile(gain, GROUP_WIDTH // HEAD_DIM)[None, :]

    xp = x_prompt.reshape(batch * seq, D_MODEL)
    xs = x_sample.reshape(n_seq * t_new, D_MODEL)
    outs = {k: [] for k in ("pool_p", "conv_p", "pool_s", "conv_s")}
    w_in_l, w_up_l, w_down_l = (w[0].astype(BF16) for w in (w_in, w_up, w_down))
    new_caches = None
    kvts = None
    for layer in range(depth):
        pw = jax.scipy.linalg.block_diag(*[pool_w[layer, j] for j in range(len(POOL_WINDOWS))]).astype(BF16)
        ps = pool_scale[layer][None, :]
        cw = conv_w[layer]
        g1 = norm1_g[layer][None, :]
        g2 = norm2_g[layer][None, :]
        qg = two_heads(q_norm_g[layer]) * (HEAD_DIM ** -0.5)
        kg = two_heads(k_norm_g[layer])

        w_out_l = w_out[layer].astype(BF16)
        (u, gb, z, qc0, qc1, qc2, kvc0, kvc1, kvc2, *kvts) = _proj(
            xp, g1, w_in_l, qg, kg, rope_p, hm, tm=PROJ_TILE, batch=batch, seq=seq, prev_kvt=kvts)
        ya = _attn_p((qc0, qc1, qc2), (kvc0, kvc1, kvc2), batch=batch, seq=seq)
        convert = (layer + 1, (w_in, w_up, w_down)) if layer + 1 < depth else None
        xp, *w_next = _mlp_p(xp, u, gb, z, ya.reshape(-1, GROUP_WIDTH), pw, ps, cw, w_out_l, g2, w_up_l, w_down_l,
                             tm=512, seq=seq, convert=convert)
        outs["pool_p"].append(u.reshape(batch, seq, POOL_WIDTH)[:, seq - POOL_HIST:])
        outs["conv_p"].append(z.reshape(batch, seq, CONV_WIDTH)[:, seq - (CONV_K - 1):])

        us, gbs, zs, q0, q1, q2, kn0, kn1, kn2 = _proj(xs, g1, w_in_l, qg, kg, rope_s, hm, tm=n_seq * t_new)
        pst = jnp.pad(state_pool[layer], ((0, 0), (HIST_ROWS - POOL_HIST, 0), (0, 0)))
        cst = jnp.pad(state_conv[layer], ((0, 0), (8 - (CONV_K - 1), 0), (0, 0)))
        yps, yas, ycs, *new_caches = _mix_s(us, (q0, q1, q2), (kn0, kn1, kn2), gbs, zs, pst, cst, caches_t,
                                           pw, ps, cw, new_caches, n_seq=n_seq, t_new=t_new)
        xs = _mlp_s(xs, yps.reshape(-1, POOL_WIDTH), yas.reshape(-1, GROUP_WIDTH), ycs.reshape(-1, CONV_WIDTH),
                    w_out_l, g2, w_up_l, w_down_l)
        us3 = us.reshape(n_seq, t_new, POOL_WIDTH)
        zs3 = zs.reshape(n_seq, t_new, CONV_WIDTH)
        outs["pool_s"].append(jnp.concatenate([state_pool[layer], us3], axis=1)[:, -POOL_HIST:])
        outs["conv_s"].append(jnp.concatenate([state_conv[layer], zs3], axis=1)[:, -(CONV_K - 1):])
        if w_next:
            w_in_l, w_up_l, w_down_l = w_next

    st = lambda k: jnp.stack(outs[k])
    return (xp.reshape(batch, seq, D_MODEL), xs.reshape(n_seq, t_new, D_MODEL),
            st("pool_p"), st("conv_p"),
            *[_from_buffer_layout(kvt) for kvt in kvts],
            st("pool_s"), st("conv_s"),
            *[_from_buffer_layout(c) for c in new_caches])
```

```python
import functools

import jax
import jax.numpy as jnp
from jax import lax
from jax.experimental import pallas as pl
from jax.experimental.pallas import tpu as pltpu

D_MODEL = 1024
HEAD_DIM = 64
POOL_WIDTH = 256
POOL_WINDOWS = (2, 4, 8, 16)
POOL_GROUP = 64
POOL_HIST = 15
ATTN_WIDTH = 384
DIL_PAIRS = ((128, 1), (512, 4), (2048, 16))
N_DIL = 3
GROUP_WIDTH = 128
CONV_WIDTH = 384
CONV_K = 3
ROPE_DIM = 16
ROPE_THETA = 500000.0
D_FF = 4096
IN_COLS = 2560
MIX_OUT = 768
EPS = 1e-6
NEG_INF = -1e30
PAST_LEN = 8192
N_BACK = 128

LANES = 128
HIST_ROWS = 16
ATTN_TILE = 128
ATTN_UNROLL = 16
PROJ_TILE = 1024
PROJ_SUB = 512
MIX_S_SEQS_FIRST = 2
MIX_S_SEQS = 4
VMEM_LIMIT = 56 * 1024 * 1024

F32 = jnp.float32
BF16 = jnp.bfloat16

_Q_OFF = POOL_WIDTH
_K_OFF = _Q_OFF + ATTN_WIDTH
_V_OFF = _K_OFF + ATTN_WIDTH
_GB_OFF = _V_OFF + ATTN_WIDTH
_GC_OFF = _GB_OFF + CONV_WIDTH
_GH_OFF = _GC_OFF + CONV_WIDTH

_NT = (((1,), (1,)), ((), ()))


def _const_spec(shape):
    return pl.BlockSpec(shape, lambda *_: (0,) * len(shape), pipeline_mode=pl.Buffered(1))


def _params():
    return pltpu.CompilerParams(dimension_semantics=("arbitrary",), vmem_limit_bytes=VMEM_LIMIT)


def _cast_specs(convert, n_steps):
    in_specs, out_specs, out_shape, args = [], [], [], []
    if convert is not None:
        layer, weights = convert
        for w in weights:
            _, k, n = w.shape
            in_specs.append(pl.BlockSpec((None, k // n_steps, n), lambda i: (layer, i, 0)))
            out_specs.append(pl.BlockSpec((k // n_steps, n), lambda i: (i, 0)))
            out_shape.append(jax.ShapeDtypeStruct((k, n), BF16))
            args.append(w)
    return in_specs, out_specs, out_shape, args


def _cast_slabs(srcs, dsts):
    for src, dst in zip(srcs, dsts):
        dst[...] = src[...].astype(BF16)


def _proj_kernel(x_ref, g1_ref, w_ref, qg_ref, kg_ref, rc_ref, ra_ref, rb_ref, hm_ref, *rest,
                 class_major, tiles, sub, stacked, n_cast):
    prev_kvt = None
    if class_major:
        if stacked:
            prev_kvt, rest = rest[:N_DIL], rest[N_DIL:]
        cast_src, rest = rest[:n_cast], rest[n_cast:]
        (u_ref, gb_ref, z_ref, qc0, qc1, qc2, kvc0, kvc1, kvc2, kvt0, kvt1, kvt2) = rest[:12]
        _cast_slabs(cast_src, rest[12:12 + n_cast])
        nat = rest[-1]
    else:
        (u_ref, gb_ref, z_ref, q0, q1, q2, kv0, kv1, kv2) = rest
    tm = x_ref.shape[0]
    hm = hm_ref[...]

    for s in range(tm // sub):
        r0 = s * sub
        rows = slice(r0, r0 + sub)
        x = x_ref[rows, :]
        ms = jnp.mean(x * x, axis=-1, keepdims=True)
        hb = (x * lax.rsqrt(ms + EPS) * g1_ref[...]).astype(BF16)
        p_att = jnp.dot(hb, w_ref[:, _Q_OFF:_GB_OFF], preferred_element_type=F32)
        rc, ra, rb = rc_ref[rows, :], ra_ref[rows, :], rb_ref[rows, :]

        def head_norm_rope(xb, gain):
            msq = jnp.dot((xb * xb).astype(BF16), hm, preferred_element_type=F32)
            xn = xb * lax.rsqrt(msq + EPS) * gain
            return xn * rc + pltpu.roll(xn, LANES - ROPE_DIM // 2, 1) * ra + pltpu.roll(xn, ROPE_DIM // 2, 1) * rb

        for g, (window, dil) in enumerate(DIL_PAIRS):
            lanes = slice(g * GROUP_WIDTH, (g + 1) * GROUP_WIDTH)
            qn = head_norm_rope(p_att[:, :ATTN_WIDTH][:, lanes], qg_ref[...])
            kn = head_norm_rope(p_att[:, ATTN_WIDTH:2 * ATTN_WIDTH][:, lanes], kg_ref[...])
            vv = p_att[:, 2 * ATTN_WIDTH:][:, lanes]
            if not class_major:
                q_ref, kv_ref = ((q0, kv0), (q1, kv1), (q2, kv2))[g]
                q_ref[rows, :] = qn
                kv_ref[rows, :GROUP_WIDTH] = kn
                kv_ref[rows, GROUP_WIDTH:] = vv
                continue
            qc_ref, kvc_ref, kvt_ref = ((qc0, kvc0, kvt0), (qc1, kvc1, kvt1), (qc2, kvc2, kvt2))[g]
            slab = 3 * (s * N_DIL + g)
            nat[slab] = qn
            nat[slab + 1] = kn
            nat[slab + 2] = vv
            per = sub // dil
            dst = slice(s * per, (s + 1) * per)
            for r in range(dil):
                src = pl.ds(r, per, stride=dil) if dil > 1 else slice(None)
                qc_ref[0, r, dst, :] = nat[slab, src, :].astype(BF16)
                kvc_ref[0, r, dst, :GROUP_WIDTH] = nat[slab + 1, src, :].astype(BF16)
                kvc_ref[0, r, dst, GROUP_WIDTH:] = nat[slab + 2, src, :].astype(BF16)
            def put_kvt(cols, part, g=g, kvt_ref=kvt_ref):
                if stacked:
                    kvt_ref[0, 0, :, cols] = prev_kvt[g][0, :, cols]
                    kvt_ref[1, 0, :, cols] = part
                else:
                    kvt_ref[0, :, cols] = part

            if window >= tiles * tm:
                put_kvt(rows, jnp.concatenate([kn, vv], axis=1).T)
            else:
                first_kept = tm - min(window, tm)
                lo = max(r0, first_kept)
                if lo < r0 + sub:
                    part = jnp.concatenate([kn[lo - r0:], vv[lo - r0:]], axis=1).T
                    put_kvt(slice(lo - first_kept, r0 + sub - first_kept), part)

        u_ref[rows, :] = jnp.dot(hb, w_ref[:, :POOL_WIDTH], preferred_element_type=F32)
        p_gate = jnp.dot(hb, w_ref[:, _GB_OFF:], preferred_element_type=F32)
        gb_ref[rows, :] = p_gate[:, :CONV_WIDTH]
        z_ref[rows, :] = p_gate[:, CONV_WIDTH:2 * CONV_WIDTH] * p_gate[:, 2 * CONV_WIDTH:]


def _proj(x, g1, w_in, qg, kg, rope, hm, *, tm, batch=None, seq=None, prev_kvt=None, convert=None):
    rows = x.shape[0]
    stacked = prev_kvt is not None
    sub = min(PROJ_SUB, tm)
    n_tab = rope[0].shape[0] // tm
    class_major = batch is not None
    tiles = seq // tm if class_major else 1
    row = lambda c: pl.BlockSpec((tm, c), lambda i: (i, 0))
    tab = pl.BlockSpec((tm, LANES), lambda i: (i % n_tab, 0))
    out_shape = [jax.ShapeDtypeStruct((rows, POOL_WIDTH), F32)] + [jax.ShapeDtypeStruct((rows, CONV_WIDTH), F32)] * 2
    out_specs = [row(POOL_WIDTH), row(CONV_WIDTH), row(CONV_WIDTH)]
    scratch = []
    prev_specs = []
    cast_in, cast_out, cast_shape, cast_args = _cast_specs(convert, rows // tm)
    assert class_major or convert is None
    if class_major:
        for width in (GROUP_WIDTH, 2 * GROUP_WIDTH):
            for (_, dil) in DIL_PAIRS:
                out_shape.append(jax.ShapeDtypeStruct((batch, dil, seq // dil, width), BF16))
                out_specs.append(pl.BlockSpec((1, dil, tm // dil, width), lambda i: (i // tiles, 0, i % tiles, 0)))
        for (window, _) in DIL_PAIRS:
            assert window >= seq or window <= tm, "kv buffer rows must come from the last row tile"
            cols = tm if window >= seq else window
            col_blk = (lambda i: i % tiles) if window >= seq else (lambda i: 0)
            kvt_spec = pl.BlockSpec((1, 2 * GROUP_WIDTH, cols), lambda i, c=col_blk: (i // tiles, 0, c(i)))
            if stacked:
                prev_specs.append(kvt_spec)
                out_shape.append(jax.ShapeDtypeStruct((2, batch, 2 * GROUP_WIDTH, window), F32))
                out_specs.append(pl.BlockSpec((2, 1, 2 * GROUP_WIDTH, cols),
                                              lambda i, c=col_blk: (0, i // tiles, 0, c(i))))
            else:
                out_shape.append(jax.ShapeDtypeStruct((batch, 2 * GROUP_WIDTH, window), F32))
                out_specs.append(kvt_spec)
        scratch = [pltpu.VMEM((3 * N_DIL * (tm // sub), sub, GROUP_WIDTH), F32)]
    else:
        out_shape += [jax.ShapeDtypeStruct((rows, GROUP_WIDTH), F32)] * N_DIL
        out_shape += [jax.ShapeDtypeStruct((rows, 2 * GROUP_WIDTH), F32)] * N_DIL
        out_specs += [row(GROUP_WIDTH)] * N_DIL + [row(2 * GROUP_WIDTH)] * N_DIL
    return pl.pallas_call(
        functools.partial(_proj_kernel, class_major=class_major, tiles=tiles, sub=sub, stacked=stacked,
                          n_cast=len(cast_args)),
        grid=(rows // tm,),
        in_specs=[row(D_MODEL), _const_spec((1, D_MODEL)), _const_spec((D_MODEL, IN_COLS)),
                  _const_spec((1, LANES)), _const_spec((1, LANES)), tab, tab, tab,
                  _const_spec((LANES, LANES))] + prev_specs + cast_in,
        out_specs=out_specs + cast_out,
        out_shape=out_shape + cast_shape,
        scratch_shapes=scratch,
        compiler_params=_params(),
        name="proj",
    )(x, g1, w_in, qg, kg, *rope, hm, *(prev_kvt or ()), *cast_args)


def _lane_lt(shape, bound):
    return lax.broadcasted_iota(jnp.int32, shape, len(shape) - 1) < bound


def _stack_heads(q):
    first = _lane_lt(q.shape, HEAD_DIM)
    zero = jnp.zeros_like(q)
    return jnp.concatenate([jnp.where(first, q, zero), jnp.where(first, zero, q)], axis=0)


def _unstack(o2, l2, m2):
    n = o2.shape[0] // 2
    first = _lane_lt((n, GROUP_WIDTH), HEAD_DIM)
    return tuple(jnp.where(first, a[:n], a[n:]) for a in (o2, l2, m2))


def _merge(a, b):
    m = jnp.maximum(a[2], b[2])
    wa = jnp.exp(a[2] - m)
    wb = jnp.exp(b[2] - m)
    return wa * a[0] + wb * b[0], wa * a[1] + wb * b[1], m


def _attn_unit(q, kv, bias):
    nk = kv.shape[0]
    k = kv[:, :GROUP_WIDTH]
    v1 = jnp.concatenate([kv[:, GROUP_WIDTH:], jnp.ones((nk, GROUP_WIDTH), BF16)], axis=1)
    s = lax.dot_general(_stack_heads(q), k, _NT, preferred_element_type=F32) + bias
    mrow = jnp.max(s, axis=-1, keepdims=True)
    p = jnp.exp(s - mrow).astype(BF16)
    r = jnp.dot(p, v1, preferred_element_type=F32)
    return _unstack(r[:, :GROUP_WIDTH], r[:, GROUP_WIDTH:], mrow)


def _attn_p_kernel(qc0, qc1, qc2, kvc0, kvc1, kvc2, y_ref, *scratch, seq):
    parts = [scratch[3 * g:3 * g + 3] for g in range(N_DIL)]
    bias_ref = scratch[3 * N_DIL]
    acc_o, acc_l, acc_m = parts[0]
    @pl.when(pl.program_id(0) == 0)
    def _():
        qi = lax.broadcasted_iota(jnp.int32, (2 * ATTN_TILE, 2 * ATTN_TILE), 0) & (ATTN_TILE - 1)
        ki = lax.broadcasted_iota(jnp.int32, (2 * ATTN_TILE, 2 * ATTN_TILE), 1)
        for first_key_back in (0, 1):
            dist = qi - ki + first_key_back * ATTN_TILE
            bias_ref[first_key_back] = jnp.where((dist >= 0) & (dist <= N_BACK), 0.0, NEG_INF)

    for g, ((_, dil), q_ref, kv_ref) in enumerate(zip(DIL_PAIRS, (qc0, qc1, qc2), (kvc0, kvc1, kvc2))):
        dst = parts[g]
        n_sub = seq // dil // ATTN_TILE

        def unit(idx, q_ref=q_ref, kv_ref=kv_ref, n_sub=n_sub, dil=dil):
            if n_sub == 1:
                r, sub = idx, 0
                res = _attn_unit(q_ref[0, r], kv_ref[0, r], bias_ref[0, :, :ATTN_TILE])
            else:
                r, sub = (idx // n_sub, idx % n_sub) if dil > 1 else (0, idx)
                back = jnp.minimum(sub, 1)
                q0 = pl.multiple_of(sub * ATTN_TILE, ATTN_TILE)
                k0 = pl.multiple_of((sub - back) * ATTN_TILE, ATTN_TILE)
                res = _attn_unit(q_ref[0, r, pl.ds(q0, ATTN_TILE), :], kv_ref[0, r, pl.ds(k0, 2 * ATTN_TILE), :],
                                 bias_ref[back])
            start = sub * (ATTN_TILE * dil) + r
            rows = pl.ds(start, ATTN_TILE, stride=dil) if dil > 1 else pl.ds(pl.multiple_of(start, ATTN_TILE), ATTN_TILE)
            return rows, res

        def units_body(it, carry, unit=unit, dst=dst):
            for j in range(ATTN_UNROLL):
                rows, res = unit(it * ATTN_UNROLL + j)
                for ref, val in zip(dst, res):
                    ref[rows, :] = val
            return carry
        n_iter = dil * n_sub // ATTN_UNROLL
        if n_iter == 1:
            units_body(0, 0)
        else:
            lax.fori_loop(0, n_iter, units_body, 0)

        if g > 0:
            for c in range(seq // 256):
                rows = slice(c * 256, (c + 1) * 256)
                o, l, m = _merge((acc_o[rows, :], acc_l[rows, :], acc_m[rows, :]),
                                 tuple(ref[rows, :] for ref in parts[g]))
                if g == N_DIL - 1:
                    y_ref[0, rows, :] = (o / l).astype(y_ref.dtype)
                else:
                    acc_o[rows, :] = o
                    acc_l[rows, :] = l
                    acc_m[rows, :] = m


def _attn_p(qcs, kvcs, *, batch, seq):
    in_specs = [pl.BlockSpec((1,) + a.shape[1:], lambda b: (b, 0, 0, 0)) for a in (*qcs, *kvcs)]
    acc = pltpu.VMEM((seq, GROUP_WIDTH), F32)
    return pl.pallas_call(
        functools.partial(_attn_p_kernel, seq=seq),
        grid=(batch,),
        in_specs=in_specs,
        out_specs=pl.BlockSpec((1, seq, GROUP_WIDTH), lambda b: (b, 0, 0)),
        out_shape=jax.ShapeDtypeStruct((batch, seq, GROUP_WIDTH), BF16),
        scratch_shapes=[acc] * (3 * N_DIL) + [pltpu.VMEM((2, 2 * ATTN_TILE, 2 * ATTN_TILE), F32)],
        compiler_params=_params(),
        name="attn_p",
    )(*qcs, *kvcs)


def _pool_rows(ext, pos, pw, ps):
    return jnp.dot(_pool_diff(ext, pos), pw, preferred_element_type=F32) * ps


def _pool_diff(ext, pos):
    s2 = ext + pltpu.roll(ext, 1, 0)
    s4 = s2 + pltpu.roll(s2, 2, 0)
    s8 = s4 + pltpu.roll(s4, 4, 0)
    s16 = s8 + pltpu.roll(s8, 8, 0)
    lane = lax.broadcasted_iota(jnp.int32, (1, POOL_WIDTH), 1)
    grp = [lane < (j + 1) * POOL_GROUP for j in range(3)]
    win = jnp.where(grp[0], s2, jnp.where(grp[1], s4, jnp.where(grp[2], s8, s16)))[HIST_ROWS:]
    width = jnp.where(grp[0], POOL_WINDOWS[0], jnp.where(grp[1], POOL_WINDOWS[1],
                      jnp.where(grp[2], POOL_WINDOWS[2], POOL_WINDOWS[3])))
    cnt = jnp.minimum(pos + 1, width).astype(F32)
    return (win / cnt - ext[HIST_ROWS:]).astype(BF16)


def _conv_rows(ext, gb, cw, hist):
    y = cw[0:1] * pltpu.roll(ext, 2, 0) + cw[1:2] * pltpu.roll(ext, 1, 0) + cw[2:3] * ext
    return gb * y[hist:]


def _attn_s_group(q, new_t, cache_t, window, dil, shifted):
    t = q.shape[0]
    q2 = _stack_heads(q).astype(BF16)
    s_c = jnp.dot(q2, cache_t[:GROUP_WIDTH].astype(BF16), preferred_element_type=F32)
    s_n = jnp.dot(q2, new_t[:GROUP_WIDTH].astype(BF16), preferred_element_type=F32)
    tq = lax.broadcasted_iota(jnp.int32, (2 * t, 1), 0) & (t - 1)
    rc = lax.broadcasted_iota(jnp.int32, (1, window), 1)
    if shifted:
        rc = (rc + t) & (window - 1)
    dist_c = window + tq - rc
    ok_c = (dist_c <= window) & ((dist_c & (dil - 1)) == 0) & (PAST_LEN - window + rc >= 0)
    tn = lax.broadcasted_iota(jnp.int32, (1, LANES), 1) - (LANES - t)
    dist_n = tq - tn
    ok_n = (tn >= 0) & (dist_n >= 0) & ((dist_n & (dil - 1)) == 0)
    s_c = jnp.where(ok_c, s_c, NEG_INF)
    s_n = jnp.where(ok_n, s_n, NEG_INF)
    mrow = jnp.maximum(jnp.max(s_c, axis=-1, keepdims=True), jnp.max(s_n, axis=-1, keepdims=True))
    p_c = jnp.exp(s_c - mrow)
    p_n = jnp.exp(s_n - mrow)
    o2 = (lax.dot_general(p_c.astype(BF16), cache_t[GROUP_WIDTH:].astype(BF16), _NT, preferred_element_type=F32)
          + lax.dot_general(p_n.astype(BF16), new_t[GROUP_WIDTH:].astype(BF16), _NT, preferred_element_type=F32))
    l2 = jnp.sum(p_c, axis=-1, keepdims=True) + jnp.sum(p_n, axis=-1, keepdims=True)
    return _unstack(o2, l2, mrow)


def _mix_s_kernel(u_ref, q0_ref, q1_ref, q2_ref, kn0_ref, kn1_ref, kn2_ref, gb_ref, z_ref, pst_ref, cst_ref,
                  pw_ref, ps_ref, cw_ref, *rest, t_new, first, n_per):
    yp_ref, ya_ref, yc_ref, nc0_ref, nc1_ref, nc2_ref = rest[-6:]
    cache_refs = rest[:-6]
    keep_old = _lane_lt((2 * GROUP_WIDTH, LANES), LANES - t_new)
    pos = PAST_LEN + lax.broadcasted_iota(jnp.int32, (t_new, 1), 0)
    for i in range(n_per):
        state = None
        for g, ((window, dil), q_ref, kn_ref, nc_ref) in enumerate(zip(
                DIL_PAIRS, (q0_ref, q1_ref, q2_ref), (kn0_ref, kn1_ref, kn2_ref), (nc0_ref, nc1_ref, nc2_ref))):
            kv_new = kn_ref[i]
            new_t = jnp.concatenate([jnp.zeros((LANES - t_new, 2 * GROUP_WIDTH), F32), kv_new], axis=0).T
            cache_t = cache_refs[g][0, i]
            part = _attn_s_group(q_ref[i], new_t, cache_t, window, dil, shifted=not first)
            state = part if state is None else _merge(state, part)
            rolled = pltpu.roll(cache_t, window - t_new, 1) if first else cache_t
            last = jnp.where(keep_old, rolled[:, window - LANES:], new_t)
            if first:
                if window > LANES:
                    nc_ref[0, i, :, :window - LANES] = rolled[:, :window - LANES]
                nc_ref[0, i, :, window - LANES:] = last
                nc_ref[1, i] = pltpu.roll(cache_refs[N_DIL + g][0, i], window - t_new, 1)
            else:
                nc_ref[0, i] = last
        ya_ref[i] = state[0] / state[1]

        u_ext = jnp.concatenate([pst_ref[i], u_ref[i]], axis=0)
        yp_ref[i] = _pool_rows(u_ext, pos, pw_ref[...], ps_ref[...])
        z_ext = jnp.concatenate([cst_ref[i], z_ref[i]], axis=0)
        yc_ref[i] = _conv_rows(z_ext, gb_ref[i], cw_ref[...], cst_ref.shape[1])


def _mix_s(u, qs, kns, gb, z, pst, cst, caches_t, pw, ps, cw, prev, *, n_seq, t_new):
    first = prev is None
    n_per = MIX_S_SEQS_FIRST if first else MIX_S_SEQS
    blk = lambda r, c: pl.BlockSpec((n_per, r, c), lambda b: (b, 0, 0))
    cache_blk = lambda layer, w: pl.BlockSpec((1, n_per, 2 * GROUP_WIDTH, w), lambda b: (layer, b, 0, 0))
    three = lambda a: a.reshape(n_seq, -1, a.shape[-1])
    windows = [w for (w, _) in DIL_PAIRS]
    in_specs = ([blk(t_new, POOL_WIDTH)] + [blk(t_new, GROUP_WIDTH)] * N_DIL + [blk(t_new, 2 * GROUP_WIDTH)] * N_DIL
                + [blk(t_new, CONV_WIDTH)] * 2 + [blk(pst.shape[1], POOL_WIDTH), blk(cst.shape[1], CONV_WIDTH)]
                + [_const_spec((POOL_WIDTH, POOL_WIDTH)), _const_spec((1, POOL_WIDTH)),
                   _const_spec((CONV_K, CONV_WIDTH))])
    args = [three(u), *[three(q) for q in qs], *[three(k) for k in kns], three(gb), three(z), pst, cst, pw, ps, cw]
    aliases = {}
    if first:
        assert all(c.shape[0] == 2 for c in caches_t), "stacked kv buffer update is written for two layers"
        in_specs += [cache_blk(0, w) for w in windows] + [cache_blk(1, w) for w in windows]
        args += [*caches_t, *caches_t]
        cache_out = [pl.BlockSpec((2, n_per, 2 * GROUP_WIDTH, w), lambda b: (0, b, 0, 0)) for w in windows]
    else:
        for k, (p, w) in enumerate(zip(prev, windows)):
            aliases[len(args)] = 3 + k
            in_specs.append(cache_blk(1, w))
            args.append(p)
        cache_out = [pl.BlockSpec((1, n_per, 2 * GROUP_WIDTH, LANES), lambda b, w=w: (1, b, 0, w // LANES - 1))
                     for w in windows]
    out_specs = [blk(t_new, POOL_WIDTH), blk(t_new, GROUP_WIDTH), blk(t_new, CONV_WIDTH)] + cache_out
    out_shape = ([jax.ShapeDtypeStruct((n_seq, t_new, c), F32) for c in (POOL_WIDTH, GROUP_WIDTH, CONV_WIDTH)]
                 + [jax.ShapeDtypeStruct(c.shape, F32) for c in caches_t])
    return pl.pallas_call(
        functools.partial(_mix_s_kernel, t_new=t_new, first=first, n_per=n_per),
        grid=(n_seq // n_per,),
        in_specs=in_specs,
        out_specs=out_specs,
        out_shape=out_shape,
        input_output_aliases=aliases,
        compiler_params=_params(),
        name="mix_s",
    )(*args)


def _mlp_head(x, yp, ya, yc, wo_ref, g2_ref):
    mixed = jnp.concatenate([yp.astype(BF16), ya.astype(BF16), yc.astype(BF16)], axis=1)
    x1 = x + jnp.dot(mixed, wo_ref[...], preferred_element_type=F32)
    ms = jnp.mean(x1 * x1, axis=-1, keepdims=True)
    return x1, (x1 * lax.rsqrt(ms + EPS) * g2_ref[...]).astype(BF16)


def _mlp_tail(x1, hb, wu_ref, wd_ref, tf):
    acc = x1
    for c in range(D_FF // tf):
        hf = jnp.dot(hb, wu_ref[:, c * tf:(c + 1) * tf], preferred_element_type=F32)
        act = jnp.square(jnp.maximum(hf, 0.0)).astype(BF16)
        acc = acc + jnp.dot(act, wd_ref[c * tf:(c + 1) * tf, :], preferred_element_type=F32)
    return acc


def _mlp_s_kernel(x_ref, yp_ref, ya_ref, yc_ref, wo_ref, g2_ref, wu_ref, wd_ref, o_ref, hb_buf):
    @pl.when(pl.program_id(0) == 0)
    def _():
        x1, hb = _mlp_head(x_ref[...], yp_ref[...], ya_ref[...], yc_ref[...], wo_ref, g2_ref)
        o_ref[...] = x1
        hb_buf[...] = hb

    hf = jnp.dot(hb_buf[...], wu_ref[...], preferred_element_type=F32)
    act = jnp.square(jnp.maximum(hf, 0.0)).astype(BF16)
    o_ref[...] += jnp.dot(act, wd_ref[...], preferred_element_type=F32)


def _mlp_p_kernel(x_ref, u_ref, uh_ref, gb_ref, z_ref, zh_ref, ya_ref, pw_ref, ps_ref, cw_ref,
                  wo_ref, g2_ref, wu_ref, wd_ref, *rest, tf, tiles):
    n_cast = len(rest) // 2
    o_ref = rest[n_cast]
    _cast_slabs(rest[:n_cast], rest[n_cast + 1:])
    tm = x_ref.shape[0]
    tile = pl.program_id(0) % tiles
    has_hist = tile > 0
    pos = tile * tm + lax.broadcasted_iota(jnp.int32, (tm, 1), 0)
    yp = _pool_rows(jnp.concatenate([jnp.where(has_hist, uh_ref[...], 0.0), u_ref[...]], axis=0), pos,
                    pw_ref[...], ps_ref[...])
    yc = _conv_rows(jnp.concatenate([jnp.where(has_hist, zh_ref[...], 0.0), z_ref[...]], axis=0),
                    gb_ref[...], cw_ref[...], HIST_ROWS)
    x1, hb = _mlp_head(x_ref[...], yp, ya_ref[...], yc, wo_ref, g2_ref)
    o_ref[...] = _mlp_tail(x1, hb, wu_ref, wd_ref, tf)


def _mlp_s(x, yp, ya, yc, w_out, g2, w_up, w_down):
    rows = x.shape[0]
    tf = 512
    full = lambda c: pl.BlockSpec((rows, c), lambda j: (0, 0))
    return pl.pallas_call(
        _mlp_s_kernel,
        grid=(D_FF // tf,),
        in_specs=[full(D_MODEL), full(POOL_WIDTH), full(GROUP_WIDTH), full(CONV_WIDTH),
                  _const_spec((MIX_OUT, D_MODEL)), _const_spec((1, D_MODEL)),
                  pl.BlockSpec((D_MODEL, tf), lambda j: (0, j)),
                  pl.BlockSpec((tf, D_MODEL), lambda j: (j, 0))],
        out_specs=full(D_MODEL),
        out_shape=jax.ShapeDtypeStruct((rows, D_MODEL), F32),
        scratch_shapes=[pltpu.VMEM((rows, D_MODEL), BF16)],
        compiler_params=_params(),
        name="mlp_s",
    )(x, yp, ya, yc, w_out, g2, w_up, w_down)


def _mlp_p(x, u, gb, z, ya, pw, ps, cw, w_out, g2, w_up, w_down, *, tm, seq, convert=None):
    rows = x.shape[0]
    row = lambda c: pl.BlockSpec((tm, c), lambda i: (i, 0))
    conv_in, conv_out_specs, conv_out_shape, conv_args = _cast_specs(convert, rows // tm)
    hist = lambda c: pl.BlockSpec((HIST_ROWS, c), lambda i: (jnp.maximum(i * (tm // HIST_ROWS) - 1, 0), 0))
    return pl.pallas_call(
        functools.partial(_mlp_p_kernel, tf=512, tiles=seq // tm),
        grid=(rows // tm,),
        in_specs=[row(D_MODEL), row(POOL_WIDTH), hist(POOL_WIDTH), row(CONV_WIDTH), row(CONV_WIDTH),
                  hist(CONV_WIDTH), row(GROUP_WIDTH), _const_spec((POOL_WIDTH, POOL_WIDTH)),
                  _const_spec((1, POOL_WIDTH)), _const_spec((CONV_K, CONV_WIDTH)),
                  _const_spec((MIX_OUT, D_MODEL)), _const_spec((1, D_MODEL)),
                  _const_spec((D_MODEL, D_FF)), _const_spec((D_FF, D_MODEL))] + conv_in,
        out_specs=[row(D_MODEL)] + conv_out_specs,
        out_shape=[jax.ShapeDtypeStruct((rows, D_MODEL), F32)] + conv_out_shape,
        compiler_params=_params(),
        name="mlp_p",
    )(x, u, u, gb, z, z, ya, pw, ps, cw, w_out, g2, w_up, w_down, *conv_args)


def _rope_tables(pos):
    half = ROPE_DIM // 2
    inv = jnp.power(jnp.float32(ROPE_THETA), -jnp.arange(half, dtype=F32) / half)
    ang = pos.astype(F32)[:, None] * inv[None, :]
    cos, sin = jnp.cos(ang), jnp.sin(ang)
    n = pos.shape[0]
    rest = HEAD_DIM - ROPE_DIM
    zh = jnp.zeros((n, half), F32)
    c = jnp.concatenate([cos, cos, jnp.ones((n, rest), F32)], axis=1)
    a = jnp.concatenate([-sin, zh, jnp.zeros((n, rest), F32)], axis=1)
    b = jnp.concatenate([zh, sin, jnp.zeros((n, rest), F32)], axis=1)
    return tuple(jnp.tile(t, (1, GROUP_WIDTH // HEAD_DIM)) for t in (c, a, b))


def _to_buffer_layout(c):
    lead = c.shape[:-4]
    n = len(lead)
    t = jnp.transpose(c, (*range(n), n + 1, n + 2, n + 3, n))
    return t.reshape(*lead, 2 * GROUP_WIDTH, c.shape[-4])


def _from_buffer_layout(t):
    lead = t.shape[:-2]
    n = len(lead)
    c = t.reshape(*lead, 2, 2, HEAD_DIM, t.shape[-1])
    return jnp.transpose(c, (*range(n), n + 3, n, n + 1, n + 2))


def kernel(x_prompt, x_sample, state_pool, state_conv, cache_kv_w128, cache_kv_w512, cache_kv_w2048,
           norm1_g, w_in, q_norm_g, k_norm_g, pool_w, pool_scale, conv_w, w_out, norm2_g, w_up, w_down):
    batch, seq, _ = x_prompt.shape
    n_seq, t_new, _ = x_sample.shape
    depth = w_in.shape[0]
    assert depth == 2, "the in-place stacking of the kv state outputs is written for two layers"
    caches_t = [_to_buffer_layout(c) for c in (cache_kv_w128, cache_kv_w512, cache_kv_w2048)]

    rope_p = _rope_tables(jnp.arange(seq, dtype=jnp.int32))
    rope_s = _rope_tables(jnp.tile(PAST_LEN + jnp.arange(t_new, dtype=jnp.int32), n_seq))
    head_id = jnp.arange(LANES) // HEAD_DIM
    hm = jnp.where(head_id[:, None] == head_id[None, :], 1.0 / HEAD_DIM, 0.0).astype(BF16)
    two_heads = lambda gain: jnp.tile(gain, GROUP_WIDTH // HEAD_DIM)[None, :]

    xp = x_prompt.reshape(batch * seq, D_MODEL)
    xs = x_sample.reshape(n_seq * t_new, D_MODEL)
    outs = {k: [] for k in ("pool_p", "conv_p", "pool_s", "conv_s")}
    w_in_l = w_in[0].astype(BF16)
    new_caches = None
    kvts = None
    for layer in range(depth):
        pw = jax.scipy.linalg.block_diag(*[pool_w[layer, j] for j in range(len(POOL_WINDOWS))]).astype(BF16)
        ps = pool_scale[layer][None, :]
        cw = conv_w[layer]
        g1 = norm1_g[layer][None, :]
        g2 = norm2_g[layer][None, :]
        qg = two_heads(q_norm_g[layer]) * (HEAD_DIM ** -0.5)
        kg = two_heads(k_norm_g[layer])

        w_out_l = w_out[layer].astype(BF16)
        convert = (0, (w_up, w_down)) if layer == 0 else None
        (u, gb, z, qc0, qc1, qc2, kvc0, kvc1, kvc2, *tail) = _proj(
            xp, g1, w_in_l, qg, kg, rope_p, hm, tm=PROJ_TILE, batch=batch, seq=seq, prev_kvt=kvts, convert=convert)
        kvts = tail[:N_DIL]
        if layer == 0:
            w_up_l, w_down_l = tail[N_DIL:]
        ya = _attn_p((qc0, qc1, qc2), (kvc0, kvc1, kvc2), batch=batch, seq=seq)
        convert = (layer + 1, (w_in, w_up, w_down)) if layer + 1 < depth else None
        xp, *w_next = _mlp_p(xp, u, gb, z, ya.reshape(-1, GROUP_WIDTH), pw, ps, cw, w_out_l, g2, w_up_l, w_down_l,
                             tm=512, seq=seq, convert=convert)
        outs["pool_p"].append(u.reshape(batch, seq, POOL_WIDTH)[:, seq - POOL_HIST:])
        outs["conv_p"].append(z.reshape(batch, seq, CONV_WIDTH)[:, seq - (CONV_K - 1):])

        us, gbs, zs, q0, q1, q2, kn0, kn1, kn2 = _proj(xs, g1, w_in_l, qg, kg, rope_s, hm, tm=n_seq * t_new)
        pst = jnp.pad(state_pool[layer], ((0, 0), (HIST_ROWS - POOL_HIST, 0), (0, 0)))
        cst = jnp.pad(state_conv[layer], ((0, 0), (8 - (CONV_K - 1), 0), (0, 0)))
        yps, yas, ycs, *new_caches = _mix_s(us, (q0, q1, q2), (kn0, kn1, kn2), gbs, zs, pst, cst, caches_t,
                                           pw, ps, cw, new_caches, n_seq=n_seq, t_new=t_new)
        xs = _mlp_s(xs, yps.reshape(-1, POOL_WIDTH), yas.reshape(-1, GROUP_WIDTH), ycs.reshape(-1, CONV_WIDTH),
                    w_out_l, g2, w_up_l, w_down_l)
        us3 = us.reshape(n_seq, t_new, POOL_WIDTH)
        zs3 = zs.reshape(n_seq, t_new, CONV_WIDTH)
        outs["pool_s"].append(jnp.concatenate([state_pool[layer], us3], axis=1)[:, -POOL_HIST:])
        outs["conv_s"].append(jnp.concatenate([state_conv[layer], zs3], axis=1)[:, -(CONV_K - 1):])
        if w_next:
            w_in_l, w_up_l, w_down_l = w_next

    st = lambda k: jnp.stack(outs[k])
    return (xp.reshape(batch, seq, D_MODEL), xs.reshape(n_seq, t_new, D_MODEL),
            st("pool_p"), st("conv_p"),
            *[_from_buffer_layout(kvt) for kvt in kvts],
            st("pool_s"), st("conv_s"),
            *[_from_buffer_layout(c) for c in new_caches])
```

```python
import functools

import jax
import jax.numpy as jnp
from jax import lax
from jax.experimental import pallas as pl
from jax.experimental.pallas import tpu as pltpu

D_MODEL = 1024
HEAD_DIM = 64
POOL_WIDTH = 256
POOL_WINDOWS = (2, 4, 8, 16)
POOL_GROUP = 64
POOL_HIST = 15
ATTN_WIDTH = 384
DIL_PAIRS = ((128, 1), (512, 4), (2048, 16))
N_DIL = 3
GROUP_WIDTH = 128
CONV_WIDTH = 384
CONV_K = 3
ROPE_DIM = 16
ROPE_THETA = 500000.0
D_FF = 4096
IN_COLS = 2560
MIX_OUT = 768
EPS = 1e-6
NEG_INF = -1e30
PAST_LEN = 8192
N_BACK = 128

LANES = 128
SUBLANES = 8
HIST_ROWS = 16
ATTN_TILE = 128
ATTN_UNROLL = 16
ATTN_MERGE_ROWS = 256
PROJ_TILE = 1024
PROJ_SUB = 512
MLP_TILE = 512
MLP_FF_CHUNK = 512
MIX_S_SEQS_FIRST = 2
MIX_S_SEQS = 4
VMEM_LIMIT = 56 * 1024 * 1024

F32 = jnp.float32
BF16 = jnp.bfloat16

_Q_OFF = POOL_WIDTH
_GB_OFF = _Q_OFF + 3 * ATTN_WIDTH

_NT = (((1,), (1,)), ((), ()))


def _const_spec(shape):
    return pl.BlockSpec(shape, lambda *_: (0,) * len(shape), pipeline_mode=pl.Buffered(1))


def _params():
    return pltpu.CompilerParams(dimension_semantics=("arbitrary",), vmem_limit_bytes=VMEM_LIMIT)


def _cast_specs(convert, n_steps):
    in_specs, out_specs, out_shape, args = [], [], [], []
    if convert is not None:
        layer, weights = convert
        for w in weights:
            _, k, n = w.shape
            in_specs.append(pl.BlockSpec((None, k // n_steps, n), lambda i: (layer, i, 0)))
            out_specs.append(pl.BlockSpec((k // n_steps, n), lambda i: (i, 0)))
            out_shape.append(jax.ShapeDtypeStruct((k, n), BF16))
            args.append(w)
    return in_specs, out_specs, out_shape, args


def _cast_slabs(srcs, dsts):
    for src, dst in zip(srcs, dsts):
        dst[...] = src[...].astype(BF16)


def _proj_kernel(x_ref, g1_ref, w_ref, qg_ref, kg_ref, rc_ref, ra_ref, rb_ref, hm_ref, *rest,
                 class_major, tiles, sub, stacked, n_cast):
    prev_kvt = None
    if class_major:
        if stacked:
            prev_kvt, rest = rest[:N_DIL], rest[N_DIL:]
        cast_src, rest = rest[:n_cast], rest[n_cast:]
        (u_ref, gb_ref, z_ref, qc0, qc1, qc2, kvc0, kvc1, kvc2, kvt0, kvt1, kvt2) = rest[:12]
        _cast_slabs(cast_src, rest[12:12 + n_cast])
        nat = rest[-1]
    else:
        (u_ref, gb_ref, z_ref, q0, q1, q2, kv0, kv1, kv2) = rest
    tm = x_ref.shape[0]
    hm = hm_ref[...]

    for s in range(tm // sub):
        r0 = s * sub
        rows = slice(r0, r0 + sub)
        x = x_ref[rows, :]
        ms = jnp.mean(x * x, axis=-1, keepdims=True)
        hb = (x * lax.rsqrt(ms + EPS) * g1_ref[...]).astype(BF16)
        p_att = jnp.dot(hb, w_ref[:, _Q_OFF:_GB_OFF], preferred_element_type=F32)
        rc, ra, rb = rc_ref[rows, :], ra_ref[rows, :], rb_ref[rows, :]

        def head_norm_rope(xb, gain):
            msq = jnp.dot((xb * xb).astype(BF16), hm, preferred_element_type=F32)
            xn = xb * lax.rsqrt(msq + EPS) * gain
            return xn * rc + pltpu.roll(xn, LANES - ROPE_DIM // 2, 1) * ra + pltpu.roll(xn, ROPE_DIM // 2, 1) * rb

        for g, (window, dil) in enumerate(DIL_PAIRS):
            lanes = slice(g * GROUP_WIDTH, (g + 1) * GROUP_WIDTH)
            qn = head_norm_rope(p_att[:, :ATTN_WIDTH][:, lanes], qg_ref[...])
            kn = head_norm_rope(p_att[:, ATTN_WIDTH:2 * ATTN_WIDTH][:, lanes], kg_ref[...])
            vv = p_att[:, 2 * ATTN_WIDTH:][:, lanes]
            if not class_major:
                q_ref, kv_ref = ((q0, kv0), (q1, kv1), (q2, kv2))[g]
                q_ref[rows, :] = qn
                kv_ref[rows, :GROUP_WIDTH] = kn
                kv_ref[rows, GROUP_WIDTH:] = vv
                continue
            qc_ref, kvc_ref, kvt_ref = ((qc0, kvc0, kvt0), (qc1, kvc1, kvt1), (qc2, kvc2, kvt2))[g]
            slab = 3 * (s * N_DIL + g)
            nat[slab] = qn
            nat[slab + 1] = kn
            nat[slab + 2] = vv
            per = sub // dil
            dst = slice(s * per, (s + 1) * per)
            for r in range(dil):
                src = pl.ds(r, per, stride=dil) if dil > 1 else slice(None)
                qc_ref[0, r, dst, :] = nat[slab, src, :].astype(BF16)
                kvc_ref[0, r, dst, :GROUP_WIDTH] = nat[slab + 1, src, :].astype(BF16)
                kvc_ref[0, r, dst, GROUP_WIDTH:] = nat[slab + 2, src, :].astype(BF16)
            def put_kvt(cols, part, g=g, kvt_ref=kvt_ref):
                if stacked:
                    kvt_ref[0, 0, :, cols] = prev_kvt[g][0, :, cols]
                    kvt_ref[1, 0, :, cols] = part
                else:
                    kvt_ref[0, :, cols] = part

            if window >= tiles * tm:
                put_kvt(rows, jnp.concatenate([kn, vv], axis=1).T)
            else:
                first_kept = tm - min(window, tm)
                lo = max(r0, first_kept)
                if lo < r0 + sub:
                    part = jnp.concatenate([kn[lo - r0:], vv[lo - r0:]], axis=1).T
                    put_kvt(slice(lo - first_kept, r0 + sub - first_kept), part)

        u_ref[rows, :] = jnp.dot(hb, w_ref[:, :POOL_WIDTH], preferred_element_type=F32)
        p_gate = jnp.dot(hb, w_ref[:, _GB_OFF:], preferred_element_type=F32)
        gb_ref[rows, :] = p_gate[:, :CONV_WIDTH]
        z_ref[rows, :] = p_gate[:, CONV_WIDTH:2 * CONV_WIDTH] * p_gate[:, 2 * CONV_WIDTH:]


def _proj(x, g1, w_in, qg, kg, rope, hm, *, tm, batch=None, seq=None, prev_kvt=None, convert=None):
    rows = x.shape[0]
    stacked = prev_kvt is not None
    sub = min(PROJ_SUB, tm)
    n_tab = rope[0].shape[0] // tm
    class_major = batch is not None
    tiles = seq // tm if class_major else 1
    row = lambda c: pl.BlockSpec((tm, c), lambda i: (i, 0))
    tab = pl.BlockSpec((tm, LANES), lambda i: (i % n_tab, 0))
    out_shape = [jax.ShapeDtypeStruct((rows, POOL_WIDTH), F32)] + [jax.ShapeDtypeStruct((rows, CONV_WIDTH), F32)] * 2
    out_specs = [row(POOL_WIDTH), row(CONV_WIDTH), row(CONV_WIDTH)]
    scratch = []
    prev_specs = []
    cast_in, cast_out, cast_shape, cast_args = _cast_specs(convert, rows // tm)
    assert class_major or convert is None
    if class_major:
        for width in (GROUP_WIDTH, 2 * GROUP_WIDTH):
            for (_, dil) in DIL_PAIRS:
                out_shape.append(jax.ShapeDtypeStruct((batch, dil, seq // dil, width), BF16))
                out_specs.append(pl.BlockSpec((1, dil, tm // dil, width), lambda i: (i // tiles, 0, i % tiles, 0)))
        for (window, _) in DIL_PAIRS:
            assert window >= seq or window <= tm, "kv buffer rows must come from the last row tile"
            cols = tm if window >= seq else window
            col_blk = (lambda i: i % tiles) if window >= seq else (lambda i: 0)
            kvt_spec = pl.BlockSpec((1, 2 * GROUP_WIDTH, cols), lambda i, c=col_blk: (i // tiles, 0, c(i)))
            if stacked:
                prev_specs.append(kvt_spec)
                out_shape.append(jax.ShapeDtypeStruct((2, batch, 2 * GROUP_WIDTH, window), F32))
                out_specs.append(pl.BlockSpec((2, 1, 2 * GROUP_WIDTH, cols),
                                              lambda i, c=col_blk: (0, i // tiles, 0, c(i))))
            else:
                out_shape.append(jax.ShapeDtypeStruct((batch, 2 * GROUP_WIDTH, window), F32))
                out_specs.append(kvt_spec)
        scratch = [pltpu.VMEM((3 * N_DIL * (tm // sub), sub, GROUP_WIDTH), F32)]
    else:
        out_shape += [jax.ShapeDtypeStruct((rows, GROUP_WIDTH), F32)] * N_DIL
        out_shape += [jax.ShapeDtypeStruct((rows, 2 * GROUP_WIDTH), F32)] * N_DIL
        out_specs += [row(GROUP_WIDTH)] * N_DIL + [row(2 * GROUP_WIDTH)] * N_DIL
    return pl.pallas_call(
        functools.partial(_proj_kernel, class_major=class_major, tiles=tiles, sub=sub, stacked=stacked,
                          n_cast=len(cast_args)),
        grid=(rows // tm,),
        in_specs=[row(D_MODEL), _const_spec((1, D_MODEL)), _const_spec((D_MODEL, IN_COLS)),
                  _const_spec((1, LANES)), _const_spec((1, LANES)), tab, tab, tab,
                  _const_spec((LANES, LANES))] + prev_specs + cast_in,
        out_specs=out_specs + cast_out,
        out_shape=out_shape + cast_shape,
        scratch_shapes=scratch,
        compiler_params=_params(),
        name="proj",
    )(x, g1, w_in, qg, kg, *rope, hm, *(prev_kvt or ()), *cast_args)


def _lane_lt(shape, bound):
    return lax.broadcasted_iota(jnp.int32, shape, len(shape) - 1) < bound


def _stack_heads(q):
    first = _lane_lt(q.shape, HEAD_DIM)
    zero = jnp.zeros_like(q)
    return jnp.concatenate([jnp.where(first, q, zero), jnp.where(first, zero, q)], axis=0)


def _unstack(o2, l2, m2):
    n = o2.shape[0] // 2
    first = _lane_lt((n, GROUP_WIDTH), HEAD_DIM)
    return tuple(jnp.where(first, a[:n], a[n:]) for a in (o2, l2, m2))


def _merge(a, b):
    m = jnp.maximum(a[2], b[2])
    wa = jnp.exp(a[2] - m)
    wb = jnp.exp(b[2] - m)
    return wa * a[0] + wb * b[0], wa * a[1] + wb * b[1], m


def _attn_unit(q, kv, bias):
    nk = kv.shape[0]
    k = kv[:, :GROUP_WIDTH]
    v1 = jnp.concatenate([kv[:, GROUP_WIDTH:], jnp.ones((nk, GROUP_WIDTH), BF16)], axis=1)
    s = lax.dot_general(_stack_heads(q), k, _NT, preferred_element_type=F32) + bias
    mrow = jnp.max(s, axis=-1, keepdims=True)
    p = jnp.exp(s - mrow).astype(BF16)
    r = jnp.dot(p, v1, preferred_element_type=F32)
    return _unstack(r[:, :GROUP_WIDTH], r[:, GROUP_WIDTH:], mrow)


def _attn_p_kernel(qc0, qc1, qc2, kvc0, kvc1, kvc2, y_ref, *scratch, seq):
    parts = [scratch[3 * g:3 * g + 3] for g in range(N_DIL)]
    bias_ref = scratch[3 * N_DIL]
    acc_o, acc_l, acc_m = parts[0]
    @pl.when(pl.program_id(0) == 0)
    def _():
        qi = lax.broadcasted_iota(jnp.int32, (2 * ATTN_TILE, 2 * ATTN_TILE), 0) & (ATTN_TILE - 1)
        ki = lax.broadcasted_iota(jnp.int32, (2 * ATTN_TILE, 2 * ATTN_TILE), 1)
        for first_key_back in (0, 1):
            dist = qi - ki + first_key_back * ATTN_TILE
            bias_ref[first_key_back] = jnp.where((dist >= 0) & (dist <= N_BACK), 0.0, NEG_INF)

    for g, ((_, dil), q_ref, kv_ref) in enumerate(zip(DIL_PAIRS, (qc0, qc1, qc2), (kvc0, kvc1, kvc2))):
        dst = parts[g]
        n_sub = seq // dil // ATTN_TILE

        def unit(idx, q_ref=q_ref, kv_ref=kv_ref, n_sub=n_sub, dil=dil):
            if n_sub == 1:
                r, sub = idx, 0
                res = _attn_unit(q_ref[0, r], kv_ref[0, r], bias_ref[0, :, :ATTN_TILE])
            else:
                r, sub = (idx // n_sub, idx % n_sub) if dil > 1 else (0, idx)
                back = jnp.minimum(sub, 1)
                q0 = pl.multiple_of(sub * ATTN_TILE, ATTN_TILE)
                k0 = pl.multiple_of((sub - back) * ATTN_TILE, ATTN_TILE)
                res = _attn_unit(q_ref[0, r, pl.ds(q0, ATTN_TILE), :], kv_ref[0, r, pl.ds(k0, 2 * ATTN_TILE), :],
                                 bias_ref[back])
            start = sub * (ATTN_TILE * dil) + r
            rows = pl.ds(start, ATTN_TILE, stride=dil) if dil > 1 else pl.ds(pl.multiple_of(start, ATTN_TILE), ATTN_TILE)
            return rows, res

        def units_body(it, carry, unit=unit, dst=dst):
            for j in range(ATTN_UNROLL):
                rows, res = unit(it * ATTN_UNROLL + j)
                for ref, val in zip(dst, res):
                    ref[rows, :] = val
            return carry
        n_iter = dil * n_sub // ATTN_UNROLL
        if n_iter == 1:
            units_body(0, 0)
        else:
            lax.fori_loop(0, n_iter, units_body, 0)

        if g > 0:
            for c in range(seq // ATTN_MERGE_ROWS):
                rows = slice(c * ATTN_MERGE_ROWS, (c + 1) * ATTN_MERGE_ROWS)
                o, l, m = _merge((acc_o[rows, :], acc_l[rows, :], acc_m[rows, :]),
                                 tuple(ref[rows, :] for ref in parts[g]))
                if g == N_DIL - 1:
                    y_ref[0, rows, :] = (o / l).astype(y_ref.dtype)
                else:
                    acc_o[rows, :] = o
                    acc_l[rows, :] = l
                    acc_m[rows, :] = m


def _attn_p(qcs, kvcs, *, batch, seq):
    in_specs = [pl.BlockSpec((1,) + a.shape[1:], lambda b: (b, 0, 0, 0)) for a in (*qcs, *kvcs)]
    acc = pltpu.VMEM((seq, GROUP_WIDTH), F32)
    return pl.pallas_call(
        functools.partial(_attn_p_kernel, seq=seq),
        grid=(batch,),
        in_specs=in_specs,
        out_specs=pl.BlockSpec((1, seq, GROUP_WIDTH), lambda b: (b, 0, 0)),
        out_shape=jax.ShapeDtypeStruct((batch, seq, GROUP_WIDTH), BF16),
        scratch_shapes=[acc] * (3 * N_DIL) + [pltpu.VMEM((2, 2 * ATTN_TILE, 2 * ATTN_TILE), F32)],
        compiler_params=_params(),
        name="attn_p",
    )(*qcs, *kvcs)


def _pool_rows(ext, pos, pw, ps):
    return jnp.dot(_pool_diff(ext, pos), pw, preferred_element_type=F32) * ps


def _pool_diff(ext, pos):
    s2 = ext + pltpu.roll(ext, 1, 0)
    s4 = s2 + pltpu.roll(s2, 2, 0)
    s8 = s4 + pltpu.roll(s4, 4, 0)
    s16 = s8 + pltpu.roll(s8, 8, 0)
    lane = lax.broadcasted_iota(jnp.int32, (1, POOL_WIDTH), 1)
    grp = [lane < (j + 1) * POOL_GROUP for j in range(3)]
    win = jnp.where(grp[0], s2, jnp.where(grp[1], s4, jnp.where(grp[2], s8, s16)))[HIST_ROWS:]
    width = jnp.where(grp[0], POOL_WINDOWS[0], jnp.where(grp[1], POOL_WINDOWS[1],
                      jnp.where(grp[2], POOL_WINDOWS[2], POOL_WINDOWS[3])))
    cnt = jnp.minimum(pos + 1, width).astype(F32)
    return (win / cnt - ext[HIST_ROWS:]).astype(BF16)


def _conv_rows(ext, gb, cw, hist):
    y = cw[0:1] * pltpu.roll(ext, 2, 0) + cw[1:2] * pltpu.roll(ext, 1, 0) + cw[2:3] * ext
    return gb * y[hist:]


def _attn_s_group(q, new_t, cache_t, window, dil, shifted):
    t = q.shape[0]
    q2 = _stack_heads(q).astype(BF16)
    s_c = jnp.dot(q2, cache_t[:GROUP_WIDTH].astype(BF16), preferred_element_type=F32)
    s_n = jnp.dot(q2, new_t[:GROUP_WIDTH].astype(BF16), preferred_element_type=F32)
    tq = lax.broadcasted_iota(jnp.int32, (2 * t, 1), 0) & (t - 1)
    rc = lax.broadcasted_iota(jnp.int32, (1, window), 1)
    if shifted:
        rc = (rc + t) & (window - 1)
    dist_c = window + tq - rc
    ok_c = (dist_c <= window) & ((dist_c & (dil - 1)) == 0) & (PAST_LEN - window + rc >= 0)
    tn = lax.broadcasted_iota(jnp.int32, (1, LANES), 1) - (LANES - t)
    dist_n = tq - tn
    ok_n = (tn >= 0) & (dist_n >= 0) & ((dist_n & (dil - 1)) == 0)
    s_c = jnp.where(ok_c, s_c, NEG_INF)
    s_n = jnp.where(ok_n, s_n, NEG_INF)
    mrow = jnp.maximum(jnp.max(s_c, axis=-1, keepdims=True), jnp.max(s_n, axis=-1, keepdims=True))
    p_c = jnp.exp(s_c - mrow)
    p_n = jnp.exp(s_n - mrow)
    o2 = (lax.dot_general(p_c.astype(BF16), cache_t[GROUP_WIDTH:].astype(BF16), _NT, preferred_element_type=F32)
          + lax.dot_general(p_n.astype(BF16), new_t[GROUP_WIDTH:].astype(BF16), _NT, preferred_element_type=F32))
    l2 = jnp.sum(p_c, axis=-1, keepdims=True) + jnp.sum(p_n, axis=-1, keepdims=True)
    return _unstack(o2, l2, mrow)


def _mix_s_kernel(u_ref, q0_ref, q1_ref, q2_ref, kn0_ref, kn1_ref, kn2_ref, gb_ref, z_ref, pst_ref, cst_ref,
                  pw_ref, ps_ref, cw_ref, *rest, t_new, first, n_per):
    yp_ref, ya_ref, yc_ref, nc0_ref, nc1_ref, nc2_ref = rest[-6:]
    cache_refs = rest[:-6]
    keep_old = _lane_lt((2 * GROUP_WIDTH, LANES), LANES - t_new)
    pos = PAST_LEN + lax.broadcasted_iota(jnp.int32, (t_new, 1), 0)
    for i in range(n_per):
        state = None
        for g, ((window, dil), q_ref, kn_ref, nc_ref) in enumerate(zip(
                DIL_PAIRS, (q0_ref, q1_ref, q2_ref), (kn0_ref, kn1_ref, kn2_ref), (nc0_ref, nc1_ref, nc2_ref))):
            kv_new = kn_ref[i]
            new_t = jnp.concatenate([jnp.zeros((LANES - t_new, 2 * GROUP_WIDTH), F32), kv_new], axis=0).T
            cache_t = cache_refs[g][0, i]
            part = _attn_s_group(q_ref[i], new_t, cache_t, window, dil, shifted=not first)
            state = part if state is None else _merge(state, part)
            rolled = pltpu.roll(cache_t, window - t_new, 1) if first else cache_t
            last = jnp.where(keep_old, rolled[:, window - LANES:], new_t)
            if first:
                if window > LANES:
                    nc_ref[0, i, :, :window - LANES] = rolled[:, :window - LANES]
                nc_ref[0, i, :, window - LANES:] = last
                nc_ref[1, i] = pltpu.roll(cache_refs[N_DIL + g][0, i], window - t_new, 1)
            else:
                nc_ref[0, i] = last
        ya_ref[i] = state[0] / state[1]

        u_ext = jnp.concatenate([pst_ref[i], u_ref[i]], axis=0)
        yp_ref[i] = _pool_rows(u_ext, pos, pw_ref[...], ps_ref[...])
        z_ext = jnp.concatenate([cst_ref[i], z_ref[i]], axis=0)
        yc_ref[i] = _conv_rows(z_ext, gb_ref[i], cw_ref[...], cst_ref.shape[1])


def _mix_s(u, qs, kns, gb, z, pst, cst, caches_t, pw, ps, cw, prev, *, n_seq, t_new):
    first = prev is None
    n_per = MIX_S_SEQS_FIRST if first else MIX_S_SEQS
    blk = lambda r, c: pl.BlockSpec((n_per, r, c), lambda b: (b, 0, 0))
    cache_blk = lambda layer, w: pl.BlockSpec((1, n_per, 2 * GROUP_WIDTH, w), lambda b: (layer, b, 0, 0))
    three = lambda a: a.reshape(n_seq, -1, a.shape[-1])
    windows = [w for (w, _) in DIL_PAIRS]
    in_specs = ([blk(t_new, POOL_WIDTH)] + [blk(t_new, GROUP_WIDTH)] * N_DIL + [blk(t_new, 2 * GROUP_WIDTH)] * N_DIL
                + [blk(t_new, CONV_WIDTH)] * 2 + [blk(pst.shape[1], POOL_WIDTH), blk(cst.shape[1], CONV_WIDTH)]
                + [_const_spec((POOL_WIDTH, POOL_WIDTH)), _const_spec((1, POOL_WIDTH)),
                   _const_spec((CONV_K, CONV_WIDTH))])
    args = [three(u), *[three(q) for q in qs], *[three(k) for k in kns], three(gb), three(z), pst, cst, pw, ps, cw]
    aliases = {}
    if first:
        assert all(c.shape[0] == 2 for c in caches_t), "stacked kv buffer update is written for two layers"
        in_specs += [cache_blk(0, w) for w in windows] + [cache_blk(1, w) for w in windows]
        args += [*caches_t, *caches_t]
        cache_out = [pl.BlockSpec((2, n_per, 2 * GROUP_WIDTH, w), lambda b: (0, b, 0, 0)) for w in windows]
    else:
        for k, (p, w) in enumerate(zip(prev, windows)):
            aliases[len(args)] = 3 + k
            in_specs.append(cache_blk(1, w))
            args.append(p)
        cache_out = [pl.BlockSpec((1, n_per, 2 * GROUP_WIDTH, LANES), lambda b, w=w: (1, b, 0, w // LANES - 1))
                     for w in windows]
    out_specs = [blk(t_new, POOL_WIDTH), blk(t_new, GROUP_WIDTH), blk(t_new, CONV_WIDTH)] + cache_out
    out_shape = ([jax.ShapeDtypeStruct((n_seq, t_new, c), F32) for c in (POOL_WIDTH, GROUP_WIDTH, CONV_WIDTH)]
                 + [jax.ShapeDtypeStruct(c.shape, F32) for c in caches_t])
    return pl.pallas_call(
        functools.partial(_mix_s_kernel, t_new=t_new, first=first, n_per=n_per),
        grid=(n_seq // n_per,),
        in_specs=in_specs,
        out_specs=out_specs,
        out_shape=out_shape,
        input_output_aliases=aliases,
        compiler_params=_params(),
        name="mix_s",
    )(*args)


def _mlp_head(x, yp, ya, yc, wo_ref, g2_ref):
    mixed = jnp.concatenate([yp.astype(BF16), ya.astype(BF16), yc.astype(BF16)], axis=1)
    x1 = x + jnp.dot(mixed, wo_ref[...], preferred_element_type=F32)
    ms = jnp.mean(x1 * x1, axis=-1, keepdims=True)
    return x1, (x1 * lax.rsqrt(ms + EPS) * g2_ref[...]).astype(BF16)


def _mlp_tail(x1, hb, wu_ref, wd_ref, tf):
    acc = x1
    for c in range(D_FF // tf):
        hf = jnp.dot(hb, wu_ref[:, c * tf:(c + 1) * tf], preferred_element_type=F32)
        act = jnp.square(jnp.maximum(hf, 0.0)).astype(BF16)
        acc = acc + jnp.dot(act, wd_ref[c * tf:(c + 1) * tf, :], preferred_element_type=F32)
    return acc


def _mlp_s_kernel(x_ref, yp_ref, ya_ref, yc_ref, wo_ref, g2_ref, wu_ref, wd_ref, o_ref, hb_buf):
    @pl.when(pl.program_id(0) == 0)
    def _():
        x1, hb = _mlp_head(x_ref[...], yp_ref[...], ya_ref[...], yc_ref[...], wo_ref, g2_ref)
        o_ref[...] = x1
        hb_buf[...] = hb

    hf = jnp.dot(hb_buf[...], wu_ref[...], preferred_element_type=F32)
    act = jnp.square(jnp.maximum(hf, 0.0)).astype(BF16)
    o_ref[...] += jnp.dot(act, wd_ref[...], preferred_element_type=F32)


def _mlp_p_kernel(x_ref, u_ref, uh_ref, gb_ref, z_ref, zh_ref, ya_ref, pw_ref, ps_ref, cw_ref,
                  wo_ref, g2_ref, wu_ref, wd_ref, *rest, tf, tiles):
    n_cast = len(rest) // 2
    o_ref = rest[n_cast]
    _cast_slabs(rest[:n_cast], rest[n_cast + 1:])
    tm = x_ref.shape[0]
    tile = pl.program_id(0) % tiles
    has_hist = tile > 0
    pos = tile * tm + lax.broadcasted_iota(jnp.int32, (tm, 1), 0)
    yp = _pool_rows(jnp.concatenate([jnp.where(has_hist, uh_ref[...], 0.0), u_ref[...]], axis=0), pos,
                    pw_ref[...], ps_ref[...])
    yc = _conv_rows(jnp.concatenate([jnp.where(has_hist, zh_ref[...], 0.0), z_ref[...]], axis=0),
                    gb_ref[...], cw_ref[...], HIST_ROWS)
    x1, hb = _mlp_head(x_ref[...], yp, ya_ref[...], yc, wo_ref, g2_ref)
    o_ref[...] = _mlp_tail(x1, hb, wu_ref, wd_ref, tf)


def _mlp_s(x, yp, ya, yc, w_out, g2, w_up, w_down):
    rows = x.shape[0]
    tf = MLP_FF_CHUNK
    full = lambda c: pl.BlockSpec((rows, c), lambda j: (0, 0))
    return pl.pallas_call(
        _mlp_s_kernel,
        grid=(D_FF // tf,),
        in_specs=[full(D_MODEL), full(POOL_WIDTH), full(GROUP_WIDTH), full(CONV_WIDTH),
                  _const_spec((MIX_OUT, D_MODEL)), _const_spec((1, D_MODEL)),
                  pl.BlockSpec((D_MODEL, tf), lambda j: (0, j)),
                  pl.BlockSpec((tf, D_MODEL), lambda j: (j, 0))],
        out_specs=full(D_MODEL),
        out_shape=jax.ShapeDtypeStruct((rows, D_MODEL), F32),
        scratch_shapes=[pltpu.VMEM((rows, D_MODEL), BF16)],
        compiler_params=_params(),
        name="mlp_s",
    )(x, yp, ya, yc, w_out, g2, w_up, w_down)


def _mlp_p(x, u, gb, z, ya, pw, ps, cw, w_out, g2, w_up, w_down, *, tm, seq, convert=None):
    rows = x.shape[0]
    row = lambda c: pl.BlockSpec((tm, c), lambda i: (i, 0))
    conv_in, conv_out_specs, conv_out_shape, conv_args = _cast_specs(convert, rows // tm)
    hist = lambda c: pl.BlockSpec((HIST_ROWS, c), lambda i: (jnp.maximum(i * (tm // HIST_ROWS) - 1, 0), 0))
    return pl.pallas_call(
        functools.partial(_mlp_p_kernel, tf=MLP_FF_CHUNK, tiles=seq // tm),
        grid=(rows // tm,),
        in_specs=[row(D_MODEL), row(POOL_WIDTH), hist(POOL_WIDTH), row(CONV_WIDTH), row(CONV_WIDTH),
                  hist(CONV_WIDTH), row(GROUP_WIDTH), _const_spec((POOL_WIDTH, POOL_WIDTH)),
                  _const_spec((1, POOL_WIDTH)), _const_spec((CONV_K, CONV_WIDTH)),
                  _const_spec((MIX_OUT, D_MODEL)), _const_spec((1, D_MODEL)),
                  _const_spec((D_MODEL, D_FF)), _const_spec((D_FF, D_MODEL))] + conv_in,
        out_specs=[row(D_MODEL)] + conv_out_specs,
        out_shape=[jax.ShapeDtypeStruct((rows, D_MODEL), F32)] + conv_out_shape,
        compiler_params=_params(),
        name="mlp_p",
    )(x, u, u, gb, z, z, ya, pw, ps, cw, w_out, g2, w_up, w_down, *conv_args)


def _rope_tables(pos):
    half = ROPE_DIM // 2
    inv = jnp.power(jnp.float32(ROPE_THETA), -jnp.arange(half, dtype=F32) / half)
    ang = pos.astype(F32)[:, None] * inv[None, :]
    cos, sin = jnp.cos(ang), jnp.sin(ang)
    n = pos.shape[0]
    rest = HEAD_DIM - ROPE_DIM
    zh = jnp.zeros((n, half), F32)
    c = jnp.concatenate([cos, cos, jnp.ones((n, rest), F32)], axis=1)
    a = jnp.concatenate([-sin, zh, jnp.zeros((n, rest), F32)], axis=1)
    b = jnp.concatenate([zh, sin, jnp.zeros((n, rest), F32)], axis=1)
    return tuple(jnp.tile(t, (1, GROUP_WIDTH // HEAD_DIM)) for t in (c, a, b))


def _to_buffer_layout(c):
    lead = c.shape[:-4]
    n = len(lead)
    t = jnp.transpose(c, (*range(n), n + 1, n + 2, n + 3, n))
    return t.reshape(*lead, 2 * GROUP_WIDTH, c.shape[-4])


def _from_buffer_layout(t):
    lead = t.shape[:-2]
    n = len(lead)
    c = t.reshape(*lead, 2, 2, HEAD_DIM, t.shape[-1])
    return jnp.transpose(c, (*range(n), n + 3, n, n + 1, n + 2))


def kernel(x_prompt, x_sample, state_pool, state_conv, cache_kv_w128, cache_kv_w512, cache_kv_w2048,
           norm1_g, w_in, q_norm_g, k_norm_g, pool_w, pool_scale, conv_w, w_out, norm2_g, w_up, w_down):
    batch, seq, _ = x_prompt.shape
    n_seq, t_new, _ = x_sample.shape
    depth = w_in.shape[0]
    assert depth == 2, "the in-place stacking of the kv state outputs is written for two layers"
    caches_t = [_to_buffer_layout(c) for c in (cache_kv_w128, cache_kv_w512, cache_kv_w2048)]

    rope_p = _rope_tables(jnp.arange(seq, dtype=jnp.int32))
    rope_s = _rope_tables(jnp.tile(PAST_LEN + jnp.arange(t_new, dtype=jnp.int32), n_seq))
    head_id = jnp.arange(LANES) // HEAD_DIM
    hm = jnp.where(head_id[:, None] == head_id[None, :], 1.0 / HEAD_DIM, 0.0).astype(BF16)
    two_heads = lambda gain: jnp.tile(gain, GROUP_WIDTH // HEAD_DIM)[None, :]

    xp = x_prompt.reshape(batch * seq, D_MODEL)
    xs = x_sample.reshape(n_seq * t_new, D_MODEL)
    outs = {k: [] for k in ("pool_p", "conv_p", "pool_s", "conv_s")}
    w_in_l = w_in[0].astype(BF16)
    new_caches = None
    kvts = None
    for layer in range(depth):
        pw = jax.scipy.linalg.block_diag(*[pool_w[layer, j] for j in range(len(POOL_WINDOWS))]).astype(BF16)
        ps = pool_scale[layer][None, :]
        cw = conv_w[layer]
        g1 = norm1_g[layer][None, :]
        g2 = norm2_g[layer][None, :]
        qg = two_heads(q_norm_g[layer]) * (HEAD_DIM ** -0.5)
        kg = two_heads(k_norm_g[layer])

        w_out_l = w_out[layer].astype(BF16)
        convert = (0, (w_up, w_down)) if layer == 0 else None
        (u, gb, z, qc0, qc1, qc2, kvc0, kvc1, kvc2, *tail) = _proj(
            xp, g1, w_in_l, qg, kg, rope_p, hm, tm=PROJ_TILE, batch=batch, seq=seq, prev_kvt=kvts, convert=convert)
        kvts = tail[:N_DIL]
        if layer == 0:
            w_up_l, w_down_l = tail[N_DIL:]
        ya = _attn_p((qc0, qc1, qc2), (kvc0, kvc1, kvc2), batch=batch, seq=seq)
        convert = (layer + 1, (w_in, w_up, w_down)) if layer + 1 < depth else None
        xp, *w_next = _mlp_p(xp, u, gb, z, ya.reshape(-1, GROUP_WIDTH), pw, ps, cw, w_out_l, g2, w_up_l, w_down_l,
                             tm=MLP_TILE, seq=seq, convert=convert)
        outs["pool_p"].append(u.reshape(batch, seq, POOL_WIDTH)[:, seq - POOL_HIST:])
        outs["conv_p"].append(z.reshape(batch, seq, CONV_WIDTH)[:, seq - (CONV_K - 1):])

        us, gbs, zs, q0, q1, q2, kn0, kn1, kn2 = _proj(xs, g1, w_in_l, qg, kg, rope_s, hm, tm=n_seq * t_new)
        pst = jnp.pad(state_pool[layer], ((0, 0), (HIST_ROWS - POOL_HIST, 0), (0, 0)))
        cst = jnp.pad(state_conv[layer], ((0, 0), (SUBLANES - (CONV_K - 1), 0), (0, 0)))
        yps, yas, ycs, *new_caches = _mix_s(us, (q0, q1, q2), (kn0, kn1, kn2), gbs, zs, pst, cst, caches_t,
                                           pw, ps, cw, new_caches, n_seq=n_seq, t_new=t_new)
        xs = _mlp_s(xs, yps.reshape(-1, POOL_WIDTH), yas.reshape(-1, GROUP_WIDTH), ycs.reshape(-1, CONV_WIDTH),
                    w_out_l, g2, w_up_l, w_down_l)
        us3 = us.reshape(n_seq, t_new, POOL_WIDTH)
        zs3 = zs.reshape(n_seq, t_new, CONV_WIDTH)
        outs["pool_s"].append(jnp.concatenate([state_pool[layer], us3], axis=1)[:, -POOL_HIST:])
        outs["conv_s"].append(jnp.concatenate([state_conv[layer], zs3], axis=1)[:, -(CONV_K - 1):])
        if w_next:
            w_in_l, w_up_l, w_down_l = w_next

    st = lambda k: jnp.stack(outs[k])
    return (xp.reshape(batch, seq, D_MODEL), xs.reshape(n_seq, t_new, D_MODEL),
            st("pool_p"), st("conv_p"),
            *[_from_buffer_layout(kvt) for kvt in kvts],
            st("pool_s"), st("conv_s"),
            *[_from_buffer_layout(c) for c in new_caches])
```

```python
import functools

import jax
import jax.numpy as jnp
from jax import lax
from jax.experimental import pallas as pl
from jax.experimental.pallas import tpu as pltpu

D_MODEL = 1024
HEAD_DIM = 64
POOL_WIDTH = 256
POOL_WINDOWS = (2, 4, 8, 16)
POOL_GROUP = 64
POOL_HIST = 15
ATTN_WIDTH = 384
DIL_PAIRS = ((128, 1), (512, 4), (2048, 16))
N_DIL = 3
GROUP_WIDTH = 128
CONV_WIDTH = 384
CONV_K = 3
ROPE_DIM = 16
ROPE_THETA = 500000.0
D_FF = 4096
IN_COLS = 2560
MIX_OUT = 768
EPS = 1e-6
NEG_INF = -1e30
PAST_LEN = 8192
N_BACK = 128

LANES = 128
SUBLANES = 8
HIST_ROWS = 16
ATTN_TILE = 128
ATTN_UNROLL = 16
ATTN_MERGE_ROWS = 256
PROJ_TILE = 1024
PROJ_SUB = 512
PROJ_TILE_ROLL = 512
PROJ_SUB_ROLL = 256
MLP_TILE = 512
MLP_FF_CHUNK = 512
MIX_S_SEQS = 4
VMEM_LIMIT = 56 * 1024 * 1024

F32 = jnp.float32
BF16 = jnp.bfloat16

_Q_OFF = POOL_WIDTH
_GB_OFF = _Q_OFF + 3 * ATTN_WIDTH

_NT = (((1,), (1,)), ((), ()))


def _const_spec(shape):
    return pl.BlockSpec(shape, lambda *_: (0,) * len(shape), pipeline_mode=pl.Buffered(1))


def _params():
    return pltpu.CompilerParams(dimension_semantics=("arbitrary",), vmem_limit_bytes=VMEM_LIMIT)


def _cast_specs(convert, n_steps):
    in_specs, out_specs, out_shape, args = [], [], [], []
    if convert is not None:
        layer, weights = convert
        for w in weights:
            _, k, n = w.shape
            in_specs.append(pl.BlockSpec((None, k // n_steps, n), lambda i: (layer, i, 0)))
            out_specs.append(pl.BlockSpec((k // n_steps, n), lambda i: (i, 0)))
            out_shape.append(jax.ShapeDtypeStruct((k, n), BF16))
            args.append(w)
    return in_specs, out_specs, out_shape, args


def _cast_slabs(srcs, dsts):
    for src, dst in zip(srcs, dsts):
        dst[...] = src[...].astype(BF16)


def _proj_kernel(x_ref, g1_ref, w_ref, qg_ref, kg_ref, rc_ref, ra_ref, rb_ref, hm_ref, *rest,
                 class_major, tiles, sub, stacked, n_cast, roll_by):
    prev_kvt = None
    if class_major:
        if stacked:
            prev_kvt, rest = rest[:N_DIL], rest[N_DIL:]
        cast_src, rest = rest[:n_cast], rest[n_cast:]
        roll_src, rest = (rest[:2], rest[2:]) if roll_by else ((), rest)
        (u_ref, gb_ref, z_ref, qc0, qc1, qc2, kvc0, kvc1, kvc2, kvt0, kvt1, kvt2) = rest[:12]
        _cast_slabs(cast_src, rest[12:12 + n_cast])
        for layer, src in enumerate(roll_src):
            window = src.shape[-1]
            rest[12 + n_cast][layer, 0] = pltpu.roll(src[0, 0], window - roll_by, 1)
        nat = rest[-1]
    else:
        (u_ref, gb_ref, z_ref, q0, q1, q2, kv0, kv1, kv2) = rest
    tm = x_ref.shape[0]
    hm = hm_ref[...]

    for s in range(tm // sub):
        r0 = s * sub
        rows = slice(r0, r0 + sub)
        x = x_ref[rows, :]
        ms = jnp.mean(x * x, axis=-1, keepdims=True)
        hb = (x * lax.rsqrt(ms + EPS) * g1_ref[...]).astype(BF16)
        p_att = jnp.dot(hb, w_ref[:, _Q_OFF:_GB_OFF], preferred_element_type=F32)
        rc, ra, rb = rc_ref[rows, :], ra_ref[rows, :], rb_ref[rows, :]

        def head_norm_rope(xb, gain):
            msq = jnp.dot((xb * xb).astype(BF16), hm, preferred_element_type=F32)
            xn = xb * lax.rsqrt(msq + EPS) * gain
            return xn * rc + pltpu.roll(xn, LANES - ROPE_DIM // 2, 1) * ra + pltpu.roll(xn, ROPE_DIM // 2, 1) * rb

        for g, (window, dil) in enumerate(DIL_PAIRS):
            lanes = slice(g * GROUP_WIDTH, (g + 1) * GROUP_WIDTH)
            qn = head_norm_rope(p_att[:, :ATTN_WIDTH][:, lanes], qg_ref[...])
            kn = head_norm_rope(p_att[:, ATTN_WIDTH:2 * ATTN_WIDTH][:, lanes], kg_ref[...])
            vv = p_att[:, 2 * ATTN_WIDTH:][:, lanes]
            if not class_major:
                q_ref, kv_ref = ((q0, kv0), (q1, kv1), (q2, kv2))[g]
                q_ref[rows, :] = qn
                kv_ref[rows, :GROUP_WIDTH] = kn
                kv_ref[rows, GROUP_WIDTH:] = vv
                continue
            qc_ref, kvc_ref, kvt_ref = ((qc0, kvc0, kvt0), (qc1, kvc1, kvt1), (qc2, kvc2, kvt2))[g]
            slab = 3 * (s * N_DIL + g)
            nat[slab] = qn
            nat[slab + 1] = kn
            nat[slab + 2] = vv
            per = sub // dil
            dst = slice(s * per, (s + 1) * per)
            for r in range(dil):
                src = pl.ds(r, per, stride=dil) if dil > 1 else slice(None)
                qc_ref[0, r, dst, :] = nat[slab, src, :].astype(BF16)
                kvc_ref[0, r, dst, :GROUP_WIDTH] = nat[slab + 1, src, :].astype(BF16)
                kvc_ref[0, r, dst, GROUP_WIDTH:] = nat[slab + 2, src, :].astype(BF16)
            def put_kvt(cols, part, g=g, kvt_ref=kvt_ref):
                if stacked:
                    kvt_ref[0, 0, :, cols] = prev_kvt[g][0, :, cols]
                    kvt_ref[1, 0, :, cols] = part
                else:
                    kvt_ref[0, :, cols] = part

            if window >= tiles * tm:
                put_kvt(rows, jnp.concatenate([kn, vv], axis=1).T)
            else:
                first_kept = tm - min(window, tm)
                lo = max(r0, first_kept)
                if lo < r0 + sub:
                    part = jnp.concatenate([kn[lo - r0:], vv[lo - r0:]], axis=1).T
                    put_kvt(slice(lo - first_kept, r0 + sub - first_kept), part)

        u_ref[rows, :] = jnp.dot(hb, w_ref[:, :POOL_WIDTH], preferred_element_type=F32)
        p_gate = jnp.dot(hb, w_ref[:, _GB_OFF:], preferred_element_type=F32)
        gb_ref[rows, :] = p_gate[:, :CONV_WIDTH]
        z_ref[rows, :] = p_gate[:, CONV_WIDTH:2 * CONV_WIDTH] * p_gate[:, 2 * CONV_WIDTH:]


def _proj(x, g1, w_in, qg, kg, rope, hm, *, tm, sub, batch=None, seq=None, prev_kvt=None, convert=None,
          roll=None):
    rows = x.shape[0]
    stacked = prev_kvt is not None
    n_tab = rope[0].shape[0] // tm
    class_major = batch is not None
    tiles = seq // tm if class_major else 1
    row = lambda c: pl.BlockSpec((tm, c), lambda i: (i, 0))
    tab = pl.BlockSpec((tm, LANES), lambda i: (i % n_tab, 0))
    out_shape = [jax.ShapeDtypeStruct((rows, POOL_WIDTH), F32)] + [jax.ShapeDtypeStruct((rows, CONV_WIDTH), F32)] * 2
    out_specs = [row(POOL_WIDTH), row(CONV_WIDTH), row(CONV_WIDTH)]
    scratch = []
    prev_specs = []
    cast_in, cast_out, cast_shape, cast_args = _cast_specs(convert, rows // tm)
    assert class_major or (convert is None and roll is None)
    roll_in, roll_out, roll_shape, roll_args, roll_by = [], [], [], [], 0
    if roll is not None:
        cache_t, roll_by = roll
        layers, n_seq, width, window = cache_t.shape
        assert layers == 2 and n_seq == rows // tm, "one sequence's buffers per grid step"
        roll_in = [pl.BlockSpec((1, 1, width, window), lambda i, l=l: (l, i, 0, 0)) for l in range(layers)]
        roll_out = [pl.BlockSpec((layers, 1, width, window), lambda i: (0, i, 0, 0))]
        roll_shape = [jax.ShapeDtypeStruct(cache_t.shape, F32)]
        roll_args = [cache_t] * layers
    if class_major:
        for width in (GROUP_WIDTH, 2 * GROUP_WIDTH):
            for (_, dil) in DIL_PAIRS:
                out_shape.append(jax.ShapeDtypeStruct((batch, dil, seq // dil, width), BF16))
                out_specs.append(pl.BlockSpec((1, dil, tm // dil, width), lambda i: (i // tiles, 0, i % tiles, 0)))
        for (window, _) in DIL_PAIRS:
            assert window >= seq or window <= tm, "kv buffer rows must come from the last row tile"
            cols = tm if window >= seq else window
            col_blk = (lambda i: i % tiles) if window >= seq else (lambda i: 0)
            kvt_spec = pl.BlockSpec((1, 2 * GROUP_WIDTH, cols), lambda i, c=col_blk: (i // tiles, 0, c(i)))
            if stacked:
                prev_specs.append(kvt_spec)
                out_shape.append(jax.ShapeDtypeStruct((2, batch, 2 * GROUP_WIDTH, window), F32))
                out_specs.append(pl.BlockSpec((2, 1, 2 * GROUP_WIDTH, cols),
                                              lambda i, c=col_blk: (0, i // tiles, 0, c(i))))
            else:
                out_shape.append(jax.ShapeDtypeStruct((batch, 2 * GROUP_WIDTH, window), F32))
                out_specs.append(kvt_spec)
        scratch = [pltpu.VMEM((3 * N_DIL * (tm // sub), sub, GROUP_WIDTH), F32)]
    else:
        out_shape += [jax.ShapeDtypeStruct((rows, GROUP_WIDTH), F32)] * N_DIL
        out_shape += [jax.ShapeDtypeStruct((rows, 2 * GROUP_WIDTH), F32)] * N_DIL
        out_specs += [row(GROUP_WIDTH)] * N_DIL + [row(2 * GROUP_WIDTH)] * N_DIL
    return pl.pallas_call(
        functools.partial(_proj_kernel, class_major=class_major, tiles=tiles, sub=sub, stacked=stacked,
                          n_cast=len(cast_args), roll_by=roll_by),
        grid=(rows // tm,),
        in_specs=[row(D_MODEL), _const_spec((1, D_MODEL)), _const_spec((D_MODEL, IN_COLS)),
                  _const_spec((1, LANES)), _const_spec((1, LANES)), tab, tab, tab,
                  _const_spec((LANES, LANES))] + prev_specs + cast_in + roll_in,
        out_specs=out_specs + cast_out + roll_out,
        out_shape=out_shape + cast_shape + roll_shape,
        scratch_shapes=scratch,
        compiler_params=_params(),
        name="proj",
    )(x, g1, w_in, qg, kg, *rope, hm, *(prev_kvt or ()), *cast_args, *roll_args)


def _lane_lt(shape, bound):
    return lax.broadcasted_iota(jnp.int32, shape, len(shape) - 1) < bound


def _stack_heads(q):
    first = _lane_lt(q.shape, HEAD_DIM)
    zero = jnp.zeros_like(q)
    return jnp.concatenate([jnp.where(first, q, zero), jnp.where(first, zero, q)], axis=0)


def _unstack(o2, l2, m2):
    n = o2.shape[0] // 2
    first = _lane_lt((n, GROUP_WIDTH), HEAD_DIM)
    return tuple(jnp.where(first, a[:n], a[n:]) for a in (o2, l2, m2))


def _merge(a, b):
    m = jnp.maximum(a[2], b[2])
    wa = jnp.exp(a[2] - m)
    wb = jnp.exp(b[2] - m)
    return wa * a[0] + wb * b[0], wa * a[1] + wb * b[1], m


def _attn_unit(q, kv, bias):
    nk = kv.shape[0]
    k = kv[:, :GROUP_WIDTH]
    v1 = jnp.concatenate([kv[:, GROUP_WIDTH:], jnp.ones((nk, GROUP_WIDTH), BF16)], axis=1)
    s = lax.dot_general(_stack_heads(q), k, _NT, preferred_element_type=F32) + bias
    mrow = jnp.max(s, axis=-1, keepdims=True)
    p = jnp.exp(s - mrow).astype(BF16)
    r = jnp.dot(p, v1, preferred_element_type=F32)
    return _unstack(r[:, :GROUP_WIDTH], r[:, GROUP_WIDTH:], mrow)


def _attn_p_kernel(qc0, qc1, qc2, kvc0, kvc1, kvc2, y_ref, *scratch, seq):
    parts = [scratch[3 * g:3 * g + 3] for g in range(N_DIL)]
    bias_ref = scratch[3 * N_DIL]
    acc_o, acc_l, acc_m = parts[0]
    @pl.when(pl.program_id(0) == 0)
    def _():
        qi = lax.broadcasted_iota(jnp.int32, (2 * ATTN_TILE, 2 * ATTN_TILE), 0) & (ATTN_TILE - 1)
        ki = lax.broadcasted_iota(jnp.int32, (2 * ATTN_TILE, 2 * ATTN_TILE), 1)
        for first_key_back in (0, 1):
            dist = qi - ki + first_key_back * ATTN_TILE
            bias_ref[first_key_back] = jnp.where((dist >= 0) & (dist <= N_BACK), 0.0, NEG_INF)

    for g, ((_, dil), q_ref, kv_ref) in enumerate(zip(DIL_PAIRS, (qc0, qc1, qc2), (kvc0, kvc1, kvc2))):
        dst = parts[g]
        n_sub = seq // dil // ATTN_TILE

        def unit(idx, q_ref=q_ref, kv_ref=kv_ref, n_sub=n_sub, dil=dil):
            if n_sub == 1:
                r, sub = idx, 0
                res = _attn_unit(q_ref[0, r], kv_ref[0, r], bias_ref[0, :, :ATTN_TILE])
            else:
                r, sub = (idx // n_sub, idx % n_sub) if dil > 1 else (0, idx)
                back = jnp.minimum(sub, 1)
                q0 = pl.multiple_of(sub * ATTN_TILE, ATTN_TILE)
                k0 = pl.multiple_of((sub - back) * ATTN_TILE, ATTN_TILE)
                res = _attn_unit(q_ref[0, r, pl.ds(q0, ATTN_TILE), :], kv_ref[0, r, pl.ds(k0, 2 * ATTN_TILE), :],
                                 bias_ref[back])
            start = sub * (ATTN_TILE * dil) + r
            rows = pl.ds(start, ATTN_TILE, stride=dil) if dil > 1 else pl.ds(pl.multiple_of(start, ATTN_TILE), ATTN_TILE)
            return rows, res

        def units_body(it, carry, unit=unit, dst=dst):
            for j in range(ATTN_UNROLL):
                rows, res = unit(it * ATTN_UNROLL + j)
                for ref, val in zip(dst, res):
                    ref[rows, :] = val
            return carry
        n_iter = dil * n_sub // ATTN_UNROLL
        if n_iter == 1:
            units_body(0, 0)
        else:
            lax.fori_loop(0, n_iter, units_body, 0)

        if g > 0:
            for c in range(seq // ATTN_MERGE_ROWS):
                rows = slice(c * ATTN_MERGE_ROWS, (c + 1) * ATTN_MERGE_ROWS)
                o, l, m = _merge((acc_o[rows, :], acc_l[rows, :], acc_m[rows, :]),
                                 tuple(ref[rows, :] for ref in parts[g]))
                if g == N_DIL - 1:
                    y_ref[0, rows, :] = (o / l).astype(y_ref.dtype)
                else:
                    acc_o[rows, :] = o
                    acc_l[rows, :] = l
                    acc_m[rows, :] = m


def _attn_p(qcs, kvcs, *, batch, seq):
    in_specs = [pl.BlockSpec((1,) + a.shape[1:], lambda b: (b, 0, 0, 0)) for a in (*qcs, *kvcs)]
    acc = pltpu.VMEM((seq, GROUP_WIDTH), F32)
    return pl.pallas_call(
        functools.partial(_attn_p_kernel, seq=seq),
        grid=(batch,),
        in_specs=in_specs,
        out_specs=pl.BlockSpec((1, seq, GROUP_WIDTH), lambda b: (b, 0, 0)),
        out_shape=jax.ShapeDtypeStruct((batch, seq, GROUP_WIDTH), BF16),
        scratch_shapes=[acc] * (3 * N_DIL) + [pltpu.VMEM((2, 2 * ATTN_TILE, 2 * ATTN_TILE), F32)],
        compiler_params=_params(),
        name="attn_p",
    )(*qcs, *kvcs)


def _pool_rows(ext, pos, pw, ps):
    return jnp.dot(_pool_diff(ext, pos), pw, preferred_element_type=F32) * ps


def _pool_diff(ext, pos):
    s2 = ext + pltpu.roll(ext, 1, 0)
    s4 = s2 + pltpu.roll(s2, 2, 0)
    s8 = s4 + pltpu.roll(s4, 4, 0)
    s16 = s8 + pltpu.roll(s8, 8, 0)
    lane = lax.broadcasted_iota(jnp.int32, (1, POOL_WIDTH), 1)
    grp = [lane < (j + 1) * POOL_GROUP for j in range(3)]
    win = jnp.where(grp[0], s2, jnp.where(grp[1], s4, jnp.where(grp[2], s8, s16)))[HIST_ROWS:]
    width = jnp.where(grp[0], POOL_WINDOWS[0], jnp.where(grp[1], POOL_WINDOWS[1],
                      jnp.where(grp[2], POOL_WINDOWS[2], POOL_WINDOWS[3])))
    cnt = jnp.minimum(pos + 1, width).astype(F32)
    return (win / cnt - ext[HIST_ROWS:]).astype(BF16)


def _conv_rows(ext, gb, cw, hist):
    y = cw[0:1] * pltpu.roll(ext, 2, 0) + cw[1:2] * pltpu.roll(ext, 1, 0) + cw[2:3] * ext
    return gb * y[hist:]


def _attn_s_group(q, new_t, cache_t, window, dil, shifted):
    t = q.shape[0]
    q2 = _stack_heads(q).astype(BF16)
    s_c = jnp.dot(q2, cache_t[:GROUP_WIDTH].astype(BF16), preferred_element_type=F32)
    s_n = jnp.dot(q2, new_t[:GROUP_WIDTH].astype(BF16), preferred_element_type=F32)
    tq = lax.broadcasted_iota(jnp.int32, (2 * t, 1), 0) & (t - 1)
    rc = lax.broadcasted_iota(jnp.int32, (1, window), 1)
    if shifted:
        rc = (rc + t) & (window - 1)
    dist_c = window + tq - rc
    ok_c = (dist_c <= window) & ((dist_c & (dil - 1)) == 0) & (PAST_LEN - window + rc >= 0)
    tn = lax.broadcasted_iota(jnp.int32, (1, LANES), 1) - (LANES - t)
    dist_n = tq - tn
    ok_n = (tn >= 0) & (dist_n >= 0) & ((dist_n & (dil - 1)) == 0)
    s_c = jnp.where(ok_c, s_c, NEG_INF)
    s_n = jnp.where(ok_n, s_n, NEG_INF)
    mrow = jnp.maximum(jnp.max(s_c, axis=-1, keepdims=True), jnp.max(s_n, axis=-1, keepdims=True))
    p_c = jnp.exp(s_c - mrow)
    p_n = jnp.exp(s_n - mrow)
    o2 = (lax.dot_general(p_c.astype(BF16), cache_t[GROUP_WIDTH:].astype(BF16), _NT, preferred_element_type=F32)
          + lax.dot_general(p_n.astype(BF16), new_t[GROUP_WIDTH:].astype(BF16), _NT, preferred_element_type=F32))
    l2 = jnp.sum(p_c, axis=-1, keepdims=True) + jnp.sum(p_n, axis=-1, keepdims=True)
    return _unstack(o2, l2, mrow)


def _mix_s_kernel(u_ref, q0_ref, q1_ref, q2_ref, kn0_ref, kn1_ref, kn2_ref, gb_ref, z_ref, pst_ref, cst_ref,
                  pw_ref, ps_ref, cw_ref, *rest, t_new, rotated, n_per):
    yp_ref, ya_ref, yc_ref, nc0_ref, nc1_ref, nc2_ref = rest[-6:]
    cur_refs = rest[:N_DIL]
    later = iter(rest[N_DIL:-6])
    next_refs = [None if r else next(later) for r in rotated]
    keep_old = _lane_lt((2 * GROUP_WIDTH, LANES), LANES - t_new)
    pos = PAST_LEN + lax.broadcasted_iota(jnp.int32, (t_new, 1), 0)
    for i in range(n_per):
        state = None
        for g, ((window, dil), q_ref, kn_ref, nc_ref) in enumerate(zip(
                DIL_PAIRS, (q0_ref, q1_ref, q2_ref), (kn0_ref, kn1_ref, kn2_ref), (nc0_ref, nc1_ref, nc2_ref))):
            kv_new = kn_ref[i]
            new_t = jnp.concatenate([jnp.zeros((LANES - t_new, 2 * GROUP_WIDTH), F32), kv_new], axis=0).T
            cache_t = cur_refs[g][0, i]
            part = _attn_s_group(q_ref[i], new_t, cache_t, window, dil, shifted=rotated[g])
            state = part if state is None else _merge(state, part)
            moved = cache_t if rotated[g] else pltpu.roll(cache_t, window - t_new, 1)
            last = jnp.where(keep_old, moved[:, window - LANES:], new_t)
            if rotated[g]:
                nc_ref[0, i] = last
            else:
                if window > LANES:
                    nc_ref[0, i, :, :window - LANES] = moved[:, :window - LANES]
                nc_ref[0, i, :, window - LANES:] = last
                nc_ref[1, i] = pltpu.roll(next_refs[g][0, i], window - t_new, 1)
        ya_ref[i] = state[0] / state[1]

        u_ext = jnp.concatenate([pst_ref[i], u_ref[i]], axis=0)
        yp_ref[i] = _pool_rows(u_ext, pos, pw_ref[...], ps_ref[...])
        z_ext = jnp.concatenate([cst_ref[i], z_ref[i]], axis=0)
        yc_ref[i] = _conv_rows(z_ext, gb_ref[i], cw_ref[...], cst_ref.shape[1])


def _mix_s(u, qs, kns, gb, z, pst, cst, caches_t, rotated_bufs, pw, ps, cw, *, layer, n_seq, t_new):
    rotated = tuple(b is not None for b in rotated_bufs)
    n_per = MIX_S_SEQS
    blk = lambda r, c: pl.BlockSpec((n_per, r, c), lambda b: (b, 0, 0))
    cache_blk = lambda l, w: pl.BlockSpec((1, n_per, 2 * GROUP_WIDTH, w), lambda b: (l, b, 0, 0))
    three = lambda a: a.reshape(n_seq, -1, a.shape[-1])
    windows = [w for (w, _) in DIL_PAIRS]
    in_specs = ([blk(t_new, POOL_WIDTH)] + [blk(t_new, GROUP_WIDTH)] * N_DIL + [blk(t_new, 2 * GROUP_WIDTH)] * N_DIL
                + [blk(t_new, CONV_WIDTH)] * 2 + [blk(pst.shape[1], POOL_WIDTH), blk(cst.shape[1], CONV_WIDTH)]
                + [_const_spec((POOL_WIDTH, POOL_WIDTH)), _const_spec((1, POOL_WIDTH)),
                   _const_spec((CONV_K, CONV_WIDTH))])
    args = [three(u), *[three(q) for q in qs], *[three(k) for k in kns], three(gb), three(z), pst, cst, pw, ps, cw]
    aliases, cache_out = {}, []
    for g, w in enumerate(windows):
        in_specs.append(cache_blk(layer, w))
        if rotated[g]:
            aliases[len(args)] = 3 + g
            args.append(rotated_bufs[g])
            cache_out.append(pl.BlockSpec((1, n_per, 2 * GROUP_WIDTH, LANES),
                                          lambda b, w=w: (layer, b, 0, w // LANES - 1)))
        else:
            assert layer == 0 and caches_t[g].shape[0] == 2, "whole-buffer update is written for two layers"
            args.append(caches_t[g])
            cache_out.append(pl.BlockSpec((2, n_per, 2 * GROUP_WIDTH, w), lambda b: (0, b, 0, 0)))
    for g, w in enumerate(windows):
        if not rotated[g]:
            in_specs.append(cache_blk(layer + 1, w))
            args.append(caches_t[g])
    out_specs = [blk(t_new, POOL_WIDTH), blk(t_new, GROUP_WIDTH), blk(t_new, CONV_WIDTH)] + cache_out
    out_shape = ([jax.ShapeDtypeStruct((n_seq, t_new, c), F32) for c in (POOL_WIDTH, GROUP_WIDTH, CONV_WIDTH)]
                 + [jax.ShapeDtypeStruct(c.shape, F32) for c in caches_t])
    return pl.pallas_call(
        functools.partial(_mix_s_kernel, t_new=t_new, rotated=rotated, n_per=n_per),
        grid=(n_seq // n_per,),
        in_specs=in_specs,
        out_specs=out_specs,
        out_shape=out_shape,
        input_output_aliases=aliases,
        compiler_params=_params(),
        name="mix_s",
    )(*args)


def _mlp_head(x, yp, ya, yc, wo_ref, g2_ref):
    mixed = jnp.concatenate([yp.astype(BF16), ya.astype(BF16), yc.astype(BF16)], axis=1)
    x1 = x + jnp.dot(mixed, wo_ref[...], preferred_element_type=F32)
    ms = jnp.mean(x1 * x1, axis=-1, keepdims=True)
    return x1, (x1 * lax.rsqrt(ms + EPS) * g2_ref[...]).astype(BF16)


def _mlp_tail(x1, hb, wu_ref, wd_ref, tf):
    acc = x1
    for c in range(D_FF // tf):
        hf = jnp.dot(hb, wu_ref[:, c * tf:(c + 1) * tf], preferred_element_type=F32)
        act = jnp.square(jnp.maximum(hf, 0.0)).astype(BF16)
        acc = acc + jnp.dot(act, wd_ref[c * tf:(c + 1) * tf, :], preferred_element_type=F32)
    return acc


def _mlp_s_kernel(x_ref, yp_ref, ya_ref, yc_ref, wo_ref, g2_ref, wu_ref, wd_ref, o_ref, hb_buf):
    @pl.when(pl.program_id(0) == 0)
    def _():
        x1, hb = _mlp_head(x_ref[...], yp_ref[...], ya_ref[...], yc_ref[...], wo_ref, g2_ref)
        o_ref[...] = x1
        hb_buf[...] = hb

    hf = jnp.dot(hb_buf[...], wu_ref[...], preferred_element_type=F32)
    act = jnp.square(jnp.maximum(hf, 0.0)).astype(BF16)
    o_ref[...] += jnp.dot(act, wd_ref[...], preferred_element_type=F32)


def _mlp_p_kernel(x_ref, u_ref, uh_ref, gb_ref, z_ref, zh_ref, ya_ref, pw_ref, ps_ref, cw_ref,
                  wo_ref, g2_ref, wu_ref, wd_ref, *rest, tf, tiles):
    n_cast = len(rest) // 2
    o_ref = rest[n_cast]
    _cast_slabs(rest[:n_cast], rest[n_cast + 1:])
    tm = x_ref.shape[0]
    tile = pl.program_id(0) % tiles
    has_hist = tile > 0
    pos = tile * tm + lax.broadcasted_iota(jnp.int32, (tm, 1), 0)
    yp = _pool_rows(jnp.concatenate([jnp.where(has_hist, uh_ref[...], 0.0), u_ref[...]], axis=0), pos,
                    pw_ref[...], ps_ref[...])
    yc = _conv_rows(jnp.concatenate([jnp.where(has_hist, zh_ref[...], 0.0), z_ref[...]], axis=0),
                    gb_ref[...], cw_ref[...], HIST_ROWS)
    x1, hb = _mlp_head(x_ref[...], yp, ya_ref[...], yc, wo_ref, g2_ref)
    o_ref[...] = _mlp_tail(x1, hb, wu_ref, wd_ref, tf)


def _mlp_s(x, yp, ya, yc, w_out, g2, w_up, w_down):
    rows = x.shape[0]
    tf = MLP_FF_CHUNK
    full = lambda c: pl.BlockSpec((rows, c), lambda j: (0, 0))
    return pl.pallas_call(
        _mlp_s_kernel,
        grid=(D_FF // tf,),
        in_specs=[full(D_MODEL), full(POOL_WIDTH), full(GROUP_WIDTH), full(CONV_WIDTH),
                  _const_spec((MIX_OUT, D_MODEL)), _const_spec((1, D_MODEL)),
                  pl.BlockSpec((D_MODEL, tf), lambda j: (0, j)),
                  pl.BlockSpec((tf, D_MODEL), lambda j: (j, 0))],
        out_specs=full(D_MODEL),
        out_shape=jax.ShapeDtypeStruct((rows, D_MODEL), F32),
        scratch_shapes=[pltpu.VMEM((rows, D_MODEL), BF16)],
        compiler_params=_params(),
        name="mlp_s",
    )(x, yp, ya, yc, w_out, g2, w_up, w_down)


def _mlp_p(x, u, gb, z, ya, pw, ps, cw, w_out, g2, w_up, w_down, *, tm, seq, convert=None):
    rows = x.shape[0]
    row = lambda c: pl.BlockSpec((tm, c), lambda i: (i, 0))
    conv_in, conv_out_specs, conv_out_shape, conv_args = _cast_specs(convert, rows // tm)
    hist = lambda c: pl.BlockSpec((HIST_ROWS, c), lambda i: (jnp.maximum(i * (tm // HIST_ROWS) - 1, 0), 0))
    return pl.pallas_call(
        functools.partial(_mlp_p_kernel, tf=MLP_FF_CHUNK, tiles=seq // tm),
        grid=(rows // tm,),
        in_specs=[row(D_MODEL), row(POOL_WIDTH), hist(POOL_WIDTH), row(CONV_WIDTH), row(CONV_WIDTH),
                  hist(CONV_WIDTH), row(GROUP_WIDTH), _const_spec((POOL_WIDTH, POOL_WIDTH)),
                  _const_spec((1, POOL_WIDTH)), _const_spec((CONV_K, CONV_WIDTH)),
                  _const_spec((MIX_OUT, D_MODEL)), _const_spec((1, D_MODEL)),
                  _const_spec((D_MODEL, D_FF)), _const_spec((D_FF, D_MODEL))] + conv_in,
        out_specs=[row(D_MODEL)] + conv_out_specs,
        out_shape=[jax.ShapeDtypeStruct((rows, D_MODEL), F32)] + conv_out_shape,
        compiler_params=_params(),
        name="mlp_p",
    )(x, u, u, gb, z, z, ya, pw, ps, cw, w_out, g2, w_up, w_down, *conv_args)


def _rope_tables(pos):
    half = ROPE_DIM // 2
    inv = jnp.power(jnp.float32(ROPE_THETA), -jnp.arange(half, dtype=F32) / half)
    ang = pos.astype(F32)[:, None] * inv[None, :]
    cos, sin = jnp.cos(ang), jnp.sin(ang)
    n = pos.shape[0]
    rest = HEAD_DIM - ROPE_DIM
    zh = jnp.zeros((n, half), F32)
    c = jnp.concatenate([cos, cos, jnp.ones((n, rest), F32)], axis=1)
    a = jnp.concatenate([-sin, zh, jnp.zeros((n, rest), F32)], axis=1)
    b = jnp.concatenate([zh, sin, jnp.zeros((n, rest), F32)], axis=1)
    return tuple(jnp.tile(t, (1, GROUP_WIDTH // HEAD_DIM)) for t in (c, a, b))


def _to_buffer_layout(c):
    lead = c.shape[:-4]
    n = len(lead)
    t = jnp.transpose(c, (*range(n), n + 1, n + 2, n + 3, n))
    return t.reshape(*lead, 2 * GROUP_WIDTH, c.shape[-4])


def _from_buffer_layout(t):
    lead = t.shape[:-2]
    n = len(lead)
    c = t.reshape(*lead, 2, 2, HEAD_DIM, t.shape[-1])
    return jnp.transpose(c, (*range(n), n + 3, n, n + 1, n + 2))


def kernel(x_prompt, x_sample, state_pool, state_conv, cache_kv_w128, cache_kv_w512, cache_kv_w2048,
           norm1_g, w_in, q_norm_g, k_norm_g, pool_w, pool_scale, conv_w, w_out, norm2_g, w_up, w_down):
    batch, seq, _ = x_prompt.shape
    n_seq, t_new, _ = x_sample.shape
    depth = w_in.shape[0]
    assert depth == 2, "the in-place stacking of the kv state outputs is written for two layers"
    caches_t = [_to_buffer_layout(c) for c in (cache_kv_w128, cache_kv_w512, cache_kv_w2048)]

    rope_p = _rope_tables(jnp.arange(seq, dtype=jnp.int32))
    rope_s = _rope_tables(jnp.tile(PAST_LEN + jnp.arange(t_new, dtype=jnp.int32), n_seq))
    head_id = jnp.arange(LANES) // HEAD_DIM
    hm = jnp.where(head_id[:, None] == head_id[None, :], 1.0 / HEAD_DIM, 0.0).astype(BF16)
    two_heads = lambda gain: jnp.tile(gain, GROUP_WIDTH // HEAD_DIM)[None, :]

    xp = x_prompt.reshape(batch * seq, D_MODEL)
    xs = x_sample.reshape(n_seq * t_new, D_MODEL)
    outs = {k: [] for k in ("pool_p", "conv_p", "pool_s", "conv_s")}
    w_in_l = w_in[0].astype(BF16)
    new_caches = [None] * N_DIL
    kvts = None
    for layer in range(depth):
        pw = jax.scipy.linalg.block_diag(*[pool_w[layer, j] for j in range(len(POOL_WINDOWS))]).astype(BF16)
        ps = pool_scale[layer][None, :]
        cw = conv_w[layer]
        g1 = norm1_g[layer][None, :]
        g2 = norm2_g[layer][None, :]
        qg = two_heads(q_norm_g[layer]) * (HEAD_DIM ** -0.5)
        kg = two_heads(k_norm_g[layer])

        w_out_l = w_out[layer].astype(BF16)
        if layer == 0:
            (u, gb, z, qc0, qc1, qc2, kvc0, kvc1, kvc2, *tail) = _proj(
                xp, g1, w_in_l, qg, kg, rope_p, hm, tm=PROJ_TILE_ROLL, sub=PROJ_SUB_ROLL, batch=batch, seq=seq,
                convert=(0, (w_up, w_down)), roll=(caches_t[-1], t_new))
            kvts, (w_up_l, w_down_l), new_caches[-1] = tail[:N_DIL], tail[N_DIL:N_DIL + 2], tail[N_DIL + 2]
        else:
            (u, gb, z, qc0, qc1, qc2, kvc0, kvc1, kvc2, *kvts) = _proj(
                xp, g1, w_in_l, qg, kg, rope_p, hm, tm=PROJ_TILE, sub=PROJ_SUB, batch=batch, seq=seq,
                prev_kvt=kvts)
        ya = _attn_p((qc0, qc1, qc2), (kvc0, kvc1, kvc2), batch=batch, seq=seq)
        convert = (layer + 1, (w_in, w_up, w_down)) if layer + 1 < depth else None
        xp, *w_next = _mlp_p(xp, u, gb, z, ya.reshape(-1, GROUP_WIDTH), pw, ps, cw, w_out_l, g2, w_up_l, w_down_l,
                             tm=MLP_TILE, seq=seq, convert=convert)
        outs["pool_p"].append(u.reshape(batch, seq, POOL_WIDTH)[:, seq - POOL_HIST:])
        outs["conv_p"].append(z.reshape(batch, seq, CONV_WIDTH)[:, seq - (CONV_K - 1):])

        us, gbs, zs, q0, q1, q2, kn0, kn1, kn2 = _proj(xs, g1, w_in_l, qg, kg, rope_s, hm, tm=n_seq * t_new,
                                                       sub=n_seq * t_new)
        pst = jnp.pad(state_pool[layer], ((0, 0), (HIST_ROWS - POOL_HIST, 0), (0, 0)))
        cst = jnp.pad(state_conv[layer], ((0, 0), (SUBLANES - (CONV_K - 1), 0), (0, 0)))
        yps, yas, ycs, *new_caches = _mix_s(us, (q0, q1, q2), (kn0, kn1, kn2), gbs, zs, pst, cst, caches_t,
                                           new_caches, pw, ps, cw, layer=layer, n_seq=n_seq, t_new=t_new)
        xs = _mlp_s(xs, yps.reshape(-1, POOL_WIDTH), yas.reshape(-1, GROUP_WIDTH), ycs.reshape(-1, CONV_WIDTH),
                    w_out_l, g2, w_up_l, w_down_l)
        us3 = us.reshape(n_seq, t_new, POOL_WIDTH)
        zs3 = zs.reshape(n_seq, t_new, CONV_WIDTH)
        outs["pool_s"].append(jnp.concatenate([state_pool[layer], us3], axis=1)[:, -POOL_HIST:])
        outs["conv_s"].append(jnp.concatenate([state_conv[layer], zs3], axis=1)[:, -(CONV_K - 1):])
        if w_next:
            w_in_l, w_up_l, w_down_l = w_next

    st = lambda k: jnp.stack(outs[k])
    return (xp.reshape(batch, seq, D_MODEL), xs.reshape(n_seq, t_new, D_MODEL),
            st("pool_p"), st("conv_p"),
            *[_from_buffer_layout(kvt) for kvt in kvts],
            st("pool_s"), st("conv_s"),
            *[_from_buffer_layout(c) for c in new_caches])
```

```python
import functools

import jax
import jax.numpy as jnp
from jax import lax
from jax.experimental import pallas as pl
from jax.experimental.pallas import tpu as pltpu

D_MODEL = 1024
HEAD_DIM = 64
POOL_WIDTH = 256
POOL_WINDOWS = (2, 4, 8, 16)
POOL_GROUP = 64
POOL_HIST = 15
ATTN_WIDTH = 384
DIL_PAIRS = ((128, 1), (512, 4), (2048, 16))
N_DIL = 3
GROUP_WIDTH = 128
CONV_WIDTH = 384
CONV_K = 3
ROPE_DIM = 16
ROPE_THETA = 500000.0
D_FF = 4096
IN_COLS = 2560
MIX_OUT = 768
EPS = 1e-6
NEG_INF = -1e30
PAST_LEN = 8192
N_BACK = 128

LANES = 128
SUBLANES = 8
HIST_ROWS = 16
ATTN_TILE = 128
ATTN_UNROLL = 16
ATTN_MERGE_ROWS = 256
PROJ_TILE = 1024
PROJ_SUB = 512
PROJ_TILE_ROLL = 512
PROJ_SUB_ROLL = 256
MLP_TILE = 512
MLP_FF_CHUNK = 512
MIX_S_SEQS = 4
VMEM_LIMIT = 56 * 1024 * 1024

F32 = jnp.float32
BF16 = jnp.bfloat16

_Q_OFF = POOL_WIDTH
_GB_OFF = _Q_OFF + 3 * ATTN_WIDTH

_NT = (((1,), (1,)), ((), ()))


def _const_spec(shape):
    return pl.BlockSpec(shape, lambda *_: (0,) * len(shape), pipeline_mode=pl.Buffered(1))


def _params():
    return pltpu.CompilerParams(dimension_semantics=("arbitrary",), vmem_limit_bytes=VMEM_LIMIT)


def _cast_specs(convert, n_steps):
    in_specs, out_specs, out_shape, args = [], [], [], []
    if convert is not None:
        layer, weights = convert
        for w in weights:
            _, k, n = w.shape
            in_specs.append(pl.BlockSpec((None, k // n_steps, n), lambda i: (layer, i, 0)))
            out_specs.append(pl.BlockSpec((k // n_steps, n), lambda i: (i, 0)))
            out_shape.append(jax.ShapeDtypeStruct((k, n), BF16))
            args.append(w)
    return in_specs, out_specs, out_shape, args


def _cast_slabs(srcs, dsts):
    for src, dst in zip(srcs, dsts):
        dst[...] = src[...].astype(BF16)


def _roll_specs(roll, n_steps):
    in_specs, out_specs, out_shape, args, by = [], [], [], [], 0
    if roll is not None:
        caches, by = roll
        for cache_t in caches:
            layers, n_seq, width, window = cache_t.shape
            assert layers == 2 and n_seq == n_steps, "one sequence's buffers per grid step"
            in_specs += [pl.BlockSpec((1, 1, width, window), lambda i, l=l: (l, i, 0, 0)) for l in range(layers)]
            out_specs.append(pl.BlockSpec((layers, 1, width, window), lambda i: (0, i, 0, 0)))
            out_shape.append(jax.ShapeDtypeStruct(cache_t.shape, F32))
            args += [cache_t] * layers
    return in_specs, out_specs, out_shape, args, by


def _roll_buffers(srcs, dsts, by):
    for g, dst in enumerate(dsts):
        for layer in range(2):
            src = srcs[2 * g + layer]
            dst[layer, 0] = pltpu.roll(src[0, 0], src.shape[-1] - by, 1)


def _proj_kernel(x_ref, g1_ref, w_ref, qg_ref, kg_ref, rc_ref, ra_ref, rb_ref, hm_ref, *rest,
                 class_major, tiles, sub, stacked, n_cast, n_roll, roll_by):
    prev_kvt = None
    if class_major:
        if stacked:
            prev_kvt, rest = rest[:N_DIL], rest[N_DIL:]
        cast_src, rest = rest[:n_cast], rest[n_cast:]
        roll_src, rest = rest[:2 * n_roll], rest[2 * n_roll:]
        (u_ref, gb_ref, z_ref, qc0, qc1, qc2, kvc0, kvc1, kvc2, kvt0, kvt1, kvt2) = rest[:12]
        _cast_slabs(cast_src, rest[12:12 + n_cast])
        _roll_buffers(roll_src, rest[12 + n_cast:12 + n_cast + n_roll], roll_by)
        nat = rest[-1]
    else:
        (u_ref, gb_ref, z_ref, q0, q1, q2, kv0, kv1, kv2) = rest
    tm = x_ref.shape[0]
    hm = hm_ref[...]

    for s in range(tm // sub):
        r0 = s * sub
        rows = slice(r0, r0 + sub)
        x = x_ref[rows, :]
        ms = jnp.mean(x * x, axis=-1, keepdims=True)
        hb = (x * lax.rsqrt(ms + EPS) * g1_ref[...]).astype(BF16)
        p_att = jnp.dot(hb, w_ref[:, _Q_OFF:_GB_OFF], preferred_element_type=F32)
        rc, ra, rb = rc_ref[rows, :], ra_ref[rows, :], rb_ref[rows, :]

        def head_norm_rope(xb, gain):
            msq = jnp.dot((xb * xb).astype(BF16), hm, preferred_element_type=F32)
            xn = xb * lax.rsqrt(msq + EPS) * gain
            return xn * rc + pltpu.roll(xn, LANES - ROPE_DIM // 2, 1) * ra + pltpu.roll(xn, ROPE_DIM // 2, 1) * rb

        for g, (window, dil) in enumerate(DIL_PAIRS):
            lanes = slice(g * GROUP_WIDTH, (g + 1) * GROUP_WIDTH)
            qn = head_norm_rope(p_att[:, :ATTN_WIDTH][:, lanes], qg_ref[...])
            kn = head_norm_rope(p_att[:, ATTN_WIDTH:2 * ATTN_WIDTH][:, lanes], kg_ref[...])
            vv = p_att[:, 2 * ATTN_WIDTH:][:, lanes]
            if not class_major:
                q_ref, kv_ref = ((q0, kv0), (q1, kv1), (q2, kv2))[g]
                q_ref[rows, :] = qn
                kv_ref[rows, :GROUP_WIDTH] = kn
                kv_ref[rows, GROUP_WIDTH:] = vv
                continue
            qc_ref, kvc_ref, kvt_ref = ((qc0, kvc0, kvt0), (qc1, kvc1, kvt1), (qc2, kvc2, kvt2))[g]
            slab = 3 * (s * N_DIL + g)
            nat[slab] = qn
            nat[slab + 1] = kn
            nat[slab + 2] = vv
            per = sub // dil
            dst = slice(s * per, (s + 1) * per)
            for r in range(dil):
                src = pl.ds(r, per, stride=dil) if dil > 1 else slice(None)
                qc_ref[0, r, dst, :] = nat[slab, src, :].astype(BF16)
                kvc_ref[0, r, dst, :GROUP_WIDTH] = nat[slab + 1, src, :].astype(BF16)
                kvc_ref[0, r, dst, GROUP_WIDTH:] = nat[slab + 2, src, :].astype(BF16)
            def put_kvt(cols, part, g=g, kvt_ref=kvt_ref):
                if stacked:
                    kvt_ref[0, 0, :, cols] = prev_kvt[g][0, :, cols]
                    kvt_ref[1, 0, :, cols] = part
                else:
                    kvt_ref[0, :, cols] = part

            if window >= tiles * tm:
                put_kvt(rows, jnp.concatenate([kn, vv], axis=1).T)
            else:
                first_kept = tm - min(window, tm)
                lo = max(r0, first_kept)
                if lo < r0 + sub:
                    part = jnp.concatenate([kn[lo - r0:], vv[lo - r0:]], axis=1).T
                    put_kvt(slice(lo - first_kept, r0 + sub - first_kept), part)

        u_ref[rows, :] = jnp.dot(hb, w_ref[:, :POOL_WIDTH], preferred_element_type=F32)
        p_gate = jnp.dot(hb, w_ref[:, _GB_OFF:], preferred_element_type=F32)
        gb_ref[rows, :] = p_gate[:, :CONV_WIDTH]
        z_ref[rows, :] = p_gate[:, CONV_WIDTH:2 * CONV_WIDTH] * p_gate[:, 2 * CONV_WIDTH:]


def _proj(x, g1, w_in, qg, kg, rope, hm, *, tm, sub, batch=None, seq=None, prev_kvt=None, convert=None,
          roll=None):
    rows = x.shape[0]
    stacked = prev_kvt is not None
    n_tab = rope[0].shape[0] // tm
    class_major = batch is not None
    tiles = seq // tm if class_major else 1
    row = lambda c: pl.BlockSpec((tm, c), lambda i: (i, 0))
    tab = pl.BlockSpec((tm, LANES), lambda i: (i % n_tab, 0))
    out_shape = [jax.ShapeDtypeStruct((rows, POOL_WIDTH), F32)] + [jax.ShapeDtypeStruct((rows, CONV_WIDTH), F32)] * 2
    out_specs = [row(POOL_WIDTH), row(CONV_WIDTH), row(CONV_WIDTH)]
    scratch = []
    prev_specs = []
    cast_in, cast_out, cast_shape, cast_args = _cast_specs(convert, rows // tm)
    assert class_major or (convert is None and roll is None)
    roll_in, roll_out, roll_shape, roll_args, roll_by = _roll_specs(roll, rows // tm)
    if class_major:
        for width in (GROUP_WIDTH, 2 * GROUP_WIDTH):
            for (_, dil) in DIL_PAIRS:
                out_shape.append(jax.ShapeDtypeStruct((batch, dil, seq // dil, width), BF16))
                out_specs.append(pl.BlockSpec((1, dil, tm // dil, width), lambda i: (i // tiles, 0, i % tiles, 0)))
        for (window, _) in DIL_PAIRS:
            assert window >= seq or window <= tm, "kv buffer rows must come from the last row tile"
            cols = tm if window >= seq else window
            col_blk = (lambda i: i % tiles) if window >= seq else (lambda i: 0)
            kvt_spec = pl.BlockSpec((1, 2 * GROUP_WIDTH, cols), lambda i, c=col_blk: (i // tiles, 0, c(i)))
            if stacked:
                prev_specs.append(kvt_spec)
                out_shape.append(jax.ShapeDtypeStruct((2, batch, 2 * GROUP_WIDTH, window), F32))
                out_specs.append(pl.BlockSpec((2, 1, 2 * GROUP_WIDTH, cols),
                                              lambda i, c=col_blk: (0, i // tiles, 0, c(i))))
            else:
                out_shape.append(jax.ShapeDtypeStruct((batch, 2 * GROUP_WIDTH, window), F32))
                out_specs.append(kvt_spec)
        scratch = [pltpu.VMEM((3 * N_DIL * (tm // sub), sub, GROUP_WIDTH), F32)]
    else:
        out_shape += [jax.ShapeDtypeStruct((rows, GROUP_WIDTH), F32)] * N_DIL
        out_shape += [jax.ShapeDtypeStruct((rows, 2 * GROUP_WIDTH), F32)] * N_DIL
        out_specs += [row(GROUP_WIDTH)] * N_DIL + [row(2 * GROUP_WIDTH)] * N_DIL
    return pl.pallas_call(
        functools.partial(_proj_kernel, class_major=class_major, tiles=tiles, sub=sub, stacked=stacked,
                          n_cast=len(cast_args), n_roll=len(roll_out), roll_by=roll_by),
        grid=(rows // tm,),
        in_specs=[row(D_MODEL), _const_spec((1, D_MODEL)), _const_spec((D_MODEL, IN_COLS)),
                  _const_spec((1, LANES)), _const_spec((1, LANES)), tab, tab, tab,
                  _const_spec((LANES, LANES))] + prev_specs + cast_in + roll_in,
        out_specs=out_specs + cast_out + roll_out,
        out_shape=out_shape + cast_shape + roll_shape,
        scratch_shapes=scratch,
        compiler_params=_params(),
        name="proj",
    )(x, g1, w_in, qg, kg, *rope, hm, *(prev_kvt or ()), *cast_args, *roll_args)


def _lane_lt(shape, bound):
    return lax.broadcasted_iota(jnp.int32, shape, len(shape) - 1) < bound


def _stack_heads(q):
    first = _lane_lt(q.shape, HEAD_DIM)
    zero = jnp.zeros_like(q)
    return jnp.concatenate([jnp.where(first, q, zero), jnp.where(first, zero, q)], axis=0)


def _unstack(o2, l2, m2):
    n = o2.shape[0] // 2
    first = _lane_lt((n, GROUP_WIDTH), HEAD_DIM)
    return tuple(jnp.where(first, a[:n], a[n:]) for a in (o2, l2, m2))


def _merge(a, b):
    m = jnp.maximum(a[2], b[2])
    wa = jnp.exp(a[2] - m)
    wb = jnp.exp(b[2] - m)
    return wa * a[0] + wb * b[0], wa * a[1] + wb * b[1], m


def _attn_unit(q, kv, bias):
    nk = kv.shape[0]
    k = kv[:, :GROUP_WIDTH]
    v1 = jnp.concatenate([kv[:, GROUP_WIDTH:], jnp.ones((nk, GROUP_WIDTH), BF16)], axis=1)
    s = lax.dot_general(_stack_heads(q), k, _NT, preferred_element_type=F32) + bias
    mrow = jnp.max(s, axis=-1, keepdims=True)
    p = jnp.exp(s - mrow).astype(BF16)
    r = jnp.dot(p, v1, preferred_element_type=F32)
    return _unstack(r[:, :GROUP_WIDTH], r[:, GROUP_WIDTH:], mrow)


def _attn_p_kernel(qc0, qc1, qc2, kvc0, kvc1, kvc2, *rest, seq, n_cast):
    cast_src, y_ref, rest = rest[:n_cast], rest[n_cast], rest[n_cast + 1:]
    _cast_slabs(cast_src, rest[:n_cast])
    scratch = rest[n_cast:]
    parts = [scratch[3 * g:3 * g + 3] for g in range(N_DIL)]
    bias_ref = scratch[3 * N_DIL]
    acc_o, acc_l, acc_m = parts[0]
    @pl.when(pl.program_id(0) == 0)
    def _():
        qi = lax.broadcasted_iota(jnp.int32, (2 * ATTN_TILE, 2 * ATTN_TILE), 0) & (ATTN_TILE - 1)
        ki = lax.broadcasted_iota(jnp.int32, (2 * ATTN_TILE, 2 * ATTN_TILE), 1)
        for first_key_back in (0, 1):
            dist = qi - ki + first_key_back * ATTN_TILE
            bias_ref[first_key_back] = jnp.where((dist >= 0) & (dist <= N_BACK), 0.0, NEG_INF)

    for g, ((_, dil), q_ref, kv_ref) in enumerate(zip(DIL_PAIRS, (qc0, qc1, qc2), (kvc0, kvc1, kvc2))):
        dst = parts[g]
        n_sub = seq // dil // ATTN_TILE

        def unit(idx, q_ref=q_ref, kv_ref=kv_ref, n_sub=n_sub, dil=dil):
            if n_sub == 1:
                r, sub = idx, 0
                res = _attn_unit(q_ref[0, r], kv_ref[0, r], bias_ref[0, :, :ATTN_TILE])
            else:
                r, sub = (idx // n_sub, idx % n_sub) if dil > 1 else (0, idx)
                back = jnp.minimum(sub, 1)
                q0 = pl.multiple_of(sub * ATTN_TILE, ATTN_TILE)
                k0 = pl.multiple_of((sub - back) * ATTN_TILE, ATTN_TILE)
                res = _attn_unit(q_ref[0, r, pl.ds(q0, ATTN_TILE), :], kv_ref[0, r, pl.ds(k0, 2 * ATTN_TILE), :],
                                 bias_ref[back])
            start = sub * (ATTN_TILE * dil) + r
            rows = pl.ds(start, ATTN_TILE, stride=dil) if dil > 1 else pl.ds(pl.multiple_of(start, ATTN_TILE), ATTN_TILE)
            return rows, res

        def units_body(it, carry, unit=unit, dst=dst):
            for j in range(ATTN_UNROLL):
                rows, res = unit(it * ATTN_UNROLL + j)
                for ref, val in zip(dst, res):
                    ref[rows, :] = val
            return carry
        n_iter = dil * n_sub // ATTN_UNROLL
        if n_iter == 1:
            units_body(0, 0)
        else:
            lax.fori_loop(0, n_iter, units_body, 0)

        if g > 0:
            for c in range(seq // ATTN_MERGE_ROWS):
                rows = slice(c * ATTN_MERGE_ROWS, (c + 1) * ATTN_MERGE_ROWS)
                o, l, m = _merge((acc_o[rows, :], acc_l[rows, :], acc_m[rows, :]),
                                 tuple(ref[rows, :] for ref in parts[g]))
                if g == N_DIL - 1:
                    y_ref[0, rows, :] = (o / l).astype(y_ref.dtype)
                else:
                    acc_o[rows, :] = o
                    acc_l[rows, :] = l
                    acc_m[rows, :] = m


def _attn_p(qcs, kvcs, *, batch, seq, convert=None):
    in_specs = [pl.BlockSpec((1,) + a.shape[1:], lambda b: (b, 0, 0, 0)) for a in (*qcs, *kvcs)]
    cast_in, cast_out, cast_shape, cast_args = _cast_specs(convert, batch)
    acc = pltpu.VMEM((seq, GROUP_WIDTH), F32)
    return pl.pallas_call(
        functools.partial(_attn_p_kernel, seq=seq, n_cast=len(cast_args)),
        grid=(batch,),
        in_specs=in_specs + cast_in,
        out_specs=[pl.BlockSpec((1, seq, GROUP_WIDTH), lambda b: (b, 0, 0))] + cast_out,
        out_shape=[jax.ShapeDtypeStruct((batch, seq, GROUP_WIDTH), BF16)] + cast_shape,
        scratch_shapes=[acc] * (3 * N_DIL) + [pltpu.VMEM((2, 2 * ATTN_TILE, 2 * ATTN_TILE), F32)],
        compiler_params=_params(),
        name="attn_p",
    )(*qcs, *kvcs, *cast_args)


def _pool_rows(ext, pos, pw, ps):
    return jnp.dot(_pool_diff(ext, pos), pw, preferred_element_type=F32) * ps


def _pool_diff(ext, pos):
    s2 = ext + pltpu.roll(ext, 1, 0)
    s4 = s2 + pltpu.roll(s2, 2, 0)
    s8 = s4 + pltpu.roll(s4, 4, 0)
    s16 = s8 + pltpu.roll(s8, 8, 0)
    lane = lax.broadcasted_iota(jnp.int32, (1, POOL_WIDTH), 1)
    grp = [lane < (j + 1) * POOL_GROUP for j in range(3)]
    win = jnp.where(grp[0], s2, jnp.where(grp[1], s4, jnp.where(grp[2], s8, s16)))[HIST_ROWS:]
    width = jnp.where(grp[0], POOL_WINDOWS[0], jnp.where(grp[1], POOL_WINDOWS[1],
                      jnp.where(grp[2], POOL_WINDOWS[2], POOL_WINDOWS[3])))
    cnt = jnp.minimum(pos + 1, width).astype(F32)
    return (win / cnt - ext[HIST_ROWS:]).astype(BF16)


def _conv_rows(ext, gb, cw, hist):
    y = cw[0:1] * pltpu.roll(ext, 2, 0) + cw[1:2] * pltpu.roll(ext, 1, 0) + cw[2:3] * ext
    return gb * y[hist:]


def _attn_s_group(q, new_t, cache_t, window, dil):
    t = q.shape[0]
    q2 = _stack_heads(q).astype(BF16)
    s_c = jnp.dot(q2, cache_t[:GROUP_WIDTH].astype(BF16), preferred_element_type=F32)
    s_n = jnp.dot(q2, new_t[:GROUP_WIDTH].astype(BF16), preferred_element_type=F32)
    tq = lax.broadcasted_iota(jnp.int32, (2 * t, 1), 0) & (t - 1)
    rc = (lax.broadcasted_iota(jnp.int32, (1, window), 1) + t) & (window - 1)
    dist_c = window + tq - rc
    ok_c = (dist_c <= window) & ((dist_c & (dil - 1)) == 0) & (PAST_LEN - window + rc >= 0)
    tn = lax.broadcasted_iota(jnp.int32, (1, LANES), 1) - (LANES - t)
    dist_n = tq - tn
    ok_n = (tn >= 0) & (dist_n >= 0) & ((dist_n & (dil - 1)) == 0)
    s_c = jnp.where(ok_c, s_c, NEG_INF)
    s_n = jnp.where(ok_n, s_n, NEG_INF)
    mrow = jnp.maximum(jnp.max(s_c, axis=-1, keepdims=True), jnp.max(s_n, axis=-1, keepdims=True))
    p_c = jnp.exp(s_c - mrow)
    p_n = jnp.exp(s_n - mrow)
    o2 = (lax.dot_general(p_c.astype(BF16), cache_t[GROUP_WIDTH:].astype(BF16), _NT, preferred_element_type=F32)
          + lax.dot_general(p_n.astype(BF16), new_t[GROUP_WIDTH:].astype(BF16), _NT, preferred_element_type=F32))
    l2 = jnp.sum(p_c, axis=-1, keepdims=True) + jnp.sum(p_n, axis=-1, keepdims=True)
    return _unstack(o2, l2, mrow)


def _mix_s_kernel(u_ref, q0_ref, q1_ref, q2_ref, kn0_ref, kn1_ref, kn2_ref, gb_ref, z_ref, pst_ref, cst_ref,
                  pw_ref, ps_ref, cw_ref, c0_ref, c1_ref, c2_ref, yp_ref, ya_ref, yc_ref, nc0_ref, nc1_ref, nc2_ref,
                  *, t_new, n_per):
    keep_old = _lane_lt((2 * GROUP_WIDTH, LANES), LANES - t_new)
    pos = PAST_LEN + lax.broadcasted_iota(jnp.int32, (t_new, 1), 0)
    for i in range(n_per):
        state = None
        for (window, dil), q_ref, kn_ref, c_ref, nc_ref in zip(
                DIL_PAIRS, (q0_ref, q1_ref, q2_ref), (kn0_ref, kn1_ref, kn2_ref),
                (c0_ref, c1_ref, c2_ref), (nc0_ref, nc1_ref, nc2_ref)):
            kv_new = kn_ref[i]
            new_t = jnp.concatenate([jnp.zeros((LANES - t_new, 2 * GROUP_WIDTH), F32), kv_new], axis=0).T
            cache_t = c_ref[0, i]
            part = _attn_s_group(q_ref[i], new_t, cache_t, window, dil)
            state = part if state is None else _merge(state, part)
            nc_ref[0, i] = jnp.where(keep_old, cache_t[:, window - LANES:], new_t)
        ya_ref[i] = state[0] / state[1]

        u_ext = jnp.concatenate([pst_ref[i], u_ref[i]], axis=0)
        yp_ref[i] = _pool_rows(u_ext, pos, pw_ref[...], ps_ref[...])
        z_ext = jnp.concatenate([cst_ref[i], z_ref[i]], axis=0)
        yc_ref[i] = _conv_rows(z_ext, gb_ref[i], cw_ref[...], cst_ref.shape[1])


def _mix_s(u, qs, kns, gb, z, pst, cst, rotated_bufs, pw, ps, cw, *, layer, n_seq, t_new):
    n_per = MIX_S_SEQS
    blk = lambda r, c: pl.BlockSpec((n_per, r, c), lambda b: (b, 0, 0))
    three = lambda a: a.reshape(n_seq, -1, a.shape[-1])
    windows = [w for (w, _) in DIL_PAIRS]
    in_specs = ([blk(t_new, POOL_WIDTH)] + [blk(t_new, GROUP_WIDTH)] * N_DIL + [blk(t_new, 2 * GROUP_WIDTH)] * N_DIL
                + [blk(t_new, CONV_WIDTH)] * 2 + [blk(pst.shape[1], POOL_WIDTH), blk(cst.shape[1], CONV_WIDTH)]
                + [_const_spec((POOL_WIDTH, POOL_WIDTH)), _const_spec((1, POOL_WIDTH)),
                   _const_spec((CONV_K, CONV_WIDTH))]
                + [pl.BlockSpec((1, n_per, 2 * GROUP_WIDTH, w), lambda b: (layer, b, 0, 0)) for w in windows])
    args = [three(u), *[three(q) for q in qs], *[three(k) for k in kns], three(gb), three(z), pst, cst, pw, ps, cw,
            *rotated_bufs]
    n_in = len(args)
    out_specs = ([blk(t_new, POOL_WIDTH), blk(t_new, GROUP_WIDTH), blk(t_new, CONV_WIDTH)]
                 + [pl.BlockSpec((1, n_per, 2 * GROUP_WIDTH, LANES), lambda b, w=w: (layer, b, 0, w // LANES - 1))
                    for w in windows])
    out_shape = ([jax.ShapeDtypeStruct((n_seq, t_new, c), F32) for c in (POOL_WIDTH, GROUP_WIDTH, CONV_WIDTH)]
                 + [jax.ShapeDtypeStruct(c.shape, F32) for c in rotated_bufs])
    return pl.pallas_call(
        functools.partial(_mix_s_kernel, t_new=t_new, n_per=n_per),
        grid=(n_seq // n_per,),
        in_specs=in_specs,
        out_specs=out_specs,
        out_shape=out_shape,
        input_output_aliases={n_in - N_DIL + g: 3 + g for g in range(N_DIL)},
        compiler_params=_params(),
        name="mix_s",
    )(*args)


def _mlp_head(x, yp, ya, yc, wo_ref, g2_ref):
    mixed = jnp.concatenate([yp.astype(BF16), ya.astype(BF16), yc.astype(BF16)], axis=1)
    x1 = x + jnp.dot(mixed, wo_ref[...], preferred_element_type=F32)
    ms = jnp.mean(x1 * x1, axis=-1, keepdims=True)
    return x1, (x1 * lax.rsqrt(ms + EPS) * g2_ref[...]).astype(BF16)


def _mlp_tail(x1, hb, wu_ref, wd_ref, tf):
    acc = x1
    for c in range(D_FF // tf):
        hf = jnp.dot(hb, wu_ref[:, c * tf:(c + 1) * tf], preferred_element_type=F32)
        act = jnp.square(jnp.maximum(hf, 0.0)).astype(BF16)
        acc = acc + jnp.dot(act, wd_ref[c * tf:(c + 1) * tf, :], preferred_element_type=F32)
    return acc


def _mlp_s_kernel(x_ref, yp_ref, ya_ref, yc_ref, wo_ref, g2_ref, wu_ref, wd_ref, o_ref, hb_buf):
    @pl.when(pl.program_id(0) == 0)
    def _():
        x1, hb = _mlp_head(x_ref[...], yp_ref[...], ya_ref[...], yc_ref[...], wo_ref, g2_ref)
        o_ref[...] = x1
        hb_buf[...] = hb

    hf = jnp.dot(hb_buf[...], wu_ref[...], preferred_element_type=F32)
    act = jnp.square(jnp.maximum(hf, 0.0)).astype(BF16)
    o_ref[...] += jnp.dot(act, wd_ref[...], preferred_element_type=F32)


def _mlp_p_kernel(x_ref, u_ref, uh_ref, gb_ref, z_ref, zh_ref, ya_ref, pw_ref, ps_ref, cw_ref,
                  wo_ref, g2_ref, wu_ref, wd_ref, *rest, tf, tiles, n_cast, n_roll, roll_by):
    cast_src, roll_src, o_ref = rest[:n_cast], rest[n_cast:n_cast + 2 * n_roll], rest[n_cast + 2 * n_roll]
    outs = rest[n_cast + 2 * n_roll + 1:]
    _cast_slabs(cast_src, outs[:n_cast])
    _roll_buffers(roll_src, outs[n_cast:], roll_by)
    tm = x_ref.shape[0]
    tile = pl.program_id(0) % tiles
    has_hist = tile > 0
    pos = tile * tm + lax.broadcasted_iota(jnp.int32, (tm, 1), 0)
    yp = _pool_rows(jnp.concatenate([jnp.where(has_hist, uh_ref[...], 0.0), u_ref[...]], axis=0), pos,
                    pw_ref[...], ps_ref[...])
    yc = _conv_rows(jnp.concatenate([jnp.where(has_hist, zh_ref[...], 0.0), z_ref[...]], axis=0),
                    gb_ref[...], cw_ref[...], HIST_ROWS)
    x1, hb = _mlp_head(x_ref[...], yp, ya_ref[...], yc, wo_ref, g2_ref)
    o_ref[...] = _mlp_tail(x1, hb, wu_ref, wd_ref, tf)


def _mlp_s(x, yp, ya, yc, w_out, g2, w_up, w_down):
    rows = x.shape[0]
    tf = MLP_FF_CHUNK
    full = lambda c: pl.BlockSpec((rows, c), lambda j: (0, 0))
    return pl.pallas_call(
        _mlp_s_kernel,
        grid=(D_FF // tf,),
        in_specs=[full(D_MODEL), full(POOL_WIDTH), full(GROUP_WIDTH), full(CONV_WIDTH),
                  _const_spec((MIX_OUT, D_MODEL)), _const_spec((1, D_MODEL)),
                  pl.BlockSpec((D_MODEL, tf), lambda j: (0, j)),
                  pl.BlockSpec((tf, D_MODEL), lambda j: (j, 0))],
        out_specs=full(D_MODEL),
        out_shape=jax.ShapeDtypeStruct((rows, D_MODEL), F32),
        scratch_shapes=[pltpu.VMEM((rows, D_MODEL), BF16)],
        compiler_params=_params(),
        name="mlp_s",
    )(x, yp, ya, yc, w_out, g2, w_up, w_down)


def _mlp_p(x, u, gb, z, ya, pw, ps, cw, w_out, g2, w_up, w_down, *, tm, seq, convert=None, roll=None):
    rows = x.shape[0]
    row = lambda c: pl.BlockSpec((tm, c), lambda i: (i, 0))
    conv_in, conv_out_specs, conv_out_shape, conv_args = _cast_specs(convert, rows // tm)
    roll_in, roll_out, roll_shape, roll_args, roll_by = _roll_specs(roll, rows // tm)
    hist = lambda c: pl.BlockSpec((HIST_ROWS, c), lambda i: (jnp.maximum(i * (tm // HIST_ROWS) - 1, 0), 0))
    return pl.pallas_call(
        functools.partial(_mlp_p_kernel, tf=MLP_FF_CHUNK, tiles=seq // tm, n_cast=len(conv_args),
                          n_roll=len(roll_out), roll_by=roll_by),
        grid=(rows // tm,),
        in_specs=[row(D_MODEL), row(POOL_WIDTH), hist(POOL_WIDTH), row(CONV_WIDTH), row(CONV_WIDTH),
                  hist(CONV_WIDTH), row(GROUP_WIDTH), _const_spec((POOL_WIDTH, POOL_WIDTH)),
                  _const_spec((1, POOL_WIDTH)), _const_spec((CONV_K, CONV_WIDTH)),
                  _const_spec((MIX_OUT, D_MODEL)), _const_spec((1, D_MODEL)),
                  _const_spec((D_MODEL, D_FF)), _const_spec((D_FF, D_MODEL))] + conv_in + roll_in,
        out_specs=[row(D_MODEL)] + conv_out_specs + roll_out,
        out_shape=[jax.ShapeDtypeStruct((rows, D_MODEL), F32)] + conv_out_shape + roll_shape,
        compiler_params=_params(),
        name="mlp_p",
    )(x, u, u, gb, z, z, ya, pw, ps, cw, w_out, g2, w_up, w_down, *conv_args, *roll_args)


def _rope_tables(pos):
    half = ROPE_DIM // 2
    inv = jnp.power(jnp.float32(ROPE_THETA), -jnp.arange(half, dtype=F32) / half)
    ang = pos.astype(F32)[:, None] * inv[None, :]
    cos, sin = jnp.cos(ang), jnp.sin(ang)
    n = pos.shape[0]
    rest = HEAD_DIM - ROPE_DIM
    zh = jnp.zeros((n, half), F32)
    c = jnp.concatenate([cos, cos, jnp.ones((n, rest), F32)], axis=1)
    a = jnp.concatenate([-sin, zh, jnp.zeros((n, rest), F32)], axis=1)
    b = jnp.concatenate([zh, sin, jnp.zeros((n, rest), F32)], axis=1)
    return tuple(jnp.tile(t, (1, GROUP_WIDTH // HEAD_DIM)) for t in (c, a, b))


def _to_buffer_layout(c):
    lead = c.shape[:-4]
    n = len(lead)
    t = jnp.transpose(c, (*range(n), n + 1, n + 2, n + 3, n))
    return t.reshape(*lead, 2 * GROUP_WIDTH, c.shape[-4])


def _from_buffer_layout(t):
    lead = t.shape[:-2]
    n = len(lead)
    c = t.reshape(*lead, 2, 2, HEAD_DIM, t.shape[-1])
    return jnp.transpose(c, (*range(n), n + 3, n, n + 1, n + 2))


def kernel(x_prompt, x_sample, state_pool, state_conv, cache_kv_w128, cache_kv_w512, cache_kv_w2048,
           norm1_g, w_in, q_norm_g, k_norm_g, pool_w, pool_scale, conv_w, w_out, norm2_g, w_up, w_down):
    batch, seq, _ = x_prompt.shape
    n_seq, t_new, _ = x_sample.shape
    depth = w_in.shape[0]
    assert depth == 2, "the in-place stacking of the kv state outputs is written for two layers"
    caches_t = [_to_buffer_layout(c) for c in (cache_kv_w128, cache_kv_w512, cache_kv_w2048)]

    rope_p = _rope_tables(jnp.arange(seq, dtype=jnp.int32))
    rope_s = _rope_tables(jnp.tile(PAST_LEN + jnp.arange(t_new, dtype=jnp.int32), n_seq))
    head_id = jnp.arange(LANES) // HEAD_DIM
    hm = jnp.where(head_id[:, None] == head_id[None, :], 1.0 / HEAD_DIM, 0.0).astype(BF16)
    two_heads = lambda gain: jnp.tile(gain, GROUP_WIDTH // HEAD_DIM)[None, :]

    xp = x_prompt.reshape(batch * seq, D_MODEL)
    xs = x_sample.reshape(n_seq * t_new, D_MODEL)
    outs = {k: [] for k in ("pool_p", "conv_p", "pool_s", "conv_s")}
    w_in_l = w_in[0].astype(BF16)
    new_caches = [None] * N_DIL
    kvts = None
    for layer in range(depth):
        pw = jax.scipy.linalg.block_diag(*[pool_w[layer, j] for j in range(len(POOL_WINDOWS))]).astype(BF16)
        ps = pool_scale[layer][None, :]
        cw = conv_w[layer]
        g1 = norm1_g[layer][None, :]
        g2 = norm2_g[layer][None, :]
        qg = two_heads(q_norm_g[layer]) * (HEAD_DIM ** -0.5)
        kg = two_heads(k_norm_g[layer])

        w_out_l = w_out[layer].astype(BF16)
        if layer == 0:
            (u, gb, z, qc0, qc1, qc2, kvc0, kvc1, kvc2, *tail) = _proj(
                xp, g1, w_in_l, qg, kg, rope_p, hm, tm=PROJ_TILE_ROLL, sub=PROJ_SUB_ROLL, batch=batch, seq=seq,
                roll=(caches_t[-1:], t_new))
            kvts, new_caches[-1] = tail[:N_DIL], tail[N_DIL]
            ya, w_up_l, w_down_l = _attn_p((qc0, qc1, qc2), (kvc0, kvc1, kvc2), batch=batch, seq=seq,
                                           convert=(0, (w_up, w_down)))
            xp, *tail = _mlp_p(xp, u, gb, z, ya.reshape(-1, GROUP_WIDTH), pw, ps, cw, w_out_l, g2, w_up_l, w_down_l,
                               tm=MLP_TILE, seq=seq, convert=(1, (w_in, w_up, w_down)), roll=(caches_t[:-1], t_new))
            w_next, new_caches[:-1] = tail[:3], tail[3:]
        else:
            (u, gb, z, qc0, qc1, qc2, kvc0, kvc1, kvc2, *kvts) = _proj(
                xp, g1, w_in_l, qg, kg, rope_p, hm, tm=PROJ_TILE, sub=PROJ_SUB, batch=batch, seq=seq,
                prev_kvt=kvts)
            (ya,) = _attn_p((qc0, qc1, qc2), (kvc0, kvc1, kvc2), batch=batch, seq=seq)
            (xp,), w_next = _mlp_p(xp, u, gb, z, ya.reshape(-1, GROUP_WIDTH), pw, ps, cw, w_out_l, g2, w_up_l,
                                   w_down_l, tm=MLP_TILE, seq=seq), None
        outs["pool_p"].append(u.reshape(batch, seq, POOL_WIDTH)[:, seq - POOL_HIST:])
        outs["conv_p"].append(z.reshape(batch, seq, CONV_WIDTH)[:, seq - (CONV_K - 1):])

        us, gbs, zs, q0, q1, q2, kn0, kn1, kn2 = _proj(xs, g1, w_in_l, qg, kg, rope_s, hm, tm=n_seq * t_new,
                                                       sub=n_seq * t_new)
        pst = jnp.pad(state_pool[layer], ((0, 0), (HIST_ROWS - POOL_HIST, 0), (0, 0)))
        cst = jnp.pad(state_conv[layer], ((0, 0), (SUBLANES - (CONV_K - 1), 0), (0, 0)))
        yps, yas, ycs, *new_caches = _mix_s(us, (q0, q1, q2), (kn0, kn1, kn2), gbs, zs, pst, cst, new_caches,
                                           pw, ps, cw, layer=layer, n_seq=n_seq, t_new=t_new)
        xs = _mlp_s(xs, yps.reshape(-1, POOL_WIDTH), yas.reshape(-1, GROUP_WIDTH), ycs.reshape(-1, CONV_WIDTH),
                    w_out_l, g2, w_up_l, w_down_l)
        us3 = us.reshape(n_seq, t_new, POOL_WIDTH)
        zs3 = zs.reshape(n_seq, t_new, CONV_WIDTH)
        outs["pool_s"].append(jnp.concatenate([state_pool[layer], us3], axis=1)[:, -POOL_HIST:])
        outs["conv_s"].append(jnp.concatenate([state_conv[layer], zs3], axis=1)[:, -(CONV_K - 1):])
        if w_next:
            w_in_l, w_up_l, w_down_l = w_next

    st = lambda k: jnp.stack(outs[k])
    return (xp.reshape(batch, seq, D_MODEL), xs.reshape(n_seq, t_new, D_MODEL),
            st("pool_p"), st("conv_p"),
            *[_from_buffer_layout(kvt) for kvt in kvts],
            st("pool_s"), st("conv_s"),
            *[_from_buffer_layout(c) for c in new_caches])
```

```python
import functools

import jax
import jax.numpy as jnp
from jax import lax
from jax.experimental import pallas as pl
from jax.experimental.pallas import tpu as pltpu

D_MODEL = 1024
HEAD_DIM = 64
POOL_WIDTH = 256
POOL_WINDOWS = (2, 4, 8, 16)
POOL_GROUP = 64
POOL_HIST = 15
ATTN_WIDTH = 384
DIL_PAIRS = ((128, 1), (512, 4), (2048, 16))
N_DIL = 3
GROUP_WIDTH = 128
CONV_WIDTH = 384
CONV_K = 3
ROPE_DIM = 16
ROPE_THETA = 500000.0
D_FF = 4096
IN_COLS = 2560
MIX_OUT = 768
EPS = 1e-6
NEG_INF = -1e30
PAST_LEN = 8192
N_BACK = 128

LANES = 128
SUBLANES = 8
HIST_ROWS = 16
ATTN_TILE = 128
ATTN_UNROLL = 16
ATTN_MERGE_ROWS = 256
PROJ_TILE = 1024
PROJ_SUB = 512
PROJ_TILE_ROLL = 512
PROJ_SUB_ROLL = 256
MLP_TILE = 512
MLP_FF_CHUNK = 512
MIX_S_SEQS = 4
VMEM_LIMIT = 56 * 1024 * 1024

F32 = jnp.float32
BF16 = jnp.bfloat16

_Q_OFF = POOL_WIDTH
_GB_OFF = _Q_OFF + 3 * ATTN_WIDTH

_NT = (((1,), (1,)), ((), ()))


def _const_spec(shape):
    return pl.BlockSpec(shape, lambda *_: (0,) * len(shape), pipeline_mode=pl.Buffered(1))


def _params():
    return pltpu.CompilerParams(dimension_semantics=("arbitrary",), vmem_limit_bytes=VMEM_LIMIT)


def _cast_specs(convert, n_steps):
    in_specs, out_specs, out_shape, args = [], [], [], []
    if convert is not None:
        layer, weights = convert
        for w in weights:
            _, k, n = w.shape
            in_specs.append(pl.BlockSpec((None, k // n_steps, n), lambda i: (layer, i, 0)))
            out_specs.append(pl.BlockSpec((k // n_steps, n), lambda i: (i, 0)))
            out_shape.append(jax.ShapeDtypeStruct((k, n), BF16))
            args.append(w)
    return in_specs, out_specs, out_shape, args


def _cast_slabs(srcs, dsts):
    for src, dst in zip(srcs, dsts):
        dst[...] = src[...].astype(BF16)


def _roll_specs(jobs, n_steps, n_operands):
    in_specs, out_specs, out_shape, args, how = [], [], [], [], []
    for cache_t, layers, into in jobs:
        depth, n_seq, width, window = cache_t.shape
        assert depth == 2 and n_seq == n_steps, "one sequence's buffers per grid step"
        in_specs += [pl.BlockSpec((1, 1, width, window), lambda i, l=l: (l, i, 0, 0)) for l in layers]
        args += [cache_t] * len(layers)
        out_shape.append(jax.ShapeDtypeStruct(cache_t.shape, F32))
        how.append((tuple(layers), into is None))
        if into is None:
            out_specs.append(pl.BlockSpec((depth, 1, width, window), lambda i: (0, i, 0, 0)))
        else:
            (layer,) = layers
            out_specs.append(pl.BlockSpec((1, 1, width, window), lambda i, l=layer: (l, i, 0, 0)))
    aliases = {}
    for k, (_, _, into) in enumerate(jobs):
        if into is not None:
            aliases[n_operands + len(args)] = k
            in_specs.append(pl.BlockSpec(memory_space=pl.ANY))
            args.append(into)
    return in_specs, out_specs, out_shape, args, aliases, tuple(how)


def _roll_buffers(srcs, dsts, how, by):
    srcs = iter(srcs)
    for dst, (layers, whole) in zip(dsts, how):
        if whole:
            for layer in range(dst.shape[0]):
                if layer not in layers:
                    dst[layer, 0] = jnp.zeros(dst.shape[2:], F32)
        for layer in layers:
            src = next(srcs)
            dst[layer if whole else 0, 0] = pltpu.roll(src[0, 0], src.shape[-1] - by, 1)


def _proj_kernel(x_ref, g1_ref, w_ref, qg_ref, kg_ref, rc_ref, ra_ref, rb_ref, hm_ref, *rest,
                 class_major, tiles, sub, stacked, n_cast, roll_how, roll_by):
    prev_kvt = None
    if class_major:
        if stacked:
            prev_kvt, rest = rest[:N_DIL], rest[N_DIL:]
        cast_src, rest = rest[:n_cast], rest[n_cast:]
        n_src = sum(len(layers) for layers, _ in roll_how)
        roll_src, rest = rest[:n_src], rest[n_src:]
        (u_ref, gb_ref, z_ref, qc0, qc1, qc2, kvc0, kvc1, kvc2, kvt0, kvt1, kvt2) = rest[:12]
        _cast_slabs(cast_src, rest[12:12 + n_cast])
        _roll_buffers(roll_src, rest[12 + n_cast:12 + n_cast + len(roll_how)], roll_how, roll_by)
        nat = rest[-1]
    else:
        (u_ref, gb_ref, z_ref, q0, q1, q2, kv0, kv1, kv2) = rest
    tm = x_ref.shape[0]
    hm = hm_ref[...]

    for s in range(tm // sub):
        r0 = s * sub
        rows = slice(r0, r0 + sub)
        x = x_ref[rows, :]
        ms = jnp.mean(x * x, axis=-1, keepdims=True)
        hb = (x * lax.rsqrt(ms + EPS) * g1_ref[...]).astype(BF16)
        p_att = jnp.dot(hb, w_ref[:, _Q_OFF:_GB_OFF], preferred_element_type=F32)
        rc, ra, rb = rc_ref[rows, :], ra_ref[rows, :], rb_ref[rows, :]

        def head_norm_rope(xb, gain):
            msq = jnp.dot((xb * xb).astype(BF16), hm, preferred_element_type=F32)
            xn = xb * lax.rsqrt(msq + EPS) * gain
            return xn * rc + pltpu.roll(xn, LANES - ROPE_DIM // 2, 1) * ra + pltpu.roll(xn, ROPE_DIM // 2, 1) * rb

        for g, (window, dil) in enumerate(DIL_PAIRS):
            lanes = slice(g * GROUP_WIDTH, (g + 1) * GROUP_WIDTH)
            qn = head_norm_rope(p_att[:, :ATTN_WIDTH][:, lanes], qg_ref[...])
            kn = head_norm_rope(p_att[:, ATTN_WIDTH:2 * ATTN_WIDTH][:, lanes], kg_ref[...])
            vv = p_att[:, 2 * ATTN_WIDTH:][:, lanes]
            if not class_major:
                q_ref, kv_ref = ((q0, kv0), (q1, kv1), (q2, kv2))[g]
                q_ref[rows, :] = qn
                kv_ref[rows, :GROUP_WIDTH] = kn
                kv_ref[rows, GROUP_WIDTH:] = vv
                continue
            qc_ref, kvc_ref, kvt_ref = ((qc0, kvc0, kvt0), (qc1, kvc1, kvt1), (qc2, kvc2, kvt2))[g]
            slab = 3 * (s * N_DIL + g)
            nat[slab] = qn
            nat[slab + 1] = kn
            nat[slab + 2] = vv
            per = sub // dil
            dst = slice(s * per, (s + 1) * per)
            for r in range(dil):
                src = pl.ds(r, per, stride=dil) if dil > 1 else slice(None)
                qc_ref[0, r, dst, :] = nat[slab, src, :].astype(BF16)
                kvc_ref[0, r, dst, :GROUP_WIDTH] = nat[slab + 1, src, :].astype(BF16)
                kvc_ref[0, r, dst, GROUP_WIDTH:] = nat[slab + 2, src, :].astype(BF16)
            def put_kvt(cols, part, g=g, kvt_ref=kvt_ref):
                if stacked:
                    kvt_ref[0, 0, :, cols] = prev_kvt[g][0, :, cols]
                    kvt_ref[1, 0, :, cols] = part
                else:
                    kvt_ref[0, :, cols] = part

            if window >= tiles * tm:
                put_kvt(rows, jnp.concatenate([kn, vv], axis=1).T)
            else:
                first_kept = tm - min(window, tm)
                lo = max(r0, first_kept)
                if lo < r0 + sub:
                    part = jnp.concatenate([kn[lo - r0:], vv[lo - r0:]], axis=1).T
                    put_kvt(slice(lo - first_kept, r0 + sub - first_kept), part)

        u_ref[rows, :] = jnp.dot(hb, w_ref[:, :POOL_WIDTH], preferred_element_type=F32)
        p_gate = jnp.dot(hb, w_ref[:, _GB_OFF:], preferred_element_type=F32)
        gb_ref[rows, :] = p_gate[:, :CONV_WIDTH]
        z_ref[rows, :] = p_gate[:, CONV_WIDTH:2 * CONV_WIDTH] * p_gate[:, 2 * CONV_WIDTH:]


def _proj(x, g1, w_in, qg, kg, rope, hm, *, tm, sub, batch=None, seq=None, prev_kvt=None, convert=None,
          roll=(), roll_by=0):
    rows = x.shape[0]
    stacked = prev_kvt is not None
    n_tab = rope[0].shape[0] // tm
    class_major = batch is not None
    tiles = seq // tm if class_major else 1
    row = lambda c: pl.BlockSpec((tm, c), lambda i: (i, 0))
    tab = pl.BlockSpec((tm, LANES), lambda i: (i % n_tab, 0))
    out_shape = [jax.ShapeDtypeStruct((rows, POOL_WIDTH), F32)] + [jax.ShapeDtypeStruct((rows, CONV_WIDTH), F32)] * 2
    out_specs = [row(POOL_WIDTH), row(CONV_WIDTH), row(CONV_WIDTH)]
    scratch = []
    prev_specs = []
    cast_in, cast_out, cast_shape, cast_args = _cast_specs(convert, rows // tm)
    assert class_major or (convert is None and not roll)
    roll_in, roll_out, roll_shape, roll_args, roll_alias, roll_how = _roll_specs(roll, rows // tm, 0)
    assert not roll_alias, "proj only creates rotated buffers"
    if class_major:
        for width in (GROUP_WIDTH, 2 * GROUP_WIDTH):
            for (_, dil) in DIL_PAIRS:
                out_shape.append(jax.ShapeDtypeStruct((batch, dil, seq // dil, width), BF16))
                out_specs.append(pl.BlockSpec((1, dil, tm // dil, width), lambda i: (i // tiles, 0, i % tiles, 0)))
        for (window, _) in DIL_PAIRS:
            assert window >= seq or window <= tm, "kv buffer rows must come from the last row tile"
            cols = tm if window >= seq else window
            col_blk = (lambda i: i % tiles) if window >= seq else (lambda i: 0)
            kvt_spec = pl.BlockSpec((1, 2 * GROUP_WIDTH, cols), lambda i, c=col_blk: (i // tiles, 0, c(i)))
            if stacked:
                prev_specs.append(kvt_spec)
                out_shape.append(jax.ShapeDtypeStruct((2, batch, 2 * GROUP_WIDTH, window), F32))
                out_specs.append(pl.BlockSpec((2, 1, 2 * GROUP_WIDTH, cols),
                                              lambda i, c=col_blk: (0, i // tiles, 0, c(i))))
            else:
                out_shape.append(jax.ShapeDtypeStruct((batch, 2 * GROUP_WIDTH, window), F32))
                out_specs.append(kvt_spec)
        scratch = [pltpu.VMEM((3 * N_DIL * (tm // sub), sub, GROUP_WIDTH), F32)]
    else:
        out_shape += [jax.ShapeDtypeStruct((rows, GROUP_WIDTH), F32)] * N_DIL
        out_shape += [jax.ShapeDtypeStruct((rows, 2 * GROUP_WIDTH), F32)] * N_DIL
        out_specs += [row(GROUP_WIDTH)] * N_DIL + [row(2 * GROUP_WIDTH)] * N_DIL
    return pl.pallas_call(
        functools.partial(_proj_kernel, class_major=class_major, tiles=tiles, sub=sub, stacked=stacked,
                          n_cast=len(cast_args), roll_how=roll_how, roll_by=roll_by),
        grid=(rows // tm,),
        in_specs=[row(D_MODEL), _const_spec((1, D_MODEL)), _const_spec((D_MODEL, IN_COLS)),
                  _const_spec((1, LANES)), _const_spec((1, LANES)), tab, tab, tab,
                  _const_spec((LANES, LANES))] + prev_specs + cast_in + roll_in,
        out_specs=out_specs + cast_out + roll_out,
        out_shape=out_shape + cast_shape + roll_shape,
        scratch_shapes=scratch,
        compiler_params=_params(),
        name="proj",
    )(x, g1, w_in, qg, kg, *rope, hm, *(prev_kvt or ()), *cast_args, *roll_args)


def _lane_lt(shape, bound):
    return lax.broadcasted_iota(jnp.int32, shape, len(shape) - 1) < bound


def _stack_heads(q):
    first = _lane_lt(q.shape, HEAD_DIM)
    zero = jnp.zeros_like(q)
    return jnp.concatenate([jnp.where(first, q, zero), jnp.where(first, zero, q)], axis=0)


def _unstack(o2, l2, m2):
    n = o2.shape[0] // 2
    first = _lane_lt((n, GROUP_WIDTH), HEAD_DIM)
    return tuple(jnp.where(first, a[:n], a[n:]) for a in (o2, l2, m2))


def _merge(a, b):
    m = jnp.maximum(a[2], b[2])
    wa = jnp.exp(a[2] - m)
    wb = jnp.exp(b[2] - m)
    return wa * a[0] + wb * b[0], wa * a[1] + wb * b[1], m


def _attn_unit(q, kv, bias):
    nk = kv.shape[0]
    k = kv[:, :GROUP_WIDTH]
    v1 = jnp.concatenate([kv[:, GROUP_WIDTH:], jnp.ones((nk, GROUP_WIDTH), BF16)], axis=1)
    s = lax.dot_general(_stack_heads(q), k, _NT, preferred_element_type=F32) + bias
    mrow = jnp.max(s, axis=-1, keepdims=True)
    p = jnp.exp(s - mrow).astype(BF16)
    r = jnp.dot(p, v1, preferred_element_type=F32)
    return _unstack(r[:, :GROUP_WIDTH], r[:, GROUP_WIDTH:], mrow)


def _attn_p_kernel(qc0, qc1, qc2, kvc0, kvc1, kvc2, *rest, seq, n_cast):
    cast_src, y_ref, rest = rest[:n_cast], rest[n_cast], rest[n_cast + 1:]
    _cast_slabs(cast_src, rest[:n_cast])
    scratch = rest[n_cast:]
    parts = [scratch[3 * g:3 * g + 3] for g in range(N_DIL)]
    bias_ref = scratch[3 * N_DIL]
    acc_o, acc_l, acc_m = parts[0]
    @pl.when(pl.program_id(0) == 0)
    def _():
        qi = lax.broadcasted_iota(jnp.int32, (2 * ATTN_TILE, 2 * ATTN_TILE), 0) & (ATTN_TILE - 1)
        ki = lax.broadcasted_iota(jnp.int32, (2 * ATTN_TILE, 2 * ATTN_TILE), 1)
        for first_key_back in (0, 1):
            dist = qi - ki + first_key_back * ATTN_TILE
            bias_ref[first_key_back] = jnp.where((dist >= 0) & (dist <= N_BACK), 0.0, NEG_INF)

    for g, ((_, dil), q_ref, kv_ref) in enumerate(zip(DIL_PAIRS, (qc0, qc1, qc2), (kvc0, kvc1, kvc2))):
        dst = parts[g]
        n_sub = seq // dil // ATTN_TILE

        def unit(idx, q_ref=q_ref, kv_ref=kv_ref, n_sub=n_sub, dil=dil):
            if n_sub == 1:
                r, sub = idx, 0
                res = _attn_unit(q_ref[0, r], kv_ref[0, r], bias_ref[0, :, :ATTN_TILE])
            else:
                r, sub = (idx // n_sub, idx % n_sub) if dil > 1 else (0, idx)
                back = jnp.minimum(sub, 1)
                q0 = pl.multiple_of(sub * ATTN_TILE, ATTN_TILE)
                k0 = pl.multiple_of((sub - back) * ATTN_TILE, ATTN_TILE)
                res = _attn_unit(q_ref[0, r, pl.ds(q0, ATTN_TILE), :], kv_ref[0, r, pl.ds(k0, 2 * ATTN_TILE), :],
                                 bias_ref[back])
            start = sub * (ATTN_TILE * dil) + r
            rows = pl.ds(start, ATTN_TILE, stride=dil) if dil > 1 else pl.ds(pl.multiple_of(start, ATTN_TILE), ATTN_TILE)
            return rows, res

        def units_body(it, carry, unit=unit, dst=dst):
            for j in range(ATTN_UNROLL):
                rows, res = unit(it * ATTN_UNROLL + j)
                for ref, val in zip(dst, res):
                    ref[rows, :] = val
            return carry
        n_iter = dil * n_sub // ATTN_UNROLL
        if n_iter == 1:
            units_body(0, 0)
        else:
            lax.fori_loop(0, n_iter, units_body, 0)

        if g > 0:
            for c in range(seq // ATTN_MERGE_ROWS):
                rows = slice(c * ATTN_MERGE_ROWS, (c + 1) * ATTN_MERGE_ROWS)
                o, l, m = _merge((acc_o[rows, :], acc_l[rows, :], acc_m[rows, :]),
                                 tuple(ref[rows, :] for ref in parts[g]))
                if g == N_DIL - 1:
                    y_ref[0, rows, :] = (o / l).astype(y_ref.dtype)
                else:
                    acc_o[rows, :] = o
                    acc_l[rows, :] = l
                    acc_m[rows, :] = m


def _attn_p(qcs, kvcs, *, batch, seq, convert=None):
    in_specs = [pl.BlockSpec((1,) + a.shape[1:], lambda b: (b, 0, 0, 0)) for a in (*qcs, *kvcs)]
    cast_in, cast_out, cast_shape, cast_args = _cast_specs(convert, batch)
    acc = pltpu.VMEM((seq, GROUP_WIDTH), F32)
    return pl.pallas_call(
        functools.partial(_attn_p_kernel, seq=seq, n_cast=len(cast_args)),
        grid=(batch,),
        in_specs=in_specs + cast_in,
        out_specs=[pl.BlockSpec((1, seq, GROUP_WIDTH), lambda b: (b, 0, 0))] + cast_out,
        out_shape=[jax.ShapeDtypeStruct((batch, seq, GROUP_WIDTH), BF16)] + cast_shape,
        scratch_shapes=[acc] * (3 * N_DIL) + [pltpu.VMEM((2, 2 * ATTN_TILE, 2 * ATTN_TILE), F32)],
        compiler_params=_params(),
        name="attn_p",
    )(*qcs, *kvcs, *cast_args)


def _pool_rows(ext, pos, pw, ps):
    return jnp.dot(_pool_diff(ext, pos), pw, preferred_element_type=F32) * ps


def _pool_diff(ext, pos):
    s2 = ext + pltpu.roll(ext, 1, 0)
    s4 = s2 + pltpu.roll(s2, 2, 0)
    s8 = s4 + pltpu.roll(s4, 4, 0)
    s16 = s8 + pltpu.roll(s8, 8, 0)
    lane = lax.broadcasted_iota(jnp.int32, (1, POOL_WIDTH), 1)
    grp = [lane < (j + 1) * POOL_GROUP for j in range(3)]
    win = jnp.where(grp[0], s2, jnp.where(grp[1], s4, jnp.where(grp[2], s8, s16)))[HIST_ROWS:]
    width = jnp.where(grp[0], POOL_WINDOWS[0], jnp.where(grp[1], POOL_WINDOWS[1],
                      jnp.where(grp[2], POOL_WINDOWS[2], POOL_WINDOWS[3])))
    cnt = jnp.minimum(pos + 1, width).astype(F32)
    return (win / cnt - ext[HIST_ROWS:]).astype(BF16)


def _conv_rows(ext, gb, cw, hist):
    y = cw[0:1] * pltpu.roll(ext, 2, 0) + cw[1:2] * pltpu.roll(ext, 1, 0) + cw[2:3] * ext
    return gb * y[hist:]


def _attn_s_group(q, new_t, cache_t, window, dil):
    t = q.shape[0]
    q2 = _stack_heads(q).astype(BF16)
    s_c = jnp.dot(q2, cache_t[:GROUP_WIDTH].astype(BF16), preferred_element_type=F32)
    s_n = jnp.dot(q2, new_t[:GROUP_WIDTH].astype(BF16), preferred_element_type=F32)
    tq = lax.broadcasted_iota(jnp.int32, (2 * t, 1), 0) & (t - 1)
    rc = (lax.broadcasted_iota(jnp.int32, (1, window), 1) + t) & (window - 1)
    dist_c = window + tq - rc
    ok_c = (dist_c <= window) & ((dist_c & (dil - 1)) == 0) & (PAST_LEN - window + rc >= 0)
    tn = lax.broadcasted_iota(jnp.int32, (1, LANES), 1) - (LANES - t)
    dist_n = tq - tn
    ok_n = (tn >= 0) & (dist_n >= 0) & ((dist_n & (dil - 1)) == 0)
    s_c = jnp.where(ok_c, s_c, NEG_INF)
    s_n = jnp.where(ok_n, s_n, NEG_INF)
    mrow = jnp.maximum(jnp.max(s_c, axis=-1, keepdims=True), jnp.max(s_n, axis=-1, keepdims=True))
    p_c = jnp.exp(s_c - mrow)
    p_n = jnp.exp(s_n - mrow)
    o2 = (lax.dot_general(p_c.astype(BF16), cache_t[GROUP_WIDTH:].astype(BF16), _NT, preferred_element_type=F32)
          + lax.dot_general(p_n.astype(BF16), new_t[GROUP_WIDTH:].astype(BF16), _NT, preferred_element_type=F32))
    l2 = jnp.sum(p_c, axis=-1, keepdims=True) + jnp.sum(p_n, axis=-1, keepdims=True)
    return _unstack(o2, l2, mrow)


def _mix_s_kernel(u_ref, q0_ref, q1_ref, q2_ref, kn0_ref, kn1_ref, kn2_ref, gb_ref, z_ref, pst_ref, cst_ref,
                  pw_ref, ps_ref, cw_ref, c0_ref, c1_ref, c2_ref, yp_ref, ya_ref, yc_ref, nc0_ref, nc1_ref, nc2_ref,
                  *, t_new, n_per):
    keep_old = _lane_lt((2 * GROUP_WIDTH, LANES), LANES - t_new)
    pos = PAST_LEN + lax.broadcasted_iota(jnp.int32, (t_new, 1), 0)
    for i in range(n_per):
        state = None
        for (window, dil), q_ref, kn_ref, c_ref, nc_ref in zip(
                DIL_PAIRS, (q0_ref, q1_ref, q2_ref), (kn0_ref, kn1_ref, kn2_ref),
                (c0_ref, c1_ref, c2_ref), (nc0_ref, nc1_ref, nc2_ref)):
            kv_new = kn_ref[i]
            new_t = jnp.concatenate([jnp.zeros((LANES - t_new, 2 * GROUP_WIDTH), F32), kv_new], axis=0).T
            cache_t = c_ref[0, i]
            part = _attn_s_group(q_ref[i], new_t, cache_t, window, dil)
            state = part if state is None else _merge(state, part)
            nc_ref[0, i] = jnp.where(keep_old, cache_t[:, window - LANES:], new_t)
        ya_ref[i] = state[0] / state[1]

        u_ext = jnp.concatenate([pst_ref[i], u_ref[i]], axis=0)
        yp_ref[i] = _pool_rows(u_ext, pos, pw_ref[...], ps_ref[...])
        z_ext = jnp.concatenate([cst_ref[i], z_ref[i]], axis=0)
        yc_ref[i] = _conv_rows(z_ext, gb_ref[i], cw_ref[...], cst_ref.shape[1])


def _mix_s(u, qs, kns, gb, z, pst, cst, rotated_bufs, pw, ps, cw, *, layer, n_seq, t_new):
    n_per = MIX_S_SEQS
    blk = lambda r, c: pl.BlockSpec((n_per, r, c), lambda b: (b, 0, 0))
    three = lambda a: a.reshape(n_seq, -1, a.shape[-1])
    windows = [w for (w, _) in DIL_PAIRS]
    in_specs = ([blk(t_new, POOL_WIDTH)] + [blk(t_new, GROUP_WIDTH)] * N_DIL + [blk(t_new, 2 * GROUP_WIDTH)] * N_DIL
                + [blk(t_new, CONV_WIDTH)] * 2 + [blk(pst.shape[1], POOL_WIDTH), blk(cst.shape[1], CONV_WIDTH)]
                + [_const_spec((POOL_WIDTH, POOL_WIDTH)), _const_spec((1, POOL_WIDTH)),
                   _const_spec((CONV_K, CONV_WIDTH))]
                + [pl.BlockSpec((1, n_per, 2 * GROUP_WIDTH, w), lambda b: (layer, b, 0, 0)) for w in windows])
    args = [three(u), *[three(q) for q in qs], *[three(k) for k in kns], three(gb), three(z), pst, cst, pw, ps, cw,
            *rotated_bufs]
    n_in = len(args)
    out_specs = ([blk(t_new, POOL_WIDTH), blk(t_new, GROUP_WIDTH), blk(t_new, CONV_WIDTH)]
                 + [pl.BlockSpec((1, n_per, 2 * GROUP_WIDTH, LANES), lambda b, w=w: (layer, b, 0, w // LANES - 1))
                    for w in windows])
    out_shape = ([jax.ShapeDtypeStruct((n_seq, t_new, c), F32) for c in (POOL_WIDTH, GROUP_WIDTH, CONV_WIDTH)]
                 + [jax.ShapeDtypeStruct(c.shape, F32) for c in rotated_bufs])
    return pl.pallas_call(
        functools.partial(_mix_s_kernel, t_new=t_new, n_per=n_per),
        grid=(n_seq // n_per,),
        in_specs=in_specs,
        out_specs=out_specs,
        out_shape=out_shape,
        input_output_aliases={n_in - N_DIL + g: 3 + g for g in range(N_DIL)},
        compiler_params=_params(),
        name="mix_s",
    )(*args)


def _mlp_head(x, yp, ya, yc, wo_ref, g2_ref):
    mixed = jnp.concatenate([yp.astype(BF16), ya.astype(BF16), yc.astype(BF16)], axis=1)
    x1 = x + jnp.dot(mixed, wo_ref[...], preferred_element_type=F32)
    ms = jnp.mean(x1 * x1, axis=-1, keepdims=True)
    return x1, (x1 * lax.rsqrt(ms + EPS) * g2_ref[...]).astype(BF16)


def _mlp_tail(x1, hb, wu_ref, wd_ref, tf):
    acc = x1
    for c in range(D_FF // tf):
        hf = jnp.dot(hb, wu_ref[:, c * tf:(c + 1) * tf], preferred_element_type=F32)
        act = jnp.square(jnp.maximum(hf, 0.0)).astype(BF16)
        acc = acc + jnp.dot(act, wd_ref[c * tf:(c + 1) * tf, :], preferred_element_type=F32)
    return acc


def _mlp_s_kernel(x_ref, yp_ref, ya_ref, yc_ref, wo_ref, g2_ref, wu_ref, wd_ref, o_ref, hb_buf):
    @pl.when(pl.program_id(0) == 0)
    def _():
        x1, hb = _mlp_head(x_ref[...], yp_ref[...], ya_ref[...], yc_ref[...], wo_ref, g2_ref)
        o_ref[...] = x1
        hb_buf[...] = hb

    hf = jnp.dot(hb_buf[...], wu_ref[...], preferred_element_type=F32)
    act = jnp.square(jnp.maximum(hf, 0.0)).astype(BF16)
    o_ref[...] += jnp.dot(act, wd_ref[...], preferred_element_type=F32)


def _mlp_p_kernel(x_ref, u_ref, uh_ref, gb_ref, z_ref, zh_ref, ya_ref, pw_ref, ps_ref, cw_ref,
                  wo_ref, g2_ref, wu_ref, wd_ref, *rest, tf, tiles, n_cast, roll_how, roll_by):
    n_src = sum(len(layers) for layers, _ in roll_how)
    n_alias = sum(not whole for _, whole in roll_how)
    cast_src, roll_src = rest[:n_cast], rest[n_cast:n_cast + n_src]
    o_ref, outs = rest[n_cast + n_src + n_alias], rest[n_cast + n_src + n_alias + 1:]
    _cast_slabs(cast_src, outs[:n_cast])
    _roll_buffers(roll_src, outs[n_cast:], roll_how, roll_by)
    tm = x_ref.shape[0]
    tile = pl.program_id(0) % tiles
    has_hist = tile > 0
    pos = tile * tm + lax.broadcasted_iota(jnp.int32, (tm, 1), 0)
    yp = _pool_rows(jnp.concatenate([jnp.where(has_hist, uh_ref[...], 0.0), u_ref[...]], axis=0), pos,
                    pw_ref[...], ps_ref[...])
    yc = _conv_rows(jnp.concatenate([jnp.where(has_hist, zh_ref[...], 0.0), z_ref[...]], axis=0),
                    gb_ref[...], cw_ref[...], HIST_ROWS)
    x1, hb = _mlp_head(x_ref[...], yp, ya_ref[...], yc, wo_ref, g2_ref)
    o_ref[...] = _mlp_tail(x1, hb, wu_ref, wd_ref, tf)


def _mlp_s(x, yp, ya, yc, w_out, g2, w_up, w_down):
    rows = x.shape[0]
    tf = MLP_FF_CHUNK
    full = lambda c: pl.BlockSpec((rows, c), lambda j: (0, 0))
    return pl.pallas_call(
        _mlp_s_kernel,
        grid=(D_FF // tf,),
        in_specs=[full(D_MODEL), full(POOL_WIDTH), full(GROUP_WIDTH), full(CONV_WIDTH),
                  _const_spec((MIX_OUT, D_MODEL)), _const_spec((1, D_MODEL)),
                  pl.BlockSpec((D_MODEL, tf), lambda j: (0, j)),
                  pl.BlockSpec((tf, D_MODEL), lambda j: (j, 0))],
        out_specs=full(D_MODEL),
        out_shape=jax.ShapeDtypeStruct((rows, D_MODEL), F32),
        scratch_shapes=[pltpu.VMEM((rows, D_MODEL), BF16)],
        compiler_params=_params(),
        name="mlp_s",
    )(x, yp, ya, yc, w_out, g2, w_up, w_down)


def _mlp_p(x, u, gb, z, ya, pw, ps, cw, w_out, g2, w_up, w_down, *, tm, seq, convert=None, roll=(), roll_by=0):
    rows = x.shape[0]
    row = lambda c: pl.BlockSpec((tm, c), lambda i: (i, 0))
    conv_in, conv_out_specs, conv_out_shape, conv_args = _cast_specs(convert, rows // tm)
    n_main = 14
    roll_in, roll_out, roll_shape, roll_args, roll_alias, roll_how = _roll_specs(roll, rows // tm,
                                                                                 n_main + len(conv_args))
    hist = lambda c: pl.BlockSpec((HIST_ROWS, c), lambda i: (jnp.maximum(i * (tm // HIST_ROWS) - 1, 0), 0))
    return pl.pallas_call(
        functools.partial(_mlp_p_kernel, tf=MLP_FF_CHUNK, tiles=seq // tm, n_cast=len(conv_args),
                          roll_how=roll_how, roll_by=roll_by),
        grid=(rows // tm,),
        in_specs=[row(D_MODEL), row(POOL_WIDTH), hist(POOL_WIDTH), row(CONV_WIDTH), row(CONV_WIDTH),
                  hist(CONV_WIDTH), row(GROUP_WIDTH), _const_spec((POOL_WIDTH, POOL_WIDTH)),
                  _const_spec((1, POOL_WIDTH)), _const_spec((CONV_K, CONV_WIDTH)),
                  _const_spec((MIX_OUT, D_MODEL)), _const_spec((1, D_MODEL)),
                  _const_spec((D_MODEL, D_FF)), _const_spec((D_FF, D_MODEL))] + conv_in + roll_in,
        out_specs=[row(D_MODEL)] + conv_out_specs + roll_out,
        out_shape=[jax.ShapeDtypeStruct((rows, D_MODEL), F32)] + conv_out_shape + roll_shape,
        input_output_aliases={i: 1 + len(conv_args) + k for i, k in roll_alias.items()},
        compiler_params=_params(),
        name="mlp_p",
    )(x, u, u, gb, z, z, ya, pw, ps, cw, w_out, g2, w_up, w_down, *conv_args, *roll_args)


def _rope_tables(pos):
    half = ROPE_DIM // 2
    inv = jnp.power(jnp.float32(ROPE_THETA), -jnp.arange(half, dtype=F32) / half)
    ang = pos.astype(F32)[:, None] * inv[None, :]
    cos, sin = jnp.cos(ang), jnp.sin(ang)
    n = pos.shape[0]
    rest = HEAD_DIM - ROPE_DIM
    zh = jnp.zeros((n, half), F32)
    c = jnp.concatenate([cos, cos, jnp.ones((n, rest), F32)], axis=1)
    a = jnp.concatenate([-sin, zh, jnp.zeros((n, rest), F32)], axis=1)
    b = jnp.concatenate([zh, sin, jnp.zeros((n, rest), F32)], axis=1)
    return tuple(jnp.tile(t, (1, GROUP_WIDTH // HEAD_DIM)) for t in (c, a, b))


def _to_buffer_layout(c):
    lead = c.shape[:-4]
    n = len(lead)
    t = jnp.transpose(c, (*range(n), n + 1, n + 2, n + 3, n))
    return t.reshape(*lead, 2 * GROUP_WIDTH, c.shape[-4])


def _from_buffer_layout(t):
    lead = t.shape[:-2]
    n = len(lead)
    c = t.reshape(*lead, 2, 2, HEAD_DIM, t.shape[-1])
    return jnp.transpose(c, (*range(n), n + 3, n, n + 1, n + 2))


def kernel(x_prompt, x_sample, state_pool, state_conv, cache_kv_w128, cache_kv_w512, cache_kv_w2048,
           norm1_g, w_in, q_norm_g, k_norm_g, pool_w, pool_scale, conv_w, w_out, norm2_g, w_up, w_down):
    batch, seq, _ = x_prompt.shape
    n_seq, t_new, _ = x_sample.shape
    depth = w_in.shape[0]
    assert depth == 2, "the in-place stacking of the kv state outputs is written for two layers"
    caches_t = [_to_buffer_layout(c) for c in (cache_kv_w128, cache_kv_w512, cache_kv_w2048)]

    rope_p = _rope_tables(jnp.arange(seq, dtype=jnp.int32))
    rope_s = _rope_tables(jnp.tile(PAST_LEN + jnp.arange(t_new, dtype=jnp.int32), n_seq))
    head_id = jnp.arange(LANES) // HEAD_DIM
    hm = jnp.where(head_id[:, None] == head_id[None, :], 1.0 / HEAD_DIM, 0.0).astype(BF16)
    two_heads = lambda gain: jnp.tile(gain, GROUP_WIDTH // HEAD_DIM)[None, :]

    xp = x_prompt.reshape(batch * seq, D_MODEL)
    xs = x_sample.reshape(n_seq * t_new, D_MODEL)
    outs = {k: [] for k in ("pool_p", "conv_p", "pool_s", "conv_s")}
    w_in_l = w_in[0].astype(BF16)
    new_caches = [None] * N_DIL
    kvts = None
    for layer in range(depth):
        pw = jax.scipy.linalg.block_diag(*[pool_w[layer, j] for j in range(len(POOL_WINDOWS))]).astype(BF16)
        ps = pool_scale[layer][None, :]
        cw = conv_w[layer]
        g1 = norm1_g[layer][None, :]
        g2 = norm2_g[layer][None, :]
        qg = two_heads(q_norm_g[layer]) * (HEAD_DIM ** -0.5)
        kg = two_heads(k_norm_g[layer])

        w_out_l = w_out[layer].astype(BF16)
        if layer == 0:
            (u, gb, z, qc0, qc1, qc2, kvc0, kvc1, kvc2, *tail) = _proj(
                xp, g1, w_in_l, qg, kg, rope_p, hm, tm=PROJ_TILE_ROLL, sub=PROJ_SUB_ROLL, batch=batch, seq=seq,
                roll=[(caches_t[-1], (0,), None)], roll_by=t_new)
            kvts, big_half = tail[:N_DIL], tail[N_DIL]
            ya, w_up_l, w_down_l = _attn_p((qc0, qc1, qc2), (kvc0, kvc1, kvc2), batch=batch, seq=seq,
                                           convert=(0, (w_up, w_down)))
            jobs = [(c, (0, 1), None) for c in caches_t[:-1]] + [(caches_t[-1], (1,), big_half)]
            xp, *tail = _mlp_p(xp, u, gb, z, ya.reshape(-1, GROUP_WIDTH), pw, ps, cw, w_out_l, g2, w_up_l, w_down_l,
                               tm=MLP_TILE, seq=seq, convert=(1, (w_in, w_up, w_down)), roll=jobs, roll_by=t_new)
            w_next, new_caches = tail[:3], tail[3:]
        else:
            (u, gb, z, qc0, qc1, qc2, kvc0, kvc1, kvc2, *kvts) = _proj(
                xp, g1, w_in_l, qg, kg, rope_p, hm, tm=PROJ_TILE, sub=PROJ_SUB, batch=batch, seq=seq,
                prev_kvt=kvts)
            (ya,) = _attn_p((qc0, qc1, qc2), (kvc0, kvc1, kvc2), batch=batch, seq=seq)
            (xp,), w_next = _mlp_p(xp, u, gb, z, ya.reshape(-1, GROUP_WIDTH), pw, ps, cw, w_out_l, g2, w_up_l,
                                   w_down_l, tm=MLP_TILE, seq=seq), None
        outs["pool_p"].append(u.reshape(batch, seq, POOL_WIDTH)[:, seq - POOL_HIST:])
        outs["conv_p"].append(z.reshape(batch, seq, CONV_WIDTH)[:, seq - (CONV_K - 1):])

        us, gbs, zs, q0, q1, q2, kn0, kn1, kn2 = _proj(xs, g1, w_in_l, qg, kg, rope_s, hm, tm=n_seq * t_new,
                                                       sub=n_seq * t_new)
        pst = jnp.pad(state_pool[layer], ((0, 0), (HIST_ROWS - POOL_HIST, 0), (0, 0)))
        cst = jnp.pad(state_conv[layer], ((0, 0), (SUBLANES - (CONV_K - 1), 0), (0, 0)))
        yps, yas, ycs, *new_caches = _mix_s(us, (q0, q1, q2), (kn0, kn1, kn2), gbs, zs, pst, cst, new_caches,
                                           pw, ps, cw, layer=layer, n_seq=n_seq, t_new=t_new)
        xs = _mlp_s(xs, yps.reshape(-1, POOL_WIDTH), yas.reshape(-1, GROUP_WIDTH), ycs.reshape(-1, CONV_WIDTH),
                    w_out_l, g2, w_up_l, w_down_l)
        us3 = us.reshape(n_seq, t_new, POOL_WIDTH)
        zs3 = zs.reshape(n_seq, t_new, CONV_WIDTH)
        outs["pool_s"].append(jnp.concatenate([state_pool[layer], us3], axis=1)[:, -POOL_HIST:])
        outs["conv_s"].append(jnp.concatenate([state_conv[layer], zs3], axis=1)[:, -(CONV_K - 1):])
        if w_next:
            w_in_l, w_up_l, w_down_l = w_next

    st = lambda k: jnp.stack(outs[k])
    return (xp.reshape(batch, seq, D_MODEL), xs.reshape(n_seq, t_new, D_MODEL),
            st("pool_p"), st("conv_p"),
            *[_from_buffer_layout(kvt) for kvt in kvts],
            st("pool_s"), st("conv_s"),
            *[_from_buffer_layout(c) for c in new_caches])
```

```python
import functools

import jax
import jax.numpy as jnp
from jax import lax
from jax.experimental import pallas as pl
from jax.experimental.pallas import tpu as pltpu

D_MODEL = 1024
HEAD_DIM = 64
POOL_WIDTH = 256
POOL_WINDOWS = (2, 4, 8, 16)
POOL_GROUP = 64
POOL_HIST = 15
ATTN_WIDTH = 384
DIL_PAIRS = ((128, 1), (512, 4), (2048, 16))
N_DIL = 3
GROUP_WIDTH = 128
CONV_WIDTH = 384
CONV_K = 3
ROPE_DIM = 16
ROPE_THETA = 500000.0
D_FF = 4096
IN_COLS = 2560
MIX_OUT = 768
EPS = 1e-6
NEG_INF = -1e30
PAST_LEN = 8192
N_BACK = 128

LANES = 128
SUBLANES = 8
HIST_ROWS = 16
ATTN_TILE = 128
ATTN_UNROLL = 16
ATTN_MERGE_ROWS = 256
PROJ_TILE = 1024
PROJ_SUB = 512
MLP_TILE = 512
MLP_FF_CHUNK = 512
MIX_S_SEQS = 4
VMEM_LIMIT = 56 * 1024 * 1024

F32 = jnp.float32
BF16 = jnp.bfloat16

_Q_OFF = POOL_WIDTH
_GB_OFF = _Q_OFF + 3 * ATTN_WIDTH

_NT = (((1,), (1,)), ((), ()))


def _const_spec(shape):
    return pl.BlockSpec(shape, lambda *_: (0,) * len(shape), pipeline_mode=pl.Buffered(1))


def _params():
    return pltpu.CompilerParams(dimension_semantics=("arbitrary",), vmem_limit_bytes=VMEM_LIMIT)


def _cast_specs(convert, n_steps):
    in_specs, out_specs, out_shape, args = [], [], [], []
    if convert is not None:
        layer, weights = convert
        for w in weights:
            _, k, n = w.shape
            in_specs.append(pl.BlockSpec((None, k // n_steps, n), lambda i: (layer, i, 0)))
            out_specs.append(pl.BlockSpec((k // n_steps, n), lambda i: (i, 0)))
            out_shape.append(jax.ShapeDtypeStruct((k, n), BF16))
            args.append(w)
    return in_specs, out_specs, out_shape, args


def _cast_slabs(srcs, dsts):
    for src, dst in zip(srcs, dsts):
        dst[...] = src[...].astype(BF16)


def _roll_specs(buffers, n_steps):
    in_specs, out_specs, out_shape, args = [], [], [], []
    for cache_t in buffers:
        depth, n_seq, width, window = cache_t.shape
        assert n_seq % n_steps == 0
        per = n_seq // n_steps
        in_specs += [pl.BlockSpec((1, per, width, window), lambda i, l=l: (l, i, 0, 0)) for l in range(depth)]
        args += [cache_t] * depth
        out_specs.append(pl.BlockSpec((depth, per, width, window), lambda i: (0, i, 0, 0)))
        out_shape.append(jax.ShapeDtypeStruct(cache_t.shape, F32))
    return in_specs, out_specs, out_shape, args


def _roll_buffers(srcs, dsts, by):
    srcs = iter(srcs)
    for dst in dsts:
        for layer in range(dst.shape[0]):
            src = next(srcs)
            for i in range(src.shape[1]):
                dst[layer, i] = pltpu.roll(src[0, i], src.shape[-1] - by, 1)


def _proj_kernel(x_ref, g1_ref, w_ref, qg_ref, kg_ref, rc_ref, ra_ref, rb_ref, hm_ref, *rest,
                 class_major, tiles, sub, stacked, n_cast, n_roll, roll_by):
    prev_kvt = None
    if class_major:
        if stacked:
            prev_kvt, rest = rest[:N_DIL], rest[N_DIL:]
        cast_src, rest = rest[:n_cast], rest[n_cast:]
        roll_src, rest = rest[:2 * n_roll], rest[2 * n_roll:]
        (u_ref, gb_ref, z_ref, qc0, qc1, qc2, kvc0, kvc1, kvc2, kvt0, kvt1, kvt2) = rest[:12]
        _cast_slabs(cast_src, rest[12:12 + n_cast])
        _roll_buffers(roll_src, rest[12 + n_cast:12 + n_cast + n_roll], roll_by)
        nat = rest[-1]
    else:
        (u_ref, gb_ref, z_ref, q0, q1, q2, kv0, kv1, kv2) = rest
    tm = x_ref.shape[0]
    hm = hm_ref[...]

    for s in range(tm // sub):
        r0 = s * sub
        rows = slice(r0, r0 + sub)
        x = x_ref[rows, :]
        ms = jnp.mean(x * x, axis=-1, keepdims=True)
        hb = (x * lax.rsqrt(ms + EPS) * g1_ref[...]).astype(BF16)
        p_att = jnp.dot(hb, w_ref[:, _Q_OFF:_GB_OFF], preferred_element_type=F32)
        rc, ra, rb = rc_ref[rows, :], ra_ref[rows, :], rb_ref[rows, :]

        def head_norm_rope(xb, gain):
            msq = jnp.dot((xb * xb).astype(BF16), hm, preferred_element_type=F32)
            xn = xb * lax.rsqrt(msq + EPS) * gain
            return xn * rc + pltpu.roll(xn, LANES - ROPE_DIM // 2, 1) * ra + pltpu.roll(xn, ROPE_DIM // 2, 1) * rb

        for g, (window, dil) in enumerate(DIL_PAIRS):
            lanes = slice(g * GROUP_WIDTH, (g + 1) * GROUP_WIDTH)
            qn = head_norm_rope(p_att[:, :ATTN_WIDTH][:, lanes], qg_ref[...])
            kn = head_norm_rope(p_att[:, ATTN_WIDTH:2 * ATTN_WIDTH][:, lanes], kg_ref[...])
            vv = p_att[:, 2 * ATTN_WIDTH:][:, lanes]
            if not class_major:
                q_ref, kv_ref = ((q0, kv0), (q1, kv1), (q2, kv2))[g]
                q_ref[rows, :] = qn
                kv_ref[rows, :GROUP_WIDTH] = kn
                kv_ref[rows, GROUP_WIDTH:] = vv
                continue
            qc_ref, kvc_ref, kvt_ref = ((qc0, kvc0, kvt0), (qc1, kvc1, kvt1), (qc2, kvc2, kvt2))[g]
            slab = 3 * (s * N_DIL + g)
            nat[slab] = qn
            nat[slab + 1] = kn
            nat[slab + 2] = vv
            per = sub // dil
            dst = slice(s * per, (s + 1) * per)
            for r in range(dil):
                src = pl.ds(r, per, stride=dil) if dil > 1 else slice(None)
                qc_ref[0, r, dst, :] = nat[slab, src, :].astype(BF16)
                kvc_ref[0, r, dst, :GROUP_WIDTH] = nat[slab + 1, src, :].astype(BF16)
                kvc_ref[0, r, dst, GROUP_WIDTH:] = nat[slab + 2, src, :].astype(BF16)
            def put_kvt(cols, part, g=g, kvt_ref=kvt_ref):
                if stacked:
                    kvt_ref[0, 0, :, cols] = prev_kvt[g][0, :, cols]
                    kvt_ref[1, 0, :, cols] = part
                else:
                    kvt_ref[0, :, cols] = part

            if window >= tiles * tm:
                put_kvt(rows, jnp.concatenate([kn, vv], axis=1).T)
            else:
                first_kept = tm - min(window, tm)
                lo = max(r0, first_kept)
                if lo < r0 + sub:
                    part = jnp.concatenate([kn[lo - r0:], vv[lo - r0:]], axis=1).T
                    put_kvt(slice(lo - first_kept, r0 + sub - first_kept), part)

        u_ref[rows, :] = jnp.dot(hb, w_ref[:, :POOL_WIDTH], preferred_element_type=F32)
        p_gate = jnp.dot(hb, w_ref[:, _GB_OFF:], preferred_element_type=F32)
        gb_ref[rows, :] = p_gate[:, :CONV_WIDTH]
        z_ref[rows, :] = p_gate[:, CONV_WIDTH:2 * CONV_WIDTH] * p_gate[:, 2 * CONV_WIDTH:]


def _proj(x, g1, w_in, qg, kg, rope, hm, *, tm, sub, batch=None, seq=None, prev_kvt=None, convert=None,
          roll=(), roll_by=0):
    rows = x.shape[0]
    stacked = prev_kvt is not None
    n_tab = rope[0].shape[0] // tm
    class_major = batch is not None
    tiles = seq // tm if class_major else 1
    row = lambda c: pl.BlockSpec((tm, c), lambda i: (i, 0))
    tab = pl.BlockSpec((tm, LANES), lambda i: (i % n_tab, 0))
    out_shape = [jax.ShapeDtypeStruct((rows, POOL_WIDTH), F32)] + [jax.ShapeDtypeStruct((rows, CONV_WIDTH), F32)] * 2
    out_specs = [row(POOL_WIDTH), row(CONV_WIDTH), row(CONV_WIDTH)]
    scratch = []
    prev_specs = []
    cast_in, cast_out, cast_shape, cast_args = _cast_specs(convert, rows // tm)
    assert class_major or (convert is None and not roll)
    roll_in, roll_out, roll_shape, roll_args = _roll_specs(roll, rows // tm)
    if class_major:
        for width in (GROUP_WIDTH, 2 * GROUP_WIDTH):
            for (_, dil) in DIL_PAIRS:
                out_shape.append(jax.ShapeDtypeStruct((batch, dil, seq // dil, width), BF16))
                out_specs.append(pl.BlockSpec((1, dil, tm // dil, width), lambda i: (i // tiles, 0, i % tiles, 0)))
        for (window, _) in DIL_PAIRS:
            assert window >= seq or window <= tm, "kv buffer rows must come from the last row tile"
            cols = tm if window >= seq else window
            col_blk = (lambda i: i % tiles) if window >= seq else (lambda i: 0)
            kvt_spec = pl.BlockSpec((1, 2 * GROUP_WIDTH, cols), lambda i, c=col_blk: (i // tiles, 0, c(i)))
            if stacked:
                prev_specs.append(kvt_spec)
                out_shape.append(jax.ShapeDtypeStruct((2, batch, 2 * GROUP_WIDTH, window), F32))
                out_specs.append(pl.BlockSpec((2, 1, 2 * GROUP_WIDTH, cols),
                                              lambda i, c=col_blk: (0, i // tiles, 0, c(i))))
            else:
                out_shape.append(jax.ShapeDtypeStruct((batch, 2 * GROUP_WIDTH, window), F32))
                out_specs.append(kvt_spec)
        scratch = [pltpu.VMEM((3 * N_DIL * (tm // sub), sub, GROUP_WIDTH), F32)]
    else:
        out_shape += [jax.ShapeDtypeStruct((rows, GROUP_WIDTH), F32)] * N_DIL
        out_shape += [jax.ShapeDtypeStruct((rows, 2 * GROUP_WIDTH), F32)] * N_DIL
        out_specs += [row(GROUP_WIDTH)] * N_DIL + [row(2 * GROUP_WIDTH)] * N_DIL
    return pl.pallas_call(
        functools.partial(_proj_kernel, class_major=class_major, tiles=tiles, sub=sub, stacked=stacked,
                          n_cast=len(cast_args), n_roll=len(roll_out), roll_by=roll_by),
        grid=(rows // tm,),
        in_specs=[row(D_MODEL), _const_spec((1, D_MODEL)), _const_spec((D_MODEL, IN_COLS)),
                  _const_spec((1, LANES)), _const_spec((1, LANES)), tab, tab, tab,
                  _const_spec((LANES, LANES))] + prev_specs + cast_in + roll_in,
        out_specs=out_specs + cast_out + roll_out,
        out_shape=out_shape + cast_shape + roll_shape,
        scratch_shapes=scratch,
        compiler_params=_params(),
        name="proj",
    )(x, g1, w_in, qg, kg, *rope, hm, *(prev_kvt or ()), *cast_args, *roll_args)


def _lane_lt(shape, bound):
    return lax.broadcasted_iota(jnp.int32, shape, len(shape) - 1) < bound


def _stack_heads(q):
    first = _lane_lt(q.shape, HEAD_DIM)
    zero = jnp.zeros_like(q)
    return jnp.concatenate([jnp.where(first, q, zero), jnp.where(first, zero, q)], axis=0)


def _unstack(o2, l2, m2):
    n = o2.shape[0] // 2
    first = _lane_lt((n, GROUP_WIDTH), HEAD_DIM)
    return tuple(jnp.where(first, a[:n], a[n:]) for a in (o2, l2, m2))


def _merge(a, b):
    m = jnp.maximum(a[2], b[2])
    wa = jnp.exp(a[2] - m)
    wb = jnp.exp(b[2] - m)
    return wa * a[0] + wb * b[0], wa * a[1] + wb * b[1], m


def _attn_unit(q, kv, bias):
    nk = kv.shape[0]
    k = kv[:, :GROUP_WIDTH]
    v1 = jnp.concatenate([kv[:, GROUP_WIDTH:], jnp.ones((nk, GROUP_WIDTH), BF16)], axis=1)
    s = lax.dot_general(_stack_heads(q), k, _NT, preferred_element_type=F32) + bias
    mrow = jnp.max(s, axis=-1, keepdims=True)
    p = jnp.exp(s - mrow).astype(BF16)
    r = jnp.dot(p, v1, preferred_element_type=F32)
    return _unstack(r[:, :GROUP_WIDTH], r[:, GROUP_WIDTH:], mrow)


def _attn_p_kernel(qc0, qc1, qc2, kvc0, kvc1, kvc2, *rest, seq, n_cast):
    cast_src, y_ref, rest = rest[:n_cast], rest[n_cast], rest[n_cast + 1:]
    _cast_slabs(cast_src, rest[:n_cast])
    scratch = rest[n_cast:]
    parts = [scratch[3 * g:3 * g + 3] for g in range(N_DIL)]
    bias_ref = scratch[3 * N_DIL]
    acc_o, acc_l, acc_m = parts[0]
    @pl.when(pl.program_id(0) == 0)
    def _():
        qi = lax.broadcasted_iota(jnp.int32, (2 * ATTN_TILE, 2 * ATTN_TILE), 0) & (ATTN_TILE - 1)
        ki = lax.broadcasted_iota(jnp.int32, (2 * ATTN_TILE, 2 * ATTN_TILE), 1)
        for first_key_back in (0, 1):
            dist = qi - ki + first_key_back * ATTN_TILE
            bias_ref[first_key_back] = jnp.where((dist >= 0) & (dist <= N_BACK), 0.0, NEG_INF)

    for g, ((_, dil), q_ref, kv_ref) in enumerate(zip(DIL_PAIRS, (qc0, qc1, qc2), (kvc0, kvc1, kvc2))):
        dst = parts[g]
        n_sub = seq // dil // ATTN_TILE

        def unit(idx, q_ref=q_ref, kv_ref=kv_ref, n_sub=n_sub, dil=dil):
            if n_sub == 1:
                r, sub = idx, 0
                res = _attn_unit(q_ref[0, r], kv_ref[0, r], bias_ref[0, :, :ATTN_TILE])
            else:
                r, sub = (idx // n_sub, idx % n_sub) if dil > 1 else (0, idx)
                back = jnp.minimum(sub, 1)
                q0 = pl.multiple_of(sub * ATTN_TILE, ATTN_TILE)
                k0 = pl.multiple_of((sub - back) * ATTN_TILE, ATTN_TILE)
                res = _attn_unit(q_ref[0, r, pl.ds(q0, ATTN_TILE), :], kv_ref[0, r, pl.ds(k0, 2 * ATTN_TILE), :],
                                 bias_ref[back])
            start = sub * (ATTN_TILE * dil) + r
            rows = pl.ds(start, ATTN_TILE, stride=dil) if dil > 1 else pl.ds(pl.multiple_of(start, ATTN_TILE), ATTN_TILE)
            return rows, res

        def units_body(it, carry, unit=unit, dst=dst):
            for j in range(ATTN_UNROLL):
                rows, res = unit(it * ATTN_UNROLL + j)
                for ref, val in zip(dst, res):
                    ref[rows, :] = val
            return carry
        n_iter = dil * n_sub // ATTN_UNROLL
        if n_iter == 1:
            units_body(0, 0)
        else:
            lax.fori_loop(0, n_iter, units_body, 0)

        if g > 0:
            for c in range(seq // ATTN_MERGE_ROWS):
                rows = slice(c * ATTN_MERGE_ROWS, (c + 1) * ATTN_MERGE_ROWS)
                o, l, m = _merge((acc_o[rows, :], acc_l[rows, :], acc_m[rows, :]),
                                 tuple(ref[rows, :] for ref in parts[g]))
                if g == N_DIL - 1:
                    y_ref[0, rows, :] = (o / l).astype(y_ref.dtype)
                else:
                    acc_o[rows, :] = o
                    acc_l[rows, :] = l
                    acc_m[rows, :] = m


def _attn_p(qcs, kvcs, *, batch, seq, convert=None):
    in_specs = [pl.BlockSpec((1,) + a.shape[1:], lambda b: (b, 0, 0, 0)) for a in (*qcs, *kvcs)]
    cast_in, cast_out, cast_shape, cast_args = _cast_specs(convert, batch)
    acc = pltpu.VMEM((seq, GROUP_WIDTH), F32)
    return pl.pallas_call(
        functools.partial(_attn_p_kernel, seq=seq, n_cast=len(cast_args)),
        grid=(batch,),
        in_specs=in_specs + cast_in,
        out_specs=[pl.BlockSpec((1, seq, GROUP_WIDTH), lambda b: (b, 0, 0))] + cast_out,
        out_shape=[jax.ShapeDtypeStruct((batch, seq, GROUP_WIDTH), BF16)] + cast_shape,
        scratch_shapes=[acc] * (3 * N_DIL) + [pltpu.VMEM((2, 2 * ATTN_TILE, 2 * ATTN_TILE), F32)],
        compiler_params=_params(),
        name="attn_p",
    )(*qcs, *kvcs, *cast_args)


def _pool_rows(ext, pos, pw, ps):
    return jnp.dot(_pool_diff(ext, pos), pw, preferred_element_type=F32) * ps


def _pool_diff(ext, pos):
    s2 = ext + pltpu.roll(ext, 1, 0)
    s4 = s2 + pltpu.roll(s2, 2, 0)
    s8 = s4 + pltpu.roll(s4, 4, 0)
    s16 = s8 + pltpu.roll(s8, 8, 0)
    lane = lax.broadcasted_iota(jnp.int32, (1, POOL_WIDTH), 1)
    grp = [lane < (j + 1) * POOL_GROUP for j in range(3)]
    win = jnp.where(grp[0], s2, jnp.where(grp[1], s4, jnp.where(grp[2], s8, s16)))[HIST_ROWS:]
    width = jnp.where(grp[0], POOL_WINDOWS[0], jnp.where(grp[1], POOL_WINDOWS[1],
                      jnp.where(grp[2], POOL_WINDOWS[2], POOL_WINDOWS[3])))
    cnt = jnp.minimum(pos + 1, width).astype(F32)
    return (win / cnt - ext[HIST_ROWS:]).astype(BF16)


def _conv_rows(ext, gb, cw, hist):
    y = cw[0:1] * pltpu.roll(ext, 2, 0) + cw[1:2] * pltpu.roll(ext, 1, 0) + cw[2:3] * ext
    return gb * y[hist:]


def _attn_s_group(q, new_t, cache_t, window, dil):
    t = q.shape[0]
    q2 = _stack_heads(q).astype(BF16)
    s_c = jnp.dot(q2, cache_t[:GROUP_WIDTH].astype(BF16), preferred_element_type=F32)
    s_n = jnp.dot(q2, new_t[:GROUP_WIDTH].astype(BF16), preferred_element_type=F32)
    tq = lax.broadcasted_iota(jnp.int32, (2 * t, 1), 0) & (t - 1)
    rc = (lax.broadcasted_iota(jnp.int32, (1, window), 1) + t) & (window - 1)
    dist_c = window + tq - rc
    ok_c = (dist_c <= window) & ((dist_c & (dil - 1)) == 0) & (PAST_LEN - window + rc >= 0)
    tn = lax.broadcasted_iota(jnp.int32, (1, LANES), 1) - (LANES - t)
    dist_n = tq - tn
    ok_n = (tn >= 0) & (dist_n >= 0) & ((dist_n & (dil - 1)) == 0)
    s_c = jnp.where(ok_c, s_c, NEG_INF)
    s_n = jnp.where(ok_n, s_n, NEG_INF)
    mrow = jnp.maximum(jnp.max(s_c, axis=-1, keepdims=True), jnp.max(s_n, axis=-1, keepdims=True))
    p_c = jnp.exp(s_c - mrow)
    p_n = jnp.exp(s_n - mrow)
    o2 = (lax.dot_general(p_c.astype(BF16), cache_t[GROUP_WIDTH:].astype(BF16), _NT, preferred_element_type=F32)
          + lax.dot_general(p_n.astype(BF16), new_t[GROUP_WIDTH:].astype(BF16), _NT, preferred_element_type=F32))
    l2 = jnp.sum(p_c, axis=-1, keepdims=True) + jnp.sum(p_n, axis=-1, keepdims=True)
    return _unstack(o2, l2, mrow)


def _mix_s_kernel(u_ref, q0_ref, q1_ref, q2_ref, kn0_ref, kn1_ref, kn2_ref, gb_ref, z_ref, pst_ref, cst_ref,
                  pw_ref, ps_ref, cw_ref, c0_ref, c1_ref, c2_ref, yp_ref, ya_ref, yc_ref, nc0_ref, nc1_ref, nc2_ref,
                  *, t_new, n_per):
    keep_old = _lane_lt((2 * GROUP_WIDTH, LANES), LANES - t_new)
    pos = PAST_LEN + lax.broadcasted_iota(jnp.int32, (t_new, 1), 0)
    for i in range(n_per):
        state = None
        for (window, dil), q_ref, kn_ref, c_ref, nc_ref in zip(
                DIL_PAIRS, (q0_ref, q1_ref, q2_ref), (kn0_ref, kn1_ref, kn2_ref),
                (c0_ref, c1_ref, c2_ref), (nc0_ref, nc1_ref, nc2_ref)):
            kv_new = kn_ref[i]
            new_t = jnp.concatenate([jnp.zeros((LANES - t_new, 2 * GROUP_WIDTH), F32), kv_new], axis=0).T
            cache_t = c_ref[0, i]
            part = _attn_s_group(q_ref[i], new_t, cache_t, window, dil)
            state = part if state is None else _merge(state, part)
            nc_ref[0, i] = jnp.where(keep_old, cache_t[:, window - LANES:], new_t)
        ya_ref[i] = state[0] / state[1]

        u_ext = jnp.concatenate([pst_ref[i], u_ref[i]], axis=0)
        yp_ref[i] = _pool_rows(u_ext, pos, pw_ref[...], ps_ref[...])
        z_ext = jnp.concatenate([cst_ref[i], z_ref[i]], axis=0)
        yc_ref[i] = _conv_rows(z_ext, gb_ref[i], cw_ref[...], cst_ref.shape[1])


def _mix_s(u, qs, kns, gb, z, pst, cst, rotated_bufs, pw, ps, cw, *, layer, n_seq, t_new):
    n_per = MIX_S_SEQS
    blk = lambda r, c: pl.BlockSpec((n_per, r, c), lambda b: (b, 0, 0))
    three = lambda a: a.reshape(n_seq, -1, a.shape[-1])
    windows = [w for (w, _) in DIL_PAIRS]
    in_specs = ([blk(t_new, POOL_WIDTH)] + [blk(t_new, GROUP_WIDTH)] * N_DIL + [blk(t_new, 2 * GROUP_WIDTH)] * N_DIL
                + [blk(t_new, CONV_WIDTH)] * 2 + [blk(pst.shape[1], POOL_WIDTH), blk(cst.shape[1], CONV_WIDTH)]
                + [_const_spec((POOL_WIDTH, POOL_WIDTH)), _const_spec((1, POOL_WIDTH)),
                   _const_spec((CONV_K, CONV_WIDTH))]
                + [pl.BlockSpec((1, n_per, 2 * GROUP_WIDTH, w), lambda b: (layer, b, 0, 0)) for w in windows])
    args = [three(u), *[three(q) for q in qs], *[three(k) for k in kns], three(gb), three(z), pst, cst, pw, ps, cw,
            *rotated_bufs]
    n_in = len(args)
    out_specs = ([blk(t_new, POOL_WIDTH), blk(t_new, GROUP_WIDTH), blk(t_new, CONV_WIDTH)]
                 + [pl.BlockSpec((1, n_per, 2 * GROUP_WIDTH, LANES), lambda b, w=w: (layer, b, 0, w // LANES - 1))
                    for w in windows])
    out_shape = ([jax.ShapeDtypeStruct((n_seq, t_new, c), F32) for c in (POOL_WIDTH, GROUP_WIDTH, CONV_WIDTH)]
                 + [jax.ShapeDtypeStruct(c.shape, F32) for c in rotated_bufs])
    return pl.pallas_call(
        functools.partial(_mix_s_kernel, t_new=t_new, n_per=n_per),
        grid=(n_seq // n_per,),
        in_specs=in_specs,
        out_specs=out_specs,
        out_shape=out_shape,
        input_output_aliases={n_in - N_DIL + g: 3 + g for g in range(N_DIL)},
        compiler_params=_params(),
        name="mix_s",
    )(*args)


def _mlp_head(x, yp, ya, yc, wo_ref, g2_ref):
    mixed = jnp.concatenate([yp.astype(BF16), ya.astype(BF16), yc.astype(BF16)], axis=1)
    x1 = x + jnp.dot(mixed, wo_ref[...], preferred_element_type=F32)
    ms = jnp.mean(x1 * x1, axis=-1, keepdims=True)
    return x1, (x1 * lax.rsqrt(ms + EPS) * g2_ref[...]).astype(BF16)


def _mlp_tail(x1, hb, wu_ref, wd_ref, tf):
    acc = x1
    for c in range(D_FF // tf):
        hf = jnp.dot(hb, wu_ref[:, c * tf:(c + 1) * tf], preferred_element_type=F32)
        act = jnp.square(jnp.maximum(hf, 0.0)).astype(BF16)
        acc = acc + jnp.dot(act, wd_ref[c * tf:(c + 1) * tf, :], preferred_element_type=F32)
    return acc


def _mlp_s_kernel(x_ref, yp_ref, ya_ref, yc_ref, wo_ref, g2_ref, wu_ref, wd_ref, o_ref, hb_buf):
    @pl.when(pl.program_id(0) == 0)
    def _():
        x1, hb = _mlp_head(x_ref[...], yp_ref[...], ya_ref[...], yc_ref[...], wo_ref, g2_ref)
        o_ref[...] = x1
        hb_buf[...] = hb

    hf = jnp.dot(hb_buf[...], wu_ref[...], preferred_element_type=F32)
    act = jnp.square(jnp.maximum(hf, 0.0)).astype(BF16)
    o_ref[...] += jnp.dot(act, wd_ref[...], preferred_element_type=F32)


def _mlp_p_kernel(x_ref, u_ref, uh_ref, gb_ref, z_ref, zh_ref, ya_ref, pw_ref, ps_ref, cw_ref,
                  wo_ref, g2_ref, wu_ref, wd_ref, *rest, tf, tiles, n_cast, n_roll, roll_by):
    cast_src, roll_src, o_ref = rest[:n_cast], rest[n_cast:n_cast + 2 * n_roll], rest[n_cast + 2 * n_roll]
    outs = rest[n_cast + 2 * n_roll + 1:]
    _cast_slabs(cast_src, outs[:n_cast])
    _roll_buffers(roll_src, outs[n_cast:], roll_by)
    tm = x_ref.shape[0]
    tile = pl.program_id(0) % tiles
    has_hist = tile > 0
    pos = tile * tm + lax.broadcasted_iota(jnp.int32, (tm, 1), 0)
    yp = _pool_rows(jnp.concatenate([jnp.where(has_hist, uh_ref[...], 0.0), u_ref[...]], axis=0), pos,
                    pw_ref[...], ps_ref[...])
    yc = _conv_rows(jnp.concatenate([jnp.where(has_hist, zh_ref[...], 0.0), z_ref[...]], axis=0),
                    gb_ref[...], cw_ref[...], HIST_ROWS)
    x1, hb = _mlp_head(x_ref[...], yp, ya_ref[...], yc, wo_ref, g2_ref)
    o_ref[...] = _mlp_tail(x1, hb, wu_ref, wd_ref, tf)


def _mlp_s(x, yp, ya, yc, w_out, g2, w_up, w_down):
    rows = x.shape[0]
    tf = MLP_FF_CHUNK
    full = lambda c: pl.BlockSpec((rows, c), lambda j: (0, 0))
    return pl.pallas_call(
        _mlp_s_kernel,
        grid=(D_FF // tf,),
        in_specs=[full(D_MODEL), full(POOL_WIDTH), full(GROUP_WIDTH), full(CONV_WIDTH),
                  _const_spec((MIX_OUT, D_MODEL)), _const_spec((1, D_MODEL)),
                  pl.BlockSpec((D_MODEL, tf), lambda j: (0, j)),
                  pl.BlockSpec((tf, D_MODEL), lambda j: (j, 0))],
        out_specs=full(D_MODEL),
        out_shape=jax.ShapeDtypeStruct((rows, D_MODEL), F32),
        scratch_shapes=[pltpu.VMEM((rows, D_MODEL), BF16)],
        compiler_params=_params(),
        name="mlp_s",
    )(x, yp, ya, yc, w_out, g2, w_up, w_down)


def _mlp_p(x, u, gb, z, ya, pw, ps, cw, w_out, g2, w_up, w_down, *, tm, seq, convert=None, roll=(), roll_by=0):
    rows = x.shape[0]
    row = lambda c: pl.BlockSpec((tm, c), lambda i: (i, 0))
    conv_in, conv_out_specs, conv_out_shape, conv_args = _cast_specs(convert, rows // tm)
    roll_in, roll_out, roll_shape, roll_args = _roll_specs(roll, rows // tm)
    hist = lambda c: pl.BlockSpec((HIST_ROWS, c), lambda i: (jnp.maximum(i * (tm // HIST_ROWS) - 1, 0), 0))
    return pl.pallas_call(
        functools.partial(_mlp_p_kernel, tf=MLP_FF_CHUNK, tiles=seq // tm, n_cast=len(conv_args),
                          n_roll=len(roll_out), roll_by=roll_by),
        grid=(rows // tm,),
        in_specs=[row(D_MODEL), row(POOL_WIDTH), hist(POOL_WIDTH), row(CONV_WIDTH), row(CONV_WIDTH),
                  hist(CONV_WIDTH), row(GROUP_WIDTH), _const_spec((POOL_WIDTH, POOL_WIDTH)),
                  _const_spec((1, POOL_WIDTH)), _const_spec((CONV_K, CONV_WIDTH)),
                  _const_spec((MIX_OUT, D_MODEL)), _const_spec((1, D_MODEL)),
                  _const_spec((D_MODEL, D_FF)), _const_spec((D_FF, D_MODEL))] + conv_in + roll_in,
        out_specs=[row(D_MODEL)] + conv_out_specs + roll_out,
        out_shape=[jax.ShapeDtypeStruct((rows, D_MODEL), F32)] + conv_out_shape + roll_shape,
        compiler_params=_params(),
        name="mlp_p",
    )(x, u, u, gb, z, z, ya, pw, ps, cw, w_out, g2, w_up, w_down, *conv_args, *roll_args)


def _rope_tables(pos):
    half = ROPE_DIM // 2
    inv = jnp.power(jnp.float32(ROPE_THETA), -jnp.arange(half, dtype=F32) / half)
    ang = pos.astype(F32)[:, None] * inv[None, :]
    cos, sin = jnp.cos(ang), jnp.sin(ang)
    n = pos.shape[0]
    rest = HEAD_DIM - ROPE_DIM
    zh = jnp.zeros((n, half), F32)
    c = jnp.concatenate([cos, cos, jnp.ones((n, rest), F32)], axis=1)
    a = jnp.concatenate([-sin, zh, jnp.zeros((n, rest), F32)], axis=1)
    b = jnp.concatenate([zh, sin, jnp.zeros((n, rest), F32)], axis=1)
    return tuple(jnp.tile(t, (1, GROUP_WIDTH // HEAD_DIM)) for t in (c, a, b))


def _to_buffer_layout(c):
    lead = c.shape[:-4]
    n = len(lead)
    t = jnp.transpose(c, (*range(n), n + 1, n + 2, n + 3, n))
    return t.reshape(*lead, 2 * GROUP_WIDTH, c.shape[-4])


def _from_buffer_layout(t):
    lead = t.shape[:-2]
    n = len(lead)
    c = t.reshape(*lead, 2, 2, HEAD_DIM, t.shape[-1])
    return jnp.transpose(c, (*range(n), n + 3, n, n + 1, n + 2))


def kernel(x_prompt, x_sample, state_pool, state_conv, cache_kv_w128, cache_kv_w512, cache_kv_w2048,
           norm1_g, w_in, q_norm_g, k_norm_g, pool_w, pool_scale, conv_w, w_out, norm2_g, w_up, w_down):
    batch, seq, _ = x_prompt.shape
    n_seq, t_new, _ = x_sample.shape
    depth = w_in.shape[0]
    assert depth == 2, "the in-place stacking of the kv state outputs is written for two layers"
    caches_t = [_to_buffer_layout(c) for c in (cache_kv_w128, cache_kv_w512, cache_kv_w2048)]

    rope_p = _rope_tables(jnp.arange(seq, dtype=jnp.int32))
    rope_s = _rope_tables(jnp.tile(PAST_LEN + jnp.arange(t_new, dtype=jnp.int32), n_seq))
    head_id = jnp.arange(LANES) // HEAD_DIM
    hm = jnp.where(head_id[:, None] == head_id[None, :], 1.0 / HEAD_DIM, 0.0).astype(BF16)
    two_heads = lambda gain: jnp.tile(gain, GROUP_WIDTH // HEAD_DIM)[None, :]

    xp = x_prompt.reshape(batch * seq, D_MODEL)
    xs = x_sample.reshape(n_seq * t_new, D_MODEL)
    outs = {k: [] for k in ("pool_p", "conv_p", "pool_s", "conv_s")}
    w_in_l = w_in[0].astype(BF16)
    new_caches = [None] * N_DIL
    kvts = None
    for layer in range(depth):
        pw = jax.scipy.linalg.block_diag(*[pool_w[layer, j] for j in range(len(POOL_WINDOWS))]).astype(BF16)
        ps = pool_scale[layer][None, :]
        cw = conv_w[layer]
        g1 = norm1_g[layer][None, :]
        g2 = norm2_g[layer][None, :]
        qg = two_heads(q_norm_g[layer]) * (HEAD_DIM ** -0.5)
        kg = two_heads(k_norm_g[layer])

        w_out_l = w_out[layer].astype(BF16)
        if layer == 0:
            (u, gb, z, qc0, qc1, qc2, kvc0, kvc1, kvc2, *tail) = _proj(
                xp, g1, w_in_l, qg, kg, rope_p, hm, tm=PROJ_TILE, sub=PROJ_SUB, batch=batch, seq=seq,
                roll=caches_t[:-1], roll_by=t_new)
            kvts, new_caches[:-1] = tail[:N_DIL], tail[N_DIL:]
            ya, w_up_l, w_down_l = _attn_p((qc0, qc1, qc2), (kvc0, kvc1, kvc2), batch=batch, seq=seq,
                                           convert=(0, (w_up, w_down)))
            xp, w_in_next, new_caches[-1] = _mlp_p(
                xp, u, gb, z, ya.reshape(-1, GROUP_WIDTH), pw, ps, cw, w_out_l, g2, w_up_l, w_down_l, tm=MLP_TILE,
                seq=seq, convert=(1, (w_in,)), roll=caches_t[-1:], roll_by=t_new)
        else:
            (u, gb, z, qc0, qc1, qc2, kvc0, kvc1, kvc2, *tail) = _proj(
                xp, g1, w_in_l, qg, kg, rope_p, hm, tm=PROJ_TILE, sub=PROJ_SUB, batch=batch, seq=seq,
                prev_kvt=kvts, convert=(1, (w_up, w_down)))
            kvts, (w_up_l, w_down_l) = tail[:N_DIL], tail[N_DIL:]
            (ya,) = _attn_p((qc0, qc1, qc2), (kvc0, kvc1, kvc2), batch=batch, seq=seq)
            (xp,) = _mlp_p(xp, u, gb, z, ya.reshape(-1, GROUP_WIDTH), pw, ps, cw, w_out_l, g2, w_up_l, w_down_l,
                           tm=MLP_TILE, seq=seq)
        outs["pool_p"].append(u.reshape(batch, seq, POOL_WIDTH)[:, seq - POOL_HIST:])
        outs["conv_p"].append(z.reshape(batch, seq, CONV_WIDTH)[:, seq - (CONV_K - 1):])

        us, gbs, zs, q0, q1, q2, kn0, kn1, kn2 = _proj(xs, g1, w_in_l, qg, kg, rope_s, hm, tm=n_seq * t_new,
                                                       sub=n_seq * t_new)
        pst = jnp.pad(state_pool[layer], ((0, 0), (HIST_ROWS - POOL_HIST, 0), (0, 0)))
        cst = jnp.pad(state_conv[layer], ((0, 0), (SUBLANES - (CONV_K - 1), 0), (0, 0)))
        yps, yas, ycs, *new_caches = _mix_s(us, (q0, q1, q2), (kn0, kn1, kn2), gbs, zs, pst, cst, new_caches,
                                           pw, ps, cw, layer=layer, n_seq=n_seq, t_new=t_new)
        xs = _mlp_s(xs, yps.reshape(-1, POOL_WIDTH), yas.reshape(-1, GROUP_WIDTH), ycs.reshape(-1, CONV_WIDTH),
                    w_out_l, g2, w_up_l, w_down_l)
        us3 = us.reshape(n_seq, t_new, POOL_WIDTH)
        zs3 = zs.reshape(n_seq, t_new, CONV_WIDTH)
        outs["pool_s"].append(jnp.concatenate([state_pool[layer], us3], axis=1)[:, -POOL_HIST:])
        outs["conv_s"].append(jnp.concatenate([state_conv[layer], zs3], axis=1)[:, -(CONV_K - 1):])
        if layer == 0:
            w_in_l = w_in_next

    st = lambda k: jnp.stack(outs[k])
    return (xp.reshape(batch, seq, D_MODEL), xs.reshape(n_seq, t_new, D_MODEL),
            st("pool_p"), st("conv_p"),
            *[_from_buffer_layout(kvt) for kvt in kvts],
            st("pool_s"), st("conv_s"),
            *[_from_buffer_layout(c) for c in new_caches])
```

```python
import functools

import jax
import jax.numpy as jnp
from jax import lax
from jax.experimental import pallas as pl
from jax.experimental.pallas import tpu as pltpu

D_MODEL = 1024
HEAD_DIM = 64
POOL_WIDTH = 256
POOL_WINDOWS = (2, 4, 8, 16)
POOL_GROUP = 64
POOL_HIST = 15
ATTN_WIDTH = 384
DIL_PAIRS = ((128, 1), (512, 4), (2048, 16))
N_DIL = 3
GROUP_WIDTH = 128
CONV_WIDTH = 384
CONV_K = 3
ROPE_DIM = 16
ROPE_THETA = 500000.0
D_FF = 4096
IN_COLS = 2560
MIX_OUT = 768
EPS = 1e-6
NEG_INF = -1e30
PAST_LEN = 8192
N_BACK = 128

LANES = 128
SUBLANES = 8
HIST_ROWS = 16
ATTN_TILE = 128
ATTN_UNROLL = 16
ATTN_MERGE_ROWS = 256
PROJ_TILE = 1024
PROJ_SUB = 512
MLP_TILE = 512
MLP_FF_CHUNK = 512
MIX_S_SEQS = 4
VMEM_LIMIT = 56 * 1024 * 1024

F32 = jnp.float32
BF16 = jnp.bfloat16

_Q_OFF = POOL_WIDTH
_GB_OFF = _Q_OFF + 3 * ATTN_WIDTH

_NT = (((1,), (1,)), ((), ()))


def _const_spec(shape):
    return pl.BlockSpec(shape, lambda *_: (0,) * len(shape), pipeline_mode=pl.Buffered(1))


def _layer_spec(shape, layer):
    return pl.BlockSpec((None, *shape), lambda *_: (layer,) + (0,) * len(shape), pipeline_mode=pl.Buffered(1))


def _params():
    return pltpu.CompilerParams(dimension_semantics=("arbitrary",), vmem_limit_bytes=VMEM_LIMIT)


def _cast_specs(convert, n_steps):
    in_specs, out_specs, out_shape, args = [], [], [], []
    if convert is not None:
        layer, weights = convert
        for w in weights:
            _, k, n = w.shape
            in_specs.append(pl.BlockSpec((None, k // n_steps, n), lambda i: (layer, i, 0)))
            out_specs.append(pl.BlockSpec((k // n_steps, n), lambda i: (i, 0)))
            out_shape.append(jax.ShapeDtypeStruct((k, n), BF16))
            args.append(w)
    return in_specs, out_specs, out_shape, args


def _cast_slabs(srcs, dsts):
    for src, dst in zip(srcs, dsts):
        dst[...] = src[...].astype(BF16)


def _roll_specs(buffers, n_steps):
    in_specs, out_specs, out_shape, args = [], [], [], []
    for cache_t in buffers:
        depth, n_seq, width, window = cache_t.shape
        assert n_seq % n_steps == 0
        per = n_seq // n_steps
        in_specs += [pl.BlockSpec((1, per, width, window), lambda i, l=l: (l, i, 0, 0)) for l in range(depth)]
        args += [cache_t] * depth
        out_specs.append(pl.BlockSpec((depth, per, width, window), lambda i: (0, i, 0, 0)))
        out_shape.append(jax.ShapeDtypeStruct(cache_t.shape, F32))
    return in_specs, out_specs, out_shape, args


def _roll_buffers(srcs, dsts, by):
    srcs = iter(srcs)
    for dst in dsts:
        for layer in range(dst.shape[0]):
            src = next(srcs)
            for i in range(src.shape[1]):
                dst[layer, i] = pltpu.roll(src[0, i], src.shape[-1] - by, 1)


def _proj_kernel(x_ref, g1_ref, w_ref, qg_ref, kg_ref, rc_ref, ra_ref, rb_ref, hm_ref, *rest,
                 class_major, tiles, sub, stacked, n_cast, n_roll, roll_by):
    prev_kvt = None
    if class_major:
        if stacked:
            prev_kvt, rest = rest[:N_DIL], rest[N_DIL:]
        cast_src, rest = rest[:n_cast], rest[n_cast:]
        roll_src, rest = rest[:2 * n_roll], rest[2 * n_roll:]
        (u_ref, gb_ref, z_ref, qc0, qc1, qc2, kvc0, kvc1, kvc2, kvt0, kvt1, kvt2) = rest[:12]
        _cast_slabs(cast_src, rest[12:12 + n_cast])
        _roll_buffers(roll_src, rest[12 + n_cast:12 + n_cast + n_roll], roll_by)
        nat = rest[-1]
    else:
        (u_ref, gb_ref, z_ref, q0, q1, q2, kv0, kv1, kv2) = rest
    tm = x_ref.shape[0]
    hm = hm_ref[...]

    for s in range(tm // sub):
        r0 = s * sub
        rows = slice(r0, r0 + sub)
        x = x_ref[rows, :]
        ms = jnp.mean(x * x, axis=-1, keepdims=True)
        hb = (x * lax.rsqrt(ms + EPS) * g1_ref[...]).astype(BF16)
        p_att = jnp.dot(hb, w_ref[:, _Q_OFF:_GB_OFF], preferred_element_type=F32)
        rc, ra, rb = rc_ref[rows, :], ra_ref[rows, :], rb_ref[rows, :]

        def head_norm_rope(xb, gain):
            msq = jnp.dot((xb * xb).astype(BF16), hm, preferred_element_type=F32)
            xn = xb * lax.rsqrt(msq + EPS) * gain
            return xn * rc + pltpu.roll(xn, LANES - ROPE_DIM // 2, 1) * ra + pltpu.roll(xn, ROPE_DIM // 2, 1) * rb

        for g, (window, dil) in enumerate(DIL_PAIRS):
            lanes = slice(g * GROUP_WIDTH, (g + 1) * GROUP_WIDTH)
            qn = head_norm_rope(p_att[:, :ATTN_WIDTH][:, lanes], qg_ref[...])
            kn = head_norm_rope(p_att[:, ATTN_WIDTH:2 * ATTN_WIDTH][:, lanes], kg_ref[...])
            vv = p_att[:, 2 * ATTN_WIDTH:][:, lanes]
            if not class_major:
                q_ref, kv_ref = ((q0, kv0), (q1, kv1), (q2, kv2))[g]
                q_ref[rows, :] = qn
                kv_ref[rows, :GROUP_WIDTH] = kn
                kv_ref[rows, GROUP_WIDTH:] = vv
                continue
            qc_ref, kvc_ref, kvt_ref = ((qc0, kvc0, kvt0), (qc1, kvc1, kvt1), (qc2, kvc2, kvt2))[g]
            slab = 3 * (s * N_DIL + g)
            nat[slab] = qn
            nat[slab + 1] = kn
            nat[slab + 2] = vv
            per = sub // dil
            dst = slice(s * per, (s + 1) * per)
            for r in range(dil):
                src = pl.ds(r, per, stride=dil) if dil > 1 else slice(None)
                qc_ref[0, r, dst, :] = nat[slab, src, :].astype(BF16)
                kvc_ref[0, r, dst, :GROUP_WIDTH] = nat[slab + 1, src, :].astype(BF16)
                kvc_ref[0, r, dst, GROUP_WIDTH:] = nat[slab + 2, src, :].astype(BF16)
            def put_kvt(cols, part, g=g, kvt_ref=kvt_ref):
                if stacked:
                    kvt_ref[0, 0, :, cols] = prev_kvt[g][0, :, cols]
                    kvt_ref[1, 0, :, cols] = part
                else:
                    kvt_ref[0, :, cols] = part

            if window >= tiles * tm:
                put_kvt(rows, jnp.concatenate([kn, vv], axis=1).T)
            else:
                first_kept = tm - min(window, tm)
                lo = max(r0, first_kept)
                if lo < r0 + sub:
                    part = jnp.concatenate([kn[lo - r0:], vv[lo - r0:]], axis=1).T
                    put_kvt(slice(lo - first_kept, r0 + sub - first_kept), part)

        u_ref[rows, :] = jnp.dot(hb, w_ref[:, :POOL_WIDTH], preferred_element_type=F32)
        p_gate = jnp.dot(hb, w_ref[:, _GB_OFF:], preferred_element_type=F32)
        gb_ref[rows, :] = p_gate[:, :CONV_WIDTH]
        z_ref[rows, :] = p_gate[:, CONV_WIDTH:2 * CONV_WIDTH] * p_gate[:, 2 * CONV_WIDTH:]


def _proj(x, g1, w_in, qg, kg, rope, hm, *, layer, tm, sub, batch=None, seq=None, prev_kvt=None, convert=None,
          roll=(), roll_by=0):
    rows = x.shape[0]
    stacked = prev_kvt is not None
    n_tab = rope[0].shape[0] // tm
    class_major = batch is not None
    tiles = seq // tm if class_major else 1
    row = lambda c: pl.BlockSpec((tm, c), lambda i: (i, 0))
    tab = pl.BlockSpec((tm, LANES), lambda i: (i % n_tab, 0))
    out_shape = [jax.ShapeDtypeStruct((rows, POOL_WIDTH), F32)] + [jax.ShapeDtypeStruct((rows, CONV_WIDTH), F32)] * 2
    out_specs = [row(POOL_WIDTH), row(CONV_WIDTH), row(CONV_WIDTH)]
    scratch = []
    prev_specs = []
    cast_in, cast_out, cast_shape, cast_args = _cast_specs(convert, rows // tm)
    assert class_major or (convert is None and not roll)
    roll_in, roll_out, roll_shape, roll_args = _roll_specs(roll, rows // tm)
    if class_major:
        for width in (GROUP_WIDTH, 2 * GROUP_WIDTH):
            for (_, dil) in DIL_PAIRS:
                out_shape.append(jax.ShapeDtypeStruct((batch, dil, seq // dil, width), BF16))
                out_specs.append(pl.BlockSpec((1, dil, tm // dil, width), lambda i: (i // tiles, 0, i % tiles, 0)))
        for (window, _) in DIL_PAIRS:
            assert window >= seq or window <= tm, "kv buffer rows must come from the last row tile"
            cols = tm if window >= seq else window
            col_blk = (lambda i: i % tiles) if window >= seq else (lambda i: 0)
            kvt_spec = pl.BlockSpec((1, 2 * GROUP_WIDTH, cols), lambda i, c=col_blk: (i // tiles, 0, c(i)))
            if stacked:
                prev_specs.append(kvt_spec)
                out_shape.append(jax.ShapeDtypeStruct((2, batch, 2 * GROUP_WIDTH, window), F32))
                out_specs.append(pl.BlockSpec((2, 1, 2 * GROUP_WIDTH, cols),
                                              lambda i, c=col_blk: (0, i // tiles, 0, c(i))))
            else:
                out_shape.append(jax.ShapeDtypeStruct((batch, 2 * GROUP_WIDTH, window), F32))
                out_specs.append(kvt_spec)
        scratch = [pltpu.VMEM((3 * N_DIL * (tm // sub), sub, GROUP_WIDTH), F32)]
    else:
        out_shape += [jax.ShapeDtypeStruct((rows, GROUP_WIDTH), F32)] * N_DIL
        out_shape += [jax.ShapeDtypeStruct((rows, 2 * GROUP_WIDTH), F32)] * N_DIL
        out_specs += [row(GROUP_WIDTH)] * N_DIL + [row(2 * GROUP_WIDTH)] * N_DIL
    return pl.pallas_call(
        functools.partial(_proj_kernel, class_major=class_major, tiles=tiles, sub=sub, stacked=stacked,
                          n_cast=len(cast_args), n_roll=len(roll_out), roll_by=roll_by),
        grid=(rows // tm,),
        in_specs=[row(D_MODEL), _layer_spec((1, D_MODEL), layer), _const_spec((D_MODEL, IN_COLS)),
                  _layer_spec((1, LANES), layer), _layer_spec((1, LANES), layer), tab, tab, tab,
                  _const_spec((LANES, LANES))] + prev_specs + cast_in + roll_in,
        out_specs=out_specs + cast_out + roll_out,
        out_shape=out_shape + cast_shape + roll_shape,
        scratch_shapes=scratch,
        compiler_params=_params(),
        name="proj",
    )(x, g1, w_in, qg, kg, *rope, hm, *(prev_kvt or ()), *cast_args, *roll_args)


def _lane_lt(shape, bound):
    return lax.broadcasted_iota(jnp.int32, shape, len(shape) - 1) < bound


def _stack_heads(q):
    first = _lane_lt(q.shape, HEAD_DIM)
    zero = jnp.zeros_like(q)
    return jnp.concatenate([jnp.where(first, q, zero), jnp.where(first, zero, q)], axis=0)


def _unstack(o2, l2, m2):
    n = o2.shape[0] // 2
    first = _lane_lt((n, GROUP_WIDTH), HEAD_DIM)
    return tuple(jnp.where(first, a[:n], a[n:]) for a in (o2, l2, m2))


def _merge(a, b):
    m = jnp.maximum(a[2], b[2])
    wa = jnp.exp(a[2] - m)
    wb = jnp.exp(b[2] - m)
    return wa * a[0] + wb * b[0], wa * a[1] + wb * b[1], m


def _attn_unit(q, kv, bias):
    nk = kv.shape[0]
    k = kv[:, :GROUP_WIDTH]
    v1 = jnp.concatenate([kv[:, GROUP_WIDTH:], jnp.ones((nk, GROUP_WIDTH), BF16)], axis=1)
    s = lax.dot_general(_stack_heads(q), k, _NT, preferred_element_type=F32) + bias
    mrow = jnp.max(s, axis=-1, keepdims=True)
    p = jnp.exp(s - mrow).astype(BF16)
    r = jnp.dot(p, v1, preferred_element_type=F32)
    return _unstack(r[:, :GROUP_WIDTH], r[:, GROUP_WIDTH:], mrow)


def _attn_p_kernel(qc0, qc1, qc2, kvc0, kvc1, kvc2, *rest, seq, n_cast):
    cast_src, y_ref, rest = rest[:n_cast], rest[n_cast], rest[n_cast + 1:]
    _cast_slabs(cast_src, rest[:n_cast])
    scratch = rest[n_cast:]
    parts = [scratch[3 * g:3 * g + 3] for g in range(N_DIL)]
    bias_ref = scratch[3 * N_DIL]
    acc_o, acc_l, acc_m = parts[0]
    @pl.when(pl.program_id(0) == 0)
    def _():
        qi = lax.broadcasted_iota(jnp.int32, (2 * ATTN_TILE, 2 * ATTN_TILE), 0) & (ATTN_TILE - 1)
        ki = lax.broadcasted_iota(jnp.int32, (2 * ATTN_TILE, 2 * ATTN_TILE), 1)
        for first_key_back in (0, 1):
            dist = qi - ki + first_key_back * ATTN_TILE
            bias_ref[first_key_back] = jnp.where((dist >= 0) & (dist <= N_BACK), 0.0, NEG_INF)

    for g, ((_, dil), q_ref, kv_ref) in enumerate(zip(DIL_PAIRS, (qc0, qc1, qc2), (kvc0, kvc1, kvc2))):
        dst = parts[g]
        n_sub = seq // dil // ATTN_TILE

        def unit(idx, q_ref=q_ref, kv_ref=kv_ref, n_sub=n_sub, dil=dil):
            if n_sub == 1:
                r, sub = idx, 0
                res = _attn_unit(q_ref[0, r], kv_ref[0, r], bias_ref[0, :, :ATTN_TILE])
            else:
                r, sub = (idx // n_sub, idx % n_sub) if dil > 1 else (0, idx)
                back = jnp.minimum(sub, 1)
                q0 = pl.multiple_of(sub * ATTN_TILE, ATTN_TILE)
                k0 = pl.multiple_of((sub - back) * ATTN_TILE, ATTN_TILE)
                res = _attn_unit(q_ref[0, r, pl.ds(q0, ATTN_TILE), :], kv_ref[0, r, pl.ds(k0, 2 * ATTN_TILE), :],
                                 bias_ref[back])
            start = sub * (ATTN_TILE * dil) + r
            rows = pl.ds(start, ATTN_TILE, stride=dil) if dil > 1 else pl.ds(pl.multiple_of(start, ATTN_TILE), ATTN_TILE)
            return rows, res

        def units_body(it, carry, unit=unit, dst=dst):
            for j in range(ATTN_UNROLL):
                rows, res = unit(it * ATTN_UNROLL + j)
                for ref, val in zip(dst, res):
                    ref[rows, :] = val
            return carry
        n_iter = dil * n_sub // ATTN_UNROLL
        if n_iter == 1:
            units_body(0, 0)
        else:
            lax.fori_loop(0, n_iter, units_body, 0)

        if g > 0:
            for c in range(seq // ATTN_MERGE_ROWS):
                rows = slice(c * ATTN_MERGE_ROWS, (c + 1) * ATTN_MERGE_ROWS)
                o, l, m = _merge((acc_o[rows, :], acc_l[rows, :], acc_m[rows, :]),
                                 tuple(ref[rows, :] for ref in parts[g]))
                if g == N_DIL - 1:
                    y_ref[0, rows, :] = (o / l).astype(y_ref.dtype)
                else:
                    acc_o[rows, :] = o
                    acc_l[rows, :] = l
                    acc_m[rows, :] = m


def _attn_p(qcs, kvcs, *, batch, seq, convert=None):
    in_specs = [pl.BlockSpec((1,) + a.shape[1:], lambda b: (b, 0, 0, 0)) for a in (*qcs, *kvcs)]
    cast_in, cast_out, cast_shape, cast_args = _cast_specs(convert, batch)
    acc = pltpu.VMEM((seq, GROUP_WIDTH), F32)
    return pl.pallas_call(
        functools.partial(_attn_p_kernel, seq=seq, n_cast=len(cast_args)),
        grid=(batch,),
        in_specs=in_specs + cast_in,
        out_specs=[pl.BlockSpec((1, seq, GROUP_WIDTH), lambda b: (b, 0, 0))] + cast_out,
        out_shape=[jax.ShapeDtypeStruct((batch, seq, GROUP_WIDTH), BF16)] + cast_shape,
        scratch_shapes=[acc] * (3 * N_DIL) + [pltpu.VMEM((2, 2 * ATTN_TILE, 2 * ATTN_TILE), F32)],
        compiler_params=_params(),
        name="attn_p",
    )(*qcs, *kvcs, *cast_args)


def _pool_rows(ext, pos, pw, ps):
    return jnp.dot(_pool_diff(ext, pos), pw, preferred_element_type=F32) * ps


def _pool_diff(ext, pos):
    s2 = ext + pltpu.roll(ext, 1, 0)
    s4 = s2 + pltpu.roll(s2, 2, 0)
    s8 = s4 + pltpu.roll(s4, 4, 0)
    s16 = s8 + pltpu.roll(s8, 8, 0)
    lane = lax.broadcasted_iota(jnp.int32, (1, POOL_WIDTH), 1)
    grp = [lane < (j + 1) * POOL_GROUP for j in range(3)]
    win = jnp.where(grp[0], s2, jnp.where(grp[1], s4, jnp.where(grp[2], s8, s16)))[HIST_ROWS:]
    width = jnp.where(grp[0], POOL_WINDOWS[0], jnp.where(grp[1], POOL_WINDOWS[1],
                      jnp.where(grp[2], POOL_WINDOWS[2], POOL_WINDOWS[3])))
    cnt = jnp.minimum(pos + 1, width).astype(F32)
    return (win / cnt - ext[HIST_ROWS:]).astype(BF16)


def _conv_rows(ext, gb, cw, hist):
    y = cw[0:1] * pltpu.roll(ext, 2, 0) + cw[1:2] * pltpu.roll(ext, 1, 0) + cw[2:3] * ext
    return gb * y[hist:]


def _attn_s_group(q, new_t, cache_t, window, dil):
    t = q.shape[0]
    q2 = _stack_heads(q).astype(BF16)
    s_c = jnp.dot(q2, cache_t[:GROUP_WIDTH].astype(BF16), preferred_element_type=F32)
    s_n = jnp.dot(q2, new_t[:GROUP_WIDTH].astype(BF16), preferred_element_type=F32)
    tq = lax.broadcasted_iota(jnp.int32, (2 * t, 1), 0) & (t - 1)
    rc = (lax.broadcasted_iota(jnp.int32, (1, window), 1) + t) & (window - 1)
    dist_c = window + tq - rc
    ok_c = (dist_c <= window) & ((dist_c & (dil - 1)) == 0) & (PAST_LEN - window + rc >= 0)
    tn = lax.broadcasted_iota(jnp.int32, (1, LANES), 1) - (LANES - t)
    dist_n = tq - tn
    ok_n = (tn >= 0) & (dist_n >= 0) & ((dist_n & (dil - 1)) == 0)
    s_c = jnp.where(ok_c, s_c, NEG_INF)
    s_n = jnp.where(ok_n, s_n, NEG_INF)
    mrow = jnp.maximum(jnp.max(s_c, axis=-1, keepdims=True), jnp.max(s_n, axis=-1, keepdims=True))
    p_c = jnp.exp(s_c - mrow)
    p_n = jnp.exp(s_n - mrow)
    o2 = (lax.dot_general(p_c.astype(BF16), cache_t[GROUP_WIDTH:].astype(BF16), _NT, preferred_element_type=F32)
          + lax.dot_general(p_n.astype(BF16), new_t[GROUP_WIDTH:].astype(BF16), _NT, preferred_element_type=F32))
    l2 = jnp.sum(p_c, axis=-1, keepdims=True) + jnp.sum(p_n, axis=-1, keepdims=True)
    return _unstack(o2, l2, mrow)


def _mix_s_kernel(u_ref, q0_ref, q1_ref, q2_ref, kn0_ref, kn1_ref, kn2_ref, gb_ref, z_ref, pst_ref, cst_ref,
                  pw_ref, ps_ref, cw_ref, c0_ref, c1_ref, c2_ref, yp_ref, ya_ref, yc_ref, nc0_ref, nc1_ref, nc2_ref,
                  *, t_new, n_per):
    keep_old = _lane_lt((2 * GROUP_WIDTH, LANES), LANES - t_new)
    pos = PAST_LEN + lax.broadcasted_iota(jnp.int32, (t_new, 1), 0)
    for i in range(n_per):
        state = None
        for (window, dil), q_ref, kn_ref, c_ref, nc_ref in zip(
                DIL_PAIRS, (q0_ref, q1_ref, q2_ref), (kn0_ref, kn1_ref, kn2_ref),
                (c0_ref, c1_ref, c2_ref), (nc0_ref, nc1_ref, nc2_ref)):
            kv_new = kn_ref[i]
            new_t = jnp.concatenate([jnp.zeros((LANES - t_new, 2 * GROUP_WIDTH), F32), kv_new], axis=0).T
            cache_t = c_ref[0, i]
            part = _attn_s_group(q_ref[i], new_t, cache_t, window, dil)
            state = part if state is None else _merge(state, part)
            nc_ref[0, i] = jnp.where(keep_old, cache_t[:, window - LANES:], new_t)
        ya_ref[i] = state[0] / state[1]

        u_ext = jnp.concatenate([pst_ref[i], u_ref[i]], axis=0)
        yp_ref[i] = _pool_rows(u_ext, pos, pw_ref[...], ps_ref[...])
        z_ext = jnp.concatenate([cst_ref[i], z_ref[i]], axis=0)
        yc_ref[i] = _conv_rows(z_ext, gb_ref[i], cw_ref[...], cst_ref.shape[1])


def _mix_s(u, qs, kns, gb, z, pst, cst, rotated_bufs, pw, ps, cw, *, layer, n_seq, t_new):
    n_per = MIX_S_SEQS
    blk = lambda r, c: pl.BlockSpec((n_per, r, c), lambda b: (b, 0, 0))
    three = lambda a: a.reshape(n_seq, -1, a.shape[-1])
    windows = [w for (w, _) in DIL_PAIRS]
    in_specs = ([blk(t_new, POOL_WIDTH)] + [blk(t_new, GROUP_WIDTH)] * N_DIL + [blk(t_new, 2 * GROUP_WIDTH)] * N_DIL
                + [blk(t_new, CONV_WIDTH)] * 2
                + [pl.BlockSpec((None, n_per) + a.shape[2:], lambda b: (layer, b, 0, 0)) for a in (pst, cst)]
                + [_layer_spec((POOL_WIDTH, POOL_WIDTH), layer), _layer_spec((1, POOL_WIDTH), layer),
                   _layer_spec((CONV_K, CONV_WIDTH), layer)]
                + [pl.BlockSpec((1, n_per, 2 * GROUP_WIDTH, w), lambda b: (layer, b, 0, 0)) for w in windows])
    args = [three(u), *[three(q) for q in qs], *[three(k) for k in kns], three(gb), three(z), pst, cst, pw, ps, cw,
            *rotated_bufs]
    n_in = len(args)
    out_specs = ([blk(t_new, POOL_WIDTH), blk(t_new, GROUP_WIDTH), blk(t_new, CONV_WIDTH)]
                 + [pl.BlockSpec((1, n_per, 2 * GROUP_WIDTH, LANES), lambda b, w=w: (layer, b, 0, w // LANES - 1))
                    for w in windows])
    out_shape = ([jax.ShapeDtypeStruct((n_seq, t_new, c), F32) for c in (POOL_WIDTH, GROUP_WIDTH, CONV_WIDTH)]
                 + [jax.ShapeDtypeStruct(c.shape, F32) for c in rotated_bufs])
    return pl.pallas_call(
        functools.partial(_mix_s_kernel, t_new=t_new, n_per=n_per),
        grid=(n_seq // n_per,),
        in_specs=in_specs,
        out_specs=out_specs,
        out_shape=out_shape,
        input_output_aliases={n_in - N_DIL + g: 3 + g for g in range(N_DIL)},
        compiler_params=_params(),
        name="mix_s",
    )(*args)


def _mlp_head(x, yp, ya, yc, wo_ref, g2_ref):
    mixed = jnp.concatenate([yp.astype(BF16), ya.astype(BF16), yc.astype(BF16)], axis=1)
    x1 = x + jnp.dot(mixed, wo_ref[...], preferred_element_type=F32)
    ms = jnp.mean(x1 * x1, axis=-1, keepdims=True)
    return x1, (x1 * lax.rsqrt(ms + EPS) * g2_ref[...]).astype(BF16)


def _mlp_tail(x1, hb, wu_ref, wd_ref, tf):
    acc = x1
    for c in range(D_FF // tf):
        hf = jnp.dot(hb, wu_ref[:, c * tf:(c + 1) * tf], preferred_element_type=F32)
        act = jnp.square(jnp.maximum(hf, 0.0)).astype(BF16)
        acc = acc + jnp.dot(act, wd_ref[c * tf:(c + 1) * tf, :], preferred_element_type=F32)
    return acc


def _mlp_s_kernel(x_ref, yp_ref, ya_ref, yc_ref, wo_ref, g2_ref, wu_ref, wd_ref, o_ref, hb_buf):
    @pl.when(pl.program_id(0) == 0)
    def _():
        x1, hb = _mlp_head(x_ref[...], yp_ref[...], ya_ref[...], yc_ref[...], wo_ref, g2_ref)
        o_ref[...] = x1
        hb_buf[...] = hb

    hf = jnp.dot(hb_buf[...], wu_ref[...], preferred_element_type=F32)
    act = jnp.square(jnp.maximum(hf, 0.0)).astype(BF16)
    o_ref[...] += jnp.dot(act, wd_ref[...], preferred_element_type=F32)


def _mlp_p_kernel(x_ref, u_ref, uh_ref, gb_ref, z_ref, zh_ref, ya_ref, pw_ref, ps_ref, cw_ref,
                  wo_ref, g2_ref, wu_ref, wd_ref, *rest, tf, tiles, n_cast, n_roll, roll_by):
    cast_src, roll_src, o_ref = rest[:n_cast], rest[n_cast:n_cast + 2 * n_roll], rest[n_cast + 2 * n_roll]
    outs = rest[n_cast + 2 * n_roll + 1:]
    _cast_slabs(cast_src, outs[:n_cast])
    _roll_buffers(roll_src, outs[n_cast:], roll_by)
    tm = x_ref.shape[0]
    tile = pl.program_id(0) % tiles
    has_hist = tile > 0
    pos = tile * tm + lax.broadcasted_iota(jnp.int32, (tm, 1), 0)
    yp = _pool_rows(jnp.concatenate([jnp.where(has_hist, uh_ref[...], 0.0), u_ref[...]], axis=0), pos,
                    pw_ref[...], ps_ref[...])
    yc = _conv_rows(jnp.concatenate([jnp.where(has_hist, zh_ref[...], 0.0), z_ref[...]], axis=0),
                    gb_ref[...], cw_ref[...], HIST_ROWS)
    x1, hb = _mlp_head(x_ref[...], yp, ya_ref[...], yc, wo_ref, g2_ref)
    o_ref[...] = _mlp_tail(x1, hb, wu_ref, wd_ref, tf)


def _mlp_s(x, yp, ya, yc, w_out, g2, w_up, w_down, *, layer):
    rows = x.shape[0]
    tf = MLP_FF_CHUNK
    full = lambda c: pl.BlockSpec((rows, c), lambda j: (0, 0))
    return pl.pallas_call(
        _mlp_s_kernel,
        grid=(D_FF // tf,),
        in_specs=[full(D_MODEL), full(POOL_WIDTH), full(GROUP_WIDTH), full(CONV_WIDTH),
                  _layer_spec((MIX_OUT, D_MODEL), layer), _layer_spec((1, D_MODEL), layer),
                  pl.BlockSpec((D_MODEL, tf), lambda j: (0, j)),
                  pl.BlockSpec((tf, D_MODEL), lambda j: (j, 0))],
        out_specs=full(D_MODEL),
        out_shape=jax.ShapeDtypeStruct((rows, D_MODEL), F32),
        scratch_shapes=[pltpu.VMEM((rows, D_MODEL), BF16)],
        compiler_params=_params(),
        name="mlp_s",
    )(x, yp, ya, yc, w_out, g2, w_up, w_down)


def _mlp_p(x, u, gb, z, ya, pw, ps, cw, w_out, g2, w_up, w_down, *, layer, tm, seq, convert=None, roll=(),
           roll_by=0):
    rows = x.shape[0]
    row = lambda c: pl.BlockSpec((tm, c), lambda i: (i, 0))
    conv_in, conv_out_specs, conv_out_shape, conv_args = _cast_specs(convert, rows // tm)
    roll_in, roll_out, roll_shape, roll_args = _roll_specs(roll, rows // tm)
    hist = lambda c: pl.BlockSpec((HIST_ROWS, c), lambda i: (jnp.maximum(i * (tm // HIST_ROWS) - 1, 0), 0))
    return pl.pallas_call(
        functools.partial(_mlp_p_kernel, tf=MLP_FF_CHUNK, tiles=seq // tm, n_cast=len(conv_args),
                          n_roll=len(roll_out), roll_by=roll_by),
        grid=(rows // tm,),
        in_specs=[row(D_MODEL), row(POOL_WIDTH), hist(POOL_WIDTH), row(CONV_WIDTH), row(CONV_WIDTH),
                  hist(CONV_WIDTH), row(GROUP_WIDTH), _layer_spec((POOL_WIDTH, POOL_WIDTH), layer),
                  _layer_spec((1, POOL_WIDTH), layer), _layer_spec((CONV_K, CONV_WIDTH), layer),
                  _layer_spec((MIX_OUT, D_MODEL), layer), _layer_spec((1, D_MODEL), layer),
                  _const_spec((D_MODEL, D_FF)), _const_spec((D_FF, D_MODEL))] + conv_in + roll_in,
        out_specs=[row(D_MODEL)] + conv_out_specs + roll_out,
        out_shape=[jax.ShapeDtypeStruct((rows, D_MODEL), F32)] + conv_out_shape + roll_shape,
        compiler_params=_params(),
        name="mlp_p",
    )(x, u, u, gb, z, z, ya, pw, ps, cw, w_out, g2, w_up, w_down, *conv_args, *roll_args)


def _rope_tables(pos):
    half = ROPE_DIM // 2
    inv = jnp.power(jnp.float32(ROPE_THETA), -jnp.arange(half, dtype=F32) / half)
    ang = pos.astype(F32)[:, None] * inv[None, :]
    cos, sin = jnp.cos(ang), jnp.sin(ang)
    n = pos.shape[0]
    rest = HEAD_DIM - ROPE_DIM
    zh = jnp.zeros((n, half), F32)
    c = jnp.concatenate([cos, cos, jnp.ones((n, rest), F32)], axis=1)
    a = jnp.concatenate([-sin, zh, jnp.zeros((n, rest), F32)], axis=1)
    b = jnp.concatenate([zh, sin, jnp.zeros((n, rest), F32)], axis=1)
    return tuple(jnp.tile(t, (1, GROUP_WIDTH // HEAD_DIM)) for t in (c, a, b))


def _to_buffer_layout(c):
    lead = c.shape[:-4]
    n = len(lead)
    t = jnp.transpose(c, (*range(n), n + 1, n + 2, n + 3, n))
    return t.reshape(*lead, 2 * GROUP_WIDTH, c.shape[-4])


def _from_buffer_layout(t):
    lead = t.shape[:-2]
    n = len(lead)
    c = t.reshape(*lead, 2, 2, HEAD_DIM, t.shape[-1])
    return jnp.transpose(c, (*range(n), n + 3, n, n + 1, n + 2))


def kernel(x_prompt, x_sample, state_pool, state_conv, cache_kv_w128, cache_kv_w512, cache_kv_w2048,
           norm1_g, w_in, q_norm_g, k_norm_g, pool_w, pool_scale, conv_w, w_out, norm2_g, w_up, w_down):
    batch, seq, _ = x_prompt.shape
    n_seq, t_new, _ = x_sample.shape
    depth = w_in.shape[0]
    assert depth == 2, "the in-place stacking of the kv state outputs is written for two layers"
    caches_t = [_to_buffer_layout(c) for c in (cache_kv_w128, cache_kv_w512, cache_kv_w2048)]

    rope_p = _rope_tables(jnp.arange(seq, dtype=jnp.int32))
    rope_s = _rope_tables(jnp.tile(PAST_LEN + jnp.arange(t_new, dtype=jnp.int32), n_seq))
    head_id = jnp.arange(LANES) // HEAD_DIM
    hm = jnp.where(head_id[:, None] == head_id[None, :], 1.0 / HEAD_DIM, 0.0).astype(BF16)
    g1 = norm1_g[:, None, :]
    g2 = norm2_g[:, None, :]
    ps = pool_scale[:, None, :]
    cw = conv_w
    two_heads = lambda gain: jnp.tile(gain, (1, GROUP_WIDTH // HEAD_DIM))[:, None, :]
    qg = two_heads(q_norm_g) * (HEAD_DIM ** -0.5)
    kg = two_heads(k_norm_g)
    n_win = len(POOL_WINDOWS)
    pw = jnp.einsum("gh,lgij->lgihj", jnp.eye(n_win, dtype=F32), pool_w).reshape(depth, POOL_WIDTH, POOL_WIDTH)
    pw = pw.astype(BF16)
    w_out_b = w_out.astype(BF16)
    pst = jnp.pad(state_pool, ((0, 0), (0, 0), (HIST_ROWS - POOL_HIST, 0), (0, 0)))
    cst = jnp.pad(state_conv, ((0, 0), (0, 0), (SUBLANES - (CONV_K - 1), 0), (0, 0)))

    xp = x_prompt.reshape(batch * seq, D_MODEL)
    xs = x_sample.reshape(n_seq * t_new, D_MODEL)
    outs = {k: [] for k in ("pool_p", "conv_p", "pool_s", "conv_s")}
    w_in_l = w_in[0].astype(BF16)
    new_caches = [None] * N_DIL
    kvts = None
    for layer in range(depth):
        if layer == 0:
            (u, gb, z, qc0, qc1, qc2, kvc0, kvc1, kvc2, *tail) = _proj(
                xp, g1, w_in_l, qg, kg, rope_p, hm, layer=layer, tm=PROJ_TILE, sub=PROJ_SUB, batch=batch,
                seq=seq, roll=caches_t[:-1], roll_by=t_new)
            kvts, new_caches[:-1] = tail[:N_DIL], tail[N_DIL:]
            ya, w_up_l, w_down_l = _attn_p((qc0, qc1, qc2), (kvc0, kvc1, kvc2), batch=batch, seq=seq,
                                           convert=(0, (w_up, w_down)))
            xp, w_in_next, new_caches[-1] = _mlp_p(
                xp, u, gb, z, ya.reshape(-1, GROUP_WIDTH), pw, ps, cw, w_out_b, g2, w_up_l, w_down_l, layer=layer,
                tm=MLP_TILE, seq=seq, convert=(1, (w_in,)), roll=caches_t[-1:], roll_by=t_new)
        else:
            (u, gb, z, qc0, qc1, qc2, kvc0, kvc1, kvc2, *tail) = _proj(
                xp, g1, w_in_l, qg, kg, rope_p, hm, layer=layer, tm=PROJ_TILE, sub=PROJ_SUB, batch=batch,
                seq=seq, prev_kvt=kvts, convert=(1, (w_up, w_down)))
            kvts, (w_up_l, w_down_l) = tail[:N_DIL], tail[N_DIL:]
            (ya,) = _attn_p((qc0, qc1, qc2), (kvc0, kvc1, kvc2), batch=batch, seq=seq)
            (xp,) = _mlp_p(xp, u, gb, z, ya.reshape(-1, GROUP_WIDTH), pw, ps, cw, w_out_b, g2, w_up_l, w_down_l,
                           layer=layer, tm=MLP_TILE, seq=seq)
        outs["pool_p"].append(u.reshape(batch, seq, POOL_WIDTH)[:, seq - POOL_HIST:])
        outs["conv_p"].append(z.reshape(batch, seq, CONV_WIDTH)[:, seq - (CONV_K - 1):])

        us, gbs, zs, q0, q1, q2, kn0, kn1, kn2 = _proj(xs, g1, w_in_l, qg, kg, rope_s, hm, layer=layer,
                                                       tm=n_seq * t_new, sub=n_seq * t_new)
        yps, yas, ycs, *new_caches = _mix_s(us, (q0, q1, q2), (kn0, kn1, kn2), gbs, zs, pst, cst, new_caches,
                                           pw, ps, cw, layer=layer, n_seq=n_seq, t_new=t_new)
        xs = _mlp_s(xs, yps.reshape(-1, POOL_WIDTH), yas.reshape(-1, GROUP_WIDTH), ycs.reshape(-1, CONV_WIDTH),
                    w_out_b, g2, w_up_l, w_down_l, layer=layer)
        us3 = us.reshape(n_seq, t_new, POOL_WIDTH)
        zs3 = zs.reshape(n_seq, t_new, CONV_WIDTH)
        outs["pool_s"].append(jnp.concatenate([state_pool[layer], us3], axis=1)[:, -POOL_HIST:])
        outs["conv_s"].append(jnp.concatenate([state_conv[layer], zs3], axis=1)[:, -(CONV_K - 1):])
        if layer == 0:
            w_in_l = w_in_next

    st = lambda k: jnp.stack(outs[k])
    return (xp.reshape(batch, seq, D_MODEL), xs.reshape(n_seq, t_new, D_MODEL),
            st("pool_p"), st("conv_p"),
            *[_from_buffer_layout(kvt) for kvt in kvts],
            st("pool_s"), st("conv_s"),
            *[_from_buffer_layout(c) for c in new_caches])
```

```python
import functools

import jax
import jax.numpy as jnp
from jax import lax
from jax.experimental import pallas as pl
from jax.experimental.pallas import tpu as pltpu

D_MODEL = 1024
HEAD_DIM = 64
POOL_WIDTH = 256
POOL_WINDOWS = (2, 4, 8, 16)
POOL_GROUP = 64
POOL_HIST = 15
ATTN_WIDTH = 384
DIL_PAIRS = ((128, 1), (512, 4), (2048, 16))
N_DIL = 3
GROUP_WIDTH = 128
CONV_WIDTH = 384
CONV_K = 3
ROPE_DIM = 16
ROPE_THETA = 500000.0
D_FF = 4096
IN_COLS = 2560
MIX_OUT = 768
EPS = 1e-6
NEG_INF = -1e30
PAST_LEN = 8192
N_BACK = 128

LANES = 128
SUBLANES = 8
HIST_ROWS = 16
ATTN_TILE = 128
ATTN_UNROLL = 16
ATTN_MERGE_ROWS = 256
PROJ_TILE = 1024
PROJ_SUB = 512
MLP_TILE = 512
MLP_FF_CHUNK = 512
MIX_S_SEQS = 8
VMEM_LIMIT = 56 * 1024 * 1024

F32 = jnp.float32
BF16 = jnp.bfloat16

_Q_OFF = POOL_WIDTH
_GB_OFF = _Q_OFF + 3 * ATTN_WIDTH

_NT = (((1,), (1,)), ((), ()))


def _const_spec(shape):
    return pl.BlockSpec(shape, lambda *_: (0,) * len(shape), pipeline_mode=pl.Buffered(1))


def _layer_spec(shape, layer):
    return pl.BlockSpec((None, *shape), lambda *_: (layer,) + (0,) * len(shape), pipeline_mode=pl.Buffered(1))


def _params():
    return pltpu.CompilerParams(dimension_semantics=("arbitrary",), vmem_limit_bytes=VMEM_LIMIT)


def _cast_specs(convert, n_steps):
    in_specs, out_specs, out_shape, args = [], [], [], []
    if convert is not None:
        layer, weights = convert
        for w in weights:
            _, k, n = w.shape
            in_specs.append(pl.BlockSpec((None, k // n_steps, n), lambda i: (layer, i, 0)))
            out_specs.append(pl.BlockSpec((k // n_steps, n), lambda i: (i, 0)))
            out_shape.append(jax.ShapeDtypeStruct((k, n), BF16))
            args.append(w)
    return in_specs, out_specs, out_shape, args


def _cast_slabs(srcs, dsts):
    for src, dst in zip(srcs, dsts):
        dst[...] = src[...].astype(BF16)


def _roll_specs(buffers, n_steps):
    in_specs, out_specs, out_shape, args = [], [], [], []
    for cache_t in buffers:
        depth, n_seq, width, window = cache_t.shape
        assert n_seq % n_steps == 0
        per = n_seq // n_steps
        in_specs += [pl.BlockSpec((1, per, width, window), lambda i, l=l: (l, i, 0, 0)) for l in range(depth)]
        args += [cache_t] * depth
        out_specs.append(pl.BlockSpec((depth, per, width, window), lambda i: (0, i, 0, 0)))
        out_shape.append(jax.ShapeDtypeStruct(cache_t.shape, F32))
    return in_specs, out_specs, out_shape, args


def _roll_buffers(srcs, dsts, by):
    srcs = iter(srcs)
    for dst in dsts:
        for layer in range(dst.shape[0]):
            src = next(srcs)
            for i in range(src.shape[1]):
                dst[layer, i] = pltpu.roll(src[0, i], src.shape[-1] - by, 1)


def _proj_kernel(x_ref, g1_ref, w_ref, qg_ref, kg_ref, rc_ref, ra_ref, rb_ref, hm_ref, *rest,
                 class_major, tiles, sub, stacked, n_cast, n_roll, roll_by):
    prev_kvt = None
    if class_major:
        if stacked:
            prev_kvt, rest = rest[:N_DIL], rest[N_DIL:]
        cast_src, rest = rest[:n_cast], rest[n_cast:]
        roll_src, rest = rest[:2 * n_roll], rest[2 * n_roll:]
        (u_ref, gb_ref, z_ref, qc0, qc1, qc2, kvc0, kvc1, kvc2, kvt0, kvt1, kvt2) = rest[:12]
        _cast_slabs(cast_src, rest[12:12 + n_cast])
        _roll_buffers(roll_src, rest[12 + n_cast:12 + n_cast + n_roll], roll_by)
        nat = rest[-1]
    else:
        (u_ref, gb_ref, z_ref, q0, q1, q2, kv0, kv1, kv2) = rest
    tm = x_ref.shape[0]
    hm = hm_ref[...]

    for s in range(tm // sub):
        r0 = s * sub
        rows = slice(r0, r0 + sub)
        x = x_ref[rows, :]
        ms = jnp.mean(x * x, axis=-1, keepdims=True)
        hb = (x * lax.rsqrt(ms + EPS) * g1_ref[...]).astype(BF16)
        p_att = jnp.dot(hb, w_ref[:, _Q_OFF:_GB_OFF], preferred_element_type=F32)
        rc, ra, rb = rc_ref[rows, :], ra_ref[rows, :], rb_ref[rows, :]

        def head_norm_rope(xb, gain):
            msq = jnp.dot((xb * xb).astype(BF16), hm, preferred_element_type=F32)
            xn = xb * lax.rsqrt(msq + EPS) * gain
            return xn * rc + pltpu.roll(xn, LANES - ROPE_DIM // 2, 1) * ra + pltpu.roll(xn, ROPE_DIM // 2, 1) * rb

        for g, (window, dil) in enumerate(DIL_PAIRS):
            lanes = slice(g * GROUP_WIDTH, (g + 1) * GROUP_WIDTH)
            qn = head_norm_rope(p_att[:, :ATTN_WIDTH][:, lanes], qg_ref[...])
            kn = head_norm_rope(p_att[:, ATTN_WIDTH:2 * ATTN_WIDTH][:, lanes], kg_ref[...])
            vv = p_att[:, 2 * ATTN_WIDTH:][:, lanes]
            if not class_major:
                q_ref, kv_ref = ((q0, kv0), (q1, kv1), (q2, kv2))[g]
                q_ref[rows, :] = qn
                kv_ref[rows, :GROUP_WIDTH] = kn
                kv_ref[rows, GROUP_WIDTH:] = vv
                continue
            qc_ref, kvc_ref, kvt_ref = ((qc0, kvc0, kvt0), (qc1, kvc1, kvt1), (qc2, kvc2, kvt2))[g]
            slab = 3 * (s * N_DIL + g)
            nat[slab] = qn
            nat[slab + 1] = kn
            nat[slab + 2] = vv
            per = sub // dil
            dst = slice(s * per, (s + 1) * per)
            for r in range(dil):
                src = pl.ds(r, per, stride=dil) if dil > 1 else slice(None)
                qc_ref[0, r, dst, :] = nat[slab, src, :].astype(BF16)
                kvc_ref[0, r, dst, :GROUP_WIDTH] = nat[slab + 1, src, :].astype(BF16)
                kvc_ref[0, r, dst, GROUP_WIDTH:] = nat[slab + 2, src, :].astype(BF16)
            def put_kvt(cols, part, g=g, kvt_ref=kvt_ref):
                if stacked:
                    kvt_ref[0, 0, :, cols] = prev_kvt[g][0, :, cols]
                    kvt_ref[1, 0, :, cols] = part
                else:
                    kvt_ref[0, :, cols] = part

            if window >= tiles * tm:
                put_kvt(rows, jnp.concatenate([kn, vv], axis=1).T)
            else:
                first_kept = tm - min(window, tm)
                lo = max(r0, first_kept)
                if lo < r0 + sub:
                    part = jnp.concatenate([kn[lo - r0:], vv[lo - r0:]], axis=1).T
                    put_kvt(slice(lo - first_kept, r0 + sub - first_kept), part)

        u_ref[rows, :] = jnp.dot(hb, w_ref[:, :POOL_WIDTH], preferred_element_type=F32)
        p_gate = jnp.dot(hb, w_ref[:, _GB_OFF:], preferred_element_type=F32)
        gb_ref[rows, :] = p_gate[:, :CONV_WIDTH]
        z_ref[rows, :] = p_gate[:, CONV_WIDTH:2 * CONV_WIDTH] * p_gate[:, 2 * CONV_WIDTH:]


def _proj(x, g1, w_in, qg, kg, rope, hm, *, layer, tm, sub, batch=None, seq=None, prev_kvt=None, convert=None,
          roll=(), roll_by=0):
    rows = x.shape[0]
    stacked = prev_kvt is not None
    n_tab = rope[0].shape[0] // tm
    class_major = batch is not None
    tiles = seq // tm if class_major else 1
    row = lambda c: pl.BlockSpec((tm, c), lambda i: (i, 0))
    tab = pl.BlockSpec((tm, LANES), lambda i: (i % n_tab, 0))
    out_shape = [jax.ShapeDtypeStruct((rows, POOL_WIDTH), F32)] + [jax.ShapeDtypeStruct((rows, CONV_WIDTH), F32)] * 2
    out_specs = [row(POOL_WIDTH), row(CONV_WIDTH), row(CONV_WIDTH)]
    scratch = []
    prev_specs = []
    cast_in, cast_out, cast_shape, cast_args = _cast_specs(convert, rows // tm)
    assert class_major or (convert is None and not roll)
    roll_in, roll_out, roll_shape, roll_args = _roll_specs(roll, rows // tm)
    if class_major:
        for width in (GROUP_WIDTH, 2 * GROUP_WIDTH):
            for (_, dil) in DIL_PAIRS:
                out_shape.append(jax.ShapeDtypeStruct((batch, dil, seq // dil, width), BF16))
                out_specs.append(pl.BlockSpec((1, dil, tm // dil, width), lambda i: (i // tiles, 0, i % tiles, 0)))
        for (window, _) in DIL_PAIRS:
            assert window >= seq or window <= tm, "kv buffer rows must come from the last row tile"
            cols = tm if window >= seq else window
            col_blk = (lambda i: i % tiles) if window >= seq else (lambda i: 0)
            kvt_spec = pl.BlockSpec((1, 2 * GROUP_WIDTH, cols), lambda i, c=col_blk: (i // tiles, 0, c(i)))
            if stacked:
                prev_specs.append(kvt_spec)
                out_shape.append(jax.ShapeDtypeStruct((2, batch, 2 * GROUP_WIDTH, window), F32))
                out_specs.append(pl.BlockSpec((2, 1, 2 * GROUP_WIDTH, cols),
                                              lambda i, c=col_blk: (0, i // tiles, 0, c(i))))
            else:
                out_shape.append(jax.ShapeDtypeStruct((batch, 2 * GROUP_WIDTH, window), F32))
                out_specs.append(kvt_spec)
        scratch = [pltpu.VMEM((3 * N_DIL * (tm // sub), sub, GROUP_WIDTH), F32)]
    else:
        out_shape += [jax.ShapeDtypeStruct((rows, GROUP_WIDTH), F32)] * N_DIL
        out_shape += [jax.ShapeDtypeStruct((rows, 2 * GROUP_WIDTH), F32)] * N_DIL
        out_specs += [row(GROUP_WIDTH)] * N_DIL + [row(2 * GROUP_WIDTH)] * N_DIL
    return pl.pallas_call(
        functools.partial(_proj_kernel, class_major=class_major, tiles=tiles, sub=sub, stacked=stacked,
                          n_cast=len(cast_args), n_roll=len(roll_out), roll_by=roll_by),
        grid=(rows // tm,),
        in_specs=[row(D_MODEL), _layer_spec((1, D_MODEL), layer), _const_spec((D_MODEL, IN_COLS)),
                  _layer_spec((1, LANES), layer), _layer_spec((1, LANES), layer), tab, tab, tab,
                  _const_spec((LANES, LANES))] + prev_specs + cast_in + roll_in,
        out_specs=out_specs + cast_out + roll_out,
        out_shape=out_shape + cast_shape + roll_shape,
        scratch_shapes=scratch,
        compiler_params=_params(),
        name="proj",
    )(x, g1, w_in, qg, kg, *rope, hm, *(prev_kvt or ()), *cast_args, *roll_args)


def _lane_lt(shape, bound):
    return lax.broadcasted_iota(jnp.int32, shape, len(shape) - 1) < bound


def _stack_heads(q):
    first = _lane_lt(q.shape, HEAD_DIM)
    zero = jnp.zeros_like(q)
    return jnp.concatenate([jnp.where(first, q, zero), jnp.where(first, zero, q)], axis=0)


def _unstack(o2, l2, m2):
    n = o2.shape[0] // 2
    first = _lane_lt((n, GROUP_WIDTH), HEAD_DIM)
    return tuple(jnp.where(first, a[:n], a[n:]) for a in (o2, l2, m2))


def _merge(a, b):
    m = jnp.maximum(a[2], b[2])
    wa = jnp.exp(a[2] - m)
    wb = jnp.exp(b[2] - m)
    return wa * a[0] + wb * b[0], wa * a[1] + wb * b[1], m


def _attn_unit(q, kv, bias):
    nk = kv.shape[0]
    k = kv[:, :GROUP_WIDTH]
    v1 = jnp.concatenate([kv[:, GROUP_WIDTH:], jnp.ones((nk, GROUP_WIDTH), BF16)], axis=1)
    s = lax.dot_general(_stack_heads(q), k, _NT, preferred_element_type=F32) + bias
    mrow = jnp.max(s, axis=-1, keepdims=True)
    p = jnp.exp(s - mrow).astype(BF16)
    r = jnp.dot(p, v1, preferred_element_type=F32)
    return _unstack(r[:, :GROUP_WIDTH], r[:, GROUP_WIDTH:], mrow)


def _attn_p_kernel(qc0, qc1, qc2, kvc0, kvc1, kvc2, *rest, seq, n_cast):
    cast_src, y_ref, rest = rest[:n_cast], rest[n_cast], rest[n_cast + 1:]
    _cast_slabs(cast_src, rest[:n_cast])
    scratch = rest[n_cast:]
    parts = [scratch[3 * g:3 * g + 3] for g in range(N_DIL)]
    bias_ref = scratch[3 * N_DIL]
    acc_o, acc_l, acc_m = parts[0]
    @pl.when(pl.program_id(0) == 0)
    def _():
        qi = lax.broadcasted_iota(jnp.int32, (2 * ATTN_TILE, 2 * ATTN_TILE), 0) & (ATTN_TILE - 1)
        ki = lax.broadcasted_iota(jnp.int32, (2 * ATTN_TILE, 2 * ATTN_TILE), 1)
        for first_key_back in (0, 1):
            dist = qi - ki + first_key_back * ATTN_TILE
            bias_ref[first_key_back] = jnp.where((dist >= 0) & (dist <= N_BACK), 0.0, NEG_INF)

    for g, ((_, dil), q_ref, kv_ref) in enumerate(zip(DIL_PAIRS, (qc0, qc1, qc2), (kvc0, kvc1, kvc2))):
        dst = parts[g]
        n_sub = seq // dil // ATTN_TILE

        def unit(idx, q_ref=q_ref, kv_ref=kv_ref, n_sub=n_sub, dil=dil):
            if n_sub == 1:
                r, sub = idx, 0
                res = _attn_unit(q_ref[0, r], kv_ref[0, r], bias_ref[0, :, :ATTN_TILE])
            else:
                r, sub = (idx // n_sub, idx % n_sub) if dil > 1 else (0, idx)
                back = jnp.minimum(sub, 1)
                q0 = pl.multiple_of(sub * ATTN_TILE, ATTN_TILE)
                k0 = pl.multiple_of((sub - back) * ATTN_TILE, ATTN_TILE)
                res = _attn_unit(q_ref[0, r, pl.ds(q0, ATTN_TILE), :], kv_ref[0, r, pl.ds(k0, 2 * ATTN_TILE), :],
                                 bias_ref[back])
            start = sub * (ATTN_TILE * dil) + r
            rows = pl.ds(start, ATTN_TILE, stride=dil) if dil > 1 else pl.ds(pl.multiple_of(start, ATTN_TILE), ATTN_TILE)
            return rows, res

        def units_body(it, carry, unit=unit, dst=dst):
            for j in range(ATTN_UNROLL):
                rows, res = unit(it * ATTN_UNROLL + j)
                for ref, val in zip(dst, res):
                    ref[rows, :] = val
            return carry
        n_iter = dil * n_sub // ATTN_UNROLL
        if n_iter == 1:
            units_body(0, 0)
        else:
            lax.fori_loop(0, n_iter, units_body, 0)

        if g > 0:
            for c in range(seq // ATTN_MERGE_ROWS):
                rows = slice(c * ATTN_MERGE_ROWS, (c + 1) * ATTN_MERGE_ROWS)
                o, l, m = _merge((acc_o[rows, :], acc_l[rows, :], acc_m[rows, :]),
                                 tuple(ref[rows, :] for ref in parts[g]))
                if g == N_DIL - 1:
                    y_ref[0, rows, :] = (o / l).astype(y_ref.dtype)
                else:
                    acc_o[rows, :] = o
                    acc_l[rows, :] = l
                    acc_m[rows, :] = m


def _attn_p(qcs, kvcs, *, batch, seq, convert=None):
    in_specs = [pl.BlockSpec((1,) + a.shape[1:], lambda b: (b, 0, 0, 0)) for a in (*qcs, *kvcs)]
    cast_in, cast_out, cast_shape, cast_args = _cast_specs(convert, batch)
    acc = pltpu.VMEM((seq, GROUP_WIDTH), F32)
    return pl.pallas_call(
        functools.partial(_attn_p_kernel, seq=seq, n_cast=len(cast_args)),
        grid=(batch,),
        in_specs=in_specs + cast_in,
        out_specs=[pl.BlockSpec((1, seq, GROUP_WIDTH), lambda b: (b, 0, 0))] + cast_out,
        out_shape=[jax.ShapeDtypeStruct((batch, seq, GROUP_WIDTH), BF16)] + cast_shape,
        scratch_shapes=[acc] * (3 * N_DIL) + [pltpu.VMEM((2, 2 * ATTN_TILE, 2 * ATTN_TILE), F32)],
        compiler_params=_params(),
        name="attn_p",
    )(*qcs, *kvcs, *cast_args)


def _pool_rows(ext, pos, pw, ps):
    return jnp.dot(_pool_diff(ext, pos), pw, preferred_element_type=F32) * ps


def _pool_diff(ext, pos):
    s2 = ext + pltpu.roll(ext, 1, 0)
    s4 = s2 + pltpu.roll(s2, 2, 0)
    s8 = s4 + pltpu.roll(s4, 4, 0)
    s16 = s8 + pltpu.roll(s8, 8, 0)
    lane = lax.broadcasted_iota(jnp.int32, (1, POOL_WIDTH), 1)
    grp = [lane < (j + 1) * POOL_GROUP for j in range(3)]
    win = jnp.where(grp[0], s2, jnp.where(grp[1], s4, jnp.where(grp[2], s8, s16)))[HIST_ROWS:]
    width = jnp.where(grp[0], POOL_WINDOWS[0], jnp.where(grp[1], POOL_WINDOWS[1],
                      jnp.where(grp[2], POOL_WINDOWS[2], POOL_WINDOWS[3])))
    cnt = jnp.minimum(pos + 1, width).astype(F32)
    return (win / cnt - ext[HIST_ROWS:]).astype(BF16)


def _conv_rows(ext, gb, cw, hist):
    y = cw[0:1] * pltpu.roll(ext, 2, 0) + cw[1:2] * pltpu.roll(ext, 1, 0) + cw[2:3] * ext
    return gb * y[hist:]


def _attn_s_group(q, new_t, cache_t, window, dil):
    t = q.shape[0]
    q2 = _stack_heads(q).astype(BF16)
    s_c = jnp.dot(q2, cache_t[:GROUP_WIDTH].astype(BF16), preferred_element_type=F32)
    s_n = jnp.dot(q2, new_t[:GROUP_WIDTH].astype(BF16), preferred_element_type=F32)
    tq = lax.broadcasted_iota(jnp.int32, (2 * t, 1), 0) & (t - 1)
    rc = (lax.broadcasted_iota(jnp.int32, (1, window), 1) + t) & (window - 1)
    dist_c = window + tq - rc
    ok_c = (dist_c <= window) & ((dist_c & (dil - 1)) == 0) & (PAST_LEN - window + rc >= 0)
    tn = lax.broadcasted_iota(jnp.int32, (1, LANES), 1) - (LANES - t)
    dist_n = tq - tn
    ok_n = (tn >= 0) & (dist_n >= 0) & ((dist_n & (dil - 1)) == 0)
    s_c = jnp.where(ok_c, s_c, NEG_INF)
    s_n = jnp.where(ok_n, s_n, NEG_INF)
    mrow = jnp.maximum(jnp.max(s_c, axis=-1, keepdims=True), jnp.max(s_n, axis=-1, keepdims=True))
    p_c = jnp.exp(s_c - mrow)
    p_n = jnp.exp(s_n - mrow)
    o2 = (lax.dot_general(p_c.astype(BF16), cache_t[GROUP_WIDTH:].astype(BF16), _NT, preferred_element_type=F32)
          + lax.dot_general(p_n.astype(BF16), new_t[GROUP_WIDTH:].astype(BF16), _NT, preferred_element_type=F32))
    l2 = jnp.sum(p_c, axis=-1, keepdims=True) + jnp.sum(p_n, axis=-1, keepdims=True)
    return _unstack(o2, l2, mrow)


def _mix_s_kernel(u_ref, q0_ref, q1_ref, q2_ref, kn0_ref, kn1_ref, kn2_ref, gb_ref, z_ref, pst_ref, cst_ref,
                  pw_ref, ps_ref, cw_ref, c0_ref, c1_ref, c2_ref, yp_ref, ya_ref, yc_ref, nc0_ref, nc1_ref, nc2_ref,
                  *, t_new, n_per):
    keep_old = _lane_lt((2 * GROUP_WIDTH, LANES), LANES - t_new)
    pos = PAST_LEN + lax.broadcasted_iota(jnp.int32, (t_new, 1), 0)
    for i in range(n_per):
        state = None
        for (window, dil), q_ref, kn_ref, c_ref, nc_ref in zip(
                DIL_PAIRS, (q0_ref, q1_ref, q2_ref), (kn0_ref, kn1_ref, kn2_ref),
                (c0_ref, c1_ref, c2_ref), (nc0_ref, nc1_ref, nc2_ref)):
            kv_new = kn_ref[i]
            new_t = jnp.concatenate([jnp.zeros((LANES - t_new, 2 * GROUP_WIDTH), F32), kv_new], axis=0).T
            cache_t = c_ref[0, i]
            part = _attn_s_group(q_ref[i], new_t, cache_t, window, dil)
            state = part if state is None else _merge(state, part)
            nc_ref[0, i] = jnp.where(keep_old, cache_t[:, window - LANES:], new_t)
        ya_ref[i] = state[0] / state[1]

        u_ext = jnp.concatenate([pst_ref[i], u_ref[i]], axis=0)
        yp_ref[i] = _pool_rows(u_ext, pos, pw_ref[...], ps_ref[...])
        z_ext = jnp.concatenate([cst_ref[i], z_ref[i]], axis=0)
        yc_ref[i] = _conv_rows(z_ext, gb_ref[i], cw_ref[...], cst_ref.shape[1])


def _mix_s(u, qs, kns, gb, z, pst, cst, rotated_bufs, pw, ps, cw, *, layer, n_seq, t_new):
    n_per = MIX_S_SEQS
    blk = lambda r, c: pl.BlockSpec((n_per, r, c), lambda b: (b, 0, 0))
    three = lambda a: a.reshape(n_seq, -1, a.shape[-1])
    windows = [w for (w, _) in DIL_PAIRS]
    in_specs = ([blk(t_new, POOL_WIDTH)] + [blk(t_new, GROUP_WIDTH)] * N_DIL + [blk(t_new, 2 * GROUP_WIDTH)] * N_DIL
                + [blk(t_new, CONV_WIDTH)] * 2
                + [pl.BlockSpec((None, n_per) + a.shape[2:], lambda b: (layer, b, 0, 0)) for a in (pst, cst)]
                + [_layer_spec((POOL_WIDTH, POOL_WIDTH), layer), _layer_spec((1, POOL_WIDTH), layer),
                   _layer_spec((CONV_K, CONV_WIDTH), layer)]
                + [pl.BlockSpec((1, n_per, 2 * GROUP_WIDTH, w), lambda b: (layer, b, 0, 0)) for w in windows])
    args = [three(u), *[three(q) for q in qs], *[three(k) for k in kns], three(gb), three(z), pst, cst, pw, ps, cw,
            *rotated_bufs]
    n_in = len(args)
    out_specs = ([blk(t_new, POOL_WIDTH), blk(t_new, GROUP_WIDTH), blk(t_new, CONV_WIDTH)]
                 + [pl.BlockSpec((1, n_per, 2 * GROUP_WIDTH, LANES), lambda b, w=w: (layer, b, 0, w // LANES - 1))
                    for w in windows])
    out_shape = ([jax.ShapeDtypeStruct((n_seq, t_new, c), F32) for c in (POOL_WIDTH, GROUP_WIDTH, CONV_WIDTH)]
                 + [jax.ShapeDtypeStruct(c.shape, F32) for c in rotated_bufs])
    return pl.pallas_call(
        functools.partial(_mix_s_kernel, t_new=t_new, n_per=n_per),
        grid=(n_seq // n_per,),
        in_specs=in_specs,
        out_specs=out_specs,
        out_shape=out_shape,
        input_output_aliases={n_in - N_DIL + g: 3 + g for g in range(N_DIL)},
        compiler_params=_params(),
        name="mix_s",
    )(*args)


def _mlp_head(x, yp, ya, yc, wo_ref, g2_ref):
    mixed = jnp.concatenate([yp.astype(BF16), ya.astype(BF16), yc.astype(BF16)], axis=1)
    x1 = x + jnp.dot(mixed, wo_ref[...], preferred_element_type=F32)
    ms = jnp.mean(x1 * x1, axis=-1, keepdims=True)
    return x1, (x1 * lax.rsqrt(ms + EPS) * g2_ref[...]).astype(BF16)


def _mlp_tail(x1, hb, wu_ref, wd_ref, tf):
    acc = x1
    for c in range(D_FF // tf):
        hf = jnp.dot(hb, wu_ref[:, c * tf:(c + 1) * tf], preferred_element_type=F32)
        act = jnp.square(jnp.maximum(hf, 0.0)).astype(BF16)
        acc = acc + jnp.dot(act, wd_ref[c * tf:(c + 1) * tf, :], preferred_element_type=F32)
    return acc


def _mlp_s_kernel(x_ref, yp_ref, ya_ref, yc_ref, wo_ref, g2_ref, wu_ref, wd_ref, o_ref, hb_buf):
    @pl.when(pl.program_id(0) == 0)
    def _():
        x1, hb = _mlp_head(x_ref[...], yp_ref[...], ya_ref[...], yc_ref[...], wo_ref, g2_ref)
        o_ref[...] = x1
        hb_buf[...] = hb

    hf = jnp.dot(hb_buf[...], wu_ref[...], preferred_element_type=F32)
    act = jnp.square(jnp.maximum(hf, 0.0)).astype(BF16)
    o_ref[...] += jnp.dot(act, wd_ref[...], preferred_element_type=F32)


def _mlp_p_kernel(x_ref, u_ref, uh_ref, gb_ref, z_ref, zh_ref, ya_ref, pw_ref, ps_ref, cw_ref,
                  wo_ref, g2_ref, wu_ref, wd_ref, *rest, tf, tiles, n_cast, n_roll, roll_by):
    cast_src, roll_src, o_ref = rest[:n_cast], rest[n_cast:n_cast + 2 * n_roll], rest[n_cast + 2 * n_roll]
    outs = rest[n_cast + 2 * n_roll + 1:]
    _cast_slabs(cast_src, outs[:n_cast])
    _roll_buffers(roll_src, outs[n_cast:], roll_by)
    tm = x_ref.shape[0]
    tile = pl.program_id(0) % tiles
    has_hist = tile > 0
    pos = tile * tm + lax.broadcasted_iota(jnp.int32, (tm, 1), 0)
    yp = _pool_rows(jnp.concatenate([jnp.where(has_hist, uh_ref[...], 0.0), u_ref[...]], axis=0), pos,
                    pw_ref[...], ps_ref[...])
    yc = _conv_rows(jnp.concatenate([jnp.where(has_hist, zh_ref[...], 0.0), z_ref[...]], axis=0),
                    gb_ref[...], cw_ref[...], HIST_ROWS)
    x1, hb = _mlp_head(x_ref[...], yp, ya_ref[...], yc, wo_ref, g2_ref)
    o_ref[...] = _mlp_tail(x1, hb, wu_ref, wd_ref, tf)


def _mlp_s(x, yp, ya, yc, w_out, g2, w_up, w_down, *, layer):
    rows = x.shape[0]
    tf = MLP_FF_CHUNK
    full = lambda c: pl.BlockSpec((rows, c), lambda j: (0, 0))
    return pl.pallas_call(
        _mlp_s_kernel,
        grid=(D_FF // tf,),
        in_specs=[full(D_MODEL), full(POOL_WIDTH), full(GROUP_WIDTH), full(CONV_WIDTH),
                  _layer_spec((MIX_OUT, D_MODEL), layer), _layer_spec((1, D_MODEL), layer),
                  pl.BlockSpec((D_MODEL, tf), lambda j: (0, j)),
                  pl.BlockSpec((tf, D_MODEL), lambda j: (j, 0))],
        out_specs=full(D_MODEL),
        out_shape=jax.ShapeDtypeStruct((rows, D_MODEL), F32),
        scratch_shapes=[pltpu.VMEM((rows, D_MODEL), BF16)],
        compiler_params=_params(),
        name="mlp_s",
    )(x, yp, ya, yc, w_out, g2, w_up, w_down)


def _mlp_p(x, u, gb, z, ya, pw, ps, cw, w_out, g2, w_up, w_down, *, layer, tm, seq, convert=None, roll=(),
           roll_by=0):
    rows = x.shape[0]
    row = lambda c: pl.BlockSpec((tm, c), lambda i: (i, 0))
    conv_in, conv_out_specs, conv_out_shape, conv_args = _cast_specs(convert, rows // tm)
    roll_in, roll_out, roll_shape, roll_args = _roll_specs(roll, rows // tm)
    hist = lambda c: pl.BlockSpec((HIST_ROWS, c), lambda i: (jnp.maximum(i * (tm // HIST_ROWS) - 1, 0), 0))
    return pl.pallas_call(
        functools.partial(_mlp_p_kernel, tf=MLP_FF_CHUNK, tiles=seq // tm, n_cast=len(conv_args),
                          n_roll=len(roll_out), roll_by=roll_by),
        grid=(rows // tm,),
        in_specs=[row(D_MODEL), row(POOL_WIDTH), hist(POOL_WIDTH), row(CONV_WIDTH), row(CONV_WIDTH),
                  hist(CONV_WIDTH), row(GROUP_WIDTH), _layer_spec((POOL_WIDTH, POOL_WIDTH), layer),
                  _layer_spec((1, POOL_WIDTH), layer), _layer_spec((CONV_K, CONV_WIDTH), layer),
                  _layer_spec((MIX_OUT, D_MODEL), layer), _layer_spec((1, D_MODEL), layer),
                  _const_spec((D_MODEL, D_FF)), _const_spec((D_FF, D_MODEL))] + conv_in + roll_in,
        out_specs=[row(D_MODEL)] + conv_out_specs + roll_out,
        out_shape=[jax.ShapeDtypeStruct((rows, D_MODEL), F32)] + conv_out_shape + roll_shape,
        compiler_params=_params(),
        name="mlp_p",
    )(x, u, u, gb, z, z, ya, pw, ps, cw, w_out, g2, w_up, w_down, *conv_args, *roll_args)


def _rope_tables(pos):
    half = ROPE_DIM // 2
    inv = jnp.power(jnp.float32(ROPE_THETA), -jnp.arange(half, dtype=F32) / half)
    ang = pos.astype(F32)[:, None] * inv[None, :]
    cos, sin = jnp.cos(ang), jnp.sin(ang)
    n = pos.shape[0]
    rest = HEAD_DIM - ROPE_DIM
    zh = jnp.zeros((n, half), F32)
    c = jnp.concatenate([cos, cos, jnp.ones((n, rest), F32)], axis=1)
    a = jnp.concatenate([-sin, zh, jnp.zeros((n, rest), F32)], axis=1)
    b = jnp.concatenate([zh, sin, jnp.zeros((n, rest), F32)], axis=1)
    return tuple(jnp.tile(t, (1, GROUP_WIDTH // HEAD_DIM)) for t in (c, a, b))


def _to_buffer_layout(c):
    lead = c.shape[:-4]
    n = len(lead)
    t = jnp.transpose(c, (*range(n), n + 1, n + 2, n + 3, n))
    return t.reshape(*lead, 2 * GROUP_WIDTH, c.shape[-4])


def _from_buffer_layout(t):
    lead = t.shape[:-2]
    n = len(lead)
    c = t.reshape(*lead, 2, 2, HEAD_DIM, t.shape[-1])
    return jnp.transpose(c, (*range(n), n + 3, n, n + 1, n + 2))


def kernel(x_prompt, x_sample, state_pool, state_conv, cache_kv_w128, cache_kv_w512, cache_kv_w2048,
           norm1_g, w_in, q_norm_g, k_norm_g, pool_w, pool_scale, conv_w, w_out, norm2_g, w_up, w_down):
    batch, seq, _ = x_prompt.shape
    n_seq, t_new, _ = x_sample.shape
    depth = w_in.shape[0]
    assert depth == 2, "the in-place stacking of the kv state outputs is written for two layers"
    caches_t = [_to_buffer_layout(c) for c in (cache_kv_w128, cache_kv_w512, cache_kv_w2048)]

    rope_p = _rope_tables(jnp.arange(seq, dtype=jnp.int32))
    rope_s = _rope_tables(jnp.tile(PAST_LEN + jnp.arange(t_new, dtype=jnp.int32), n_seq))
    head_id = jnp.arange(LANES) // HEAD_DIM
    hm = jnp.where(head_id[:, None] == head_id[None, :], 1.0 / HEAD_DIM, 0.0).astype(BF16)
    g1 = norm1_g[:, None, :]
    g2 = norm2_g[:, None, :]
    ps = pool_scale[:, None, :]
    cw = conv_w
    two_heads = lambda gain: jnp.tile(gain, (1, GROUP_WIDTH // HEAD_DIM))[:, None, :]
    qg = two_heads(q_norm_g) * (HEAD_DIM ** -0.5)
    kg = two_heads(k_norm_g)
    n_win = len(POOL_WINDOWS)
    pw = jnp.einsum("gh,lgij->lgihj", jnp.eye(n_win, dtype=F32), pool_w).reshape(depth, POOL_WIDTH, POOL_WIDTH)
    pw = pw.astype(BF16)
    w_out_b = w_out.astype(BF16)
    pst = jnp.pad(state_pool, ((0, 0), (0, 0), (HIST_ROWS - POOL_HIST, 0), (0, 0)))
    cst = jnp.pad(state_conv, ((0, 0), (0, 0), (SUBLANES - (CONV_K - 1), 0), (0, 0)))

    xp = x_prompt.reshape(batch * seq, D_MODEL)
    xs = x_sample.reshape(n_seq * t_new, D_MODEL)
    outs = {k: [] for k in ("pool_p", "conv_p", "pool_s", "conv_s")}
    w_in_l = w_in[0].astype(BF16)
    new_caches = [None] * N_DIL
    kvts = None
    for layer in range(depth):
        if layer == 0:
            (u, gb, z, qc0, qc1, qc2, kvc0, kvc1, kvc2, *tail) = _proj(
                xp, g1, w_in_l, qg, kg, rope_p, hm, layer=layer, tm=PROJ_TILE, sub=PROJ_SUB, batch=batch,
                seq=seq, roll=caches_t[:-1], roll_by=t_new)
            kvts, new_caches[:-1] = tail[:N_DIL], tail[N_DIL:]
            ya, w_up_l, w_down_l = _attn_p((qc0, qc1, qc2), (kvc0, kvc1, kvc2), batch=batch, seq=seq,
                                           convert=(0, (w_up, w_down)))
            xp, w_in_next, new_caches[-1] = _mlp_p(
                xp, u, gb, z, ya.reshape(-1, GROUP_WIDTH), pw, ps, cw, w_out_b, g2, w_up_l, w_down_l, layer=layer,
                tm=MLP_TILE, seq=seq, convert=(1, (w_in,)), roll=caches_t[-1:], roll_by=t_new)
        else:
            (u, gb, z, qc0, qc1, qc2, kvc0, kvc1, kvc2, *tail) = _proj(
                xp, g1, w_in_l, qg, kg, rope_p, hm, layer=layer, tm=PROJ_TILE, sub=PROJ_SUB, batch=batch,
                seq=seq, prev_kvt=kvts, convert=(1, (w_up, w_down)))
            kvts, (w_up_l, w_down_l) = tail[:N_DIL], tail[N_DIL:]
            (ya,) = _attn_p((qc0, qc1, qc2), (kvc0, kvc1, kvc2), batch=batch, seq=seq)
            (xp,) = _mlp_p(xp, u, gb, z, ya.reshape(-1, GROUP_WIDTH), pw, ps, cw, w_out_b, g2, w_up_l, w_down_l,
                           layer=layer, tm=MLP_TILE, seq=seq)
        outs["pool_p"].append(u.reshape(batch, seq, POOL_WIDTH)[:, seq - POOL_HIST:])
        outs["conv_p"].append(z.reshape(batch, seq, CONV_WIDTH)[:, seq - (CONV_K - 1):])

        us, gbs, zs, q0, q1, q2, kn0, kn1, kn2 = _proj(xs, g1, w_in_l, qg, kg, rope_s, hm, layer=layer,
                                                       tm=n_seq * t_new, sub=n_seq * t_new)
        yps, yas, ycs, *new_caches = _mix_s(us, (q0, q1, q2), (kn0, kn1, kn2), gbs, zs, pst, cst, new_caches,
                                           pw, ps, cw, layer=layer, n_seq=n_seq, t_new=t_new)
        xs = _mlp_s(xs, yps.reshape(-1, POOL_WIDTH), yas.reshape(-1, GROUP_WIDTH), ycs.reshape(-1, CONV_WIDTH),
                    w_out_b, g2, w_up_l, w_down_l, layer=layer)
        us3 = us.reshape(n_seq, t_new, POOL_WIDTH)
        zs3 = zs.reshape(n_seq, t_new, CONV_WIDTH)
        outs["pool_s"].append(jnp.concatenate([state_pool[layer], us3], axis=1)[:, -POOL_HIST:])
        outs["conv_s"].append(jnp.concatenate([state_conv[layer], zs3], axis=1)[:, -(CONV_K - 1):])
        if layer == 0:
            w_in_l = w_in_next

    st = lambda k: jnp.stack(outs[k])
    return (xp.reshape(batch, seq, D_MODEL), xs.reshape(n_seq, t_new, D_MODEL),
            st("pool_p"), st("conv_p"),
            *[_from_buffer_layout(kvt) for kvt in kvts],
            st("pool_s"), st("conv_s"),
            *[_from_buffer_layout(c) for c in new_caches])
```

```python
import functools

import jax
import jax.numpy as jnp
from jax import lax
from jax.experimental import pallas as pl
from jax.experimental.pallas import tpu as pltpu

D_MODEL = 1024
HEAD_DIM = 64
POOL_WIDTH = 256
POOL_WINDOWS = (2, 4, 8, 16)
POOL_GROUP = 64
POOL_HIST = 15
ATTN_WIDTH = 384
DIL_PAIRS = ((128, 1), (512, 4), (2048, 16))
N_DIL = 3
GROUP_WIDTH = 128
CONV_WIDTH = 384
CONV_K = 3
ROPE_DIM = 16
ROPE_THETA = 500000.0
D_FF = 4096
IN_COLS = 2560
MIX_OUT = 768
EPS = 1e-6
NEG_INF = -1e30
PAST_LEN = 8192
N_BACK = 128

LANES = 128
SUBLANES = 8
HIST_ROWS = 16
ATTN_TILE = 128
ATTN_UNROLL = 16
ATTN_MERGE_ROWS = 256
PROJ_TILE = 1024
PROJ_SUB = 512
MLP_TILE = 512
MLP_FF_CHUNK = 512
MIX_S_SEQS = 8
VMEM_LIMIT = 56 * 1024 * 1024

F32 = jnp.float32
BF16 = jnp.bfloat16

_Q_OFF = POOL_WIDTH
_GB_OFF = _Q_OFF + 3 * ATTN_WIDTH

_NT = (((1,), (1,)), ((), ()))


def _const_spec(shape):
    return pl.BlockSpec(shape, lambda *_: (0,) * len(shape), pipeline_mode=pl.Buffered(1))


def _layer_spec(shape, layer):
    return pl.BlockSpec((None, *shape), lambda *_: (layer,) + (0,) * len(shape), pipeline_mode=pl.Buffered(1))


def _params():
    return pltpu.CompilerParams(dimension_semantics=("arbitrary",), vmem_limit_bytes=VMEM_LIMIT)


def _cast_specs(convert, n_steps):
    in_specs, out_specs, out_shape, args = [], [], [], []
    if convert is not None:
        layer, weights = convert
        for w in weights:
            _, k, n = w.shape
            in_specs.append(pl.BlockSpec((None, k // n_steps, n), lambda i: (layer, i, 0)))
            out_specs.append(pl.BlockSpec((k // n_steps, n), lambda i: (i, 0)))
            out_shape.append(jax.ShapeDtypeStruct((k, n), BF16))
            args.append(w)
    return in_specs, out_specs, out_shape, args


def _cast_slabs(srcs, dsts):
    for src, dst in zip(srcs, dsts):
        dst[...] = src[...].astype(BF16)


def _roll_specs(buffers, n_steps):
    in_specs, out_specs, out_shape, args = [], [], [], []
    for cache_t in buffers:
        depth, n_seq, width, window = cache_t.shape
        assert n_seq % n_steps == 0
        per = n_seq // n_steps
        in_specs += [pl.BlockSpec((1, per, width, window), lambda i, l=l: (l, i, 0, 0)) for l in range(depth)]
        args += [cache_t] * depth
        out_specs.append(pl.BlockSpec((depth, per, width, window), lambda i: (0, i, 0, 0)))
        out_shape.append(jax.ShapeDtypeStruct(cache_t.shape, F32))
    return in_specs, out_specs, out_shape, args


def _roll_buffers(srcs, dsts, by):
    srcs = iter(srcs)
    for dst in dsts:
        for layer in range(dst.shape[0]):
            src = next(srcs)
            for i in range(src.shape[1]):
                dst[layer, i] = pltpu.roll(src[0, i], src.shape[-1] - by, 1)


def _proj_kernel(x_ref, g1_ref, w_ref, qg_ref, kg_ref, rc_ref, ra_ref, rb_ref, hm_ref, *rest,
                 class_major, tiles, sub, stacked, n_cast, n_roll, roll_by):
    prev_kvt = None
    if class_major:
        if stacked:
            prev_kvt, rest = rest[:N_DIL], rest[N_DIL:]
        cast_src, rest = rest[:n_cast], rest[n_cast:]
        roll_src, rest = rest[:2 * n_roll], rest[2 * n_roll:]
        (u_ref, gb_ref, z_ref, qc0, qc1, qc2, kvc0, kvc1, kvc2, kvt0, kvt1, kvt2) = rest[:12]
        _cast_slabs(cast_src, rest[12:12 + n_cast])
        _roll_buffers(roll_src, rest[12 + n_cast:12 + n_cast + n_roll], roll_by)
        nat = rest[-1]
    else:
        (u_ref, gb_ref, z_ref, q0, q1, q2, kv0, kv1, kv2) = rest
    tm = x_ref.shape[0]
    hm = hm_ref[...]

    for s in range(tm // sub):
        r0 = s * sub
        rows = slice(r0, r0 + sub)
        x = x_ref[rows, :]
        ms = jnp.mean(x * x, axis=-1, keepdims=True)
        hb = (x * lax.rsqrt(ms + EPS) * g1_ref[...]).astype(BF16)
        p_att = jnp.dot(hb, w_ref[:, _Q_OFF:_GB_OFF], preferred_element_type=F32)
        rc, ra, rb = rc_ref[rows, :], ra_ref[rows, :], rb_ref[rows, :]

        def head_norm_rope(xb, gain):
            msq = jnp.dot((xb * xb).astype(BF16), hm, preferred_element_type=F32)
            xn = xb * lax.rsqrt(msq + EPS) * gain
            return xn * rc + pltpu.roll(xn, LANES - ROPE_DIM // 2, 1) * ra + pltpu.roll(xn, ROPE_DIM // 2, 1) * rb

        for g, (window, dil) in enumerate(DIL_PAIRS):
            lanes = slice(g * GROUP_WIDTH, (g + 1) * GROUP_WIDTH)
            qn = head_norm_rope(p_att[:, :ATTN_WIDTH][:, lanes], qg_ref[...])
            kn = head_norm_rope(p_att[:, ATTN_WIDTH:2 * ATTN_WIDTH][:, lanes], kg_ref[...])
            vv = p_att[:, 2 * ATTN_WIDTH:][:, lanes]
            if not class_major:
                q_ref, kv_ref = ((q0, kv0), (q1, kv1), (q2, kv2))[g]
                q_ref[rows, :] = qn
                kv_ref[rows, :GROUP_WIDTH] = kn
                kv_ref[rows, GROUP_WIDTH:] = vv
                continue
            qc_ref, kvc_ref, kvt_ref = ((qc0, kvc0, kvt0), (qc1, kvc1, kvt1), (qc2, kvc2, kvt2))[g]
            slab = 3 * (s * N_DIL + g)
            nat[slab] = qn
            nat[slab + 1] = kn
            nat[slab + 2] = vv
            per = sub // dil
            dst = slice(s * per, (s + 1) * per)
            for r in range(dil):
                src = pl.ds(r, per, stride=dil) if dil > 1 else slice(None)
                qc_ref[0, r, dst, :] = nat[slab, src, :].astype(BF16)
                kvc_ref[0, r, dst, :GROUP_WIDTH] = nat[slab + 1, src, :].astype(BF16)
                kvc_ref[0, r, dst, GROUP_WIDTH:] = nat[slab + 2, src, :].astype(BF16)
            def put_kvt(cols, part, g=g, kvt_ref=kvt_ref):
                if stacked:
                    kvt_ref[0, 0, :, cols] = prev_kvt[g][0, :, cols]
                    kvt_ref[1, 0, :, cols] = part
                else:
                    kvt_ref[0, :, cols] = part

            if window >= tiles * tm:
                put_kvt(rows, jnp.concatenate([kn, vv], axis=1).T)
            else:
                first_kept = tm - min(window, tm)
                lo = max(r0, first_kept)
                if lo < r0 + sub:
                    part = jnp.concatenate([kn[lo - r0:], vv[lo - r0:]], axis=1).T
                    put_kvt(slice(lo - first_kept, r0 + sub - first_kept), part)

        u_ref[rows, :] = jnp.dot(hb, w_ref[:, :POOL_WIDTH], preferred_element_type=F32)
        p_gate = jnp.dot(hb, w_ref[:, _GB_OFF:], preferred_element_type=F32)
        gb_ref[rows, :] = p_gate[:, :CONV_WIDTH]
        z_ref[rows, :] = p_gate[:, CONV_WIDTH:2 * CONV_WIDTH] * p_gate[:, 2 * CONV_WIDTH:]


def _proj(x, g1, w_in, qg, kg, rope, hm, *, layer, tm, sub, batch=None, seq=None, prev_kvt=None, convert=None,
          roll=(), roll_by=0):
    rows = x.shape[0]
    stacked = prev_kvt is not None
    n_tab = rope[0].shape[0] // tm
    class_major = batch is not None
    tiles = seq // tm if class_major else 1
    row = lambda c: pl.BlockSpec((tm, c), lambda i: (i, 0))
    tab = pl.BlockSpec((tm, LANES), lambda i: (i % n_tab, 0))
    out_shape = [jax.ShapeDtypeStruct((rows, POOL_WIDTH), F32)] + [jax.ShapeDtypeStruct((rows, CONV_WIDTH), F32)] * 2
    out_specs = [row(POOL_WIDTH), row(CONV_WIDTH), row(CONV_WIDTH)]
    scratch = []
    prev_specs = []
    cast_in, cast_out, cast_shape, cast_args = _cast_specs(convert, rows // tm)
    assert class_major or (convert is None and not roll)
    roll_in, roll_out, roll_shape, roll_args = _roll_specs(roll, rows // tm)
    if class_major:
        for width in (GROUP_WIDTH, 2 * GROUP_WIDTH):
            for (_, dil) in DIL_PAIRS:
                out_shape.append(jax.ShapeDtypeStruct((batch, dil, seq // dil, width), BF16))
                out_specs.append(pl.BlockSpec((1, dil, tm // dil, width), lambda i: (i // tiles, 0, i % tiles, 0)))
        for (window, _) in DIL_PAIRS:
            assert window >= seq or window <= tm, "kv buffer rows must come from the last row tile"
            cols = tm if window >= seq else window
            col_blk = (lambda i: i % tiles) if window >= seq else (lambda i: 0)
            kvt_spec = pl.BlockSpec((1, 2 * GROUP_WIDTH, cols), lambda i, c=col_blk: (i // tiles, 0, c(i)))
            if stacked:
                prev_specs.append(kvt_spec)
                out_shape.append(jax.ShapeDtypeStruct((2, batch, 2 * GROUP_WIDTH, window), F32))
                out_specs.append(pl.BlockSpec((2, 1, 2 * GROUP_WIDTH, cols),
                                              lambda i, c=col_blk: (0, i // tiles, 0, c(i))))
            else:
                out_shape.append(jax.ShapeDtypeStruct((batch, 2 * GROUP_WIDTH, window), F32))
                out_specs.append(kvt_spec)
        scratch = [pltpu.VMEM((3 * N_DIL * (tm // sub), sub, GROUP_WIDTH), F32)]
    else:
        out_shape += [jax.ShapeDtypeStruct((rows, GROUP_WIDTH), F32)] * N_DIL
        out_shape += [jax.ShapeDtypeStruct((rows, 2 * GROUP_WIDTH), F32)] * N_DIL
        out_specs += [row(GROUP_WIDTH)] * N_DIL + [row(2 * GROUP_WIDTH)] * N_DIL
    return pl.pallas_call(
        functools.partial(_proj_kernel, class_major=class_major, tiles=tiles, sub=sub, stacked=stacked,
                          n_cast=len(cast_args), n_roll=len(roll_out), roll_by=roll_by),
        grid=(rows // tm,),
        in_specs=[row(D_MODEL), _layer_spec((1, D_MODEL), layer), _const_spec((D_MODEL, IN_COLS)),
                  _layer_spec((1, LANES), layer), _layer_spec((1, LANES), layer), tab, tab, tab,
                  _const_spec((LANES, LANES))] + prev_specs + cast_in + roll_in,
        out_specs=out_specs + cast_out + roll_out,
        out_shape=out_shape + cast_shape + roll_shape,
        scratch_shapes=scratch,
        compiler_params=_params(),
        name="proj",
    )(x, g1, w_in, qg, kg, *rope, hm, *(prev_kvt or ()), *cast_args, *roll_args)


def _lane_lt(shape, bound):
    return lax.broadcasted_iota(jnp.int32, shape, len(shape) - 1) < bound


def _stack_heads(q):
    first = _lane_lt(q.shape, HEAD_DIM)
    zero = jnp.zeros_like(q)
    return jnp.concatenate([jnp.where(first, q, zero), jnp.where(first, zero, q)], axis=0)


def _unstack(o2, l2, m2):
    n = o2.shape[0] // 2
    first = _lane_lt((n, GROUP_WIDTH), HEAD_DIM)
    return tuple(jnp.where(first, a[:n], a[n:]) for a in (o2, l2, m2))


def _merge(a, b):
    m = jnp.maximum(a[2], b[2])
    wa = jnp.exp(a[2] - m)
    wb = jnp.exp(b[2] - m)
    return wa * a[0] + wb * b[0], wa * a[1] + wb * b[1], m


def _attn_unit(q, kv, bias):
    nk = kv.shape[0]
    k = kv[:, :GROUP_WIDTH]
    v1 = jnp.concatenate([kv[:, GROUP_WIDTH:], jnp.ones((nk, GROUP_WIDTH), BF16)], axis=1)
    s = lax.dot_general(_stack_heads(q), k, _NT, preferred_element_type=F32) + bias
    mrow = jnp.max(s, axis=-1, keepdims=True)
    p = jnp.exp(s - mrow).astype(BF16)
    r = jnp.dot(p, v1, preferred_element_type=F32)
    return _unstack(r[:, :GROUP_WIDTH], r[:, GROUP_WIDTH:], mrow)


def _attn_p_kernel(qc0, qc1, qc2, kvc0, kvc1, kvc2, *rest, seq, n_cast):
    cast_src, y_ref, rest = rest[:n_cast], rest[n_cast], rest[n_cast + 1:]
    _cast_slabs(cast_src, rest[:n_cast])
    scratch = rest[n_cast:]
    parts = [scratch[3 * g:3 * g + 3] for g in range(N_DIL)]
    bias_ref = scratch[3 * N_DIL]
    acc_o, acc_l, acc_m = parts[0]
    @pl.when(pl.program_id(0) == 0)
    def _():
        qi = lax.broadcasted_iota(jnp.int32, (2 * ATTN_TILE, 2 * ATTN_TILE), 0) & (ATTN_TILE - 1)
        ki = lax.broadcasted_iota(jnp.int32, (2 * ATTN_TILE, 2 * ATTN_TILE), 1)
        for first_key_back in (0, 1):
            dist = qi - ki + first_key_back * ATTN_TILE
            bias_ref[first_key_back] = jnp.where((dist >= 0) & (dist <= N_BACK), 0.0, NEG_INF)

    for g, ((_, dil), q_ref, kv_ref) in enumerate(zip(DIL_PAIRS, (qc0, qc1, qc2), (kvc0, kvc1, kvc2))):
        dst = parts[g]
        n_sub = seq // dil // ATTN_TILE

        def unit(idx, q_ref=q_ref, kv_ref=kv_ref, n_sub=n_sub, dil=dil):
            if n_sub == 1:
                r, sub = idx, 0
                res = _attn_unit(q_ref[0, r], kv_ref[0, r], bias_ref[0, :, :ATTN_TILE])
            else:
                r, sub = (idx // n_sub, idx % n_sub) if dil > 1 else (0, idx)
                back = jnp.minimum(sub, 1)
                q0 = pl.multiple_of(sub * ATTN_TILE, ATTN_TILE)
                k0 = pl.multiple_of((sub - back) * ATTN_TILE, ATTN_TILE)
                res = _attn_unit(q_ref[0, r, pl.ds(q0, ATTN_TILE), :], kv_ref[0, r, pl.ds(k0, 2 * ATTN_TILE), :],
                                 bias_ref[back])
            start = sub * (ATTN_TILE * dil) + r
            rows = pl.ds(start, ATTN_TILE, stride=dil) if dil > 1 else pl.ds(pl.multiple_of(start, ATTN_TILE), ATTN_TILE)
            return rows, res

        def units_body(it, carry, unit=unit, dst=dst):
            for j in range(ATTN_UNROLL):
                rows, res = unit(it * ATTN_UNROLL + j)
                for ref, val in zip(dst, res):
                    ref[rows, :] = val
            return carry
        n_iter = dil * n_sub // ATTN_UNROLL
        if n_iter == 1:
            units_body(0, 0)
        else:
            lax.fori_loop(0, n_iter, units_body, 0)

        if g > 0:
            for c in range(seq // ATTN_MERGE_ROWS):
                rows = slice(c * ATTN_MERGE_ROWS, (c + 1) * ATTN_MERGE_ROWS)
                o, l, m = _merge((acc_o[rows, :], acc_l[rows, :], acc_m[rows, :]),
                                 tuple(ref[rows, :] for ref in parts[g]))
                if g == N_DIL - 1:
                    y_ref[0, rows, :] = (o / l).astype(y_ref.dtype)
                else:
                    acc_o[rows, :] = o
                    acc_l[rows, :] = l
                    acc_m[rows, :] = m


def _attn_p(qcs, kvcs, *, batch, seq, convert=None):
    in_specs = [pl.BlockSpec((1,) + a.shape[1:], lambda b: (b, 0, 0, 0)) for a in (*qcs, *kvcs)]
    cast_in, cast_out, cast_shape, cast_args = _cast_specs(convert, batch)
    acc = pltpu.VMEM((seq, GROUP_WIDTH), F32)
    return pl.pallas_call(
        functools.partial(_attn_p_kernel, seq=seq, n_cast=len(cast_args)),
        grid=(batch,),
        in_specs=in_specs + cast_in,
        out_specs=[pl.BlockSpec((1, seq, GROUP_WIDTH), lambda b: (b, 0, 0))] + cast_out,
        out_shape=[jax.ShapeDtypeStruct((batch, seq, GROUP_WIDTH), BF16)] + cast_shape,
        scratch_shapes=[acc] * (3 * N_DIL) + [pltpu.VMEM((2, 2 * ATTN_TILE, 2 * ATTN_TILE), F32)],
        compiler_params=_params(),
        name="attn_p",
    )(*qcs, *kvcs, *cast_args)


def _pool_rows(ext, pos, pw, ps):
    return jnp.dot(_pool_diff(ext, pos), pw, preferred_element_type=F32) * ps


def _pool_diff(ext, pos):
    s2 = ext + pltpu.roll(ext, 1, 0)
    s4 = s2 + pltpu.roll(s2, 2, 0)
    s8 = s4 + pltpu.roll(s4, 4, 0)
    s16 = s8 + pltpu.roll(s8, 8, 0)
    lane = lax.broadcasted_iota(jnp.int32, (1, POOL_WIDTH), 1)
    grp = [lane < (j + 1) * POOL_GROUP for j in range(3)]
    win = jnp.where(grp[0], s2, jnp.where(grp[1], s4, jnp.where(grp[2], s8, s16)))[HIST_ROWS:]
    width = jnp.where(grp[0], POOL_WINDOWS[0], jnp.where(grp[1], POOL_WINDOWS[1],
                      jnp.where(grp[2], POOL_WINDOWS[2], POOL_WINDOWS[3])))
    cnt = jnp.minimum(pos + 1, width).astype(F32)
    return (win / cnt - ext[HIST_ROWS:]).astype(BF16)


def _conv_rows(ext, gb, cw, hist):
    y = cw[0:1] * pltpu.roll(ext, 2, 0) + cw[1:2] * pltpu.roll(ext, 1, 0) + cw[2:3] * ext
    return gb * y[hist:]


def _attn_s_group(q, new_t, cache_t, window, dil):
    t = q.shape[0]
    q2 = _stack_heads(q).astype(BF16)
    s_c = jnp.dot(q2, cache_t[:GROUP_WIDTH].astype(BF16), preferred_element_type=F32)
    s_n = jnp.dot(q2, new_t[:GROUP_WIDTH].astype(BF16), preferred_element_type=F32)
    tq = lax.broadcasted_iota(jnp.int32, (2 * t, 1), 0) & (t - 1)
    rc = (lax.broadcasted_iota(jnp.int32, (1, window), 1) + t) & (window - 1)
    dist_c = window + tq - rc
    ok_c = (dist_c <= window) & ((dist_c & (dil - 1)) == 0) & (PAST_LEN - window + rc >= 0)
    tn = lax.broadcasted_iota(jnp.int32, (1, LANES), 1) - (LANES - t)
    dist_n = tq - tn
    ok_n = (tn >= 0) & (dist_n >= 0) & ((dist_n & (dil - 1)) == 0)
    s_c = jnp.where(ok_c, s_c, NEG_INF)
    s_n = jnp.where(ok_n, s_n, NEG_INF)
    mrow = jnp.maximum(jnp.max(s_c, axis=-1, keepdims=True), jnp.max(s_n, axis=-1, keepdims=True))
    p_c = jnp.exp(s_c - mrow)
    p_n = jnp.exp(s_n - mrow)
    o2 = (lax.dot_general(p_c.astype(BF16), cache_t[GROUP_WIDTH:].astype(BF16), _NT, preferred_element_type=F32)
          + lax.dot_general(p_n.astype(BF16), new_t[GROUP_WIDTH:].astype(BF16), _NT, preferred_element_type=F32))
    l2 = jnp.sum(p_c, axis=-1, keepdims=True) + jnp.sum(p_n, axis=-1, keepdims=True)
    return _unstack(o2, l2, mrow)


def _mix_s_kernel(u_ref, q0_ref, q1_ref, q2_ref, kn0_ref, kn1_ref, kn2_ref, gb_ref, z_ref, pst_ref, cst_ref,
                  pw_ref, ps_ref, cw_ref, c0_ref, c1_ref, c2_ref, yp_ref, ya_ref, yc_ref, nc0_ref, nc1_ref, nc2_ref,
                  *, t_new, n_per):
    keep_old = _lane_lt((2 * GROUP_WIDTH, LANES), LANES - t_new)
    pos = PAST_LEN + lax.broadcasted_iota(jnp.int32, (t_new, 1), 0)
    for i in range(n_per):
        state = None
        for (window, dil), q_ref, kn_ref, c_ref, nc_ref in zip(
                DIL_PAIRS, (q0_ref, q1_ref, q2_ref), (kn0_ref, kn1_ref, kn2_ref),
                (c0_ref, c1_ref, c2_ref), (nc0_ref, nc1_ref, nc2_ref)):
            kv_new = kn_ref[i]
            new_t = jnp.concatenate([jnp.zeros((LANES - t_new, 2 * GROUP_WIDTH), F32), kv_new], axis=0).T
            cache_t = c_ref[0, i]
            part = _attn_s_group(q_ref[i], new_t, cache_t, window, dil)
            state = part if state is None else _merge(state, part)
            nc_ref[0, i] = jnp.where(keep_old, cache_t[:, window - LANES:], new_t)
        ya_ref[i] = state[0] / state[1]

        u_ext = jnp.concatenate([pst_ref[i], u_ref[i]], axis=0)
        yp_ref[i] = _pool_rows(u_ext, pos, pw_ref[...], ps_ref[...])
        z_ext = jnp.concatenate([cst_ref[i], z_ref[i]], axis=0)
        yc_ref[i] = _conv_rows(z_ext, gb_ref[i], cw_ref[...], cst_ref.shape[1])


def _mix_s(u, qs, kns, gb, z, pst, cst, rotated_bufs, pw, ps, cw, *, layer, n_seq, t_new):
    n_per = MIX_S_SEQS
    blk = lambda r, c: pl.BlockSpec((n_per, r, c), lambda b: (b, 0, 0))
    three = lambda a: a.reshape(n_seq, -1, a.shape[-1])
    windows = [w for (w, _) in DIL_PAIRS]
    in_specs = ([blk(t_new, POOL_WIDTH)] + [blk(t_new, GROUP_WIDTH)] * N_DIL + [blk(t_new, 2 * GROUP_WIDTH)] * N_DIL
                + [blk(t_new, CONV_WIDTH)] * 2
                + [pl.BlockSpec((None, n_per) + a.shape[2:], lambda b: (layer, b, 0, 0)) for a in (pst, cst)]
                + [_layer_spec((POOL_WIDTH, POOL_WIDTH), layer), _layer_spec((1, POOL_WIDTH), layer),
                   _layer_spec((CONV_K, CONV_WIDTH), layer)]
                + [pl.BlockSpec((1, n_per, 2 * GROUP_WIDTH, w), lambda b: (layer, b, 0, 0)) for w in windows])
    args = [three(u), *[three(q) for q in qs], *[three(k) for k in kns], three(gb), three(z), pst, cst, pw, ps, cw,
            *rotated_bufs]
    n_in = len(args)
    out_specs = ([blk(t_new, POOL_WIDTH), blk(t_new, GROUP_WIDTH), blk(t_new, CONV_WIDTH)]
                 + [pl.BlockSpec((1, n_per, 2 * GROUP_WIDTH, LANES), lambda b, w=w: (layer, b, 0, w // LANES - 1))
                    for w in windows])
    out_shape = ([jax.ShapeDtypeStruct((n_seq, t_new, c), F32) for c in (POOL_WIDTH, GROUP_WIDTH, CONV_WIDTH)]
                 + [jax.ShapeDtypeStruct(c.shape, F32) for c in rotated_bufs])
    return pl.pallas_call(
        functools.partial(_mix_s_kernel, t_new=t_new, n_per=n_per),
        grid=(n_seq // n_per,),
        in_specs=in_specs,
        out_specs=out_specs,
        out_shape=out_shape,
        input_output_aliases={n_in - N_DIL + g: 3 + g for g in range(N_DIL)},
        compiler_params=_params(),
        name="mix_s",
    )(*args)


def _mlp_head(x, yp, ya, yc, wo_ref, g2_ref):
    mixed = jnp.concatenate([yp.astype(BF16), ya.astype(BF16), yc.astype(BF16)], axis=1)
    x1 = x + jnp.dot(mixed, wo_ref[...], preferred_element_type=F32)
    ms = jnp.mean(x1 * x1, axis=-1, keepdims=True)
    return x1, (x1 * lax.rsqrt(ms + EPS) * g2_ref[...]).astype(BF16)


def _mlp_s_kernel(x_ref, yp_ref, ya_ref, yc_ref, wo_ref, g2_ref, wu_ref, wd_ref, o_ref, hb_buf):
    @pl.when(pl.program_id(0) == 0)
    def _():
        x1, hb = _mlp_head(x_ref[...], yp_ref[...], ya_ref[...], yc_ref[...], wo_ref, g2_ref)
        o_ref[...] = x1
        hb_buf[...] = hb

    hf = jnp.dot(hb_buf[...], wu_ref[...], preferred_element_type=F32)
    act = jnp.square(jnp.maximum(hf, 0.0)).astype(BF16)
    o_ref[...] += jnp.dot(act, wd_ref[...], preferred_element_type=F32)


def _mlp_p_kernel(x_ref, u_ref, uh_ref, gb_ref, z_ref, zh_ref, ya_ref, pw_ref, ps_ref, cw_ref,
                  wo_ref, g2_ref, wu_ref, wd_ref, *rest, tf, tiles, n_cast, n_roll, roll_by):
    cast_src, roll_src, o_ref = rest[:n_cast], rest[n_cast:n_cast + 2 * n_roll], rest[n_cast + 2 * n_roll]
    outs, hb_buf = rest[n_cast + 2 * n_roll + 1:-1], rest[-1]
    _cast_slabs(cast_src, outs[:n_cast])
    _roll_buffers(roll_src, outs[n_cast:], roll_by)
    tm = x_ref.shape[0]
    tile = pl.program_id(0) % tiles
    has_hist = tile > 0
    pos = tile * tm + lax.broadcasted_iota(jnp.int32, (tm, 1), 0)
    yp = _pool_rows(jnp.concatenate([jnp.where(has_hist, uh_ref[...], 0.0), u_ref[...]], axis=0), pos,
                    pw_ref[...], ps_ref[...])
    yc = _conv_rows(jnp.concatenate([jnp.where(has_hist, zh_ref[...], 0.0), z_ref[...]], axis=0),
                    gb_ref[...], cw_ref[...], HIST_ROWS)
    x1, hb = _mlp_head(x_ref[...], yp, ya_ref[...], yc, wo_ref, g2_ref)
    hb_buf[...] = hb
    o_ref[...] = x1
    for c in range(D_FF // tf):
        hf = jnp.dot(hb_buf[...], wu_ref[:, c * tf:(c + 1) * tf], preferred_element_type=F32)
        act = jnp.square(jnp.maximum(hf, 0.0)).astype(BF16)
        o_ref[...] += jnp.dot(act, wd_ref[c * tf:(c + 1) * tf, :], preferred_element_type=F32)


def _mlp_s(x, yp, ya, yc, w_out, g2, w_up, w_down, *, layer):
    rows = x.shape[0]
    tf = MLP_FF_CHUNK
    full = lambda c: pl.BlockSpec((rows, c), lambda j: (0, 0))
    return pl.pallas_call(
        _mlp_s_kernel,
        grid=(D_FF // tf,),
        in_specs=[full(D_MODEL), full(POOL_WIDTH), full(GROUP_WIDTH), full(CONV_WIDTH),
                  _layer_spec((MIX_OUT, D_MODEL), layer), _layer_spec((1, D_MODEL), layer),
                  pl.BlockSpec((D_MODEL, tf), lambda j: (0, j)),
                  pl.BlockSpec((tf, D_MODEL), lambda j: (j, 0))],
        out_specs=full(D_MODEL),
        out_shape=jax.ShapeDtypeStruct((rows, D_MODEL), F32),
        scratch_shapes=[pltpu.VMEM((rows, D_MODEL), BF16)],
        compiler_params=_params(),
        name="mlp_s",
    )(x, yp, ya, yc, w_out, g2, w_up, w_down)


def _mlp_p(x, u, gb, z, ya, pw, ps, cw, w_out, g2, w_up, w_down, *, layer, tm, seq, convert=None, roll=(),
           roll_by=0):
    rows = x.shape[0]
    row = lambda c: pl.BlockSpec((tm, c), lambda i: (i, 0))
    conv_in, conv_out_specs, conv_out_shape, conv_args = _cast_specs(convert, rows // tm)
    roll_in, roll_out, roll_shape, roll_args = _roll_specs(roll, rows // tm)
    hist = lambda c: pl.BlockSpec((HIST_ROWS, c), lambda i: (jnp.maximum(i * (tm // HIST_ROWS) - 1, 0), 0))
    return pl.pallas_call(
        functools.partial(_mlp_p_kernel, tf=MLP_FF_CHUNK, tiles=seq // tm, n_cast=len(conv_args),
                          n_roll=len(roll_out), roll_by=roll_by),
        grid=(rows // tm,),
        in_specs=[row(D_MODEL), row(POOL_WIDTH), hist(POOL_WIDTH), row(CONV_WIDTH), row(CONV_WIDTH),
                  hist(CONV_WIDTH), row(GROUP_WIDTH), _layer_spec((POOL_WIDTH, POOL_WIDTH), layer),
                  _layer_spec((1, POOL_WIDTH), layer), _layer_spec((CONV_K, CONV_WIDTH), layer),
                  _layer_spec((MIX_OUT, D_MODEL), layer), _layer_spec((1, D_MODEL), layer),
                  _const_spec((D_MODEL, D_FF)), _const_spec((D_FF, D_MODEL))] + conv_in + roll_in,
        out_specs=[row(D_MODEL)] + conv_out_specs + roll_out,
        out_shape=[jax.ShapeDtypeStruct((rows, D_MODEL), F32)] + conv_out_shape + roll_shape,
        scratch_shapes=[pltpu.VMEM((tm, D_MODEL), BF16)],
        compiler_params=_params(),
        name="mlp_p",
    )(x, u, u, gb, z, z, ya, pw, ps, cw, w_out, g2, w_up, w_down, *conv_args, *roll_args)


def _rope_tables(pos):
    half = ROPE_DIM // 2
    inv = jnp.power(jnp.float32(ROPE_THETA), -jnp.arange(half, dtype=F32) / half)
    ang = pos.astype(F32)[:, None] * inv[None, :]
    cos, sin = jnp.cos(ang), jnp.sin(ang)
    n = pos.shape[0]
    rest = HEAD_DIM - ROPE_DIM
    zh = jnp.zeros((n, half), F32)
    c = jnp.concatenate([cos, cos, jnp.ones((n, rest), F32)], axis=1)
    a = jnp.concatenate([-sin, zh, jnp.zeros((n, rest), F32)], axis=1)
    b = jnp.concatenate([zh, sin, jnp.zeros((n, rest), F32)], axis=1)
    return tuple(jnp.tile(t, (1, GROUP_WIDTH // HEAD_DIM)) for t in (c, a, b))


def _to_buffer_layout(c):
    lead = c.shape[:-4]
    n = len(lead)
    t = jnp.transpose(c, (*range(n), n + 1, n + 2, n + 3, n))
    return t.reshape(*lead, 2 * GROUP_WIDTH, c.shape[-4])


def _from_buffer_layout(t):
    lead = t.shape[:-2]
    n = len(lead)
    c = t.reshape(*lead, 2, 2, HEAD_DIM, t.shape[-1])
    return jnp.transpose(c, (*range(n), n + 3, n, n + 1, n + 2))


def kernel(x_prompt, x_sample, state_pool, state_conv, cache_kv_w128, cache_kv_w512, cache_kv_w2048,
           norm1_g, w_in, q_norm_g, k_norm_g, pool_w, pool_scale, conv_w, w_out, norm2_g, w_up, w_down):
    batch, seq, _ = x_prompt.shape
    n_seq, t_new, _ = x_sample.shape
    depth = w_in.shape[0]
    assert depth == 2, "the in-place stacking of the kv state outputs is written for two layers"
    caches_t = [_to_buffer_layout(c) for c in (cache_kv_w128, cache_kv_w512, cache_kv_w2048)]

    rope_p = _rope_tables(jnp.arange(seq, dtype=jnp.int32))
    rope_s = _rope_tables(jnp.tile(PAST_LEN + jnp.arange(t_new, dtype=jnp.int32), n_seq))
    head_id = jnp.arange(LANES) // HEAD_DIM
    hm = jnp.where(head_id[:, None] == head_id[None, :], 1.0 / HEAD_DIM, 0.0).astype(BF16)
    g1 = norm1_g[:, None, :]
    g2 = norm2_g[:, None, :]
    ps = pool_scale[:, None, :]
    cw = conv_w
    two_heads = lambda gain: jnp.tile(gain, (1, GROUP_WIDTH // HEAD_DIM))[:, None, :]
    qg = two_heads(q_norm_g) * (HEAD_DIM ** -0.5)
    kg = two_heads(k_norm_g)
    n_win = len(POOL_WINDOWS)
    pw = jnp.einsum("gh,lgij->lgihj", jnp.eye(n_win, dtype=F32), pool_w).reshape(depth, POOL_WIDTH, POOL_WIDTH)
    pw = pw.astype(BF16)
    w_out_b = w_out.astype(BF16)
    pst = jnp.pad(state_pool, ((0, 0), (0, 0), (HIST_ROWS - POOL_HIST, 0), (0, 0)))
    cst = jnp.pad(state_conv, ((0, 0), (0, 0), (SUBLANES - (CONV_K - 1), 0), (0, 0)))

    xp = x_prompt.reshape(batch * seq, D_MODEL)
    xs = x_sample.reshape(n_seq * t_new, D_MODEL)
    outs = {k: [] for k in ("pool_p", "conv_p", "pool_s", "conv_s")}
    w_in_l = w_in[0].astype(BF16)
    new_caches = [None] * N_DIL
    kvts = None
    for layer in range(depth):
        if layer == 0:
            (u, gb, z, qc0, qc1, qc2, kvc0, kvc1, kvc2, *tail) = _proj(
                xp, g1, w_in_l, qg, kg, rope_p, hm, layer=layer, tm=PROJ_TILE, sub=PROJ_SUB, batch=batch,
                seq=seq, roll=caches_t[:-1], roll_by=t_new)
            kvts, new_caches[:-1] = tail[:N_DIL], tail[N_DIL:]
            ya, w_up_l, w_down_l = _attn_p((qc0, qc1, qc2), (kvc0, kvc1, kvc2), batch=batch, seq=seq,
                                           convert=(0, (w_up, w_down)))
            xp, w_in_next, new_caches[-1] = _mlp_p(
                xp, u, gb, z, ya.reshape(-1, GROUP_WIDTH), pw, ps, cw, w_out_b, g2, w_up_l, w_down_l, layer=layer,
                tm=MLP_TILE, seq=seq, convert=(1, (w_in,)), roll=caches_t[-1:], roll_by=t_new)
        else:
            (u, gb, z, qc0, qc1, qc2, kvc0, kvc1, kvc2, *tail) = _proj(
                xp, g1, w_in_l, qg, kg, rope_p, hm, layer=layer, tm=PROJ_TILE, sub=PROJ_SUB, batch=batch,
                seq=seq, prev_kvt=kvts, convert=(1, (w_up, w_down)))
            kvts, (w_up_l, w_down_l) = tail[:N_DIL], tail[N_DIL:]
            (ya,) = _attn_p((qc0, qc1, qc2), (kvc0, kvc1, kvc2), batch=batch, seq=seq)
            (xp,) = _mlp_p(xp, u, gb, z, ya.reshape(-1, GROUP_WIDTH), pw, ps, cw, w_out_b, g2, w_up_l, w_down_l,
                           layer=layer, tm=MLP_TILE, seq=seq)
        outs["pool_p"].append(u.reshape(batch, seq, POOL_WIDTH)[:, seq - POOL_HIST:])
        outs["conv_p"].append(z.reshape(batch, seq, CONV_WIDTH)[:, seq - (CONV_K - 1):])

        us, gbs, zs, q0, q1, q2, kn0, kn1, kn2 = _proj(xs, g1, w_in_l, qg, kg, rope_s, hm, layer=layer,
                                                       tm=n_seq * t_new, sub=n_seq * t_new)
        yps, yas, ycs, *new_caches = _mix_s(us, (q0, q1, q2), (kn0, kn1, kn2), gbs, zs, pst, cst, new_caches,
                                           pw, ps, cw, layer=layer, n_seq=n_seq, t_new=t_new)
        xs = _mlp_s(xs, yps.reshape(-1, POOL_WIDTH), yas.reshape(-1, GROUP_WIDTH), ycs.reshape(-1, CONV_WIDTH),
                    w_out_b, g2, w_up_l, w_down_l, layer=layer)
        us3 = us.reshape(n_seq, t_new, POOL_WIDTH)
        zs3 = zs.reshape(n_seq, t_new, CONV_WIDTH)
        outs["pool_s"].append(jnp.concatenate([state_pool[layer], us3], axis=1)[:, -POOL_HIST:])
        outs["conv_s"].append(jnp.concatenate([state_conv[layer], zs3], axis=1)[:, -(CONV_K - 1):])
        if layer == 0:
            w_in_l = w_in_next

    st = lambda k: jnp.stack(outs[k])
    return (xp.reshape(batch, seq, D_MODEL), xs.reshape(n_seq, t_new, D_MODEL),
            st("pool_p"), st("conv_p"),
            *[_from_buffer_layout(kvt) for kvt in kvts],
            st("pool_s"), st("conv_s"),
            *[_from_buffer_layout(c) for c in new_caches])
```

```python
import functools

import jax
import jax.numpy as jnp
from jax import lax
from jax.experimental import pallas as pl
from jax.experimental.pallas import tpu as pltpu

D_MODEL = 1024
HEAD_DIM = 64
POOL_WIDTH = 256
POOL_WINDOWS = (2, 4, 8, 16)
POOL_GROUP = 64
POOL_HIST = 15
ATTN_WIDTH = 384
DIL_PAIRS = ((128, 1), (512, 4), (2048, 16))
N_DIL = 3
GROUP_WIDTH = 128
CONV_WIDTH = 384
CONV_K = 3
ROPE_DIM = 16
ROPE_THETA = 500000.0
D_FF = 4096
IN_COLS = 2560
MIX_OUT = 768
EPS = 1e-6
NEG_INF = -1e30
PAST_LEN = 8192
N_BACK = 128

LANES = 128
SUBLANES = 8
HIST_ROWS = 16
ATTN_TILE = 128
ATTN_UNROLL = 16
ATTN_MERGE_ROWS = 256
PROJ_TILE = 1024
PROJ_SUB = 512
MLP_TILE = 512
MLP_FF_CHUNK = 512
MIX_S_SEQS = 8
VMEM_LIMIT = 56 * 1024 * 1024

F32 = jnp.float32
BF16 = jnp.bfloat16

_Q_OFF = POOL_WIDTH
_GB_OFF = _Q_OFF + 3 * ATTN_WIDTH

_NT = (((1,), (1,)), ((), ()))


def _const_spec(shape):
    return pl.BlockSpec(shape, lambda *_: (0,) * len(shape), pipeline_mode=pl.Buffered(1))


def _layer_spec(shape, layer):
    return pl.BlockSpec((None, *shape), lambda *_: (layer,) + (0,) * len(shape), pipeline_mode=pl.Buffered(1))


def _params():
    return pltpu.CompilerParams(dimension_semantics=("arbitrary",), vmem_limit_bytes=VMEM_LIMIT)


def _cast_specs(convert, n_steps):
    in_specs, out_specs, out_shape, args = [], [], [], []
    if convert is not None:
        layer, weights, *stacked = convert
        for w in weights:
            _, k, n = w.shape
            in_specs.append(pl.BlockSpec((None, k // n_steps, n), lambda i: (layer, i, 0)))
            out_specs.append(pl.BlockSpec((k // n_steps, n), lambda i: (i, 0)))
            out_shape.append(jax.ShapeDtypeStruct((k, n), BF16))
            args.append(w)
        for w in (stacked[0] if stacked else ()):
            depth, k, n = w.shape
            in_specs.append(pl.BlockSpec((depth, k // n_steps, n), lambda i: (0, i, 0)))
            out_specs.append(pl.BlockSpec((depth, k // n_steps, n), lambda i: (0, i, 0)))
            out_shape.append(jax.ShapeDtypeStruct(w.shape, BF16))
            args.append(w)
    return in_specs, out_specs, out_shape, args


def _cast_slabs(srcs, dsts):
    for src, dst in zip(srcs, dsts):
        dst[...] = src[...].astype(BF16)


def _roll_specs(buffers, n_steps):
    in_specs, out_specs, out_shape, args = [], [], [], []
    for cache_t in buffers:
        depth, n_seq, width, window = cache_t.shape
        assert n_seq % n_steps == 0
        per = n_seq // n_steps
        in_specs += [pl.BlockSpec((1, per, width, window), lambda i, l=l: (l, i, 0, 0)) for l in range(depth)]
        args += [cache_t] * depth
        out_specs.append(pl.BlockSpec((depth, per, width, window), lambda i: (0, i, 0, 0)))
        out_shape.append(jax.ShapeDtypeStruct(cache_t.shape, F32))
    return in_specs, out_specs, out_shape, args


def _roll_buffers(srcs, dsts, by):
    srcs = iter(srcs)
    for dst in dsts:
        for layer in range(dst.shape[0]):
            src = next(srcs)
            for i in range(src.shape[1]):
                dst[layer, i] = pltpu.roll(src[0, i], src.shape[-1] - by, 1)


def _proj_kernel(x_ref, g1_ref, w_ref, qg_ref, kg_ref, rc_ref, ra_ref, rb_ref, hm_ref, *rest,
                 class_major, tiles, sub, stacked, n_cast, n_roll, roll_by):
    prev_kvt = None
    if class_major:
        if stacked:
            prev_kvt, rest = rest[:N_DIL], rest[N_DIL:]
        cast_src, rest = rest[:n_cast], rest[n_cast:]
        roll_src, rest = rest[:2 * n_roll], rest[2 * n_roll:]
        (u_ref, gb_ref, z_ref, qc0, qc1, qc2, kvc0, kvc1, kvc2, kvt0, kvt1, kvt2) = rest[:12]
        _cast_slabs(cast_src, rest[12:12 + n_cast])
        _roll_buffers(roll_src, rest[12 + n_cast:12 + n_cast + n_roll], roll_by)
        nat = rest[-1]
    else:
        (u_ref, gb_ref, z_ref, q0, q1, q2, kv0, kv1, kv2) = rest
    tm = x_ref.shape[0]
    hm = hm_ref[...]

    for s in range(tm // sub):
        r0 = s * sub
        rows = slice(r0, r0 + sub)
        x = x_ref[rows, :]
        ms = jnp.mean(x * x, axis=-1, keepdims=True)
        hb = (x * lax.rsqrt(ms + EPS) * g1_ref[...]).astype(BF16)
        p_att = jnp.dot(hb, w_ref[:, _Q_OFF:_GB_OFF], preferred_element_type=F32)
        rc, ra, rb = rc_ref[rows, :], ra_ref[rows, :], rb_ref[rows, :]

        def head_norm_rope(xb, gain):
            msq = jnp.dot((xb * xb).astype(BF16), hm, preferred_element_type=F32)
            xn = xb * lax.rsqrt(msq + EPS) * gain
            return xn * rc + pltpu.roll(xn, LANES - ROPE_DIM // 2, 1) * ra + pltpu.roll(xn, ROPE_DIM // 2, 1) * rb

        for g, (window, dil) in enumerate(DIL_PAIRS):
            lanes = slice(g * GROUP_WIDTH, (g + 1) * GROUP_WIDTH)
            qn = head_norm_rope(p_att[:, :ATTN_WIDTH][:, lanes], qg_ref[...])
            kn = head_norm_rope(p_att[:, ATTN_WIDTH:2 * ATTN_WIDTH][:, lanes], kg_ref[...])
            vv = p_att[:, 2 * ATTN_WIDTH:][:, lanes]
            if not class_major:
                q_ref, kv_ref = ((q0, kv0), (q1, kv1), (q2, kv2))[g]
                q_ref[rows, :] = qn
                kv_ref[rows, :GROUP_WIDTH] = kn
                kv_ref[rows, GROUP_WIDTH:] = vv
                continue
            qc_ref, kvc_ref, kvt_ref = ((qc0, kvc0, kvt0), (qc1, kvc1, kvt1), (qc2, kvc2, kvt2))[g]
            slab = 3 * (s * N_DIL + g)
            nat[slab] = qn
            nat[slab + 1] = kn
            nat[slab + 2] = vv
            per = sub // dil
            dst = slice(s * per, (s + 1) * per)
            for r in range(dil):
                src = pl.ds(r, per, stride=dil) if dil > 1 else slice(None)
                qc_ref[0, r, dst, :] = nat[slab, src, :].astype(BF16)
                kvc_ref[0, r, dst, :GROUP_WIDTH] = nat[slab + 1, src, :].astype(BF16)
                kvc_ref[0, r, dst, GROUP_WIDTH:] = nat[slab + 2, src, :].astype(BF16)
            def put_kvt(cols, part, g=g, kvt_ref=kvt_ref):
                if stacked:
                    kvt_ref[0, 0, :, cols] = prev_kvt[g][0, :, cols]
                    kvt_ref[1, 0, :, cols] = part
                else:
                    kvt_ref[0, :, cols] = part

            if window >= tiles * tm:
                put_kvt(rows, jnp.concatenate([kn, vv], axis=1).T)
            else:
                first_kept = tm - min(window, tm)
                lo = max(r0, first_kept)
                if lo < r0 + sub:
                    part = jnp.concatenate([kn[lo - r0:], vv[lo - r0:]], axis=1).T
                    put_kvt(slice(lo - first_kept, r0 + sub - first_kept), part)

        u_ref[rows, :] = jnp.dot(hb, w_ref[:, :POOL_WIDTH], preferred_element_type=F32)
        p_gate = jnp.dot(hb, w_ref[:, _GB_OFF:], preferred_element_type=F32)
        gb_ref[rows, :] = p_gate[:, :CONV_WIDTH]
        z_ref[rows, :] = p_gate[:, CONV_WIDTH:2 * CONV_WIDTH] * p_gate[:, 2 * CONV_WIDTH:]


def _proj(x, g1, w_in, qg, kg, rope, hm, *, layer, tm, sub, batch=None, seq=None, prev_kvt=None, convert=None,
          roll=(), roll_by=0):
    rows = x.shape[0]
    stacked = prev_kvt is not None
    n_tab = rope[0].shape[0] // tm
    class_major = batch is not None
    tiles = seq // tm if class_major else 1
    row = lambda c: pl.BlockSpec((tm, c), lambda i: (i, 0))
    tab = pl.BlockSpec((tm, LANES), lambda i: (i % n_tab, 0))
    out_shape = [jax.ShapeDtypeStruct((rows, POOL_WIDTH), F32)] + [jax.ShapeDtypeStruct((rows, CONV_WIDTH), F32)] * 2
    out_specs = [row(POOL_WIDTH), row(CONV_WIDTH), row(CONV_WIDTH)]
    scratch = []
    prev_specs = []
    cast_in, cast_out, cast_shape, cast_args = _cast_specs(convert, rows // tm)
    assert class_major or (convert is None and not roll)
    roll_in, roll_out, roll_shape, roll_args = _roll_specs(roll, rows // tm)
    if class_major:
        for width in (GROUP_WIDTH, 2 * GROUP_WIDTH):
            for (_, dil) in DIL_PAIRS:
                out_shape.append(jax.ShapeDtypeStruct((batch, dil, seq // dil, width), BF16))
                out_specs.append(pl.BlockSpec((1, dil, tm // dil, width), lambda i: (i // tiles, 0, i % tiles, 0)))
        for (window, _) in DIL_PAIRS:
            assert window >= seq or window <= tm, "kv buffer rows must come from the last row tile"
            cols = tm if window >= seq else window
            col_blk = (lambda i: i % tiles) if window >= seq else (lambda i: 0)
            kvt_spec = pl.BlockSpec((1, 2 * GROUP_WIDTH, cols), lambda i, c=col_blk: (i // tiles, 0, c(i)))
            if stacked:
                prev_specs.append(kvt_spec)
                out_shape.append(jax.ShapeDtypeStruct((2, batch, 2 * GROUP_WIDTH, window), F32))
                out_specs.append(pl.BlockSpec((2, 1, 2 * GROUP_WIDTH, cols),
                                              lambda i, c=col_blk: (0, i // tiles, 0, c(i))))
            else:
                out_shape.append(jax.ShapeDtypeStruct((batch, 2 * GROUP_WIDTH, window), F32))
                out_specs.append(kvt_spec)
        scratch = [pltpu.VMEM((3 * N_DIL * (tm // sub), sub, GROUP_WIDTH), F32)]
    else:
        out_shape += [jax.ShapeDtypeStruct((rows, GROUP_WIDTH), F32)] * N_DIL
        out_shape += [jax.ShapeDtypeStruct((rows, 2 * GROUP_WIDTH), F32)] * N_DIL
        out_specs += [row(GROUP_WIDTH)] * N_DIL + [row(2 * GROUP_WIDTH)] * N_DIL
    return pl.pallas_call(
        functools.partial(_proj_kernel, class_major=class_major, tiles=tiles, sub=sub, stacked=stacked,
                          n_cast=len(cast_args), n_roll=len(roll_out), roll_by=roll_by),
        grid=(rows // tm,),
        in_specs=[row(D_MODEL), _layer_spec((1, D_MODEL), layer), _const_spec((D_MODEL, IN_COLS)),
                  _layer_spec((1, LANES), layer), _layer_spec((1, LANES), layer), tab, tab, tab,
                  _const_spec((LANES, LANES))] + prev_specs + cast_in + roll_in,
        out_specs=out_specs + cast_out + roll_out,
        out_shape=out_shape + cast_shape + roll_shape,
        scratch_shapes=scratch,
        compiler_params=_params(),
        name="proj",
    )(x, g1, w_in, qg, kg, *rope, hm, *(prev_kvt or ()), *cast_args, *roll_args)


def _lane_lt(shape, bound):
    return lax.broadcasted_iota(jnp.int32, shape, len(shape) - 1) < bound


def _stack_heads(q):
    first = _lane_lt(q.shape, HEAD_DIM)
    zero = jnp.zeros_like(q)
    return jnp.concatenate([jnp.where(first, q, zero), jnp.where(first, zero, q)], axis=0)


def _unstack(o2, l2, m2):
    n = o2.shape[0] // 2
    first = _lane_lt((n, GROUP_WIDTH), HEAD_DIM)
    return tuple(jnp.where(first, a[:n], a[n:]) for a in (o2, l2, m2))


def _merge(a, b):
    m = jnp.maximum(a[2], b[2])
    wa = jnp.exp(a[2] - m)
    wb = jnp.exp(b[2] - m)
    return wa * a[0] + wb * b[0], wa * a[1] + wb * b[1], m


def _attn_unit(q, kv, bias):
    nk = kv.shape[0]
    k = kv[:, :GROUP_WIDTH]
    v1 = jnp.concatenate([kv[:, GROUP_WIDTH:], jnp.ones((nk, GROUP_WIDTH), BF16)], axis=1)
    s = lax.dot_general(_stack_heads(q), k, _NT, preferred_element_type=F32) + bias
    mrow = jnp.max(s, axis=-1, keepdims=True)
    p = jnp.exp(s - mrow).astype(BF16)
    r = jnp.dot(p, v1, preferred_element_type=F32)
    return _unstack(r[:, :GROUP_WIDTH], r[:, GROUP_WIDTH:], mrow)


def _attn_p_kernel(qc0, qc1, qc2, kvc0, kvc1, kvc2, *rest, seq, n_cast):
    cast_src, y_ref, rest = rest[:n_cast], rest[n_cast], rest[n_cast + 1:]
    _cast_slabs(cast_src, rest[:n_cast])
    scratch = rest[n_cast:]
    parts = [scratch[3 * g:3 * g + 3] for g in range(N_DIL)]
    bias_ref = scratch[3 * N_DIL]
    acc_o, acc_l, acc_m = parts[0]
    @pl.when(pl.program_id(0) == 0)
    def _():
        qi = lax.broadcasted_iota(jnp.int32, (2 * ATTN_TILE, 2 * ATTN_TILE), 0) & (ATTN_TILE - 1)
        ki = lax.broadcasted_iota(jnp.int32, (2 * ATTN_TILE, 2 * ATTN_TILE), 1)
        for first_key_back in (0, 1):
            dist = qi - ki + first_key_back * ATTN_TILE
            bias_ref[first_key_back] = jnp.where((dist >= 0) & (dist <= N_BACK), 0.0, NEG_INF)

    for g, ((_, dil), q_ref, kv_ref) in enumerate(zip(DIL_PAIRS, (qc0, qc1, qc2), (kvc0, kvc1, kvc2))):
        dst = parts[g]
        n_sub = seq // dil // ATTN_TILE

        def unit(idx, q_ref=q_ref, kv_ref=kv_ref, n_sub=n_sub, dil=dil):
            if n_sub == 1:
                r, sub = idx, 0
                res = _attn_unit(q_ref[0, r], kv_ref[0, r], bias_ref[0, :, :ATTN_TILE])
            else:
                r, sub = (idx // n_sub, idx % n_sub) if dil > 1 else (0, idx)
                back = jnp.minimum(sub, 1)
                q0 = pl.multiple_of(sub * ATTN_TILE, ATTN_TILE)
                k0 = pl.multiple_of((sub - back) * ATTN_TILE, ATTN_TILE)
                res = _attn_unit(q_ref[0, r, pl.ds(q0, ATTN_TILE), :], kv_ref[0, r, pl.ds(k0, 2 * ATTN_TILE), :],
                                 bias_ref[back])
            start = sub * (ATTN_TILE * dil) + r
            rows = pl.ds(start, ATTN_TILE, stride=dil) if dil > 1 else pl.ds(pl.multiple_of(start, ATTN_TILE), ATTN_TILE)
            return rows, res

        def units_body(it, carry, unit=unit, dst=dst):
            for j in range(ATTN_UNROLL):
                rows, res = unit(it * ATTN_UNROLL + j)
                for ref, val in zip(dst, res):
                    ref[rows, :] = val
            return carry
        n_iter = dil * n_sub // ATTN_UNROLL
        if n_iter == 1:
            units_body(0, 0)
        else:
            lax.fori_loop(0, n_iter, units_body, 0)

        if g > 0:
            for c in range(seq // ATTN_MERGE_ROWS):
                rows = slice(c * ATTN_MERGE_ROWS, (c + 1) * ATTN_MERGE_ROWS)
                o, l, m = _merge((acc_o[rows, :], acc_l[rows, :], acc_m[rows, :]),
                                 tuple(ref[rows, :] for ref in parts[g]))
                if g == N_DIL - 1:
                    y_ref[0, rows, :] = (o / l).astype(y_ref.dtype)
                else:
                    acc_o[rows, :] = o
                    acc_l[rows, :] = l
                    acc_m[rows, :] = m


def _attn_p(qcs, kvcs, *, batch, seq, convert=None):
    in_specs = [pl.BlockSpec((1,) + a.shape[1:], lambda b: (b, 0, 0, 0)) for a in (*qcs, *kvcs)]
    cast_in, cast_out, cast_shape, cast_args = _cast_specs(convert, batch)
    acc = pltpu.VMEM((seq, GROUP_WIDTH), F32)
    return pl.pallas_call(
        functools.partial(_attn_p_kernel, seq=seq, n_cast=len(cast_args)),
        grid=(batch,),
        in_specs=in_specs + cast_in,
        out_specs=[pl.BlockSpec((1, seq, GROUP_WIDTH), lambda b: (b, 0, 0))] + cast_out,
        out_shape=[jax.ShapeDtypeStruct((batch, seq, GROUP_WIDTH), BF16)] + cast_shape,
        scratch_shapes=[acc] * (3 * N_DIL) + [pltpu.VMEM((2, 2 * ATTN_TILE, 2 * ATTN_TILE), F32)],
        compiler_params=_params(),
        name="attn_p",
    )(*qcs, *kvcs, *cast_args)


def _pool_rows(ext, pos, pw, ps):
    return jnp.dot(_pool_diff(ext, pos), pw, preferred_element_type=F32) * ps


def _pool_diff(ext, pos):
    s2 = ext + pltpu.roll(ext, 1, 0)
    s4 = s2 + pltpu.roll(s2, 2, 0)
    s8 = s4 + pltpu.roll(s4, 4, 0)
    s16 = s8 + pltpu.roll(s8, 8, 0)
    lane = lax.broadcasted_iota(jnp.int32, (1, POOL_WIDTH), 1)
    grp = [lane < (j + 1) * POOL_GROUP for j in range(3)]
    win = jnp.where(grp[0], s2, jnp.where(grp[1], s4, jnp.where(grp[2], s8, s16)))[HIST_ROWS:]
    width = jnp.where(grp[0], POOL_WINDOWS[0], jnp.where(grp[1], POOL_WINDOWS[1],
                      jnp.where(grp[2], POOL_WINDOWS[2], POOL_WINDOWS[3])))
    cnt = jnp.minimum(pos + 1, width).astype(F32)
    return (win / cnt - ext[HIST_ROWS:]).astype(BF16)


def _conv_rows(ext, gb, cw, hist):
    y = cw[0:1] * pltpu.roll(ext, 2, 0) + cw[1:2] * pltpu.roll(ext, 1, 0) + cw[2:3] * ext
    return gb * y[hist:]


def _attn_s_group(q, new_t, cache_t, window, dil):
    t = q.shape[0]
    q2 = _stack_heads(q).astype(BF16)
    s_c = jnp.dot(q2, cache_t[:GROUP_WIDTH].astype(BF16), preferred_element_type=F32)
    s_n = jnp.dot(q2, new_t[:GROUP_WIDTH].astype(BF16), preferred_element_type=F32)
    tq = lax.broadcasted_iota(jnp.int32, (2 * t, 1), 0) & (t - 1)
    rc = (lax.broadcasted_iota(jnp.int32, (1, window), 1) + t) & (window - 1)
    dist_c = window + tq - rc
    ok_c = (dist_c <= window) & ((dist_c & (dil - 1)) == 0) & (PAST_LEN - window + rc >= 0)
    tn = lax.broadcasted_iota(jnp.int32, (1, LANES), 1) - (LANES - t)
    dist_n = tq - tn
    ok_n = (tn >= 0) & (dist_n >= 0) & ((dist_n & (dil - 1)) == 0)
    s_c = jnp.where(ok_c, s_c, NEG_INF)
    s_n = jnp.where(ok_n, s_n, NEG_INF)
    mrow = jnp.maximum(jnp.max(s_c, axis=-1, keepdims=True), jnp.max(s_n, axis=-1, keepdims=True))
    p_c = jnp.exp(s_c - mrow)
    p_n = jnp.exp(s_n - mrow)
    o2 = (lax.dot_general(p_c.astype(BF16), cache_t[GROUP_WIDTH:].astype(BF16), _NT, preferred_element_type=F32)
          + lax.dot_general(p_n.astype(BF16), new_t[GROUP_WIDTH:].astype(BF16), _NT, preferred_element_type=F32))
    l2 = jnp.sum(p_c, axis=-1, keepdims=True) + jnp.sum(p_n, axis=-1, keepdims=True)
    return _unstack(o2, l2, mrow)


def _mix_s_kernel(u_ref, q0_ref, q1_ref, q2_ref, kn0_ref, kn1_ref, kn2_ref, gb_ref, z_ref, pst_ref, cst_ref,
                  pw_ref, ps_ref, cw_ref, c0_ref, c1_ref, c2_ref, yp_ref, ya_ref, yc_ref, nc0_ref, nc1_ref, nc2_ref,
                  *, t_new, n_per):
    keep_old = _lane_lt((2 * GROUP_WIDTH, LANES), LANES - t_new)
    pos = PAST_LEN + lax.broadcasted_iota(jnp.int32, (t_new, 1), 0)
    for i in range(n_per):
        state = None
        for (window, dil), q_ref, kn_ref, c_ref, nc_ref in zip(
                DIL_PAIRS, (q0_ref, q1_ref, q2_ref), (kn0_ref, kn1_ref, kn2_ref),
                (c0_ref, c1_ref, c2_ref), (nc0_ref, nc1_ref, nc2_ref)):
            kv_new = kn_ref[i]
            new_t = jnp.concatenate([jnp.zeros((LANES - t_new, 2 * GROUP_WIDTH), F32), kv_new], axis=0).T
            cache_t = c_ref[0, i]
            part = _attn_s_group(q_ref[i], new_t, cache_t, window, dil)
            state = part if state is None else _merge(state, part)
            nc_ref[0, i] = jnp.where(keep_old, cache_t[:, window - LANES:], new_t)
        ya_ref[i] = state[0] / state[1]

        u_ext = jnp.concatenate([pst_ref[i], u_ref[i]], axis=0)
        yp_ref[i] = _pool_rows(u_ext, pos, pw_ref[...], ps_ref[...])
        z_ext = jnp.concatenate([cst_ref[i], z_ref[i]], axis=0)
        yc_ref[i] = _conv_rows(z_ext, gb_ref[i], cw_ref[...], cst_ref.shape[1])


def _mix_s(u, qs, kns, gb, z, pst, cst, rotated_bufs, pw, ps, cw, *, layer, n_seq, t_new):
    n_per = MIX_S_SEQS
    blk = lambda r, c: pl.BlockSpec((n_per, r, c), lambda b: (b, 0, 0))
    three = lambda a: a.reshape(n_seq, -1, a.shape[-1])
    windows = [w for (w, _) in DIL_PAIRS]
    in_specs = ([blk(t_new, POOL_WIDTH)] + [blk(t_new, GROUP_WIDTH)] * N_DIL + [blk(t_new, 2 * GROUP_WIDTH)] * N_DIL
                + [blk(t_new, CONV_WIDTH)] * 2
                + [pl.BlockSpec((None, n_per) + a.shape[2:], lambda b: (layer, b, 0, 0)) for a in (pst, cst)]
                + [_layer_spec((POOL_WIDTH, POOL_WIDTH), layer), _layer_spec((1, POOL_WIDTH), layer),
                   _layer_spec((CONV_K, CONV_WIDTH), layer)]
                + [pl.BlockSpec((1, n_per, 2 * GROUP_WIDTH, w), lambda b: (layer, b, 0, 0)) for w in windows])
    args = [three(u), *[three(q) for q in qs], *[three(k) for k in kns], three(gb), three(z), pst, cst, pw, ps, cw,
            *rotated_bufs]
    n_in = len(args)
    out_specs = ([blk(t_new, POOL_WIDTH), blk(t_new, GROUP_WIDTH), blk(t_new, CONV_WIDTH)]
                 + [pl.BlockSpec((1, n_per, 2 * GROUP_WIDTH, LANES), lambda b, w=w: (layer, b, 0, w // LANES - 1))
                    for w in windows])
    out_shape = ([jax.ShapeDtypeStruct((n_seq, t_new, c), F32) for c in (POOL_WIDTH, GROUP_WIDTH, CONV_WIDTH)]
                 + [jax.ShapeDtypeStruct(c.shape, F32) for c in rotated_bufs])
    return pl.pallas_call(
        functools.partial(_mix_s_kernel, t_new=t_new, n_per=n_per),
        grid=(n_seq // n_per,),
        in_specs=in_specs,
        out_specs=out_specs,
        out_shape=out_shape,
        input_output_aliases={n_in - N_DIL + g: 3 + g for g in range(N_DIL)},
        compiler_params=_params(),
        name="mix_s",
    )(*args)


def _mlp_head(x, yp, ya, yc, wo_ref, g2_ref):
    mixed = jnp.concatenate([yp.astype(BF16), ya.astype(BF16), yc.astype(BF16)], axis=1)
    x1 = x + jnp.dot(mixed, wo_ref[...], preferred_element_type=F32)
    ms = jnp.mean(x1 * x1, axis=-1, keepdims=True)
    return x1, (x1 * lax.rsqrt(ms + EPS) * g2_ref[...]).astype(BF16)


def _mlp_s_kernel(x_ref, yp_ref, ya_ref, yc_ref, wo_ref, g2_ref, wu_ref, wd_ref, o_ref, hb_buf):
    @pl.when(pl.program_id(0) == 0)
    def _():
        x1, hb = _mlp_head(x_ref[...], yp_ref[...], ya_ref[...], yc_ref[...], wo_ref, g2_ref)
        o_ref[...] = x1
        hb_buf[...] = hb

    hf = jnp.dot(hb_buf[...], wu_ref[...], preferred_element_type=F32)
    act = jnp.square(jnp.maximum(hf, 0.0)).astype(BF16)
    o_ref[...] += jnp.dot(act, wd_ref[...], preferred_element_type=F32)


def _mlp_p_kernel(x_ref, u_ref, uh_ref, gb_ref, z_ref, zh_ref, ya_ref, pw_ref, ps_ref, cw_ref,
                  wo_ref, g2_ref, wu_ref, wd_ref, *rest, tf, tiles, n_cast, n_roll, roll_by):
    cast_src, roll_src, o_ref = rest[:n_cast], rest[n_cast:n_cast + 2 * n_roll], rest[n_cast + 2 * n_roll]
    outs, hb_buf = rest[n_cast + 2 * n_roll + 1:-1], rest[-1]
    _cast_slabs(cast_src, outs[:n_cast])
    _roll_buffers(roll_src, outs[n_cast:], roll_by)
    tm = x_ref.shape[0]
    tile = pl.program_id(0) % tiles
    has_hist = tile > 0
    pos = tile * tm + lax.broadcasted_iota(jnp.int32, (tm, 1), 0)
    yp = _pool_rows(jnp.concatenate([jnp.where(has_hist, uh_ref[...], 0.0), u_ref[...]], axis=0), pos,
                    pw_ref[...], ps_ref[...])
    yc = _conv_rows(jnp.concatenate([jnp.where(has_hist, zh_ref[...], 0.0), z_ref[...]], axis=0),
                    gb_ref[...], cw_ref[...], HIST_ROWS)
    x1, hb = _mlp_head(x_ref[...], yp, ya_ref[...], yc, wo_ref, g2_ref)
    hb_buf[...] = hb
    o_ref[...] = x1
    for c in range(D_FF // tf):
        hf = jnp.dot(hb_buf[...], wu_ref[:, c * tf:(c + 1) * tf], preferred_element_type=F32)
        act = jnp.square(jnp.maximum(hf, 0.0)).astype(BF16)
        o_ref[...] += jnp.dot(act, wd_ref[c * tf:(c + 1) * tf, :], preferred_element_type=F32)


def _mlp_s(x, yp, ya, yc, w_out, g2, w_up, w_down, *, layer):
    rows = x.shape[0]
    tf = MLP_FF_CHUNK
    full = lambda c: pl.BlockSpec((rows, c), lambda j: (0, 0))
    return pl.pallas_call(
        _mlp_s_kernel,
        grid=(D_FF // tf,),
        in_specs=[full(D_MODEL), full(POOL_WIDTH), full(GROUP_WIDTH), full(CONV_WIDTH),
                  _layer_spec((MIX_OUT, D_MODEL), layer), _layer_spec((1, D_MODEL), layer),
                  pl.BlockSpec((D_MODEL, tf), lambda j: (0, j)),
                  pl.BlockSpec((tf, D_MODEL), lambda j: (j, 0))],
        out_specs=full(D_MODEL),
        out_shape=jax.ShapeDtypeStruct((rows, D_MODEL), F32),
        scratch_shapes=[pltpu.VMEM((rows, D_MODEL), BF16)],
        compiler_params=_params(),
        name="mlp_s",
    )(x, yp, ya, yc, w_out, g2, w_up, w_down)


def _mlp_p(x, u, gb, z, ya, pw, ps, cw, w_out, g2, w_up, w_down, *, layer, tm, seq, convert=None, roll=(),
           roll_by=0):
    rows = x.shape[0]
    row = lambda c: pl.BlockSpec((tm, c), lambda i: (i, 0))
    conv_in, conv_out_specs, conv_out_shape, conv_args = _cast_specs(convert, rows // tm)
    roll_in, roll_out, roll_shape, roll_args = _roll_specs(roll, rows // tm)
    hist = lambda c: pl.BlockSpec((HIST_ROWS, c), lambda i: (jnp.maximum(i * (tm // HIST_ROWS) - 1, 0), 0))
    return pl.pallas_call(
        functools.partial(_mlp_p_kernel, tf=MLP_FF_CHUNK, tiles=seq // tm, n_cast=len(conv_args),
                          n_roll=len(roll_out), roll_by=roll_by),
        grid=(rows // tm,),
        in_specs=[row(D_MODEL), row(POOL_WIDTH), hist(POOL_WIDTH), row(CONV_WIDTH), row(CONV_WIDTH),
                  hist(CONV_WIDTH), row(GROUP_WIDTH), _layer_spec((POOL_WIDTH, POOL_WIDTH), layer),
                  _layer_spec((1, POOL_WIDTH), layer), _layer_spec((CONV_K, CONV_WIDTH), layer),
                  _layer_spec((MIX_OUT, D_MODEL), layer), _layer_spec((1, D_MODEL), layer),
                  _const_spec((D_MODEL, D_FF)), _const_spec((D_FF, D_MODEL))] + conv_in + roll_in,
        out_specs=[row(D_MODEL)] + conv_out_specs + roll_out,
        out_shape=[jax.ShapeDtypeStruct((rows, D_MODEL), F32)] + conv_out_shape + roll_shape,
        scratch_shapes=[pltpu.VMEM((tm, D_MODEL), BF16)],
        compiler_params=_params(),
        name="mlp_p",
    )(x, u, u, gb, z, z, ya, pw, ps, cw, w_out, g2, w_up, w_down, *conv_args, *roll_args)


def _rope_tables(pos):
    half = ROPE_DIM // 2
    inv = jnp.power(jnp.float32(ROPE_THETA), -jnp.arange(half, dtype=F32) / half)
    ang = pos.astype(F32)[:, None] * inv[None, :]
    cos, sin = jnp.cos(ang), jnp.sin(ang)
    n = pos.shape[0]
    rest = HEAD_DIM - ROPE_DIM
    zh = jnp.zeros((n, half), F32)
    c = jnp.concatenate([cos, cos, jnp.ones((n, rest), F32)], axis=1)
    a = jnp.concatenate([-sin, zh, jnp.zeros((n, rest), F32)], axis=1)
    b = jnp.concatenate([zh, sin, jnp.zeros((n, rest), F32)], axis=1)
    return tuple(jnp.tile(t, (1, GROUP_WIDTH // HEAD_DIM)) for t in (c, a, b))


def _to_buffer_layout(c):
    lead = c.shape[:-4]
    n = len(lead)
    t = jnp.transpose(c, (*range(n), n + 1, n + 2, n + 3, n))
    return t.reshape(*lead, 2 * GROUP_WIDTH, c.shape[-4])


def _from_buffer_layout(t):
    lead = t.shape[:-2]
    n = len(lead)
    c = t.reshape(*lead, 2, 2, HEAD_DIM, t.shape[-1])
    return jnp.transpose(c, (*range(n), n + 3, n, n + 1, n + 2))


def kernel(x_prompt, x_sample, state_pool, state_conv, cache_kv_w128, cache_kv_w512, cache_kv_w2048,
           norm1_g, w_in, q_norm_g, k_norm_g, pool_w, pool_scale, conv_w, w_out, norm2_g, w_up, w_down):
    batch, seq, _ = x_prompt.shape
    n_seq, t_new, _ = x_sample.shape
    depth = w_in.shape[0]
    assert depth == 2, "the in-place stacking of the kv state outputs is written for two layers"
    caches_t = [_to_buffer_layout(c) for c in (cache_kv_w128, cache_kv_w512, cache_kv_w2048)]

    rope_p = _rope_tables(jnp.arange(seq, dtype=jnp.int32))
    rope_s = _rope_tables(jnp.tile(PAST_LEN + jnp.arange(t_new, dtype=jnp.int32), n_seq))
    head_id = jnp.arange(LANES) // HEAD_DIM
    hm = jnp.where(head_id[:, None] == head_id[None, :], 1.0 / HEAD_DIM, 0.0).astype(BF16)
    g1 = norm1_g[:, None, :]
    g2 = norm2_g[:, None, :]
    ps = pool_scale[:, None, :]
    cw = conv_w
    two_heads = lambda gain: jnp.tile(gain, (1, GROUP_WIDTH // HEAD_DIM))[:, None, :]
    qg = two_heads(q_norm_g) * (HEAD_DIM ** -0.5)
    kg = two_heads(k_norm_g)
    n_win = len(POOL_WINDOWS)
    pw = jnp.einsum("gh,lgij->lgihj", jnp.eye(n_win, dtype=F32), pool_w).reshape(depth, POOL_WIDTH, POOL_WIDTH)
    pw = pw.astype(BF16)
    pst = jnp.pad(state_pool, ((0, 0), (0, 0), (HIST_ROWS - POOL_HIST, 0), (0, 0)))
    cst = jnp.pad(state_conv, ((0, 0), (0, 0), (SUBLANES - (CONV_K - 1), 0), (0, 0)))

    xp = x_prompt.reshape(batch * seq, D_MODEL)
    xs = x_sample.reshape(n_seq * t_new, D_MODEL)
    outs = {k: [] for k in ("pool_p", "conv_p", "pool_s", "conv_s")}
    w_in_l = w_in[0].astype(BF16)
    new_caches = [None] * N_DIL
    kvts = None
    for layer in range(depth):
        if layer == 0:
            (u, gb, z, qc0, qc1, qc2, kvc0, kvc1, kvc2, *tail) = _proj(
                xp, g1, w_in_l, qg, kg, rope_p, hm, layer=layer, tm=PROJ_TILE, sub=PROJ_SUB, batch=batch,
                seq=seq, roll=caches_t[:-1], roll_by=t_new)
            kvts, new_caches[:-1] = tail[:N_DIL], tail[N_DIL:]
            ya, w_up_l, w_down_l, w_out_b = _attn_p((qc0, qc1, qc2), (kvc0, kvc1, kvc2), batch=batch, seq=seq,
                                                    convert=(0, (w_up, w_down), (w_out,)))
            xp, w_in_next, new_caches[-1] = _mlp_p(
                xp, u, gb, z, ya.reshape(-1, GROUP_WIDTH), pw, ps, cw, w_out_b, g2, w_up_l, w_down_l, layer=layer,
                tm=MLP_TILE, seq=seq, convert=(1, (w_in,)), roll=caches_t[-1:], roll_by=t_new)
        else:
            (u, gb, z, qc0, qc1, qc2, kvc0, kvc1, kvc2, *tail) = _proj(
                xp, g1, w_in_l, qg, kg, rope_p, hm, layer=layer, tm=PROJ_TILE, sub=PROJ_SUB, batch=batch,
                seq=seq, prev_kvt=kvts, convert=(1, (w_up, w_down)))
            kvts, (w_up_l, w_down_l) = tail[:N_DIL], tail[N_DIL:]
            (ya,) = _attn_p((qc0, qc1, qc2), (kvc0, kvc1, kvc2), batch=batch, seq=seq)
            (xp,) = _mlp_p(xp, u, gb, z, ya.reshape(-1, GROUP_WIDTH), pw, ps, cw, w_out_b, g2, w_up_l, w_down_l,
                           layer=layer, tm=MLP_TILE, seq=seq)
        outs["pool_p"].append(u.reshape(batch, seq, POOL_WIDTH)[:, seq - POOL_HIST:])
        outs["conv_p"].append(z.reshape(batch, seq, CONV_WIDTH)[:, seq - (CONV_K - 1):])

        us, gbs, zs, q0, q1, q2, kn0, kn1, kn2 = _proj(xs, g1, w_in_l, qg, kg, rope_s, hm, layer=layer,
                                                       tm=n_seq * t_new, sub=n_seq * t_new)
        yps, yas, ycs, *new_caches = _mix_s(us, (q0, q1, q2), (kn0, kn1, kn2), gbs, zs, pst, cst, new_caches,
                                           pw, ps, cw, layer=layer, n_seq=n_seq, t_new=t_new)
        xs = _mlp_s(xs, yps.reshape(-1, POOL_WIDTH), yas.reshape(-1, GROUP_WIDTH), ycs.reshape(-1, CONV_WIDTH),
                    w_out_b, g2, w_up_l, w_down_l, layer=layer)
        us3 = us.reshape(n_seq, t_new, POOL_WIDTH)
        zs3 = zs.reshape(n_seq, t_new, CONV_WIDTH)
        outs["pool_s"].append(jnp.concatenate([state_pool[layer], us3], axis=1)[:, -POOL_HIST:])
        outs["conv_s"].append(jnp.concatenate([state_conv[layer], zs3], axis=1)[:, -(CONV_K - 1):])
        if layer == 0:
            w_in_l = w_in_next

    st = lambda k: jnp.stack(outs[k])
    return (xp.reshape(batch, seq, D_MODEL), xs.reshape(n_seq, t_new, D_MODEL),
            st("pool_p"), st("conv_p"),
            *[_from_buffer_layout(kvt) for kvt in kvts],
            st("pool_s"), st("conv_s"),
            *[_from_buffer_layout(c) for c in new_caches])
```

```python
import functools

import jax
import jax.numpy as jnp
import numpy as np
from jax import lax
from jax.experimental import pallas as pl
from jax.experimental.pallas import tpu as pltpu

D_MODEL = 1024
HEAD_DIM = 64
POOL_WIDTH = 256
POOL_WINDOWS = (2, 4, 8, 16)
POOL_GROUP = 64
POOL_HIST = 15
ATTN_WIDTH = 384
DIL_PAIRS = ((128, 1), (512, 4), (2048, 16))
N_DIL = 3
GROUP_WIDTH = 128
CONV_WIDTH = 384
CONV_K = 3
ROPE_DIM = 16
ROPE_THETA = 500000.0
D_FF = 4096
IN_COLS = 2560
MIX_OUT = 768
EPS = 1e-6
NEG_INF = -1e30
PAST_LEN = 8192
N_BACK = 128

LANES = 128
SUBLANES = 8
HIST_ROWS = 16
ATTN_TILE = 128
ATTN_UNROLL = 16
ATTN_MERGE_ROWS = 256
PROJ_TILE = 1024
PROJ_SUB = 512
MLP_TILE = 512
MLP_FF_CHUNK = 512
MIX_S_SEQS = 8
VMEM_LIMIT = 56 * 1024 * 1024

F32 = jnp.float32
BF16 = jnp.bfloat16

_Q_OFF = POOL_WIDTH
_GB_OFF = _Q_OFF + 3 * ATTN_WIDTH

_NT = (((1,), (1,)), ((), ()))


def _const_spec(shape):
    return pl.BlockSpec(shape, lambda *_: (0,) * len(shape), pipeline_mode=pl.Buffered(1))


def _layer_spec(shape, layer):
    return pl.BlockSpec((None, *shape), lambda *_: (layer,) + (0,) * len(shape), pipeline_mode=pl.Buffered(1))


def _params():
    return pltpu.CompilerParams(dimension_semantics=("arbitrary",), vmem_limit_bytes=VMEM_LIMIT)


def _cast_specs(convert, n_steps):
    in_specs, out_specs, out_shape, args = [], [], [], []
    if convert is not None:
        layer, weights, *stacked = convert
        for w in weights:
            _, k, n = w.shape
            in_specs.append(pl.BlockSpec((None, k // n_steps, n), lambda i: (layer, i, 0)))
            out_specs.append(pl.BlockSpec((k // n_steps, n), lambda i: (i, 0)))
            out_shape.append(jax.ShapeDtypeStruct((k, n), BF16))
            args.append(w)
        for w in (stacked[0] if stacked else ()):
            depth, k, n = w.shape
            in_specs.append(pl.BlockSpec((depth, k // n_steps, n), lambda i: (0, i, 0)))
            out_specs.append(pl.BlockSpec((depth, k // n_steps, n), lambda i: (0, i, 0)))
            out_shape.append(jax.ShapeDtypeStruct(w.shape, BF16))
            args.append(w)
    return in_specs, out_specs, out_shape, args


def _cast_slabs(srcs, dsts):
    for src, dst in zip(srcs, dsts):
        dst[...] = src[...].astype(BF16)


def _roll_specs(buffers, n_steps):
    in_specs, out_specs, out_shape, args = [], [], [], []
    for cache_t in buffers:
        depth, n_seq, width, window = cache_t.shape
        assert n_seq % n_steps == 0
        per = n_seq // n_steps
        in_specs += [pl.BlockSpec((1, per, width, window), lambda i, l=l: (l, i, 0, 0)) for l in range(depth)]
        args += [cache_t] * depth
        out_specs.append(pl.BlockSpec((depth, per, width, window), lambda i: (0, i, 0, 0)))
        out_shape.append(jax.ShapeDtypeStruct(cache_t.shape, F32))
    return in_specs, out_specs, out_shape, args


def _roll_buffers(srcs, dsts, by):
    srcs = iter(srcs)
    for dst in dsts:
        for layer in range(dst.shape[0]):
            src = next(srcs)
            for i in range(src.shape[1]):
                dst[layer, i] = pltpu.roll(src[0, i], src.shape[-1] - by, 1)


def _proj_kernel(x_ref, g1_ref, w_ref, qg_ref, kg_ref, rc_ref, ra_ref, rb_ref, hm_ref, *rest,
                 class_major, tiles, sub, stacked, n_cast, n_roll, roll_by):
    prev_kvt = None
    if class_major:
        if stacked:
            prev_kvt, rest = rest[:N_DIL], rest[N_DIL:]
        cast_src, rest = rest[:n_cast], rest[n_cast:]
        roll_src, rest = rest[:2 * n_roll], rest[2 * n_roll:]
        (u_ref, gb_ref, z_ref, qc0, qc1, qc2, kvc0, kvc1, kvc2, kvt0, kvt1, kvt2) = rest[:12]
        _cast_slabs(cast_src, rest[12:12 + n_cast])
        _roll_buffers(roll_src, rest[12 + n_cast:12 + n_cast + n_roll], roll_by)
        nat = rest[-1]
    else:
        (u_ref, gb_ref, z_ref, q0, q1, q2, kv0, kv1, kv2) = rest
    tm = x_ref.shape[0]
    hm = hm_ref[...]

    for s in range(tm // sub):
        r0 = s * sub
        rows = slice(r0, r0 + sub)
        x = x_ref[rows, :]
        ms = jnp.mean(x * x, axis=-1, keepdims=True)
        hb = (x * lax.rsqrt(ms + EPS) * g1_ref[...]).astype(BF16)
        p_att = jnp.dot(hb, w_ref[:, _Q_OFF:_GB_OFF], preferred_element_type=F32)
        rc, ra, rb = rc_ref[rows, :], ra_ref[rows, :], rb_ref[rows, :]

        def head_norm_rope(xb, gain):
            msq = jnp.dot((xb * xb).astype(BF16), hm, preferred_element_type=F32)
            xn = xb * lax.rsqrt(msq + EPS) * gain
            return xn * rc + pltpu.roll(xn, LANES - ROPE_DIM // 2, 1) * ra + pltpu.roll(xn, ROPE_DIM // 2, 1) * rb

        for g, (window, dil) in enumerate(DIL_PAIRS):
            lanes = slice(g * GROUP_WIDTH, (g + 1) * GROUP_WIDTH)
            qn = head_norm_rope(p_att[:, :ATTN_WIDTH][:, lanes], qg_ref[...])
            kn = head_norm_rope(p_att[:, ATTN_WIDTH:2 * ATTN_WIDTH][:, lanes], kg_ref[...])
            vv = p_att[:, 2 * ATTN_WIDTH:][:, lanes]
            if not class_major:
                q_ref, kv_ref = ((q0, kv0), (q1, kv1), (q2, kv2))[g]
                q_ref[rows, :] = qn
                kv_ref[rows, :GROUP_WIDTH] = kn
                kv_ref[rows, GROUP_WIDTH:] = vv
                continue
            qc_ref, kvc_ref, kvt_ref = ((qc0, kvc0, kvt0), (qc1, kvc1, kvt1), (qc2, kvc2, kvt2))[g]
            slab = 3 * (s * N_DIL + g)
            nat[slab] = qn
            nat[slab + 1] = kn
            nat[slab + 2] = vv
            per = sub // dil
            dst = slice(s * per, (s + 1) * per)
            for r in range(dil):
                src = pl.ds(r, per, stride=dil) if dil > 1 else slice(None)
                qc_ref[0, r, dst, :] = nat[slab, src, :].astype(BF16)
                kvc_ref[0, r, dst, :GROUP_WIDTH] = nat[slab + 1, src, :].astype(BF16)
                kvc_ref[0, r, dst, GROUP_WIDTH:] = nat[slab + 2, src, :].astype(BF16)
            def put_kvt(cols, part, g=g, kvt_ref=kvt_ref):
                if stacked:
                    kvt_ref[0, 0, :, cols] = prev_kvt[g][0, :, cols]
                    kvt_ref[1, 0, :, cols] = part
                else:
                    kvt_ref[0, :, cols] = part

            if window >= tiles * tm:
                put_kvt(rows, jnp.concatenate([kn, vv], axis=1).T)
            else:
                first_kept = tm - min(window, tm)
                lo = max(r0, first_kept)
                if lo < r0 + sub:
                    part = jnp.concatenate([kn[lo - r0:], vv[lo - r0:]], axis=1).T
                    put_kvt(slice(lo - first_kept, r0 + sub - first_kept), part)

        u_ref[rows, :] = jnp.dot(hb, w_ref[:, :POOL_WIDTH], preferred_element_type=F32)
        p_gate = jnp.dot(hb, w_ref[:, _GB_OFF:], preferred_element_type=F32)
        gb_ref[rows, :] = p_gate[:, :CONV_WIDTH]
        z_ref[rows, :] = p_gate[:, CONV_WIDTH:2 * CONV_WIDTH] * p_gate[:, 2 * CONV_WIDTH:]


def _proj(x, g1, w_in, qg, kg, rope, hm, *, layer, tm, sub, batch=None, seq=None, prev_kvt=None, convert=None,
          roll=(), roll_by=0):
    rows = x.shape[0]
    stacked = prev_kvt is not None
    n_tab = rope[0].shape[0] // tm
    class_major = batch is not None
    tiles = seq // tm if class_major else 1
    row = lambda c: pl.BlockSpec((tm, c), lambda i: (i, 0))
    tab = pl.BlockSpec((tm, LANES), lambda i: (i % n_tab, 0))
    out_shape = [jax.ShapeDtypeStruct((rows, POOL_WIDTH), F32)] + [jax.ShapeDtypeStruct((rows, CONV_WIDTH), F32)] * 2
    out_specs = [row(POOL_WIDTH), row(CONV_WIDTH), row(CONV_WIDTH)]
    scratch = []
    prev_specs = []
    cast_in, cast_out, cast_shape, cast_args = _cast_specs(convert, rows // tm)
    assert class_major or (convert is None and not roll)
    roll_in, roll_out, roll_shape, roll_args = _roll_specs(roll, rows // tm)
    if class_major:
        for width in (GROUP_WIDTH, 2 * GROUP_WIDTH):
            for (_, dil) in DIL_PAIRS:
                out_shape.append(jax.ShapeDtypeStruct((batch, dil, seq // dil, width), BF16))
                out_specs.append(pl.BlockSpec((1, dil, tm // dil, width), lambda i: (i // tiles, 0, i % tiles, 0)))
        for (window, _) in DIL_PAIRS:
            assert window >= seq or window <= tm, "kv buffer rows must come from the last row tile"
            cols = tm if window >= seq else window
            col_blk = (lambda i: i % tiles) if window >= seq else (lambda i: 0)
            kvt_spec = pl.BlockSpec((1, 2 * GROUP_WIDTH, cols), lambda i, c=col_blk: (i // tiles, 0, c(i)))
            if stacked:
                prev_specs.append(kvt_spec)
                out_shape.append(jax.ShapeDtypeStruct((2, batch, 2 * GROUP_WIDTH, window), F32))
                out_specs.append(pl.BlockSpec((2, 1, 2 * GROUP_WIDTH, cols),
                                              lambda i, c=col_blk: (0, i // tiles, 0, c(i))))
            else:
                out_shape.append(jax.ShapeDtypeStruct((batch, 2 * GROUP_WIDTH, window), F32))
                out_specs.append(kvt_spec)
        scratch = [pltpu.VMEM((3 * N_DIL * (tm // sub), sub, GROUP_WIDTH), F32)]
    else:
        out_shape += [jax.ShapeDtypeStruct((rows, GROUP_WIDTH), F32)] * N_DIL
        out_shape += [jax.ShapeDtypeStruct((rows, 2 * GROUP_WIDTH), F32)] * N_DIL
        out_specs += [row(GROUP_WIDTH)] * N_DIL + [row(2 * GROUP_WIDTH)] * N_DIL
    return pl.pallas_call(
        functools.partial(_proj_kernel, class_major=class_major, tiles=tiles, sub=sub, stacked=stacked,
                          n_cast=len(cast_args), n_roll=len(roll_out), roll_by=roll_by),
        grid=(rows // tm,),
        in_specs=[row(D_MODEL), _layer_spec((1, D_MODEL), layer), _const_spec((D_MODEL, IN_COLS)),
                  _layer_spec((1, LANES), layer), _layer_spec((1, LANES), layer), tab, tab, tab,
                  _const_spec((LANES, LANES))] + prev_specs + cast_in + roll_in,
        out_specs=out_specs + cast_out + roll_out,
        out_shape=out_shape + cast_shape + roll_shape,
        scratch_shapes=scratch,
        compiler_params=_params(),
        name="proj",
    )(x, g1, w_in, qg, kg, *rope, hm, *(prev_kvt or ()), *cast_args, *roll_args)


def _lane_lt(shape, bound):
    return lax.broadcasted_iota(jnp.int32, shape, len(shape) - 1) < bound


def _stack_heads(q):
    first = _lane_lt(q.shape, HEAD_DIM)
    zero = jnp.zeros_like(q)
    return jnp.concatenate([jnp.where(first, q, zero), jnp.where(first, zero, q)], axis=0)


def _unstack(o2, l2, m2):
    n = o2.shape[0] // 2
    first = _lane_lt((n, GROUP_WIDTH), HEAD_DIM)
    return tuple(jnp.where(first, a[:n], a[n:]) for a in (o2, l2, m2))


def _merge(a, b):
    m = jnp.maximum(a[2], b[2])
    wa = jnp.exp(a[2] - m)
    wb = jnp.exp(b[2] - m)
    return wa * a[0] + wb * b[0], wa * a[1] + wb * b[1], m


def _attn_unit(q, kv, bias):
    nk = kv.shape[0]
    k = kv[:, :GROUP_WIDTH]
    v1 = jnp.concatenate([kv[:, GROUP_WIDTH:], jnp.ones((nk, GROUP_WIDTH), BF16)], axis=1)
    s = lax.dot_general(_stack_heads(q), k, _NT, preferred_element_type=F32) + bias
    mrow = jnp.max(s, axis=-1, keepdims=True)
    p = jnp.exp(s - mrow).astype(BF16)
    r = jnp.dot(p, v1, preferred_element_type=F32)
    return _unstack(r[:, :GROUP_WIDTH], r[:, GROUP_WIDTH:], mrow)


def _attn_p_kernel(qc0, qc1, qc2, kvc0, kvc1, kvc2, *rest, seq, n_cast):
    cast_src, y_ref, rest = rest[:n_cast], rest[n_cast], rest[n_cast + 1:]
    _cast_slabs(cast_src, rest[:n_cast])
    scratch = rest[n_cast:]
    parts = [scratch[3 * g:3 * g + 3] for g in range(N_DIL)]
    bias_ref = scratch[3 * N_DIL]
    acc_o, acc_l, acc_m = parts[0]
    @pl.when(pl.program_id(0) == 0)
    def _():
        qi = lax.broadcasted_iota(jnp.int32, (2 * ATTN_TILE, 2 * ATTN_TILE), 0) & (ATTN_TILE - 1)
        ki = lax.broadcasted_iota(jnp.int32, (2 * ATTN_TILE, 2 * ATTN_TILE), 1)
        for first_key_back in (0, 1):
            dist = qi - ki + first_key_back * ATTN_TILE
            bias_ref[first_key_back] = jnp.where((dist >= 0) & (dist <= N_BACK), 0.0, NEG_INF)

    for g, ((_, dil), q_ref, kv_ref) in enumerate(zip(DIL_PAIRS, (qc0, qc1, qc2), (kvc0, kvc1, kvc2))):
        dst = parts[g]
        n_sub = seq // dil // ATTN_TILE

        def unit(idx, q_ref=q_ref, kv_ref=kv_ref, n_sub=n_sub, dil=dil):
            if n_sub == 1:
                r, sub = idx, 0
                res = _attn_unit(q_ref[0, r], kv_ref[0, r], bias_ref[0, :, :ATTN_TILE])
            else:
                r, sub = (idx // n_sub, idx % n_sub) if dil > 1 else (0, idx)
                back = jnp.minimum(sub, 1)
                q0 = pl.multiple_of(sub * ATTN_TILE, ATTN_TILE)
                k0 = pl.multiple_of((sub - back) * ATTN_TILE, ATTN_TILE)
                res = _attn_unit(q_ref[0, r, pl.ds(q0, ATTN_TILE), :], kv_ref[0, r, pl.ds(k0, 2 * ATTN_TILE), :],
                                 bias_ref[back])
            start = sub * (ATTN_TILE * dil) + r
            rows = pl.ds(start, ATTN_TILE, stride=dil) if dil > 1 else pl.ds(pl.multiple_of(start, ATTN_TILE), ATTN_TILE)
            return rows, res

        def units_body(it, carry, unit=unit, dst=dst):
            for j in range(ATTN_UNROLL):
                rows, res = unit(it * ATTN_UNROLL + j)
                for ref, val in zip(dst, res):
                    ref[rows, :] = val
            return carry
        n_iter = dil * n_sub // ATTN_UNROLL
        if n_iter == 1:
            units_body(0, 0)
        else:
            lax.fori_loop(0, n_iter, units_body, 0)

        if g > 0:
            for c in range(seq // ATTN_MERGE_ROWS):
                rows = slice(c * ATTN_MERGE_ROWS, (c + 1) * ATTN_MERGE_ROWS)
                o, l, m = _merge((acc_o[rows, :], acc_l[rows, :], acc_m[rows, :]),
                                 tuple(ref[rows, :] for ref in parts[g]))
                if g == N_DIL - 1:
                    y_ref[0, rows, :] = (o / l).astype(y_ref.dtype)
                else:
                    acc_o[rows, :] = o
                    acc_l[rows, :] = l
                    acc_m[rows, :] = m


def _attn_p(qcs, kvcs, *, batch, seq, convert=None):
    in_specs = [pl.BlockSpec((1,) + a.shape[1:], lambda b: (b, 0, 0, 0)) for a in (*qcs, *kvcs)]
    cast_in, cast_out, cast_shape, cast_args = _cast_specs(convert, batch)
    acc = pltpu.VMEM((seq, GROUP_WIDTH), F32)
    return pl.pallas_call(
        functools.partial(_attn_p_kernel, seq=seq, n_cast=len(cast_args)),
        grid=(batch,),
        in_specs=in_specs + cast_in,
        out_specs=[pl.BlockSpec((1, seq, GROUP_WIDTH), lambda b: (b, 0, 0))] + cast_out,
        out_shape=[jax.ShapeDtypeStruct((batch, seq, GROUP_WIDTH), BF16)] + cast_shape,
        scratch_shapes=[acc] * (3 * N_DIL) + [pltpu.VMEM((2, 2 * ATTN_TILE, 2 * ATTN_TILE), F32)],
        compiler_params=_params(),
        name="attn_p",
    )(*qcs, *kvcs, *cast_args)


def _pool_rows(ext, pos, pw, ps):
    return jnp.dot(_pool_diff(ext, pos), pw, preferred_element_type=F32) * ps


def _pool_diff(ext, pos):
    s2 = ext + pltpu.roll(ext, 1, 0)
    s4 = s2 + pltpu.roll(s2, 2, 0)
    s8 = s4 + pltpu.roll(s4, 4, 0)
    s16 = s8 + pltpu.roll(s8, 8, 0)
    lane = lax.broadcasted_iota(jnp.int32, (1, POOL_WIDTH), 1)
    grp = [lane < (j + 1) * POOL_GROUP for j in range(3)]
    win = jnp.where(grp[0], s2, jnp.where(grp[1], s4, jnp.where(grp[2], s8, s16)))[HIST_ROWS:]
    width = jnp.where(grp[0], POOL_WINDOWS[0], jnp.where(grp[1], POOL_WINDOWS[1],
                      jnp.where(grp[2], POOL_WINDOWS[2], POOL_WINDOWS[3])))
    cnt = jnp.minimum(pos + 1, width).astype(F32)
    return (win / cnt - ext[HIST_ROWS:]).astype(BF16)


def _conv_rows(ext, gb, cw, hist):
    y = cw[0:1] * pltpu.roll(ext, 2, 0) + cw[1:2] * pltpu.roll(ext, 1, 0) + cw[2:3] * ext
    return gb * y[hist:]


def _attn_s_group(q, new_t, cache_t, window, dil):
    t = q.shape[0]
    q2 = _stack_heads(q).astype(BF16)
    s_c = jnp.dot(q2, cache_t[:GROUP_WIDTH].astype(BF16), preferred_element_type=F32)
    s_n = jnp.dot(q2, new_t[:GROUP_WIDTH].astype(BF16), preferred_element_type=F32)
    tq = lax.broadcasted_iota(jnp.int32, (2 * t, 1), 0) & (t - 1)
    rc = (lax.broadcasted_iota(jnp.int32, (1, window), 1) + t) & (window - 1)
    dist_c = window + tq - rc
    ok_c = (dist_c <= window) & ((dist_c & (dil - 1)) == 0) & (PAST_LEN - window + rc >= 0)
    tn = lax.broadcasted_iota(jnp.int32, (1, LANES), 1) - (LANES - t)
    dist_n = tq - tn
    ok_n = (tn >= 0) & (dist_n >= 0) & ((dist_n & (dil - 1)) == 0)
    s_c = jnp.where(ok_c, s_c, NEG_INF)
    s_n = jnp.where(ok_n, s_n, NEG_INF)
    mrow = jnp.maximum(jnp.max(s_c, axis=-1, keepdims=True), jnp.max(s_n, axis=-1, keepdims=True))
    p_c = jnp.exp(s_c - mrow)
    p_n = jnp.exp(s_n - mrow)
    o2 = (lax.dot_general(p_c.astype(BF16), cache_t[GROUP_WIDTH:].astype(BF16), _NT, preferred_element_type=F32)
          + lax.dot_general(p_n.astype(BF16), new_t[GROUP_WIDTH:].astype(BF16), _NT, preferred_element_type=F32))
    l2 = jnp.sum(p_c, axis=-1, keepdims=True) + jnp.sum(p_n, axis=-1, keepdims=True)
    return _unstack(o2, l2, mrow)


def _mix_s_kernel(u_ref, q0_ref, q1_ref, q2_ref, kn0_ref, kn1_ref, kn2_ref, gb_ref, z_ref, pst_ref, cst_ref,
                  pw_ref, ps_ref, cw_ref, c0_ref, c1_ref, c2_ref, yp_ref, ya_ref, yc_ref, nc0_ref, nc1_ref, nc2_ref,
                  *, t_new, n_per):
    keep_old = _lane_lt((2 * GROUP_WIDTH, LANES), LANES - t_new)
    pos = PAST_LEN + lax.broadcasted_iota(jnp.int32, (t_new, 1), 0)
    for i in range(n_per):
        state = None
        for (window, dil), q_ref, kn_ref, c_ref, nc_ref in zip(
                DIL_PAIRS, (q0_ref, q1_ref, q2_ref), (kn0_ref, kn1_ref, kn2_ref),
                (c0_ref, c1_ref, c2_ref), (nc0_ref, nc1_ref, nc2_ref)):
            kv_new = kn_ref[i]
            new_t = jnp.concatenate([jnp.zeros((LANES - t_new, 2 * GROUP_WIDTH), F32), kv_new], axis=0).T
            cache_t = c_ref[0, i]
            part = _attn_s_group(q_ref[i], new_t, cache_t, window, dil)
            state = part if state is None else _merge(state, part)
            nc_ref[0, i] = jnp.where(keep_old, cache_t[:, window - LANES:], new_t)
        ya_ref[i] = state[0] / state[1]

        u_ext = jnp.concatenate([pst_ref[i], u_ref[i]], axis=0)
        yp_ref[i] = _pool_rows(u_ext, pos, pw_ref[...], ps_ref[...])
        z_ext = jnp.concatenate([cst_ref[i], z_ref[i]], axis=0)
        yc_ref[i] = _conv_rows(z_ext, gb_ref[i], cw_ref[...], cst_ref.shape[1])


def _mix_s(u, qs, kns, gb, z, pst, cst, rotated_bufs, pw, ps, cw, *, layer, n_seq, t_new):
    n_per = MIX_S_SEQS
    blk = lambda r, c: pl.BlockSpec((n_per, r, c), lambda b: (b, 0, 0))
    three = lambda a: a.reshape(n_seq, -1, a.shape[-1])
    windows = [w for (w, _) in DIL_PAIRS]
    in_specs = ([blk(t_new, POOL_WIDTH)] + [blk(t_new, GROUP_WIDTH)] * N_DIL + [blk(t_new, 2 * GROUP_WIDTH)] * N_DIL
                + [blk(t_new, CONV_WIDTH)] * 2
                + [pl.BlockSpec((None, n_per) + a.shape[2:], lambda b: (layer, b, 0, 0)) for a in (pst, cst)]
                + [_layer_spec((POOL_WIDTH, POOL_WIDTH), layer), _layer_spec((1, POOL_WIDTH), layer),
                   _layer_spec((CONV_K, CONV_WIDTH), layer)]
                + [pl.BlockSpec((1, n_per, 2 * GROUP_WIDTH, w), lambda b: (layer, b, 0, 0)) for w in windows])
    args = [three(u), *[three(q) for q in qs], *[three(k) for k in kns], three(gb), three(z), pst, cst, pw, ps, cw,
            *rotated_bufs]
    n_in = len(args)
    out_specs = ([blk(t_new, POOL_WIDTH), blk(t_new, GROUP_WIDTH), blk(t_new, CONV_WIDTH)]
                 + [pl.BlockSpec((1, n_per, 2 * GROUP_WIDTH, LANES), lambda b, w=w: (layer, b, 0, w // LANES - 1))
                    for w in windows])
    out_shape = ([jax.ShapeDtypeStruct((n_seq, t_new, c), F32) for c in (POOL_WIDTH, GROUP_WIDTH, CONV_WIDTH)]
                 + [jax.ShapeDtypeStruct(c.shape, F32) for c in rotated_bufs])
    return pl.pallas_call(
        functools.partial(_mix_s_kernel, t_new=t_new, n_per=n_per),
        grid=(n_seq // n_per,),
        in_specs=in_specs,
        out_specs=out_specs,
        out_shape=out_shape,
        input_output_aliases={n_in - N_DIL + g: 3 + g for g in range(N_DIL)},
        compiler_params=_params(),
        name="mix_s",
    )(*args)


def _mlp_head(x, yp, ya, yc, wo_ref, g2_ref):
    mixed = jnp.concatenate([yp.astype(BF16), ya.astype(BF16), yc.astype(BF16)], axis=1)
    x1 = x + jnp.dot(mixed, wo_ref[...], preferred_element_type=F32)
    ms = jnp.mean(x1 * x1, axis=-1, keepdims=True)
    return x1, (x1 * lax.rsqrt(ms + EPS) * g2_ref[...]).astype(BF16)


def _mlp_s_kernel(x_ref, yp_ref, ya_ref, yc_ref, wo_ref, g2_ref, wu_ref, wd_ref, o_ref, hb_buf):
    @pl.when(pl.program_id(0) == 0)
    def _():
        x1, hb = _mlp_head(x_ref[...], yp_ref[...], ya_ref[...], yc_ref[...], wo_ref, g2_ref)
        o_ref[...] = x1
        hb_buf[...] = hb

    hf = jnp.dot(hb_buf[...], wu_ref[...], preferred_element_type=F32)
    act = jnp.square(jnp.maximum(hf, 0.0)).astype(BF16)
    o_ref[...] += jnp.dot(act, wd_ref[...], preferred_element_type=F32)


def _mlp_p_kernel(x_ref, u_ref, uh_ref, gb_ref, z_ref, zh_ref, ya_ref, pw_ref, ps_ref, cw_ref,
                  wo_ref, g2_ref, wu_ref, wd_ref, *rest, tf, tiles, n_cast, n_roll, roll_by):
    cast_src, roll_src, o_ref = rest[:n_cast], rest[n_cast:n_cast + 2 * n_roll], rest[n_cast + 2 * n_roll]
    outs, hb_buf = rest[n_cast + 2 * n_roll + 1:-1], rest[-1]
    _cast_slabs(cast_src, outs[:n_cast])
    _roll_buffers(roll_src, outs[n_cast:], roll_by)
    tm = x_ref.shape[0]
    tile = pl.program_id(0) % tiles
    has_hist = tile > 0
    pos = tile * tm + lax.broadcasted_iota(jnp.int32, (tm, 1), 0)
    yp = _pool_rows(jnp.concatenate([jnp.where(has_hist, uh_ref[...], 0.0), u_ref[...]], axis=0), pos,
                    pw_ref[...], ps_ref[...])
    yc = _conv_rows(jnp.concatenate([jnp.where(has_hist, zh_ref[...], 0.0), z_ref[...]], axis=0),
                    gb_ref[...], cw_ref[...], HIST_ROWS)
    x1, hb = _mlp_head(x_ref[...], yp, ya_ref[...], yc, wo_ref, g2_ref)
    hb_buf[...] = hb
    o_ref[...] = x1
    for c in range(D_FF // tf):
        hf = jnp.dot(hb_buf[...], wu_ref[:, c * tf:(c + 1) * tf], preferred_element_type=F32)
        act = jnp.square(jnp.maximum(hf, 0.0)).astype(BF16)
        o_ref[...] += jnp.dot(act, wd_ref[c * tf:(c + 1) * tf, :], preferred_element_type=F32)


def _mlp_s(x, yp, ya, yc, w_out, g2, w_up, w_down, *, layer):
    rows = x.shape[0]
    tf = MLP_FF_CHUNK
    full = lambda c: pl.BlockSpec((rows, c), lambda j: (0, 0))
    return pl.pallas_call(
        _mlp_s_kernel,
        grid=(D_FF // tf,),
        in_specs=[full(D_MODEL), full(POOL_WIDTH), full(GROUP_WIDTH), full(CONV_WIDTH),
                  _layer_spec((MIX_OUT, D_MODEL), layer), _layer_spec((1, D_MODEL), layer),
                  pl.BlockSpec((D_MODEL, tf), lambda j: (0, j)),
                  pl.BlockSpec((tf, D_MODEL), lambda j: (j, 0))],
        out_specs=full(D_MODEL),
        out_shape=jax.ShapeDtypeStruct((rows, D_MODEL), F32),
        scratch_shapes=[pltpu.VMEM((rows, D_MODEL), BF16)],
        compiler_params=_params(),
        name="mlp_s",
    )(x, yp, ya, yc, w_out, g2, w_up, w_down)


def _mlp_p(x, u, gb, z, ya, pw, ps, cw, w_out, g2, w_up, w_down, *, layer, tm, seq, convert=None, roll=(),
           roll_by=0):
    rows = x.shape[0]
    row = lambda c: pl.BlockSpec((tm, c), lambda i: (i, 0))
    conv_in, conv_out_specs, conv_out_shape, conv_args = _cast_specs(convert, rows // tm)
    roll_in, roll_out, roll_shape, roll_args = _roll_specs(roll, rows // tm)
    hist = lambda c: pl.BlockSpec((HIST_ROWS, c), lambda i: (jnp.maximum(i * (tm // HIST_ROWS) - 1, 0), 0))
    return pl.pallas_call(
        functools.partial(_mlp_p_kernel, tf=MLP_FF_CHUNK, tiles=seq // tm, n_cast=len(conv_args),
                          n_roll=len(roll_out), roll_by=roll_by),
        grid=(rows // tm,),
        in_specs=[row(D_MODEL), row(POOL_WIDTH), hist(POOL_WIDTH), row(CONV_WIDTH), row(CONV_WIDTH),
                  hist(CONV_WIDTH), row(GROUP_WIDTH), _layer_spec((POOL_WIDTH, POOL_WIDTH), layer),
                  _layer_spec((1, POOL_WIDTH), layer), _layer_spec((CONV_K, CONV_WIDTH), layer),
                  _layer_spec((MIX_OUT, D_MODEL), layer), _layer_spec((1, D_MODEL), layer),
                  _const_spec((D_MODEL, D_FF)), _const_spec((D_FF, D_MODEL))] + conv_in + roll_in,
        out_specs=[row(D_MODEL)] + conv_out_specs + roll_out,
        out_shape=[jax.ShapeDtypeStruct((rows, D_MODEL), F32)] + conv_out_shape + roll_shape,
        scratch_shapes=[pltpu.VMEM((tm, D_MODEL), BF16)],
        compiler_params=_params(),
        name="mlp_p",
    )(x, u, u, gb, z, z, ya, pw, ps, cw, w_out, g2, w_up, w_down, *conv_args, *roll_args)


def _rope_tables(pos):
    half = ROPE_DIM // 2
    inv = np.power(np.float64(ROPE_THETA), -np.arange(half, dtype=np.float64) / half)
    ang = pos.astype(np.float64)[:, None] * inv[None, :]
    cos, sin = np.cos(ang), np.sin(ang)
    n = pos.shape[0]
    rest = HEAD_DIM - ROPE_DIM
    zh = np.zeros((n, half))
    c = np.concatenate([cos, cos, np.ones((n, rest))], axis=1)
    a = np.concatenate([-sin, zh, np.zeros((n, rest))], axis=1)
    b = np.concatenate([zh, sin, np.zeros((n, rest))], axis=1)
    return tuple(jnp.asarray(np.tile(t, (1, GROUP_WIDTH // HEAD_DIM)), dtype=F32) for t in (c, a, b))


def _to_buffer_layout(c):
    lead = c.shape[:-4]
    n = len(lead)
    t = jnp.transpose(c, (*range(n), n + 1, n + 2, n + 3, n))
    return t.reshape(*lead, 2 * GROUP_WIDTH, c.shape[-4])


def _from_buffer_layout(t):
    lead = t.shape[:-2]
    n = len(lead)
    c = t.reshape(*lead, 2, 2, HEAD_DIM, t.shape[-1])
    return jnp.transpose(c, (*range(n), n + 3, n, n + 1, n + 2))


def kernel(x_prompt, x_sample, state_pool, state_conv, cache_kv_w128, cache_kv_w512, cache_kv_w2048,
           norm1_g, w_in, q_norm_g, k_norm_g, pool_w, pool_scale, conv_w, w_out, norm2_g, w_up, w_down):
    batch, seq, _ = x_prompt.shape
    n_seq, t_new, _ = x_sample.shape
    depth = w_in.shape[0]
    assert depth == 2, "the in-place stacking of the kv state outputs is written for two layers"
    caches_t = [_to_buffer_layout(c) for c in (cache_kv_w128, cache_kv_w512, cache_kv_w2048)]

    rope_p = _rope_tables(np.arange(seq))
    rope_s = _rope_tables(np.tile(PAST_LEN + np.arange(t_new), n_seq))
    head_id = np.arange(LANES) // HEAD_DIM
    hm = jnp.asarray(np.where(head_id[:, None] == head_id[None, :], 1.0 / HEAD_DIM, 0.0), dtype=BF16)
    g1 = norm1_g[:, None, :]
    g2 = norm2_g[:, None, :]
    ps = pool_scale[:, None, :]
    cw = conv_w
    two_heads = lambda gain: jnp.tile(gain, (1, GROUP_WIDTH // HEAD_DIM))[:, None, :]
    qg = two_heads(q_norm_g) * (HEAD_DIM ** -0.5)
    kg = two_heads(k_norm_g)
    n_win = len(POOL_WINDOWS)
    pw = jnp.einsum("gh,lgij->lgihj", jnp.eye(n_win, dtype=F32), pool_w).reshape(depth, POOL_WIDTH, POOL_WIDTH)
    pw = pw.astype(BF16)
    pst = jnp.pad(state_pool, ((0, 0), (0, 0), (HIST_ROWS - POOL_HIST, 0), (0, 0)))
    cst = jnp.pad(state_conv, ((0, 0), (0, 0), (SUBLANES - (CONV_K - 1), 0), (0, 0)))

    xp = x_prompt.reshape(batch * seq, D_MODEL)
    xs = x_sample.reshape(n_seq * t_new, D_MODEL)
    outs = {k: [] for k in ("pool_p", "conv_p", "pool_s", "conv_s")}
    w_in_l = w_in[0].astype(BF16)
    new_caches = [None] * N_DIL
    kvts = None
    for layer in range(depth):
        if layer == 0:
            (u, gb, z, qc0, qc1, qc2, kvc0, kvc1, kvc2, *tail) = _proj(
                xp, g1, w_in_l, qg, kg, rope_p, hm, layer=layer, tm=PROJ_TILE, sub=PROJ_SUB, batch=batch,
                seq=seq, roll=caches_t[:-1], roll_by=t_new)
            kvts, new_caches[:-1] = tail[:N_DIL], tail[N_DIL:]
            ya, w_up_l, w_down_l, w_out_b = _attn_p((qc0, qc1, qc2), (kvc0, kvc1, kvc2), batch=batch, seq=seq,
                                                    convert=(0, (w_up, w_down), (w_out,)))
            xp, w_in_next, new_caches[-1] = _mlp_p(
                xp, u, gb, z, ya.reshape(-1, GROUP_WIDTH), pw, ps, cw, w_out_b, g2, w_up_l, w_down_l, layer=layer,
                tm=MLP_TILE, seq=seq, convert=(1, (w_in,)), roll=caches_t[-1:], roll_by=t_new)
        else:
            (u, gb, z, qc0, qc1, qc2, kvc0, kvc1, kvc2, *tail) = _proj(
                xp, g1, w_in_l, qg, kg, rope_p, hm, layer=layer, tm=PROJ_TILE, sub=PROJ_SUB, batch=batch,
                seq=seq, prev_kvt=kvts, convert=(1, (w_up, w_down)))
            kvts, (w_up_l, w_down_l) = tail[:N_DIL], tail[N_DIL:]
            (ya,) = _attn_p((qc0, qc1, qc2), (kvc0, kvc1, kvc2), batch=batch, seq=seq)
            (xp,) = _mlp_p(xp, u, gb, z, ya.reshape(-1, GROUP_WIDTH), pw, ps, cw, w_out_b, g2, w_up_l, w_down_l,
                           layer=layer, tm=MLP_TILE, seq=seq)
        outs["pool_p"].append(u.reshape(batch, seq, POOL_WIDTH)[:, seq - POOL_HIST:])
        outs["conv_p"].append(z.reshape(batch, seq, CONV_WIDTH)[:, seq - (CONV_K - 1):])

        us, gbs, zs, q0, q1, q2, kn0, kn1, kn2 = _proj(xs, g1, w_in_l, qg, kg, rope_s, hm, layer=layer,
                                                       tm=n_seq * t_new, sub=n_seq * t_new)
        yps, yas, ycs, *new_caches = _mix_s(us, (q0, q1, q2), (kn0, kn1, kn2), gbs, zs, pst, cst, new_caches,
                                           pw, ps, cw, layer=layer, n_seq=n_seq, t_new=t_new)
        xs = _mlp_s(xs, yps.reshape(-1, POOL_WIDTH), yas.reshape(-1, GROUP_WIDTH), ycs.reshape(-1, CONV_WIDTH),
                    w_out_b, g2, w_up_l, w_down_l, layer=layer)
        us3 = us.reshape(n_seq, t_new, POOL_WIDTH)
        zs3 = zs.reshape(n_seq, t_new, CONV_WIDTH)
        outs["pool_s"].append(jnp.concatenate([state_pool[layer], us3], axis=1)[:, -POOL_HIST:])
        outs["conv_s"].append(jnp.concatenate([state_conv[layer], zs3], axis=1)[:, -(CONV_K - 1):])
        if layer == 0:
            w_in_l = w_in_next

    st = lambda k: jnp.stack(outs[k])
    return (xp.reshape(batch, seq, D_MODEL), xs.reshape(n_seq, t_new, D_MODEL),
            st("pool_p"), st("conv_p"),
            *[_from_buffer_layout(kvt) for kvt in kvts],
            st("pool_s"), st("conv_s"),
            *[_from_buffer_layout(c) for c in new_caches])
```
